```python
import math
import jax, jax.numpy as jnp
from jax import lax
import numpy as np

D_MODEL = 2048
BATCH = 1
SEQ = 16384
DEPTH = 1
DEC_BATCH = 8
DEC_SEQ = 2048
PAST_LEN = 128

MLA_HEADS = 8
MLA_NOPE = 128
MLA_ROPE = 64
MLA_QK = MLA_NOPE + MLA_ROPE
MLA_V = 128
Q_LORA = 512
KV_LORA = 256
ROPE_THETA = 10000.0
MLA_OUT = MLA_HEADS * MLA_V
DIFF_HEADS = 8
DIFF_QK = 64
DIFF_V = 2 * DIFF_QK
DIFF_OUT = DIFF_HEADS * DIFF_V
REL_BUCKETS = 32
REL_MAX_DIST = 128
D_FF = -(-(8 * D_MODEL) // (3 * 256)) * 256
Q_BLOCK = 128
EPS = 1e-6
COL_SIZES = (Q_LORA, KV_LORA, MLA_ROPE,
             DIFF_HEADS * 2 * DIFF_QK, DIFF_HEADS * 2 * DIFF_QK, DIFF_OUT,
             D_MODEL, D_MODEL)
D_IN = sum(COL_SIZES)

kernel_name = "hybrid_mla_diffattn_gated_encoder"


def _split_points():
    pts, acc = [], 0
    for s in COL_SIZES[:-1]:
        acc += s
        pts.append(acc)
    return pts


def rmsnorm(x, g):
    xf = x.astype(jnp.float32)
    y = xf * lax.rsqrt(jnp.mean(xf * xf, axis=-1, keepdims=True) + EPS)
    return (y * g.astype(jnp.float32)).astype(x.dtype)


def rope(x, pos):
    half = x.shape[-1] // 2
    inv = ROPE_THETA ** (-jnp.arange(half, dtype=jnp.float32) / half)
    ang = pos[:, None].astype(jnp.float32) * inv[None, :]
    cos = jnp.cos(ang)[None, :, None, :].astype(x.dtype)
    sin = jnp.sin(ang)[None, :, None, :].astype(x.dtype)
    x1, x2 = x[..., :half], x[..., half:]
    return jnp.concatenate([x1 * cos - x2 * sin, x2 * cos + x1 * sin], axis=-1)


def t5_bucket(rel):
    nb = REL_BUCKETS // 2
    ret = jnp.where(rel > 0, nb, 0)
    n = jnp.abs(rel)
    max_exact = nb // 2
    nf = jnp.maximum(n, 1).astype(jnp.float32)
    large = max_exact + (jnp.log(nf / max_exact) / math.log(REL_MAX_DIST / max_exact)
                         * (nb - max_exact)).astype(jnp.int32)
    large = jnp.minimum(large, nb - 1)
    return ret + jnp.where(n < max_exact, n, large)


def mla_attend(q, k, v):
    B, S, H, D = q.shape
    nb = S // Q_BLOCK
    qb = q.reshape(B, nb, Q_BLOCK, H, D).swapaxes(0, 1)
    scale = D ** -0.5

    def blk(qi):
        s = jnp.einsum('bqhd,bkhd->bhqk', qi, k).astype(jnp.float32) * scale
        p = jax.nn.softmax(s, axis=-1)
        return jnp.einsum('bhqk,bkhd->bqhd', p.astype(v.dtype), v)

    o = lax.map(blk, qb)
    return o.swapaxes(0, 1).reshape(B, S, H * v.shape[-1])


def diff_attend(q, k, v, lam, rel_bias):
    B, S, H, _, D = q.shape
    nb = S // Q_BLOCK
    qb = q.reshape(B, nb, Q_BLOCK, H, 2, D).swapaxes(0, 1)
    kpos = jnp.arange(S, dtype=jnp.int32)
    scale = D ** -0.5

    def blk(args):
        qi, i = args
        qpos = i * Q_BLOCK + jnp.arange(Q_BLOCK, dtype=jnp.int32)
        bias = rel_bias[t5_bucket(kpos[None, :] - qpos[:, None])]
        bias = jnp.transpose(bias, (2, 0, 1)).astype(jnp.float32)[None, :, None]
        s = jnp.einsum('bqhmd,bkhmd->bhmqk', qi, k).astype(jnp.float32) * scale + bias
        p = jax.nn.softmax(s, axis=-1)
        a = p[:, :, 0] - lam * p[:, :, 1]
        return jnp.einsum('bhqk,bkhd->bqhd', a.astype(v.dtype), v)

    o = lax.map(blk, (qb, jnp.arange(nb, dtype=jnp.int32)))
    return o.swapaxes(0, 1).reshape(B, S, H, v.shape[-1])


def encoder_layer(x, l, p, rel_bias):
    B, S, _ = x.shape
    pos = jnp.arange(S, dtype=jnp.int32)
    h = rmsnorm(x, p['mix_norm'][l])
    z = h @ p['w_in'][l]
    c_q, c_kv, k_pe, dq, dk, dv, g_a, g_b = jnp.split(z, _split_points(), axis=-1)

    q = (rmsnorm(c_q, p['q_a_norm'][l]) @ p['wq_b'][l]).reshape(B, S, MLA_HEADS, MLA_QK)
    kv = (rmsnorm(c_kv, p['kv_a_norm'][l]) @ p['wkv_b'][l]).reshape(B, S, MLA_HEADS, MLA_NOPE + MLA_V)
    k_nope, v_a = kv[..., :MLA_NOPE], kv[..., MLA_NOPE:]
    k_pe = jnp.broadcast_to(k_pe[:, :, None, :], (B, S, MLA_HEADS, MLA_ROPE))
    k = jnp.concatenate([k_nope, k_pe], axis=-1)
    q = rmsnorm(q, p['mla_q_norm'][l])
    k = rmsnorm(k, p['mla_k_norm'][l])
    q = jnp.concatenate([q[..., :MLA_NOPE], rope(q[..., MLA_NOPE:], pos)], axis=-1)
    k = jnp.concatenate([k[..., :MLA_NOPE], rope(k[..., MLA_NOPE:], pos)], axis=-1)
    o_a = mla_attend(q, k, v_a)

    lam_init = 0.8 - 0.6 * math.exp(-0.3 * l)
    lam = (jnp.exp(jnp.sum(p['lambda_q1'][l].astype(jnp.float32) * p['lambda_k1'][l].astype(jnp.float32)))
           - jnp.exp(jnp.sum(p['lambda_q2'][l].astype(jnp.float32) * p['lambda_k2'][l].astype(jnp.float32)))
           + lam_init)
    qd = rmsnorm(dq.reshape(B, S, DIFF_HEADS, 2, DIFF_QK), p['diff_q_norm'][l])
    kd = rmsnorm(dk.reshape(B, S, DIFF_HEADS, 2, DIFF_QK), p['diff_k_norm'][l])
    vd = dv.reshape(B, S, DIFF_HEADS, DIFF_V)
    o_b = diff_attend(qd, kd, vd, lam, rel_bias)
    o_b = (rmsnorm(o_b, p['diff_subln'][l]) * (1.0 - lam_init)).reshape(B, S, DIFF_OUT)

    merged = (jax.nn.sigmoid(g_a) * (o_a @ p['w_up_mla'][l])
              + jax.nn.sigmoid(g_b) * (o_b @ p['w_up_diff'][l]))
    x = x + merged @ p['w_o'][l]

    h = rmsnorm(x, p['ffn_norm'][l])
    x = x + (jax.nn.silu(h @ p['w_gate'][l]) * (h @ p['w_up'][l])) @ p['w_down'][l]
    return x


def setup_inputs(seed: int = 0) -> dict:
    key = jax.random.key(seed)
    ks = jax.random.split(key, 32)
    f32 = jnp.float32

    def w(k, shape, fan_in):
        return jax.random.normal(k, shape, f32) * fan_in ** -0.5

    def g(k, shape):
        return 1.0 + 0.02 * jax.random.normal(k, shape, f32)

    L = DEPTH
    return {
        'x_prompt': jax.random.normal(ks[0], (BATCH, SEQ, D_MODEL), f32),
        'x_sample': jax.random.normal(ks[1], (DEC_BATCH, DEC_SEQ, D_MODEL), f32),
        'mix_norm': g(ks[2], (L, D_MODEL)),
        'w_in': w(ks[3], (L, D_MODEL, D_IN), D_MODEL),
        'q_a_norm': g(ks[4], (L, Q_LORA)),
        'wq_b': w(ks[5], (L, Q_LORA, MLA_HEADS * MLA_QK), Q_LORA),
        'kv_a_norm': g(ks[6], (L, KV_LORA)),
        'wkv_b': w(ks[7], (L, KV_LORA, MLA_HEADS * (MLA_NOPE + MLA_V)), KV_LORA),
        'mla_q_norm': g(ks[8], (L, MLA_QK)),
        'mla_k_norm': g(ks[9], (L, MLA_QK)),
        'diff_q_norm': g(ks[10], (L, DIFF_QK)),
        'diff_k_norm': g(ks[11], (L, DIFF_QK)),
        'lambda_q1': 0.1 * jax.random.normal(ks[12], (L, DIFF_QK), f32),
        'lambda_k1': 0.1 * jax.random.normal(ks[13], (L, DIFF_QK), f32),
        'lambda_q2': 0.1 * jax.random.normal(ks[14], (L, DIFF_QK), f32),
        'lambda_k2': 0.1 * jax.random.normal(ks[15], (L, DIFF_QK), f32),
        'diff_subln': g(ks[16], (L, DIFF_V)),
        'w_up_mla': w(ks[17], (L, MLA_OUT, D_MODEL), MLA_OUT),
        'w_up_diff': w(ks[18], (L, DIFF_OUT, D_MODEL), DIFF_OUT),
        'w_o': w(ks[19], (L, D_MODEL, D_MODEL), D_MODEL),
        'ffn_norm': g(ks[20], (L, D_MODEL)),
        'w_gate': w(ks[21], (L, D_MODEL, D_FF), D_MODEL),
        'w_up': w(ks[22], (L, D_MODEL, D_FF), D_MODEL),
        'w_down': w(ks[23], (L, D_FF, D_MODEL), D_FF),
        'rel_bias': 0.5 * jax.random.normal(ks[24], (REL_BUCKETS, DIFF_HEADS), f32),
    }


def reference(x_prompt, x_sample, mix_norm, w_in, q_a_norm, wq_b, kv_a_norm, wkv_b,
              mla_q_norm, mla_k_norm, diff_q_norm, diff_k_norm, lambda_q1, lambda_k1,
              lambda_q2, lambda_k2, diff_subln, w_up_mla, w_up_diff, w_o, ffn_norm,
              w_gate, w_up, w_down, rel_bias):
    p = dict(mix_norm=mix_norm, w_in=w_in, q_a_norm=q_a_norm, wq_b=wq_b,
             kv_a_norm=kv_a_norm, wkv_b=wkv_b, mla_q_norm=mla_q_norm, mla_k_norm=mla_k_norm,
             diff_q_norm=diff_q_norm, diff_k_norm=diff_k_norm, lambda_q1=lambda_q1,
             lambda_k1=lambda_k1, lambda_q2=lambda_q2, lambda_k2=lambda_k2,
             diff_subln=diff_subln, w_up_mla=w_up_mla, w_up_diff=w_up_diff, w_o=w_o,
             ffn_norm=ffn_norm, w_gate=w_gate, w_up=w_up, w_down=w_down)
    y_prompt = x_prompt
    y_sample = x_sample
    for l in range(DEPTH):
        y_prompt = encoder_layer(y_prompt, l, p, rel_bias)
        y_sample = encoder_layer(y_sample, l, p, rel_bias)
    return (y_prompt, y_sample)
```

```python
import functools
import math

import jax
import jax.numpy as jnp
from jax import lax
from jax.experimental import pallas as pl
from jax.experimental.pallas import tpu as pltpu

D_MODEL = 2048
MLA_HEADS = 8
MLA_NOPE = 128
MLA_ROPE = 64
MLA_QK = MLA_NOPE + MLA_ROPE
MLA_V = 128
Q_LORA = 512
KV_LORA = 256
ROPE_THETA = 10000.0
ROPE_HALF = MLA_ROPE // 2
DIFF_HEADS = 8
DIFF_QK = 64
DIFF_V = 2 * DIFF_QK
REL_BUCKETS = 32
REL_MAX_DIST = 128
D_FF = 5632
EPS = 1e-6
HEAD_OUT = MLA_HEADS * MLA_V

LANES = 128
MLA_QK_PAD = 2 * LANES
VMEM_LIMIT = 56 * 1024 * 1024

F32 = jnp.float32
BF16 = jnp.bfloat16
NEG_BIG = -1e30

NT_DIMS = (((1,), (1,)), ((), ()))


def _params(semantics):
    return pltpu.CompilerParams(dimension_semantics=semantics, vmem_limit_bytes=VMEM_LIMIT)


def _const_spec(shape):
    zeros = (0,) * len(shape)
    return pl.BlockSpec(shape, lambda *_: zeros)


def _rms_rows(x, gain):
    return x * lax.rsqrt(jnp.mean(x * x, axis=-1, keepdims=True) + EPS) * gain


def _rope_table_kernel(inv_row_ref, inv_col_ref, cos_t_ref, sin_t_ref, cos_f_ref, sin_f_ref):
    ts = cos_t_ref.shape[0]
    base = pl.program_id(0) * ts
    pos_rows = (base + lax.broadcasted_iota(jnp.int32, (ts, LANES), 0)).astype(F32)
    ang_t = pos_rows * inv_row_ref[...]
    cos_t_ref[...] = jnp.cos(ang_t)
    sin_t_ref[...] = jnp.sin(ang_t)
    pos_cols = (base + lax.broadcasted_iota(jnp.int32, (ROPE_HALF, ts), 1)).astype(F32)
    ang_f = pos_cols * inv_col_ref[...]
    cos_f_ref[...] = jnp.cos(ang_f)
    sin_f_ref[...] = jnp.sin(ang_f)


def _rope_tables(seq):
    inv = ROPE_THETA ** (-jnp.arange(ROPE_HALF, dtype=F32) / ROPE_HALF)
    inv_row = jnp.tile(inv, LANES // ROPE_HALF)[None, :]
    inv_col = inv[:, None]
    ts = min(seq, 2048)
    tok = jax.ShapeDtypeStruct((seq, LANES), F32)
    feat = jax.ShapeDtypeStruct((ROPE_HALF, seq), F32)
    return pl.pallas_call(
        _rope_table_kernel,
        out_shape=(tok, tok, feat, feat),
        grid=(seq // ts,),
        in_specs=[_const_spec((1, LANES)), _const_spec((ROPE_HALF, 1))],
        out_specs=(
            pl.BlockSpec((ts, LANES), lambda i: (i, 0)),
            pl.BlockSpec((ts, LANES), lambda i: (i, 0)),
            pl.BlockSpec((ROPE_HALF, ts), lambda i: (0, i)),
            pl.BlockSpec((ROPE_HALF, ts), lambda i: (0, i)),
        ),
        compiler_params=_params(("arbitrary",)), name="rope_tables",
    )(inv_row, inv_col)


def _mla_prep_kernel(x_ref, g_mix_ref, w_a_ref, g_qa_ref, g_kva_ref, wq_t_ref, g_q_ref,
                     wk_ref, wv_t_ref, g_kn_ref, g_kr_ref, g_krot_ref,
                     cos_t_ref, sin_t_ref, cos_f_ref, sin_f_ref,
                     q_ref, k_ref, v_ref, *, tk):
    tm = x_ref.shape[0]
    h = _rms_rows(x_ref[...], g_mix_ref[...]).astype(BF16)
    c = jnp.dot(h, w_a_ref[...], preferred_element_type=F32)
    cq = _rms_rows(c[:, :Q_LORA], g_qa_ref[...]).astype(BF16)
    ckv = _rms_rows(c[:, Q_LORA:Q_LORA + KV_LORA], g_kva_ref[...]).astype(BF16)
    k_pe = c[:, 6 * LANES:7 * LANES]
    k_pe_rot = c[:, 7 * LANES:8 * LANES]

    q_t = lax.dot_general(wq_t_ref[...], cq, NT_DIMS, preferred_element_type=F32)
    cos_f = cos_f_ref[...]
    sin_f = sin_f_ref[...]
    scale = MLA_QK ** -0.5
    g_q = g_q_ref[...]
    for hd in range(MLA_HEADS):
        xh = q_t[hd * MLA_QK:(hd + 1) * MLA_QK]
        rinv = lax.rsqrt(jnp.mean(xh * xh, axis=0, keepdims=True) + EPS)
        xn = xh * rinv * g_q
        x1 = xn[MLA_NOPE:MLA_NOPE + ROPE_HALF]
        x2 = xn[MLA_NOPE + ROPE_HALF:]
        q_ref[0, hd, 0:MLA_NOPE, :] = (xn[:MLA_NOPE] * scale).astype(BF16)
        q_ref[0, hd, MLA_NOPE:MLA_NOPE + ROPE_HALF, :] = ((x1 * cos_f - x2 * sin_f) * scale).astype(BF16)
        q_ref[0, hd, MLA_NOPE + ROPE_HALF:MLA_QK, :] = ((x2 * cos_f + x1 * sin_f) * scale).astype(BF16)
        q_ref[0, hd, MLA_QK:, :] = jnp.zeros((MLA_QK_PAD - MLA_QK, tm), BF16)

    k_nope = jnp.dot(ckv, wk_ref[...], preferred_element_type=F32)
    rope_base = (k_pe * g_kr_ref[...]) * cos_t_ref[...] + (k_pe_rot * g_krot_ref[...]) * sin_t_ref[...]
    ss_pe = jnp.sum(k_pe * k_pe, axis=-1, keepdims=True)
    for hd in range(MLA_HEADS):
        kh = k_nope[:, hd * MLA_NOPE:(hd + 1) * MLA_NOPE]
        ss = jnp.sum(kh * kh, axis=-1, keepdims=True) + ss_pe
        rinv = lax.rsqrt(ss * (1.0 / MLA_QK) + EPS)
        k_ref[0, hd, :, 0:LANES] = (kh * rinv * g_kn_ref[...]).astype(BF16)
        k_ref[0, hd, :, LANES:] = (rope_base * rinv).astype(BF16)

    v_t = lax.dot_general(wv_t_ref[...], ckv, NT_DIMS, preferred_element_type=F32)
    for hd in range(MLA_HEADS):
        for cc in range(tm // tk):
            v_ref[0, hd, cc] = v_t[hd * MLA_V:(hd + 1) * MLA_V, cc * tk:(cc + 1) * tk].astype(BF16)


def _mla_prep(x2d, gb, seq, tm, tk, w, tables):
    tokens = x2d.shape[0]
    nt = seq // tm
    cos_t, sin_t, cos_f, sin_f = tables
    kern = functools.partial(_mla_prep_kernel, tk=tk)
    out_shape = (
        jax.ShapeDtypeStruct((gb, MLA_HEADS, MLA_QK_PAD, seq), BF16),
        jax.ShapeDtypeStruct((gb, MLA_HEADS, seq, MLA_QK_PAD), BF16),
        jax.ShapeDtypeStruct((gb, MLA_HEADS, seq // tk, MLA_V, tk), BF16),
    )
    in_specs = [
        pl.BlockSpec((tm, D_MODEL), lambda i: (i, 0)),
        _const_spec((1, D_MODEL)),
        _const_spec((D_MODEL, 8 * LANES)),
        _const_spec((1, Q_LORA)),
        _const_spec((1, KV_LORA)),
        _const_spec((MLA_HEADS * MLA_QK, Q_LORA)),
        _const_spec((MLA_QK, 1)),
        _const_spec((KV_LORA, MLA_HEADS * MLA_NOPE)),
        _const_spec((MLA_HEADS * MLA_V, KV_LORA)),
        _const_spec((1, LANES)),
        _const_spec((1, LANES)),
        _const_spec((1, LANES)),
        pl.BlockSpec((tm, LANES), lambda i: (i % nt, 0)),
        pl.BlockSpec((tm, LANES), lambda i: (i % nt, 0)),
        pl.BlockSpec((ROPE_HALF, tm), lambda i: (0, i % nt)),
        pl.BlockSpec((ROPE_HALF, tm), lambda i: (0, i % nt)),
    ]
    out_specs = (
        pl.BlockSpec((1, MLA_HEADS, MLA_QK_PAD, tm), lambda i: (i // nt, 0, 0, i % nt)),
        pl.BlockSpec((1, MLA_HEADS, tm, MLA_QK_PAD), lambda i: (i // nt, 0, i % nt, 0)),
        pl.BlockSpec((1, MLA_HEADS, tm // tk, MLA_V, tk), lambda i: (i // nt, 0, i % nt, 0, 0)),
    )
    return pl.pallas_call(
        kern, out_shape=out_shape, grid=(tokens // tm,), in_specs=in_specs, out_specs=out_specs,
        compiler_params=_params(("arbitrary",)), name="mla_prep",
    )(x2d, w["g_mix"], w["w_a"], w["g_qa"], w["g_kva"], w["wq_t"], w["g_q"], w["wk_nope"],
      w["wv_t"], w["g_kn"], w["g_kr"], w["g_krot"], cos_t, sin_t, cos_f, sin_f)


def _group_norm_cols(x_t, gain_col):
    rows, tm = x_t.shape
    x3 = x_t.reshape(rows // DIFF_QK, DIFF_QK, tm)
    rinv = lax.rsqrt(jnp.mean(x3 * x3, axis=1, keepdims=True) + EPS)
    return (x3 * rinv * gain_col[None]).reshape(rows, tm)


def _diff_prep_kernel(x_ref, g_mix_ref, wq_t_ref, wk_t_ref, wv_t_ref, g_q_ref, g_k_ref,
                      q_ref, k_ref, v_ref, *, tk):
    tm = x_ref.shape[0]
    h = _rms_rows(x_ref[...], g_mix_ref[...]).astype(BF16)
    scale = DIFF_QK ** -0.5
    q_t = lax.dot_general(wq_t_ref[...], h, NT_DIMS, preferred_element_type=F32)
    qn = (_group_norm_cols(q_t, g_q_ref[...]) * scale).astype(BF16)
    zeros = jnp.zeros((DIFF_QK, tm), BF16)
    for hd in range(DIFF_HEADS):
        r0 = hd * DIFF_V
        q_ref[0, hd, 0, 0:DIFF_QK, :] = qn[r0:r0 + DIFF_QK]
        q_ref[0, hd, 0, DIFF_QK:, :] = zeros
        q_ref[0, hd, 1, 0:DIFF_QK, :] = zeros
        q_ref[0, hd, 1, DIFF_QK:, :] = qn[r0 + DIFF_QK:r0 + DIFF_V]

    k_t = lax.dot_general(wk_t_ref[...], h, NT_DIMS, preferred_element_type=F32)
    kn = _group_norm_cols(k_t, g_k_ref[...]).T
    for hd in range(DIFF_HEADS):
        k_ref[0, hd] = kn[:, hd * DIFF_V:(hd + 1) * DIFF_V].astype(BF16)

    v_t = lax.dot_general(wv_t_ref[...], h, NT_DIMS, preferred_element_type=F32)
    for hd in range(DIFF_HEADS):
        for cc in range(tm // tk):
            v_ref[0, hd, cc] = v_t[hd * DIFF_V:(hd + 1) * DIFF_V, cc * tk:(cc + 1) * tk].astype(BF16)


def _diff_prep(x2d, gb, seq, tm, tk, w):
    tokens = x2d.shape[0]
    nt = seq // tm
    kern = functools.partial(_diff_prep_kernel, tk=tk)
    out_shape = (
        jax.ShapeDtypeStruct((gb, DIFF_HEADS, 2, DIFF_V, seq), BF16),
        jax.ShapeDtypeStruct((gb, DIFF_HEADS, seq, DIFF_V), BF16),
        jax.ShapeDtypeStruct((gb, DIFF_HEADS, seq // tk, DIFF_V, tk), BF16),
    )
    in_specs = [
        pl.BlockSpec((tm, D_MODEL), lambda i: (i, 0)),
        _const_spec((1, D_MODEL)),
        _const_spec((HEAD_OUT, D_MODEL)),
        _const_spec((HEAD_OUT, D_MODEL)),
        _const_spec((HEAD_OUT, D_MODEL)),
        _const_spec((DIFF_QK, 1)),
        _const_spec((DIFF_QK, 1)),
    ]
    out_specs = (
        pl.BlockSpec((1, DIFF_HEADS, 2, DIFF_V, tm), lambda i: (i // nt, 0, 0, 0, i % nt)),
        pl.BlockSpec((1, DIFF_HEADS, tm, DIFF_V), lambda i: (i // nt, 0, i % nt, 0)),
        pl.BlockSpec((1, DIFF_HEADS, tm // tk, DIFF_V, tk), lambda i: (i // nt, 0, i % nt, 0, 0)),
    )
    return pl.pallas_call(
        kern, out_shape=out_shape, grid=(tokens // tm,), in_specs=in_specs, out_specs=out_specs,
        compiler_params=_params(("arbitrary",)), name="diff_prep",
    )(x2d, w["g_mix"], w["w_dq_t"], w["w_dk_t"], w["w_dv_t"], w["g_dq"], w["g_dk"])


def _bias_tile_kernel(bucket_ref, rb_ref, out_ref):
    hd = pl.program_id(0)
    bucket = bucket_ref[0]
    acc = jnp.zeros(bucket.shape, F32)
    for b in range(REL_BUCKETS):
        acc = jnp.where(bucket == b, rb_ref[b, hd], acc)
    out_ref[0, 0] = acc


def _t5_bucket(rel):
    nb = REL_BUCKETS // 2
    ret = jnp.where(rel > 0, nb, 0)
    n = jnp.abs(rel)
    max_exact = nb // 2
    nf = jnp.maximum(n, 1).astype(F32)
    large = max_exact + (jnp.log(nf / max_exact) / math.log(REL_MAX_DIST / max_exact)
                         * (nb - max_exact)).astype(jnp.int32)
    large = jnp.minimum(large, nb - 1)
    return ret + jnp.where(n < max_exact, n, large)


def _bias_tiles(rel_bias, blk):
    d = jnp.arange(blk, dtype=jnp.int32)
    offs = (jnp.arange(3, dtype=jnp.int32) - 1) * blk
    rel = offs[:, None, None] + d[None, :, None] - d[None, None, :]
    bucket = _t5_bucket(rel)
    return pl.pallas_call(
        _bias_tile_kernel,
        out_shape=jax.ShapeDtypeStruct((DIFF_HEADS, 3, blk, blk), F32),
        grid=(DIFF_HEADS, 3),
        in_specs=[
            pl.BlockSpec((1, blk, blk), lambda hd, t: (t, 0, 0)),
            pl.BlockSpec(memory_space=pltpu.SMEM),
        ],
        out_specs=pl.BlockSpec((1, 1, blk, blk), lambda hd, t: (hd, t, 0, 0)),
        compiler_params=_params(("arbitrary", "arbitrary")), name="bias_tiles",
    )(bucket, rel_bias)


def _softmax_update(s, cst, v_chunk, m_ref, l_ref, acc_ref):
    mc = jnp.max(s, axis=0, keepdims=True)
    if cst is not None:
        mc = mc + cst
    m_old = m_ref[...]
    m_new = jnp.maximum(m_old, mc)
    alpha = jnp.exp(m_old - m_new)
    shift = m_new if cst is None else m_new - cst
    p = jnp.exp(s - shift)
    l_ref[...] = alpha * l_ref[...] + jnp.sum(p, axis=0, keepdims=True)
    acc_ref[...] = alpha * acc_ref[...] + jnp.dot(v_chunk, p.astype(BF16), preferred_element_type=F32)
    m_ref[...] = m_new


def _mla_attn_kernel(q_ref, k_ref, v_ref, o_ref, m_ref, l_ref, acc_ref, *, tk):
    nc = v_ref.shape[2]
    m_ref[...] = jnp.full(m_ref.shape, NEG_BIG, F32)
    l_ref[...] = jnp.zeros(l_ref.shape, F32)
    acc_ref[...] = jnp.zeros(acc_ref.shape, F32)
    q_t = q_ref[0, 0]

    def chunk(c, carry):
        kc = k_ref[0, 0, pl.ds(pl.multiple_of(c * tk, tk), tk), :]
        s = jnp.dot(kc, q_t, preferred_element_type=F32)
        _softmax_update(s, None, v_ref[0, 0, c], m_ref, l_ref, acc_ref)
        return carry

    lax.fori_loop(0, nc, chunk, 0)
    o_t = acc_ref[...] / l_ref[...]
    o_ref[...] = o_t.T.astype(BF16)


def _mla_attention(q, k, v, gb, seq, tq, tk):
    nq = seq // tq
    kern = functools.partial(_mla_attn_kernel, tk=tk)
    return pl.pallas_call(
        kern,
        out_shape=jax.ShapeDtypeStruct((gb * seq, HEAD_OUT), BF16),
        grid=(gb, MLA_HEADS, nq),
        in_specs=[
            pl.BlockSpec((1, 1, MLA_QK_PAD, tq), lambda b, hd, i: (b, hd, 0, i)),
            pl.BlockSpec((1, 1, seq, MLA_QK_PAD), lambda b, hd, i: (b, hd, 0, 0)),
            pl.BlockSpec((1, 1, seq // tk, MLA_V, tk), lambda b, hd, i: (b, hd, 0, 0, 0)),
        ],
        out_specs=pl.BlockSpec((tq, MLA_V), lambda b, hd, i: (b * nq + i, hd)),
        scratch_shapes=[
            pltpu.VMEM((1, tq), F32), pltpu.VMEM((1, tq), F32), pltpu.VMEM((MLA_V, tq), F32),
        ],
        compiler_params=_params(("arbitrary", "arbitrary", "arbitrary")), name="mla_attn",
    )(q, k, v)


def _diff_attn_kernel(q_ref, k_ref, v_ref, bias_ref, rb_ref, lam_ref, g_sub_ref, o_ref,
                      m_ref, l_ref, acc_ref, *, blk, lam_init):
    nc = v_ref.shape[2]
    hd = pl.program_id(1)
    i = pl.program_id(2)
    m_ref[...] = jnp.full(m_ref.shape, NEG_BIG, F32)
    l_ref[...] = jnp.zeros(l_ref.shape, F32)
    acc_ref[...] = jnp.zeros(acc_ref.shape, F32)

    def step(c, cst, tile):
        kc = k_ref[0, 0, pl.ds(pl.multiple_of(c * blk, blk), blk), :]
        vc = v_ref[0, 0, c]
        for mp in range(2):
            s = jnp.dot(kc, q_ref[0, 0, mp], preferred_element_type=F32)
            if tile is not None:
                s = s + bias_ref[0, tile]
            _softmax_update(s, cst, vc, m_ref.at[mp], l_ref.at[mp], acc_ref.at[mp])

    far_before = rb_ref[REL_BUCKETS // 2 - 1, hd]
    far_after = rb_ref[REL_BUCKETS - 1, hd]

    def before(c, carry):
        step(c, far_before, None)
        return carry

    def after(c, carry):
        step(c, far_after, None)
        return carry

    lax.fori_loop(0, jnp.maximum(i - 1, 0), before, 0)
    for tile in range(3):
        c = i + (tile - 1)

        @pl.when(jnp.logical_and(c >= 0, c < nc))
        def _():
            step(c, None, tile)

    lax.fori_loop(jnp.minimum(i + 2, nc), nc, after, 0)

    lam_v = lam_ref[...]
    lam = (jnp.exp(jnp.sum(lam_v[0:1] * lam_v[1:2], axis=-1, keepdims=True))
           - jnp.exp(jnp.sum(lam_v[2:3] * lam_v[3:4], axis=-1, keepdims=True)) + lam_init)
    o_t = acc_ref[0] / l_ref[0] - lam * (acc_ref[1] / l_ref[1])
    o_t = o_t * lax.rsqrt(jnp.mean(o_t * o_t, axis=0, keepdims=True) + EPS) * g_sub_ref[...]
    o_ref[...] = (o_t * (1.0 - lam_init)).T.astype(BF16)


def _diff_attention(q, k, v, bias, rel_bias, lam_rows, g_sub, gb, seq, blk, lam_init):
    nq = seq // blk
    kern = functools.partial(_diff_attn_kernel, blk=blk, lam_init=lam_init)
    return pl.pallas_call(
        kern,
        out_shape=jax.ShapeDtypeStruct((gb * seq, HEAD_OUT), BF16),
        grid=(gb, DIFF_HEADS, nq),
        in_specs=[
            pl.BlockSpec((1, 1, 2, DIFF_V, blk), lambda b, hd, i: (b, hd, 0, 0, i)),
            pl.BlockSpec((1, 1, seq, DIFF_V), lambda b, hd, i: (b, hd, 0, 0)),
            pl.BlockSpec((1, 1, seq // blk, DIFF_V, blk), lambda b, hd, i: (b, hd, 0, 0, 0)),
            pl.BlockSpec((1, 3, blk, blk), lambda b, hd, i: (hd, 0, 0, 0)),
            pl.BlockSpec(memory_space=pltpu.SMEM),
            _const_spec((4, LANES)),
            _const_spec((DIFF_V, 1)),
        ],
        out_specs=pl.BlockSpec((blk, DIFF_V), lambda b, hd, i: (b * nq + i, hd)),
        scratch_shapes=[
            pltpu.VMEM((2, 1, blk), F32), pltpu.VMEM((2, 1, blk), F32),
            pltpu.VMEM((2, DIFF_V, blk), F32),
        ],
        compiler_params=_params(("arbitrary", "arbitrary", "arbitrary")), name="diff_attn",
    )(q, k, v, bias, rel_bias, lam_rows, g_sub)


def _merge_kernel(x_ref, g_mix_ref, oa_ref, ob_ref, wga_ref, wgb_ref, wua_ref, wub_ref,
                  out_ref, h_ref):
    @pl.when(pl.program_id(1) == 0)
    def _():
        h_ref[...] = _rms_rows(x_ref[...], g_mix_ref[...]).astype(BF16)

    h = h_ref[...]
    ga = jnp.dot(h, wga_ref[...], preferred_element_type=F32)
    gb_ = jnp.dot(h, wgb_ref[...], preferred_element_type=F32)
    ua = jnp.dot(oa_ref[...], wua_ref[...], preferred_element_type=F32)
    ub = jnp.dot(ob_ref[...], wub_ref[...], preferred_element_type=F32)
    out_ref[...] = (jax.nn.sigmoid(ga) * ua + jax.nn.sigmoid(gb_) * ub).astype(BF16)


def _merge(x2d, o_a, o_b, w, tm, tn):
    tokens = x2d.shape[0]
    return pl.pallas_call(
        _merge_kernel,
        out_shape=jax.ShapeDtypeStruct((tokens, D_MODEL), BF16),
        grid=(tokens // tm, D_MODEL // tn),
        in_specs=[
            pl.BlockSpec((tm, D_MODEL), lambda i, j: (i, 0)),
            _const_spec((1, D_MODEL)),
            pl.BlockSpec((tm, HEAD_OUT), lambda i, j: (i, 0)),
            pl.BlockSpec((tm, HEAD_OUT), lambda i, j: (i, 0)),
            pl.BlockSpec((D_MODEL, tn), lambda i, j: (0, j)),
            pl.BlockSpec((D_MODEL, tn), lambda i, j: (0, j)),
            pl.BlockSpec((HEAD_OUT, tn), lambda i, j: (0, j)),
            pl.BlockSpec((HEAD_OUT, tn), lambda i, j: (0, j)),
        ],
        out_specs=pl.BlockSpec((tm, tn), lambda i, j: (i, j)),
        scratch_shapes=[pltpu.VMEM((tm, D_MODEL), BF16)],
        compiler_params=_params(("arbitrary", "arbitrary")), name="gated_merge",
    )(x2d, w["g_mix"], o_a, o_b, w["w_ga"], w["w_gb"], w["w_ua"], w["w_ub"])


def _out_proj_kernel(x_ref, m_ref, wo_ref, out_ref):
    out_ref[...] = x_ref[...] + jnp.dot(m_ref[...], wo_ref[...], preferred_element_type=F32)


def _out_proj(x2d, merged, w, tm):
    tokens = x2d.shape[0]
    return pl.pallas_call(
        _out_proj_kernel,
        out_shape=jax.ShapeDtypeStruct((tokens, D_MODEL), F32),
        grid=(tokens // tm,),
        in_specs=[
            pl.BlockSpec((tm, D_MODEL), lambda i: (i, 0)),
            pl.BlockSpec((tm, D_MODEL), lambda i: (i, 0)),
            _const_spec((D_MODEL, D_MODEL)),
        ],
        out_specs=pl.BlockSpec((tm, D_MODEL), lambda i: (i, 0)),
        compiler_params=_params(("arbitrary",)), name="out_proj",
    )(x2d, merged, w["w_o"])


def _ffn_kernel(x_ref, g_ref, wg_ref, wu_ref, wd_ref, out_ref, h_ref):
    @pl.when(pl.program_id(1) == 0)
    def _():
        x = x_ref[...]
        h_ref[...] = _rms_rows(x, g_ref[...]).astype(BF16)
        out_ref[...] = x

    h = h_ref[...]
    g = jnp.dot(h, wg_ref[...], preferred_element_type=F32)
    u = jnp.dot(h, wu_ref[...], preferred_element_type=F32)
    a = (g * jax.nn.sigmoid(g) * u).astype(BF16)
    out_ref[...] += jnp.dot(a, wd_ref[...], preferred_element_type=F32)


def _ffn(x2d, w, tm, tf):
    tokens = x2d.shape[0]
    return pl.pallas_call(
        _ffn_kernel,
        out_shape=jax.ShapeDtypeStruct((tokens, D_MODEL), F32),
        grid=(tokens // tm, D_FF // tf),
        in_specs=[
            pl.BlockSpec((tm, D_MODEL), lambda i, j: (i, 0)),
            _const_spec((1, D_MODEL)),
            pl.BlockSpec((D_MODEL, tf), lambda i, j: (0, j)),
            pl.BlockSpec((D_MODEL, tf), lambda i, j: (0, j)),
            pl.BlockSpec((tf, D_MODEL), lambda i, j: (j, 0)),
        ],
        out_specs=pl.BlockSpec((tm, D_MODEL), lambda i, j: (i, 0)),
        scratch_shapes=[pltpu.VMEM((tm, D_MODEL), BF16)],
        compiler_params=_params(("arbitrary", "arbitrary")), name="swiglu_ffn",
    )(x2d, w["g_ffn"], w["w_gate"], w["w_up"], w["w_down"])


def _prepare_weights(mix_norm, w_in, q_a_norm, wq_b, kv_a_norm, wkv_b, mla_q_norm, mla_k_norm,
                     diff_q_norm, diff_k_norm, diff_subln, w_up_mla, w_up_diff, w_o, ffn_norm,
                     w_gate, w_up, w_down, layer):
    win = w_in[layer]
    o_cq, o_ckv, o_kpe = 0, Q_LORA, Q_LORA + KV_LORA
    o_dq = o_kpe + MLA_ROPE
    o_dk = o_dq + HEAD_OUT
    o_dv = o_dk + HEAD_OUT
    o_ga = o_dv + HEAD_OUT
    o_gb = o_ga + D_MODEL
    w_kpe = win[:, o_kpe:o_dq]
    w_kpe_rot = jnp.concatenate([-w_kpe[:, ROPE_HALF:], w_kpe[:, :ROPE_HALF]], axis=1)
    zpad = jnp.zeros((D_MODEL, LANES - MLA_ROPE), F32)
    w_a = jnp.concatenate([win[:, o_cq:o_kpe], w_kpe, zpad, w_kpe_rot, zpad], axis=1)
    wkv = wkv_b[layer].reshape(KV_LORA, MLA_HEADS, MLA_NOPE + MLA_V)
    gk = mla_k_norm[layer]
    gk_rope = gk[MLA_NOPE:]
    lane_pad = jnp.zeros((LANES - MLA_ROPE,), F32)
    return {
        "g_mix": mix_norm[layer][None, :],
        "w_a": w_a.astype(BF16),
        "g_qa": q_a_norm[layer][None, :],
        "g_kva": kv_a_norm[layer][None, :],
        "wq_t": wq_b[layer].T.astype(BF16),
        "g_q": mla_q_norm[layer][:, None],
        "wk_nope": wkv[:, :, :MLA_NOPE].reshape(KV_LORA, MLA_HEADS * MLA_NOPE).astype(BF16),
        "wv_t": wkv[:, :, MLA_NOPE:].reshape(KV_LORA, MLA_HEADS * MLA_V).T.astype(BF16),
        "g_kn": gk[None, :MLA_NOPE],
        "g_kr": jnp.concatenate([gk_rope, lane_pad])[None, :],
        "g_krot": jnp.concatenate([gk_rope[ROPE_HALF:], gk_rope[:ROPE_HALF], lane_pad])[None, :],
        "w_dq_t": win[:, o_dq:o_dk].T.astype(BF16),
        "w_dk_t": win[:, o_dk:o_dv].T.astype(BF16),
        "w_dv_t": win[:, o_dv:o_ga].T.astype(BF16),
        "g_dq": diff_q_norm[layer][:, None],
        "g_dk": diff_k_norm[layer][:, None],
        "g_sub": diff_subln[layer][:, None],
        "w_ga": win[:, o_ga:o_gb].astype(BF16),
        "w_gb": win[:, o_gb:].astype(BF16),
        "w_ua": w_up_mla[layer].astype(BF16),
        "w_ub": w_up_diff[layer].astype(BF16),
        "w_o": w_o[layer].astype(BF16),
        "g_ffn": ffn_norm[layer][None, :],
        "w_gate": w_gate[layer].astype(BF16),
        "w_up": w_up[layer].astype(BF16),
        "w_down": w_down[layer].astype(BF16),
    }


def _encoder_layer(x, w, tables, bias, rel_bias, lam_rows, lam_init, *, tm, blk, tn, tf):
    gb, seq, _ = x.shape
    x2d = x.reshape(gb * seq, D_MODEL)
    q_a, k_a, v_a = _mla_prep(x2d, gb, seq, tm, blk, w, tables)
    q_d, k_d, v_d = _diff_prep(x2d, gb, seq, tm, blk, w)
    o_a = _mla_attention(q_a, k_a, v_a, gb, seq, blk, blk)
    o_b = _diff_attention(q_d, k_d, v_d, bias, rel_bias, lam_rows, w["g_sub"], gb, seq, blk, lam_init)
    merged = _merge(x2d, o_a, o_b, w, tm, tn)
    x1 = _out_proj(x2d, merged, w, tm)
    y = _ffn(x1, w, tm, tf)
    return y.reshape(gb, seq, D_MODEL)


def kernel(x_prompt, x_sample, mix_norm, w_in, q_a_norm, wq_b, kv_a_norm, wkv_b, mla_q_norm, mla_k_norm, diff_q_norm, diff_k_norm, lambda_q1, lambda_k1, lambda_q2, lambda_k2, diff_subln, w_up_mla, w_up_diff, w_o, ffn_norm, w_gate, w_up, w_down, rel_bias):
    tm, blk, tn, tf = 512, 512, 512, 512
    depth = w_in.shape[0]
    max_seq = max(x_prompt.shape[1], x_sample.shape[1])
    tables = _rope_tables(max_seq)
    bias = _bias_tiles(rel_bias, blk)
    y_prompt, y_sample = x_prompt, x_sample
    for layer in range(depth):
        w = _prepare_weights(mix_norm, w_in, q_a_norm, wq_b, kv_a_norm, wkv_b, mla_q_norm,
                             mla_k_norm, diff_q_norm, diff_k_norm, diff_subln, w_up_mla,
                             w_up_diff, w_o, ffn_norm, w_gate, w_up, w_down, layer)
        lam_init = 0.8 - 0.6 * math.exp(-0.3 * layer)
        lam_pad = jnp.zeros((LANES - DIFF_QK,), F32)
        lam_rows = jnp.stack([jnp.concatenate([v[layer], lam_pad])
                              for v in (lambda_q1, lambda_k1, lambda_q2, lambda_k2)])
        run = functools.partial(_encoder_layer, w=w, tables=tables, bias=bias, rel_bias=rel_bias,
                                lam_rows=lam_rows, lam_init=lam_init, tm=tm, blk=blk, tn=tn, tf=tf)
        y_prompt = run(y_prompt)
        y_sample = run(y_sample)
    return (y_prompt, y_sample)
```

```python
import functools
import math

import jax
import jax.numpy as jnp
from jax import lax
from jax.experimental import pallas as pl
from jax.experimental.pallas import tpu as pltpu

D_MODEL = 2048
MLA_HEADS = 8
MLA_NOPE = 128
MLA_ROPE = 64
MLA_QK = MLA_NOPE + MLA_ROPE
MLA_V = 128
Q_LORA = 512
KV_LORA = 256
ROPE_THETA = 10000.0
ROPE_HALF = MLA_ROPE // 2
DIFF_HEADS = 8
DIFF_QK = 64
DIFF_V = 2 * DIFF_QK
REL_BUCKETS = 32
REL_MAX_DIST = 128
D_FF = 5632
EPS = 1e-6
HEAD_OUT = MLA_HEADS * MLA_V

LANES = 128
MLA_QK_PAD = 2 * LANES
VMEM_LIMIT = 56 * 1024 * 1024

F32 = jnp.float32
BF16 = jnp.bfloat16
NEG_BIG = -1e30
LOG2E = math.log2(math.e)
NEAR_TILES = 5

NT_DIMS = (((1,), (1,)), ((), ()))


def _params(semantics):
    return pltpu.CompilerParams(dimension_semantics=semantics, vmem_limit_bytes=VMEM_LIMIT)


def _const_spec(shape):
    zeros = (0,) * len(shape)
    return pl.BlockSpec(shape, lambda *_: zeros)


def _rms_rows(x, gain):
    return x * lax.rsqrt(jnp.mean(x * x, axis=-1, keepdims=True) + EPS) * gain


def _rope_table_kernel(inv_row_ref, inv_col_ref, cos_t_ref, sin_t_ref, cos_f_ref, sin_f_ref):
    ts = cos_t_ref.shape[0]
    base = pl.program_id(0) * ts
    pos_rows = (base + lax.broadcasted_iota(jnp.int32, (ts, LANES), 0)).astype(F32)
    ang_t = pos_rows * inv_row_ref[...]
    cos_t_ref[...] = jnp.cos(ang_t)
    sin_t_ref[...] = jnp.sin(ang_t)
    pos_cols = (base + lax.broadcasted_iota(jnp.int32, (ROPE_HALF, ts), 1)).astype(F32)
    ang_f = pos_cols * inv_col_ref[...]
    cos_f_ref[...] = jnp.cos(ang_f)
    sin_f_ref[...] = jnp.sin(ang_f)


def _rope_tables(seq):
    inv = ROPE_THETA ** (-jnp.arange(ROPE_HALF, dtype=F32) / ROPE_HALF)
    inv_row = jnp.tile(inv, LANES // ROPE_HALF)[None, :]
    inv_col = inv[:, None]
    ts = min(seq, 2048)
    tok = jax.ShapeDtypeStruct((seq, LANES), F32)
    feat = jax.ShapeDtypeStruct((ROPE_HALF, seq), F32)
    return pl.pallas_call(
        _rope_table_kernel,
        out_shape=(tok, tok, feat, feat),
        grid=(seq // ts,),
        in_specs=[_const_spec((1, LANES)), _const_spec((ROPE_HALF, 1))],
        out_specs=(
            pl.BlockSpec((ts, LANES), lambda i: (i, 0)),
            pl.BlockSpec((ts, LANES), lambda i: (i, 0)),
            pl.BlockSpec((ROPE_HALF, ts), lambda i: (0, i)),
            pl.BlockSpec((ROPE_HALF, ts), lambda i: (0, i)),
        ),
        compiler_params=_params(("arbitrary",)), name="rope_tables",
    )(inv_row, inv_col)


def _mla_prep_kernel(x_ref, g_mix_ref, w_a_ref, g_qa_ref, g_kva_ref, wq_t_ref, g_q_ref,
                     wk_ref, wv_t_ref, g_kn_ref, g_kr_ref, g_krot_ref,
                     cos_t_ref, sin_t_ref, cos_f_ref, sin_f_ref,
                     q_ref, k_ref, v_ref, *, tk):
    tm = x_ref.shape[0]
    h = _rms_rows(x_ref[...], g_mix_ref[...]).astype(BF16)
    c = jnp.dot(h, w_a_ref[...], preferred_element_type=F32)
    cq = _rms_rows(c[:, :Q_LORA], g_qa_ref[...]).astype(BF16)
    ckv = _rms_rows(c[:, Q_LORA:Q_LORA + KV_LORA], g_kva_ref[...]).astype(BF16)
    k_pe = c[:, 6 * LANES:7 * LANES]
    k_pe_rot = c[:, 7 * LANES:8 * LANES]

    q_t = lax.dot_general(wq_t_ref[...], cq, NT_DIMS, preferred_element_type=F32)
    cos_f = cos_f_ref[...]
    sin_f = sin_f_ref[...]
    scale = MLA_QK ** -0.5 * LOG2E
    g_q = g_q_ref[...]
    for hd in range(MLA_HEADS):
        xh = q_t[hd * MLA_QK:(hd + 1) * MLA_QK]
        rinv = lax.rsqrt(jnp.mean(xh * xh, axis=0, keepdims=True) + EPS)
        xn = xh * rinv * g_q
        x1 = xn[MLA_NOPE:MLA_NOPE + ROPE_HALF]
        x2 = xn[MLA_NOPE + ROPE_HALF:]
        q_ref[0, hd, 0:MLA_NOPE, :] = (xn[:MLA_NOPE] * scale).astype(BF16)
        q_ref[0, hd, MLA_NOPE:MLA_NOPE + ROPE_HALF, :] = ((x1 * cos_f - x2 * sin_f) * scale).astype(BF16)
        q_ref[0, hd, MLA_NOPE + ROPE_HALF:MLA_QK, :] = ((x2 * cos_f + x1 * sin_f) * scale).astype(BF16)
        q_ref[0, hd, MLA_QK:, :] = jnp.zeros((MLA_QK_PAD - MLA_QK, tm), BF16)

    k_nope = jnp.dot(ckv, wk_ref[...], preferred_element_type=F32)
    rope_base = (k_pe * g_kr_ref[...]) * cos_t_ref[...] + (k_pe_rot * g_krot_ref[...]) * sin_t_ref[...]
    ss_pe = jnp.sum(k_pe * k_pe, axis=-1, keepdims=True)
    for hd in range(MLA_HEADS):
        kh = k_nope[:, hd * MLA_NOPE:(hd + 1) * MLA_NOPE]
        ss = jnp.sum(kh * kh, axis=-1, keepdims=True) + ss_pe
        rinv = lax.rsqrt(ss * (1.0 / MLA_QK) + EPS)
        k_ref[0, hd, :, 0:LANES] = (kh * rinv * g_kn_ref[...]).astype(BF16)
        k_ref[0, hd, :, LANES:] = (rope_base * rinv).astype(BF16)

    v_t = lax.dot_general(wv_t_ref[...], ckv, NT_DIMS, preferred_element_type=F32)
    for hd in range(MLA_HEADS):
        for cc in range(tm // tk):
            v_ref[0, hd, cc] = v_t[hd * MLA_V:(hd + 1) * MLA_V, cc * tk:(cc + 1) * tk].astype(BF16)


def _mla_prep(x2d, gb, seq, tm, tk, w, tables):
    tokens = x2d.shape[0]
    nt = seq // tm
    cos_t, sin_t, cos_f, sin_f = tables
    kern = functools.partial(_mla_prep_kernel, tk=tk)
    out_shape = (
        jax.ShapeDtypeStruct((gb, MLA_HEADS, MLA_QK_PAD, seq), BF16),
        jax.ShapeDtypeStruct((gb, MLA_HEADS, seq, MLA_QK_PAD), BF16),
        jax.ShapeDtypeStruct((gb, MLA_HEADS, seq // tk, MLA_V, tk), BF16),
    )
    in_specs = [
        pl.BlockSpec((tm, D_MODEL), lambda i: (i, 0)),
        _const_spec((1, D_MODEL)),
        _const_spec((D_MODEL, 8 * LANES)),
        _const_spec((1, Q_LORA)),
        _const_spec((1, KV_LORA)),
        _const_spec((MLA_HEADS * MLA_QK, Q_LORA)),
        _const_spec((MLA_QK, 1)),
        _const_spec((KV_LORA, MLA_HEADS * MLA_NOPE)),
        _const_spec((MLA_HEADS * MLA_V, KV_LORA)),
        _const_spec((1, LANES)),
        _const_spec((1, LANES)),
        _const_spec((1, LANES)),
        pl.BlockSpec((tm, LANES), lambda i: (i % nt, 0)),
        pl.BlockSpec((tm, LANES), lambda i: (i % nt, 0)),
        pl.BlockSpec((ROPE_HALF, tm), lambda i: (0, i % nt)),
        pl.BlockSpec((ROPE_HALF, tm), lambda i: (0, i % nt)),
    ]
    out_specs = (
        pl.BlockSpec((1, MLA_HEADS, MLA_QK_PAD, tm), lambda i: (i // nt, 0, 0, i % nt)),
        pl.BlockSpec((1, MLA_HEADS, tm, MLA_QK_PAD), lambda i: (i // nt, 0, i % nt, 0)),
        pl.BlockSpec((1, MLA_HEADS, tm // tk, MLA_V, tk), lambda i: (i // nt, 0, i % nt, 0, 0)),
    )
    return pl.pallas_call(
        kern, out_shape=out_shape, grid=(tokens // tm,), in_specs=in_specs, out_specs=out_specs,
        compiler_params=_params(("arbitrary",)), name="mla_prep",
    )(x2d, w["g_mix"], w["w_a"], w["g_qa"], w["g_kva"], w["wq_t"], w["g_q"], w["wk_nope"],
      w["wv_t"], w["g_kn"], w["g_kr"], w["g_krot"], cos_t, sin_t, cos_f, sin_f)


def _group_norm_cols(x_t, gain_col):
    rows, tm = x_t.shape
    x3 = x_t.reshape(rows // DIFF_QK, DIFF_QK, tm)
    rinv = lax.rsqrt(jnp.mean(x3 * x3, axis=1, keepdims=True) + EPS)
    return (x3 * rinv * gain_col[None]).reshape(rows, tm)


def _diff_prep_kernel(x_ref, g_mix_ref, wq_t_ref, wk_t_ref, wv_t_ref, g_q_ref, g_k_ref,
                      q_ref, k_ref, v_ref, *, tk):
    tm = x_ref.shape[0]
    h = _rms_rows(x_ref[...], g_mix_ref[...]).astype(BF16)
    scale = DIFF_QK ** -0.5 * LOG2E
    q_t = lax.dot_general(wq_t_ref[...], h, NT_DIMS, preferred_element_type=F32)
    qn = (_group_norm_cols(q_t, g_q_ref[...]) * scale).astype(BF16)
    zeros = jnp.zeros((DIFF_QK, tm), BF16)
    for hd in range(DIFF_HEADS):
        r0 = hd * DIFF_V
        q_ref[0, hd, 0, 0:DIFF_QK, :] = qn[r0:r0 + DIFF_QK]
        q_ref[0, hd, 0, DIFF_QK:, :] = zeros
        q_ref[0, hd, 1, 0:DIFF_QK, :] = zeros
        q_ref[0, hd, 1, DIFF_QK:, :] = qn[r0 + DIFF_QK:r0 + DIFF_V]

    k_t = lax.dot_general(wk_t_ref[...], h, NT_DIMS, preferred_element_type=F32)
    kn = _group_norm_cols(k_t, g_k_ref[...]).T
    for hd in range(DIFF_HEADS):
        k_ref[0, hd] = kn[:, hd * DIFF_V:(hd + 1) * DIFF_V].astype(BF16)

    v_t = lax.dot_general(wv_t_ref[...], h, NT_DIMS, preferred_element_type=F32)
    for hd in range(DIFF_HEADS):
        for cc in range(tm // tk):
            v_ref[0, hd, cc] = v_t[hd * DIFF_V:(hd + 1) * DIFF_V, cc * tk:(cc + 1) * tk].astype(BF16)


def _diff_prep(x2d, gb, seq, tm, tk, w):
    tokens = x2d.shape[0]
    nt = seq // tm
    kern = functools.partial(_diff_prep_kernel, tk=tk)
    out_shape = (
        jax.ShapeDtypeStruct((gb, DIFF_HEADS, 2, DIFF_V, seq), BF16),
        jax.ShapeDtypeStruct((gb, DIFF_HEADS, seq, DIFF_V), BF16),
        jax.ShapeDtypeStruct((gb, DIFF_HEADS, seq // tk, DIFF_V, tk), BF16),
    )
    in_specs = [
        pl.BlockSpec((tm, D_MODEL), lambda i: (i, 0)),
        _const_spec((1, D_MODEL)),
        _const_spec((HEAD_OUT, D_MODEL)),
        _const_spec((HEAD_OUT, D_MODEL)),
        _const_spec((HEAD_OUT, D_MODEL)),
        _const_spec((DIFF_QK, 1)),
        _const_spec((DIFF_QK, 1)),
    ]
    out_specs = (
        pl.BlockSpec((1, DIFF_HEADS, 2, DIFF_V, tm), lambda i: (i // nt, 0, 0, 0, i % nt)),
        pl.BlockSpec((1, DIFF_HEADS, tm, DIFF_V), lambda i: (i // nt, 0, i % nt, 0)),
        pl.BlockSpec((1, DIFF_HEADS, tm // tk, DIFF_V, tk), lambda i: (i // nt, 0, i % nt, 0, 0)),
    )
    return pl.pallas_call(
        kern, out_shape=out_shape, grid=(tokens // tm,), in_specs=in_specs, out_specs=out_specs,
        compiler_params=_params(("arbitrary",)), name="diff_prep",
    )(x2d, w["g_mix"], w["w_dq_t"], w["w_dk_t"], w["w_dv_t"], w["g_dq"], w["g_dk"])


def _bias_tile_kernel(bucket_ref, rb_ref, out_ref):
    hd = pl.program_id(0)
    bucket = bucket_ref[0]
    acc = jnp.zeros(bucket.shape, F32)
    for b in range(REL_BUCKETS):
        acc = jnp.where(bucket == b, rb_ref[b, hd], acc)
    out_ref[0, 0] = acc * LOG2E


def _t5_bucket(rel):
    nb = REL_BUCKETS // 2
    ret = jnp.where(rel > 0, nb, 0)
    n = jnp.abs(rel)
    max_exact = nb // 2
    nf = jnp.maximum(n, 1).astype(F32)
    large = max_exact + (jnp.log(nf / max_exact) / math.log(REL_MAX_DIST / max_exact)
                         * (nb - max_exact)).astype(jnp.int32)
    large = jnp.minimum(large, nb - 1)
    return ret + jnp.where(n < max_exact, n, large)


def _bias_tiles(rel_bias, blk):
    d = jnp.arange(blk, dtype=jnp.int32)
    offs = (jnp.arange(NEAR_TILES, dtype=jnp.int32) - NEAR_TILES // 2) * blk
    rel = offs[:, None, None] + d[None, :, None] - d[None, None, :]
    bucket = _t5_bucket(rel)
    return pl.pallas_call(
        _bias_tile_kernel,
        out_shape=jax.ShapeDtypeStruct((DIFF_HEADS, NEAR_TILES, blk, blk), F32),
        grid=(DIFF_HEADS, NEAR_TILES),
        in_specs=[
            pl.BlockSpec((1, blk, blk), lambda hd, t: (t, 0, 0)),
            pl.BlockSpec(memory_space=pltpu.SMEM),
        ],
        out_specs=pl.BlockSpec((1, 1, blk, blk), lambda hd, t: (hd, t, 0, 0)),
        compiler_params=_params(("arbitrary", "arbitrary")), name="bias_tiles",
    )(bucket, rel_bias)


class _FlashState:
    def __init__(self, q_ref, k_ref, v_ref, s_ref, p_ref, alpha_ref, m_ref, l_ref, acc_ref, blk):
        self.q_ref, self.k_ref, self.v_ref = q_ref, k_ref, v_ref
        self.s_ref, self.p_ref, self.alpha_ref = s_ref, p_ref, alpha_ref
        self.m_ref, self.l_ref, self.acc_ref = m_ref, l_ref, acc_ref
        self.blk = blk
        self.n_maps = q_ref.shape[2]
        self.nc = v_ref.shape[2]

    def init(self):
        self.m_ref[...] = jnp.full(self.m_ref.shape, NEG_BIG, F32)
        self.l_ref[...] = jnp.zeros(self.l_ref.shape, F32)
        self.acc_ref[...] = jnp.zeros(self.acc_ref.shape, F32)
        self.p_ref[...] = jnp.zeros(self.p_ref.shape, BF16)
        self.alpha_ref[...] = jnp.ones(self.alpha_ref.shape, F32)
        self.qk(0, 0)

    def qk(self, c, half):
        kc = self.k_ref[0, 0, pl.ds(pl.multiple_of(c * self.blk, self.blk), self.blk), :]
        for mp in range(self.n_maps):
            self.s_ref[half, mp] = jnp.dot(kc, self.q_ref[0, 0, mp], preferred_element_type=F32)

    def pv(self, c):
        vc = self.v_ref[0, 0, c]
        for mp in range(self.n_maps):
            self.acc_ref[mp] = (self.alpha_ref[mp] * self.acc_ref[mp]
                                + jnp.dot(vc, self.p_ref[mp], preferred_element_type=F32))

    def softmax(self, half, cst, bias_tile):
        for mp in range(self.n_maps):
            s = self.s_ref[half, mp]
            if bias_tile is not None:
                s = s + bias_tile
            mc = jnp.max(s, axis=0, keepdims=True)
            if cst is not None:
                mc = mc + cst
            m_old = self.m_ref[mp]
            m_new = jnp.maximum(m_old, mc)
            alpha = jnp.exp2(m_old - m_new)
            shift = m_new if cst is None else m_new - cst
            p = jnp.exp2(s - shift)
            self.l_ref[mp] = alpha * self.l_ref[mp] + jnp.sum(p, axis=0, keepdims=True)
            self.p_ref[mp] = p.astype(BF16)
            self.alpha_ref[mp] = alpha
            self.m_ref[mp] = m_new

    def step(self, c, half, cst=None, bias_tile=None):
        self.pv(jnp.maximum(c - 1, 0))
        self.qk(jnp.minimum(c + 1, self.nc - 1), 1 - half)
        self.softmax(half, cst, bias_tile)

    def finish(self):
        self.pv(self.nc - 1)


def _flash_scratch(n_maps, head_dim, blk):
    return [
        pltpu.VMEM((2, n_maps, blk, blk), F32),
        pltpu.VMEM((n_maps, blk, blk), BF16),
        pltpu.VMEM((n_maps, 1, blk), F32),
        pltpu.VMEM((n_maps, 1, blk), F32),
        pltpu.VMEM((n_maps, 1, blk), F32),
        pltpu.VMEM((n_maps, head_dim, blk), F32),
    ]


def _mla_attn_kernel(q_ref, k_ref, v_ref, o_ref, s_ref, p_ref, alpha_ref, m_ref, l_ref, acc_ref,
                     *, blk):
    st = _FlashState(q_ref, k_ref, v_ref, s_ref, p_ref, alpha_ref, m_ref, l_ref, acc_ref, blk)
    st.init()

    def pair(cc, carry):
        st.step(2 * cc, 0)
        st.step(2 * cc + 1, 1)
        return carry

    lax.fori_loop(0, st.nc // 2, pair, 0)
    st.finish()
    o_t = acc_ref[0] / l_ref[0]
    o_ref[...] = o_t.T.astype(BF16)


def _mla_attention(q, k, v, gb, seq, blk):
    nq = seq // blk
    assert nq % 2 == 0
    kern = functools.partial(_mla_attn_kernel, blk=blk)
    q = q.reshape(gb, MLA_HEADS, 1, MLA_QK_PAD, seq)
    return pl.pallas_call(
        kern,
        out_shape=jax.ShapeDtypeStruct((gb * seq, HEAD_OUT), BF16),
        grid=(gb, MLA_HEADS, nq),
        in_specs=[
            pl.BlockSpec((1, 1, 1, MLA_QK_PAD, blk), lambda b, hd, i: (b, hd, 0, 0, i)),
            pl.BlockSpec((1, 1, seq, MLA_QK_PAD), lambda b, hd, i: (b, hd, 0, 0)),
            pl.BlockSpec((1, 1, nq, MLA_V, blk), lambda b, hd, i: (b, hd, 0, 0, 0)),
        ],
        out_specs=pl.BlockSpec((blk, MLA_V), lambda b, hd, i: (b * nq + i, hd)),
        scratch_shapes=_flash_scratch(1, MLA_V, blk),
        compiler_params=_params(("arbitrary", "arbitrary", "arbitrary")), name="mla_attn",
    )(q, k, v)


def _diff_attn_kernel(q_ref, k_ref, v_ref, bias_ref, rb_ref, lam_ref, g_sub_ref, o_ref,
                      s_ref, p_ref, alpha_ref, m_ref, l_ref, acc_ref, *, blk, lam_init):
    hd = pl.program_id(1)
    i = pl.program_id(2)
    st = _FlashState(q_ref, k_ref, v_ref, s_ref, p_ref, alpha_ref, m_ref, l_ref, acc_ref, blk)
    st.init()
    n_pairs = st.nc // 2

    far_before = rb_ref[REL_BUCKETS // 2 - 1, hd] * LOG2E
    far_after = rb_ref[REL_BUCKETS - 1, hd] * LOG2E
    first_near = (i + 1) // 2 - 1

    def far_pair(cst):
        def body(cc, carry):
            st.step(2 * cc, 0, cst=cst)
            st.step(2 * cc + 1, 1, cst=cst)
            return carry
        return body

    def near_pair(cc, carry):
        for half in range(2):
            c = 2 * cc + half
            st.step(c, half, bias_tile=bias_ref[0, c - i + NEAR_TILES // 2])
        return carry

    near_lo = jnp.maximum(first_near, 0)
    near_hi = jnp.minimum(first_near + 2, n_pairs)
    lax.fori_loop(0, near_lo, far_pair(far_before), 0)
    lax.fori_loop(near_lo, near_hi, near_pair, 0)
    lax.fori_loop(near_hi, n_pairs, far_pair(far_after), 0)
    st.finish()

    lam_v = lam_ref[...]
    lam = (jnp.exp(jnp.sum(lam_v[0:1] * lam_v[1:2], axis=-1, keepdims=True))
           - jnp.exp(jnp.sum(lam_v[2:3] * lam_v[3:4], axis=-1, keepdims=True)) + lam_init)
    o_t = acc_ref[0] / l_ref[0] - lam * (acc_ref[1] / l_ref[1])
    o_t = o_t * lax.rsqrt(jnp.mean(o_t * o_t, axis=0, keepdims=True) + EPS) * g_sub_ref[...]
    o_ref[...] = (o_t * (1.0 - lam_init)).T.astype(BF16)


def _diff_attention(q, k, v, bias, rel_bias, lam_rows, g_sub, gb, seq, blk, lam_init):
    nq = seq // blk
    assert nq % 2 == 0 and blk >= REL_MAX_DIST
    kern = functools.partial(_diff_attn_kernel, blk=blk, lam_init=lam_init)
    return pl.pallas_call(
        kern,
        out_shape=jax.ShapeDtypeStruct((gb * seq, HEAD_OUT), BF16),
        grid=(gb, DIFF_HEADS, nq),
        in_specs=[
            pl.BlockSpec((1, 1, 2, DIFF_V, blk), lambda b, hd, i: (b, hd, 0, 0, i)),
            pl.BlockSpec((1, 1, seq, DIFF_V), lambda b, hd, i: (b, hd, 0, 0)),
            pl.BlockSpec((1, 1, seq // blk, DIFF_V, blk), lambda b, hd, i: (b, hd, 0, 0, 0)),
            pl.BlockSpec((1, NEAR_TILES, blk, blk), lambda b, hd, i: (hd, 0, 0, 0)),
            pl.BlockSpec(memory_space=pltpu.SMEM),
            _const_spec((4, LANES)),
            _const_spec((DIFF_V, 1)),
        ],
        out_specs=pl.BlockSpec((blk, DIFF_V), lambda b, hd, i: (b * nq + i, hd)),
        scratch_shapes=_flash_scratch(2, DIFF_V, blk),
        compiler_params=_params(("arbitrary", "arbitrary", "arbitrary")), name="diff_attn",
    )(q, k, v, bias, rel_bias, lam_rows, g_sub)


def _merge_kernel(x_ref, g_mix_ref, oa_ref, ob_ref, wga_ref, wgb_ref, wua_ref, wub_ref,
                  out_ref, h_ref):
    @pl.when(pl.program_id(1) == 0)
    def _():
        h_ref[...] = _rms_rows(x_ref[...], g_mix_ref[...]).astype(BF16)

    h = h_ref[...]
    ga = jnp.dot(h, wga_ref[...], preferred_element_type=F32)
    gb_ = jnp.dot(h, wgb_ref[...], preferred_element_type=F32)
    ua = jnp.dot(oa_ref[...], wua_ref[...], preferred_element_type=F32)
    ub = jnp.dot(ob_ref[...], wub_ref[...], preferred_element_type=F32)
    out_ref[...] = (jax.nn.sigmoid(ga) * ua + jax.nn.sigmoid(gb_) * ub).astype(BF16)


def _merge(x2d, o_a, o_b, w, tm, tn):
    tokens = x2d.shape[0]
    return pl.pallas_call(
        _merge_kernel,
        out_shape=jax.ShapeDtypeStruct((tokens, D_MODEL), BF16),
        grid=(tokens // tm, D_MODEL // tn),
        in_specs=[
            pl.BlockSpec((tm, D_MODEL), lambda i, j: (i, 0)),
            _const_spec((1, D_MODEL)),
            pl.BlockSpec((tm, HEAD_OUT), lambda i, j: (i, 0)),
            pl.BlockSpec((tm, HEAD_OUT), lambda i, j: (i, 0)),
            pl.BlockSpec((D_MODEL, tn), lambda i, j: (0, j)),
            pl.BlockSpec((D_MODEL, tn), lambda i, j: (0, j)),
            pl.BlockSpec((HEAD_OUT, tn), lambda i, j: (0, j)),
            pl.BlockSpec((HEAD_OUT, tn), lambda i, j: (0, j)),
        ],
        out_specs=pl.BlockSpec((tm, tn), lambda i, j: (i, j)),
        scratch_shapes=[pltpu.VMEM((tm, D_MODEL), BF16)],
        compiler_params=_params(("arbitrary", "arbitrary")), name="gated_merge",
    )(x2d, w["g_mix"], o_a, o_b, w["w_ga"], w["w_gb"], w["w_ua"], w["w_ub"])


def _out_proj_kernel(x_ref, m_ref, wo_ref, out_ref):
    out_ref[...] = x_ref[...] + jnp.dot(m_ref[...], wo_ref[...], preferred_element_type=F32)


def _out_proj(x2d, merged, w, tm):
    tokens = x2d.shape[0]
    return pl.pallas_call(
        _out_proj_kernel,
        out_shape=jax.ShapeDtypeStruct((tokens, D_MODEL), F32),
        grid=(tokens // tm,),
        in_specs=[
            pl.BlockSpec((tm, D_MODEL), lambda i: (i, 0)),
            pl.BlockSpec((tm, D_MODEL), lambda i: (i, 0)),
            _const_spec((D_MODEL, D_MODEL)),
        ],
        out_specs=pl.BlockSpec((tm, D_MODEL), lambda i: (i, 0)),
        compiler_params=_params(("arbitrary",)), name="out_proj",
    )(x2d, merged, w["w_o"])


def _ffn_kernel(x_ref, g_ref, wg_ref, wu_ref, wd_ref, out_ref, h_ref):
    @pl.when(pl.program_id(1) == 0)
    def _():
        x = x_ref[...]
        h_ref[...] = _rms_rows(x, g_ref[...]).astype(BF16)
        out_ref[...] = x

    h = h_ref[...]
    g = jnp.dot(h, wg_ref[...], preferred_element_type=F32)
    u = jnp.dot(h, wu_ref[...], preferred_element_type=F32)
    a = (g * jax.nn.sigmoid(g) * u).astype(BF16)
    out_ref[...] += jnp.dot(a, wd_ref[...], preferred_element_type=F32)


def _ffn(x2d, w, tm, tf):
    tokens = x2d.shape[0]
    return pl.pallas_call(
        _ffn_kernel,
        out_shape=jax.ShapeDtypeStruct((tokens, D_MODEL), F32),
        grid=(tokens // tm, D_FF // tf),
        in_specs=[
            pl.BlockSpec((tm, D_MODEL), lambda i, j: (i, 0)),
            _const_spec((1, D_MODEL)),
            pl.BlockSpec((D_MODEL, tf), lambda i, j: (0, j)),
            pl.BlockSpec((D_MODEL, tf), lambda i, j: (0, j)),
            pl.BlockSpec((tf, D_MODEL), lambda i, j: (j, 0)),
        ],
        out_specs=pl.BlockSpec((tm, D_MODEL), lambda i, j: (i, 0)),
        scratch_shapes=[pltpu.VMEM((tm, D_MODEL), BF16)],
        compiler_params=_params(("arbitrary", "arbitrary")), name="swiglu_ffn",
    )(x2d, w["g_ffn"], w["w_gate"], w["w_up"], w["w_down"])


def _prepare_weights(mix_norm, w_in, q_a_norm, wq_b, kv_a_norm, wkv_b, mla_q_norm, mla_k_norm,
                     diff_q_norm, diff_k_norm, diff_subln, w_up_mla, w_up_diff, w_o, ffn_norm,
                     w_gate, w_up, w_down, layer):
    win = w_in[layer]
    o_cq, o_ckv, o_kpe = 0, Q_LORA, Q_LORA + KV_LORA
    o_dq = o_kpe + MLA_ROPE
    o_dk = o_dq + HEAD_OUT
    o_dv = o_dk + HEAD_OUT
    o_ga = o_dv + HEAD_OUT
    o_gb = o_ga + D_MODEL
    w_kpe = win[:, o_kpe:o_dq]
    w_kpe_rot = jnp.concatenate([-w_kpe[:, ROPE_HALF:], w_kpe[:, :ROPE_HALF]], axis=1)
    zpad = jnp.zeros((D_MODEL, LANES - MLA_ROPE), F32)
    w_a = jnp.concatenate([win[:, o_cq:o_kpe], w_kpe, zpad, w_kpe_rot, zpad], axis=1)
    wkv = wkv_b[layer].reshape(KV_LORA, MLA_HEADS, MLA_NOPE + MLA_V)
    gk = mla_k_norm[layer]
    gk_rope = gk[MLA_NOPE:]
    lane_pad = jnp.zeros((LANES - MLA_ROPE,), F32)
    return {
        "g_mix": mix_norm[layer][None, :],
        "w_a": w_a.astype(BF16),
        "g_qa": q_a_norm[layer][None, :],
        "g_kva": kv_a_norm[layer][None, :],
        "wq_t": wq_b[layer].T.astype(BF16),
        "g_q": mla_q_norm[layer][:, None],
        "wk_nope": wkv[:, :, :MLA_NOPE].reshape(KV_LORA, MLA_HEADS * MLA_NOPE).astype(BF16),
        "wv_t": wkv[:, :, MLA_NOPE:].reshape(KV_LORA, MLA_HEADS * MLA_V).T.astype(BF16),
        "g_kn": gk[None, :MLA_NOPE],
        "g_kr": jnp.concatenate([gk_rope, lane_pad])[None, :],
        "g_krot": jnp.concatenate([gk_rope[ROPE_HALF:], gk_rope[:ROPE_HALF], lane_pad])[None, :],
        "w_dq_t": win[:, o_dq:o_dk].T.astype(BF16),
        "w_dk_t": win[:, o_dk:o_dv].T.astype(BF16),
        "w_dv_t": win[:, o_dv:o_ga].T.astype(BF16),
        "g_dq": diff_q_norm[layer][:, None],
        "g_dk": diff_k_norm[layer][:, None],
        "g_sub": diff_subln[layer][:, None],
        "w_ga": win[:, o_ga:o_gb].astype(BF16),
        "w_gb": win[:, o_gb:].astype(BF16),
        "w_ua": w_up_mla[layer].astype(BF16),
        "w_ub": w_up_diff[layer].astype(BF16),
        "w_o": w_o[layer].astype(BF16),
        "g_ffn": ffn_norm[layer][None, :],
        "w_gate": w_gate[layer].astype(BF16),
        "w_up": w_up[layer].astype(BF16),
        "w_down": w_down[layer].astype(BF16),
    }


def _encoder_layer(x, w, tables, bias, rel_bias, lam_rows, lam_init, *, tm, blk, tn, tf):
    gb, seq, _ = x.shape
    x2d = x.reshape(gb * seq, D_MODEL)
    q_a, k_a, v_a = _mla_prep(x2d, gb, seq, tm, blk, w, tables)
    q_d, k_d, v_d = _diff_prep(x2d, gb, seq, tm, blk, w)
    o_a = _mla_attention(q_a, k_a, v_a, gb, seq, blk)
    o_b = _diff_attention(q_d, k_d, v_d, bias, rel_bias, lam_rows, w["g_sub"], gb, seq, blk, lam_init)
    merged = _merge(x2d, o_a, o_b, w, tm, tn)
    x1 = _out_proj(x2d, merged, w, tm)
    y = _ffn(x1, w, tm, tf)
    return y.reshape(gb, seq, D_MODEL)


def kernel(x_prompt, x_sample, mix_norm, w_in, q_a_norm, wq_b, kv_a_norm, wkv_b, mla_q_norm, mla_k_norm, diff_q_norm, diff_k_norm, lambda_q1, lambda_k1, lambda_q2, lambda_k2, diff_subln, w_up_mla, w_up_diff, w_o, ffn_norm, w_gate, w_up, w_down, rel_bias):
    tm, blk, tn, tf = 512, 512, 512, 512
    depth = w_in.shape[0]
    max_seq = max(x_prompt.shape[1], x_sample.shape[1])
    tables = _rope_tables(max_seq)
    bias = _bias_tiles(rel_bias, blk)
    y_prompt, y_sample = x_prompt, x_sample
    for layer in range(depth):
        w = _prepare_weights(mix_norm, w_in, q_a_norm, wq_b, kv_a_norm, wkv_b, mla_q_norm,
                             mla_k_norm, diff_q_norm, diff_k_norm, diff_subln, w_up_mla,
                             w_up_diff, w_o, ffn_norm, w_gate, w_up, w_down, layer)
        lam_init = 0.8 - 0.6 * math.exp(-0.3 * layer)
        lam_pad = jnp.zeros((LANES - DIFF_QK,), F32)
        lam_rows = jnp.stack([jnp.concatenate([v[layer], lam_pad])
                              for v in (lambda_q1, lambda_k1, lambda_q2, lambda_k2)])
        run = functools.partial(_encoder_layer, w=w, tables=tables, bias=bias, rel_bias=rel_bias,
                                lam_rows=lam_rows, lam_init=lam_init, tm=tm, blk=blk, tn=tn, tf=tf)
        y_prompt = run(y_prompt)
        y_sample = run(y_sample)
    return (y_prompt, y_sample)
```

```python
import functools
import math

import jax
import jax.numpy as jnp
from jax import lax
from jax.experimental import pallas as pl
from jax.experimental.pallas import tpu as pltpu

D_MODEL = 2048
MLA_HEADS = 8
MLA_NOPE = 128
MLA_ROPE = 64
MLA_QK = MLA_NOPE + MLA_ROPE
MLA_V = 128
Q_LORA = 512
KV_LORA = 256
ROPE_THETA = 10000.0
ROPE_HALF = MLA_ROPE // 2
DIFF_HEADS = 8
DIFF_QK = 64
DIFF_V = 2 * DIFF_QK
REL_BUCKETS = 32
REL_MAX_DIST = 128
D_FF = 5632
EPS = 1e-6
HEAD_OUT = MLA_HEADS * MLA_V

LANES = 128
MLA_QK_PAD = 2 * LANES
VMEM_LIMIT = 56 * 1024 * 1024

F32 = jnp.float32
BF16 = jnp.bfloat16
NEG_BIG = -1e30
LOG2E = math.log2(math.e)
NEAR_TILES = 5
BOUNDED_SOFTMAX_LIMIT = 50.0

NT_DIMS = (((1,), (1,)), ((), ()))


def _params(semantics):
    return pltpu.CompilerParams(dimension_semantics=semantics, vmem_limit_bytes=VMEM_LIMIT)


def _const_spec(shape):
    zeros = (0,) * len(shape)
    return pl.BlockSpec(shape, lambda *_: zeros)


def _rms_rows(x, gain):
    return x * lax.rsqrt(jnp.mean(x * x, axis=-1, keepdims=True) + EPS) * gain


def _rope_table_kernel(inv_row_ref, inv_col_ref, cos_t_ref, sin_t_ref, cos_f_ref, sin_f_ref):
    ts = cos_t_ref.shape[0]
    base = pl.program_id(0) * ts
    pos_rows = (base + lax.broadcasted_iota(jnp.int32, (ts, LANES), 0)).astype(F32)
    ang_t = pos_rows * inv_row_ref[...]
    cos_t_ref[...] = jnp.cos(ang_t)
    sin_t_ref[...] = jnp.sin(ang_t)
    pos_cols = (base + lax.broadcasted_iota(jnp.int32, (ROPE_HALF, ts), 1)).astype(F32)
    ang_f = pos_cols * inv_col_ref[...]
    cos_f_ref[...] = jnp.cos(ang_f)
    sin_f_ref[...] = jnp.sin(ang_f)


def _rope_tables(seq):
    inv = ROPE_THETA ** (-jnp.arange(ROPE_HALF, dtype=F32) / ROPE_HALF)
    inv_row = jnp.tile(inv, LANES // ROPE_HALF)[None, :]
    inv_col = inv[:, None]
    ts = min(seq, 2048)
    tok = jax.ShapeDtypeStruct((seq, LANES), F32)
    feat = jax.ShapeDtypeStruct((ROPE_HALF, seq), F32)
    return pl.pallas_call(
        _rope_table_kernel,
        out_shape=(tok, tok, feat, feat),
        grid=(seq // ts,),
        in_specs=[_const_spec((1, LANES)), _const_spec((ROPE_HALF, 1))],
        out_specs=(
            pl.BlockSpec((ts, LANES), lambda i: (i, 0)),
            pl.BlockSpec((ts, LANES), lambda i: (i, 0)),
            pl.BlockSpec((ROPE_HALF, ts), lambda i: (0, i)),
            pl.BlockSpec((ROPE_HALF, ts), lambda i: (0, i)),
        ),
        compiler_params=_params(("arbitrary",)), name="rope_tables",
    )(inv_row, inv_col)


def _mla_prep_kernel(x_ref, g_mix_ref, w_a_ref, g_qa_ref, g_kva_ref, wq_t_ref, g_q_ref,
                     wk_ref, wv_t_ref, g_kn_ref, g_kr_ref, g_krot_ref,
                     cos_t_ref, sin_t_ref, cos_f_ref, sin_f_ref,
                     q_ref, k_ref, v_ref, *, tk):
    tm = x_ref.shape[0]
    h = _rms_rows(x_ref[...], g_mix_ref[...]).astype(BF16)
    c = jnp.dot(h, w_a_ref[...], preferred_element_type=F32)
    cq = _rms_rows(c[:, :Q_LORA], g_qa_ref[...]).astype(BF16)
    ckv = _rms_rows(c[:, Q_LORA:Q_LORA + KV_LORA], g_kva_ref[...]).astype(BF16)
    k_pe = c[:, 6 * LANES:7 * LANES]
    k_pe_rot = c[:, 7 * LANES:8 * LANES]

    q_t = lax.dot_general(wq_t_ref[...], cq, NT_DIMS, preferred_element_type=F32)
    cos_f = cos_f_ref[...]
    sin_f = sin_f_ref[...]
    scale = MLA_QK ** -0.5 * LOG2E
    g_q = g_q_ref[...]
    for hd in range(MLA_HEADS):
        xh = q_t[hd * MLA_QK:(hd + 1) * MLA_QK]
        rinv = lax.rsqrt(jnp.mean(xh * xh, axis=0, keepdims=True) + EPS)
        xn = xh * rinv * g_q
        x1 = xn[MLA_NOPE:MLA_NOPE + ROPE_HALF]
        x2 = xn[MLA_NOPE + ROPE_HALF:]
        q_ref[0, hd, 0:MLA_NOPE, :] = (xn[:MLA_NOPE] * scale).astype(BF16)
        q_ref[0, hd, MLA_NOPE:MLA_NOPE + ROPE_HALF, :] = ((x1 * cos_f - x2 * sin_f) * scale).astype(BF16)
        q_ref[0, hd, MLA_NOPE + ROPE_HALF:MLA_QK, :] = ((x2 * cos_f + x1 * sin_f) * scale).astype(BF16)
        q_ref[0, hd, MLA_QK:, :] = jnp.zeros((MLA_QK_PAD - MLA_QK, tm), BF16)

    k_nope = jnp.dot(ckv, wk_ref[...], preferred_element_type=F32)
    rope_base = (k_pe * g_kr_ref[...]) * cos_t_ref[...] + (k_pe_rot * g_krot_ref[...]) * sin_t_ref[...]
    ss_pe = jnp.sum(k_pe * k_pe, axis=-1, keepdims=True)
    for hd in range(MLA_HEADS):
        kh = k_nope[:, hd * MLA_NOPE:(hd + 1) * MLA_NOPE]
        ss = jnp.sum(kh * kh, axis=-1, keepdims=True) + ss_pe
        rinv = lax.rsqrt(ss * (1.0 / MLA_QK) + EPS)
        k_ref[0, hd, :, 0:LANES] = (kh * rinv * g_kn_ref[...]).astype(BF16)
        k_ref[0, hd, :, LANES:] = (rope_base * rinv).astype(BF16)

    v_t = lax.dot_general(wv_t_ref[...], ckv, NT_DIMS, preferred_element_type=F32)
    for hd in range(MLA_HEADS):
        for cc in range(tm // tk):
            v_ref[0, hd, cc] = v_t[hd * MLA_V:(hd + 1) * MLA_V, cc * tk:(cc + 1) * tk].astype(BF16)


def _mla_prep(x2d, gb, seq, tm, tk, w, tables):
    tokens = x2d.shape[0]
    nt = seq // tm
    cos_t, sin_t, cos_f, sin_f = tables
    kern = functools.partial(_mla_prep_kernel, tk=tk)
    out_shape = (
        jax.ShapeDtypeStruct((gb, MLA_HEADS, MLA_QK_PAD, seq), BF16),
        jax.ShapeDtypeStruct((gb, MLA_HEADS, seq, MLA_QK_PAD), BF16),
        jax.ShapeDtypeStruct((gb, MLA_HEADS, seq // tk, MLA_V, tk), BF16),
    )
    in_specs = [
        pl.BlockSpec((tm, D_MODEL), lambda i: (i, 0)),
        _const_spec((1, D_MODEL)),
        _const_spec((D_MODEL, 8 * LANES)),
        _const_spec((1, Q_LORA)),
        _const_spec((1, KV_LORA)),
        _const_spec((MLA_HEADS * MLA_QK, Q_LORA)),
        _const_spec((MLA_QK, 1)),
        _const_spec((KV_LORA, MLA_HEADS * MLA_NOPE)),
        _const_spec((MLA_HEADS * MLA_V, KV_LORA)),
        _const_spec((1, LANES)),
        _const_spec((1, LANES)),
        _const_spec((1, LANES)),
        pl.BlockSpec((tm, LANES), lambda i: (i % nt, 0)),
        pl.BlockSpec((tm, LANES), lambda i: (i % nt, 0)),
        pl.BlockSpec((ROPE_HALF, tm), lambda i: (0, i % nt)),
        pl.BlockSpec((ROPE_HALF, tm), lambda i: (0, i % nt)),
    ]
    out_specs = (
        pl.BlockSpec((1, MLA_HEADS, MLA_QK_PAD, tm), lambda i: (i // nt, 0, 0, i % nt)),
        pl.BlockSpec((1, MLA_HEADS, tm, MLA_QK_PAD), lambda i: (i // nt, 0, i % nt, 0)),
        pl.BlockSpec((1, MLA_HEADS, tm // tk, MLA_V, tk), lambda i: (i // nt, 0, i % nt, 0, 0)),
    )
    return pl.pallas_call(
        kern, out_shape=out_shape, grid=(tokens // tm,), in_specs=in_specs, out_specs=out_specs,
        compiler_params=_params(("arbitrary",)), name="mla_prep",
    )(x2d, w["g_mix"], w["w_a"], w["g_qa"], w["g_kva"], w["wq_t"], w["g_q"], w["wk_nope"],
      w["wv_t"], w["g_kn"], w["g_kr"], w["g_krot"], cos_t, sin_t, cos_f, sin_f)


def _group_norm_cols(x_t, gain_col):
    rows, tm = x_t.shape
    x3 = x_t.reshape(rows // DIFF_QK, DIFF_QK, tm)
    rinv = lax.rsqrt(jnp.mean(x3 * x3, axis=1, keepdims=True) + EPS)
    return (x3 * rinv * gain_col[None]).reshape(rows, tm)


def _diff_prep_kernel(x_ref, g_mix_ref, wq_t_ref, wk_t_ref, wv_t_ref, g_q_ref, g_k_ref,
                      q_ref, k_ref, v_ref, *, tk):
    tm = x_ref.shape[0]
    h = _rms_rows(x_ref[...], g_mix_ref[...]).astype(BF16)
    scale = DIFF_QK ** -0.5 * LOG2E
    q_t = lax.dot_general(wq_t_ref[...], h, NT_DIMS, preferred_element_type=F32)
    qn = (_group_norm_cols(q_t, g_q_ref[...]) * scale).astype(BF16)
    zeros = jnp.zeros((DIFF_QK, tm), BF16)
    for hd in range(DIFF_HEADS):
        r0 = hd * DIFF_V
        q_ref[0, hd, 0, 0:DIFF_QK, :] = qn[r0:r0 + DIFF_QK]
        q_ref[0, hd, 0, DIFF_QK:, :] = zeros
        q_ref[0, hd, 1, 0:DIFF_QK, :] = zeros
        q_ref[0, hd, 1, DIFF_QK:, :] = qn[r0 + DIFF_QK:r0 + DIFF_V]

    k_t = lax.dot_general(wk_t_ref[...], h, NT_DIMS, preferred_element_type=F32)
    kn = _group_norm_cols(k_t, g_k_ref[...]).T
    for hd in range(DIFF_HEADS):
        k_ref[0, hd] = kn[:, hd * DIFF_V:(hd + 1) * DIFF_V].astype(BF16)

    v_t = lax.dot_general(wv_t_ref[...], h, NT_DIMS, preferred_element_type=F32)
    for hd in range(DIFF_HEADS):
        for cc in range(tm // tk):
            v_ref[0, hd, cc] = v_t[hd * DIFF_V:(hd + 1) * DIFF_V, cc * tk:(cc + 1) * tk].astype(BF16)


def _diff_prep(x2d, gb, seq, tm, tk, w):
    tokens = x2d.shape[0]
    nt = seq // tm
    kern = functools.partial(_diff_prep_kernel, tk=tk)
    out_shape = (
        jax.ShapeDtypeStruct((gb, DIFF_HEADS, 2, DIFF_V, seq), BF16),
        jax.ShapeDtypeStruct((gb, DIFF_HEADS, seq, DIFF_V), BF16),
        jax.ShapeDtypeStruct((gb, DIFF_HEADS, seq // tk, DIFF_V, tk), BF16),
    )
    in_specs = [
        pl.BlockSpec((tm, D_MODEL), lambda i: (i, 0)),
        _const_spec((1, D_MODEL)),
        _const_spec((HEAD_OUT, D_MODEL)),
        _const_spec((HEAD_OUT, D_MODEL)),
        _const_spec((HEAD_OUT, D_MODEL)),
        _const_spec((DIFF_QK, 1)),
        _const_spec((DIFF_QK, 1)),
    ]
    out_specs = (
        pl.BlockSpec((1, DIFF_HEADS, 2, DIFF_V, tm), lambda i: (i // nt, 0, 0, 0, i % nt)),
        pl.BlockSpec((1, DIFF_HEADS, tm, DIFF_V), lambda i: (i // nt, 0, i % nt, 0)),
        pl.BlockSpec((1, DIFF_HEADS, tm // tk, DIFF_V, tk), lambda i: (i // nt, 0, i % nt, 0, 0)),
    )
    return pl.pallas_call(
        kern, out_shape=out_shape, grid=(tokens // tm,), in_specs=in_specs, out_specs=out_specs,
        compiler_params=_params(("arbitrary",)), name="diff_prep",
    )(x2d, w["g_mix"], w["w_dq_t"], w["w_dk_t"], w["w_dv_t"], w["g_dq"], w["g_dk"])


def _bias_tile_kernel(bucket_ref, rb_ref, out_ref):
    hd = pl.program_id(0)
    bucket = bucket_ref[0]
    acc = jnp.zeros(bucket.shape, F32)
    for b in range(REL_BUCKETS):
        acc = jnp.where(bucket == b, rb_ref[b, hd], acc)
    out_ref[0, 0] = acc * LOG2E


def _t5_bucket(rel):
    nb = REL_BUCKETS // 2
    ret = jnp.where(rel > 0, nb, 0)
    n = jnp.abs(rel)
    max_exact = nb // 2
    nf = jnp.maximum(n, 1).astype(F32)
    large = max_exact + (jnp.log(nf / max_exact) / math.log(REL_MAX_DIST / max_exact)
                         * (nb - max_exact)).astype(jnp.int32)
    large = jnp.minimum(large, nb - 1)
    return ret + jnp.where(n < max_exact, n, large)


def _bias_tiles(rel_bias, blk):
    d = jnp.arange(blk, dtype=jnp.int32)
    offs = (jnp.arange(NEAR_TILES, dtype=jnp.int32) - NEAR_TILES // 2) * blk
    rel = offs[:, None, None] + d[None, :, None] - d[None, None, :]
    bucket = _t5_bucket(rel)
    return pl.pallas_call(
        _bias_tile_kernel,
        out_shape=jax.ShapeDtypeStruct((DIFF_HEADS, NEAR_TILES, blk, blk), F32),
        grid=(DIFF_HEADS, NEAR_TILES),
        in_specs=[
            pl.BlockSpec((1, blk, blk), lambda hd, t: (t, 0, 0)),
            pl.BlockSpec(memory_space=pltpu.SMEM),
        ],
        out_specs=pl.BlockSpec((1, 1, blk, blk), lambda hd, t: (hd, t, 0, 0)),
        compiler_params=_params(("arbitrary", "arbitrary")), name="bias_tiles",
    )(bucket, rel_bias)


class _FlashState:
    def __init__(self, q_ref, k_ref, v_ref, scratch, blk):
        self.q_ref, self.k_ref, self.v_ref = q_ref, k_ref, v_ref
        (self.s_ref, self.p_ref, self.alpha_ref, self.m_ref, self.l_ref, self.acc_ref,
         self.kmax_ref) = scratch
        self.blk = blk
        self.n_maps = q_ref.shape[2]
        self.nc = v_ref.shape[2]

    def key_chunk(self, c):
        return self.k_ref[0, 0, pl.ds(pl.multiple_of(c * self.blk, self.blk), self.blk), :]

    def update_key_norm(self):
        def body(c, mx):
            kc = self.key_chunk(c).astype(F32)
            n2 = jnp.sum(kc * kc, axis=-1, keepdims=True)
            return jnp.maximum(mx, jnp.max(n2, axis=0, keepdims=True))

        mx = lax.fori_loop(0, self.nc, body, jnp.zeros((1, 1), F32))
        self.kmax_ref[...] = jnp.broadcast_to(mx, self.kmax_ref.shape)

    def score_bound(self, extra):
        kmax2 = self.kmax_ref[:, 0:1]
        worst = None
        for mp in range(self.n_maps):
            q = self.q_ref[0, 0, mp].astype(F32)
            bound = jnp.sqrt(jnp.sum(q * q, axis=0, keepdims=True) * kmax2) + extra
            self.m_ref[mp] = bound
            tile_max = jnp.max(bound)
            worst = tile_max if worst is None else jnp.maximum(worst, tile_max)
        return worst


    def init_bounded(self):
        self.l_ref[...] = jnp.zeros(self.l_ref.shape, F32)
        self.acc_ref[...] = jnp.zeros(self.acc_ref.shape, F32)
        self.p_ref[1] = jnp.zeros(self.p_ref.shape[1:], BF16)

    def pv_bounded(self, c, half):
        vc = self.v_ref[0, 0, c]
        for mp in range(self.n_maps):
            self.acc_ref[mp] += jnp.dot(vc, self.p_ref[half, mp], preferred_element_type=F32)

    def step_bounded(self, c, half, cst=None, bias_tile=None):
        self.pv_bounded(jnp.maximum(c - 1, 0), 1 - half)
        kc = self.key_chunk(c)
        for mp in range(self.n_maps):
            s = jnp.dot(kc, self.q_ref[0, 0, mp], preferred_element_type=F32)
            if bias_tile is not None:
                s = s + bias_tile
            shift = self.m_ref[mp] if cst is None else self.m_ref[mp] - cst
            p = jnp.exp2(s - shift)
            self.l_ref[mp] += jnp.sum(p, axis=0, keepdims=True)
            self.p_ref[half, mp] = p.astype(BF16)

    def finish_bounded(self):
        self.pv_bounded(self.nc - 1, 1)


    def init_online(self):
        self.m_ref[...] = jnp.full(self.m_ref.shape, NEG_BIG, F32)
        self.l_ref[...] = jnp.zeros(self.l_ref.shape, F32)
        self.acc_ref[...] = jnp.zeros(self.acc_ref.shape, F32)
        self.p_ref[0] = jnp.zeros(self.p_ref.shape[1:], BF16)
        self.alpha_ref[...] = jnp.ones(self.alpha_ref.shape, F32)
        self.qk(0, 0)

    def qk(self, c, half):
        kc = self.key_chunk(c)
        for mp in range(self.n_maps):
            self.s_ref[half, mp] = jnp.dot(kc, self.q_ref[0, 0, mp], preferred_element_type=F32)

    def pv_online(self, c):
        vc = self.v_ref[0, 0, c]
        for mp in range(self.n_maps):
            self.acc_ref[mp] = (self.alpha_ref[mp] * self.acc_ref[mp]
                                + jnp.dot(vc, self.p_ref[0, mp], preferred_element_type=F32))

    def softmax_online(self, half, cst, bias_tile):
        for mp in range(self.n_maps):
            s = self.s_ref[half, mp]
            if bias_tile is not None:
                s = s + bias_tile
            mc = jnp.max(s, axis=0, keepdims=True)
            if cst is not None:
                mc = mc + cst
            m_old = self.m_ref[mp]
            m_new = jnp.maximum(m_old, mc)
            alpha = jnp.exp2(m_old - m_new)
            shift = m_new if cst is None else m_new - cst
            p = jnp.exp2(s - shift)
            self.l_ref[mp] = alpha * self.l_ref[mp] + jnp.sum(p, axis=0, keepdims=True)
            self.p_ref[0, mp] = p.astype(BF16)
            self.alpha_ref[mp] = alpha
            self.m_ref[mp] = m_new

    def step_online(self, c, half, cst=None, bias_tile=None):
        self.pv_online(jnp.maximum(c - 1, 0))
        self.qk(jnp.minimum(c + 1, self.nc - 1), 1 - half)
        self.softmax_online(half, cst, bias_tile)

    def finish_online(self):
        self.pv_online(self.nc - 1)

    def run(self, extra, schedule):
        use_bounded = self.score_bound(extra) <= BOUNDED_SOFTMAX_LIMIT

        @pl.when(use_bounded)
        def _():
            self.init_bounded()
            schedule(self.step_bounded)
            self.finish_bounded()

        @pl.when(jnp.logical_not(use_bounded))
        def _():
            self.init_online()
            schedule(self.step_online)
            self.finish_online()


def _flash_scratch(n_maps, head_dim, blk):
    return [
        pltpu.VMEM((2, n_maps, blk, blk), F32),
        pltpu.VMEM((2, n_maps, blk, blk), BF16),
        pltpu.VMEM((n_maps, 1, blk), F32),
        pltpu.VMEM((n_maps, 1, blk), F32),
        pltpu.VMEM((n_maps, 1, blk), F32),
        pltpu.VMEM((n_maps, head_dim, blk), F32),
        pltpu.VMEM((1, LANES), F32),
    ]


def _mla_attn_kernel(q_ref, k_ref, v_ref, o_ref, *scratch, blk, unroll):
    st = _FlashState(q_ref, k_ref, v_ref, scratch, blk)

    @pl.when(pl.program_id(2) == 0)
    def _():
        st.update_key_norm()

    def schedule(step):
        def group(cc, carry):
            for u in range(unroll):
                step(unroll * cc + u, u % 2)
            return carry

        lax.fori_loop(0, st.nc // unroll, group, 0)

    st.run(0.0, schedule)
    o_t = st.acc_ref[0] / st.l_ref[0]
    o_ref[...] = o_t.T.astype(BF16)


def _mla_attention(q, k, v, gb, seq, blk):
    nq = seq // blk
    unroll = 4 if nq % 4 == 0 else 2
    assert nq % unroll == 0
    kern = functools.partial(_mla_attn_kernel, blk=blk, unroll=unroll)
    q = q.reshape(gb, MLA_HEADS, 1, MLA_QK_PAD, seq)
    return pl.pallas_call(
        kern,
        out_shape=jax.ShapeDtypeStruct((gb * seq, HEAD_OUT), BF16),
        grid=(gb, MLA_HEADS, nq),
        in_specs=[
            pl.BlockSpec((1, 1, 1, MLA_QK_PAD, blk), lambda b, hd, i: (b, hd, 0, 0, i)),
            pl.BlockSpec((1, 1, seq, MLA_QK_PAD), lambda b, hd, i: (b, hd, 0, 0)),
            pl.BlockSpec((1, 1, nq, MLA_V, blk), lambda b, hd, i: (b, hd, 0, 0, 0)),
        ],
        out_specs=pl.BlockSpec((blk, MLA_V), lambda b, hd, i: (b * nq + i, hd)),
        scratch_shapes=_flash_scratch(1, MLA_V, blk),
        compiler_params=_params(("arbitrary", "arbitrary", "arbitrary")), name="mla_attn",
    )(q, k, v)


def _diff_attn_kernel(q_ref, k_ref, v_ref, bias_ref, rb_ref, lam_ref, g_sub_ref, o_ref,
                      *scratch, blk, lam_init):
    hd = pl.program_id(1)
    i = pl.program_id(2)
    st = _FlashState(q_ref, k_ref, v_ref, scratch, blk)
    acc_ref, l_ref = st.acc_ref, st.l_ref
    n_pairs = st.nc // 2

    @pl.when(i == 0)
    def _():
        st.update_key_norm()

    bias_max = jnp.abs(rb_ref[0, hd])
    for b in range(1, REL_BUCKETS):
        bias_max = jnp.maximum(bias_max, jnp.abs(rb_ref[b, hd]))

    far_before = rb_ref[REL_BUCKETS // 2 - 1, hd] * LOG2E
    far_after = rb_ref[REL_BUCKETS - 1, hd] * LOG2E
    first_near = (i + 1) // 2 - 1

    near_lo = jnp.maximum(first_near, 0)
    near_hi = jnp.minimum(first_near + 2, n_pairs)

    def schedule(step):
        def far_pair(cst):
            def body(cc, carry):
                step(2 * cc, 0, cst=cst)
                step(2 * cc + 1, 1, cst=cst)
                return carry
            return body

        def near_pair(cc, carry):
            for half in range(2):
                c = 2 * cc + half
                step(c, half, bias_tile=bias_ref[0, c - i + NEAR_TILES // 2])
            return carry

        lax.fori_loop(0, near_lo, far_pair(far_before), 0)
        lax.fori_loop(near_lo, near_hi, near_pair, 0)
        lax.fori_loop(near_hi, n_pairs, far_pair(far_after), 0)

    st.run(bias_max * LOG2E, schedule)

    lam_v = lam_ref[...]
    lam = (jnp.exp(jnp.sum(lam_v[0:1] * lam_v[1:2], axis=-1, keepdims=True))
           - jnp.exp(jnp.sum(lam_v[2:3] * lam_v[3:4], axis=-1, keepdims=True)) + lam_init)
    o_t = acc_ref[0] / l_ref[0] - lam * (acc_ref[1] / l_ref[1])
    o_t = o_t * lax.rsqrt(jnp.mean(o_t * o_t, axis=0, keepdims=True) + EPS) * g_sub_ref[...]
    o_ref[...] = (o_t * (1.0 - lam_init)).T.astype(BF16)


def _diff_attention(q, k, v, bias, rel_bias, lam_rows, g_sub, gb, seq, blk, lam_init):
    nq = seq // blk
    assert nq % 2 == 0 and blk >= REL_MAX_DIST
    kern = functools.partial(_diff_attn_kernel, blk=blk, lam_init=lam_init)
    return pl.pallas_call(
        kern,
        out_shape=jax.ShapeDtypeStruct((gb * seq, HEAD_OUT), BF16),
        grid=(gb, DIFF_HEADS, nq),
        in_specs=[
            pl.BlockSpec((1, 1, 2, DIFF_V, blk), lambda b, hd, i: (b, hd, 0, 0, i)),
            pl.BlockSpec((1, 1, seq, DIFF_V), lambda b, hd, i: (b, hd, 0, 0)),
            pl.BlockSpec((1, 1, seq // blk, DIFF_V, blk), lambda b, hd, i: (b, hd, 0, 0, 0)),
            pl.BlockSpec((1, NEAR_TILES, blk, blk), lambda b, hd, i: (hd, 0, 0, 0)),
            pl.BlockSpec(memory_space=pltpu.SMEM),
            _const_spec((4, LANES)),
            _const_spec((DIFF_V, 1)),
        ],
        out_specs=pl.BlockSpec((blk, DIFF_V), lambda b, hd, i: (b * nq + i, hd)),
        scratch_shapes=_flash_scratch(2, DIFF_V, blk),
        compiler_params=_params(("arbitrary", "arbitrary", "arbitrary")), name="diff_attn",
    )(q, k, v, bias, rel_bias, lam_rows, g_sub)


def _merge_kernel(x_ref, g_mix_ref, oa_ref, ob_ref, wga_ref, wgb_ref, wua_ref, wub_ref,
                  out_ref, h_ref):
    @pl.when(pl.program_id(1) == 0)
    def _():
        h_ref[...] = _rms_rows(x_ref[...], g_mix_ref[...]).astype(BF16)

    h = h_ref[...]
    ga = jnp.dot(h, wga_ref[...], preferred_element_type=F32)
    gb_ = jnp.dot(h, wgb_ref[...], preferred_element_type=F32)
    ua = jnp.dot(oa_ref[...], wua_ref[...], preferred_element_type=F32)
    ub = jnp.dot(ob_ref[...], wub_ref[...], preferred_element_type=F32)
    out_ref[...] = (jax.nn.sigmoid(ga) * ua + jax.nn.sigmoid(gb_) * ub).astype(BF16)


def _merge(x2d, o_a, o_b, w, tm, tn):
    tokens = x2d.shape[0]
    return pl.pallas_call(
        _merge_kernel,
        out_shape=jax.ShapeDtypeStruct((tokens, D_MODEL), BF16),
        grid=(tokens // tm, D_MODEL // tn),
        in_specs=[
            pl.BlockSpec((tm, D_MODEL), lambda i, j: (i, 0)),
            _const_spec((1, D_MODEL)),
            pl.BlockSpec((tm, HEAD_OUT), lambda i, j: (i, 0)),
            pl.BlockSpec((tm, HEAD_OUT), lambda i, j: (i, 0)),
            pl.BlockSpec((D_MODEL, tn), lambda i, j: (0, j)),
            pl.BlockSpec((D_MODEL, tn), lambda i, j: (0, j)),
            pl.BlockSpec((HEAD_OUT, tn), lambda i, j: (0, j)),
            pl.BlockSpec((HEAD_OUT, tn), lambda i, j: (0, j)),
        ],
        out_specs=pl.BlockSpec((tm, tn), lambda i, j: (i, j)),
        scratch_shapes=[pltpu.VMEM((tm, D_MODEL), BF16)],
        compiler_params=_params(("arbitrary", "arbitrary")), name="gated_merge",
    )(x2d, w["g_mix"], o_a, o_b, w["w_ga"], w["w_gb"], w["w_ua"], w["w_ub"])


def _out_proj_kernel(x_ref, m_ref, wo_ref, out_ref):
    out_ref[...] = x_ref[...] + jnp.dot(m_ref[...], wo_ref[...], preferred_element_type=F32)


def _out_proj(x2d, merged, w, tm):
    tokens = x2d.shape[0]
    return pl.pallas_call(
        _out_proj_kernel,
        out_shape=jax.ShapeDtypeStruct((tokens, D_MODEL), F32),
        grid=(tokens // tm,),
        in_specs=[
            pl.BlockSpec((tm, D_MODEL), lambda i: (i, 0)),
            pl.BlockSpec((tm, D_MODEL), lambda i: (i, 0)),
            _const_spec((D_MODEL, D_MODEL)),
        ],
        out_specs=pl.BlockSpec((tm, D_MODEL), lambda i: (i, 0)),
        compiler_params=_params(("arbitrary",)), name="out_proj",
    )(x2d, merged, w["w_o"])


def _ffn_kernel(x_ref, g_ref, wg_ref, wu_ref, wd_ref, out_ref, h_ref):
    @pl.when(pl.program_id(1) == 0)
    def _():
        x = x_ref[...]
        h_ref[...] = _rms_rows(x, g_ref[...]).astype(BF16)
        out_ref[...] = x

    h = h_ref[...]
    g = jnp.dot(h, wg_ref[...], preferred_element_type=F32)
    u = jnp.dot(h, wu_ref[...], preferred_element_type=F32)
    a = (g * jax.nn.sigmoid(g) * u).astype(BF16)
    out_ref[...] += jnp.dot(a, wd_ref[...], preferred_element_type=F32)


def _ffn(x2d, w, tm, tf):
    tokens = x2d.shape[0]
    return pl.pallas_call(
        _ffn_kernel,
        out_shape=jax.ShapeDtypeStruct((tokens, D_MODEL), F32),
        grid=(tokens // tm, D_FF // tf),
        in_specs=[
            pl.BlockSpec((tm, D_MODEL), lambda i, j: (i, 0)),
            _const_spec((1, D_MODEL)),
            pl.BlockSpec((D_MODEL, tf), lambda i, j: (0, j)),
            pl.BlockSpec((D_MODEL, tf), lambda i, j: (0, j)),
            pl.BlockSpec((tf, D_MODEL), lambda i, j: (j, 0)),
        ],
        out_specs=pl.BlockSpec((tm, D_MODEL), lambda i, j: (i, 0)),
        scratch_shapes=[pltpu.VMEM((tm, D_MODEL), BF16)],
        compiler_params=_params(("arbitrary", "arbitrary")), name="swiglu_ffn",
    )(x2d, w["g_ffn"], w["w_gate"], w["w_up"], w["w_down"])


def _prepare_weights(mix_norm, w_in, q_a_norm, wq_b, kv_a_norm, wkv_b, mla_q_norm, mla_k_norm,
                     diff_q_norm, diff_k_norm, diff_subln, w_up_mla, w_up_diff, w_o, ffn_norm,
                     w_gate, w_up, w_down, layer):
    win = w_in[layer]
    o_cq, o_ckv, o_kpe = 0, Q_LORA, Q_LORA + KV_LORA
    o_dq = o_kpe + MLA_ROPE
    o_dk = o_dq + HEAD_OUT
    o_dv = o_dk + HEAD_OUT
    o_ga = o_dv + HEAD_OUT
    o_gb = o_ga + D_MODEL
    w_kpe = win[:, o_kpe:o_dq]
    w_kpe_rot = jnp.concatenate([-w_kpe[:, ROPE_HALF:], w_kpe[:, :ROPE_HALF]], axis=1)
    zpad = jnp.zeros((D_MODEL, LANES - MLA_ROPE), F32)
    w_a = jnp.concatenate([win[:, o_cq:o_kpe], w_kpe, zpad, w_kpe_rot, zpad], axis=1)
    wkv = wkv_b[layer].reshape(KV_LORA, MLA_HEADS, MLA_NOPE + MLA_V)
    gk = mla_k_norm[layer]
    gk_rope = gk[MLA_NOPE:]
    lane_pad = jnp.zeros((LANES - MLA_ROPE,), F32)
    return {
        "g_mix": mix_norm[layer][None, :],
        "w_a": w_a.astype(BF16),
        "g_qa": q_a_norm[layer][None, :],
        "g_kva": kv_a_norm[layer][None, :],
        "wq_t": wq_b[layer].T.astype(BF16),
        "g_q": mla_q_norm[layer][:, None],
        "wk_nope": wkv[:, :, :MLA_NOPE].reshape(KV_LORA, MLA_HEADS * MLA_NOPE).astype(BF16),
        "wv_t": wkv[:, :, MLA_NOPE:].reshape(KV_LORA, MLA_HEADS * MLA_V).T.astype(BF16),
        "g_kn": gk[None, :MLA_NOPE],
        "g_kr": jnp.concatenate([gk_rope, lane_pad])[None, :],
        "g_krot": jnp.concatenate([gk_rope[ROPE_HALF:], gk_rope[:ROPE_HALF], lane_pad])[None, :],
        "w_dq_t": win[:, o_dq:o_dk].T.astype(BF16),
        "w_dk_t": win[:, o_dk:o_dv].T.astype(BF16),
        "w_dv_t": win[:, o_dv:o_ga].T.astype(BF16),
        "g_dq": diff_q_norm[layer][:, None],
        "g_dk": diff_k_norm[layer][:, None],
        "g_sub": diff_subln[layer][:, None],
        "w_ga": win[:, o_ga:o_gb].astype(BF16),
        "w_gb": win[:, o_gb:].astype(BF16),
        "w_ua": w_up_mla[layer].astype(BF16),
        "w_ub": w_up_diff[layer].astype(BF16),
        "w_o": w_o[layer].astype(BF16),
        "g_ffn": ffn_norm[layer][None, :],
        "w_gate": w_gate[layer].astype(BF16),
        "w_up": w_up[layer].astype(BF16),
        "w_down": w_down[layer].astype(BF16),
    }


def _encoder_layer(x, w, tables, bias, rel_bias, lam_rows, lam_init, *, tm, blk, tn, tf):
    gb, seq, _ = x.shape
    x2d = x.reshape(gb * seq, D_MODEL)
    q_a, k_a, v_a = _mla_prep(x2d, gb, seq, tm, blk, w, tables)
    q_d, k_d, v_d = _diff_prep(x2d, gb, seq, tm, blk, w)
    o_a = _mla_attention(q_a, k_a, v_a, gb, seq, blk)
    o_b = _diff_attention(q_d, k_d, v_d, bias, rel_bias, lam_rows, w["g_sub"], gb, seq, blk, lam_init)
    merged = _merge(x2d, o_a, o_b, w, tm, tn)
    x1 = _out_proj(x2d, merged, w, tm)
    y = _ffn(x1, w, tm, tf)
    return y.reshape(gb, seq, D_MODEL)


def kernel(x_prompt, x_sample, mix_norm, w_in, q_a_norm, wq_b, kv_a_norm, wkv_b, mla_q_norm, mla_k_norm, diff_q_norm, diff_k_norm, lambda_q1, lambda_k1, lambda_q2, lambda_k2, diff_subln, w_up_mla, w_up_diff, w_o, ffn_norm, w_gate, w_up, w_down, rel_bias):
    tm, blk, tn, tf = 512, 512, 512, 512
    depth = w_in.shape[0]
    max_seq = max(x_prompt.shape[1], x_sample.shape[1])
    tables = _rope_tables(max_seq)
    bias = _bias_tiles(rel_bias, blk)
    y_prompt, y_sample = x_prompt, x_sample
    for layer in range(depth):
        w = _prepare_weights(mix_norm, w_in, q_a_norm, wq_b, kv_a_norm, wkv_b, mla_q_norm,
                             mla_k_norm, diff_q_norm, diff_k_norm, diff_subln, w_up_mla,
                             w_up_diff, w_o, ffn_norm, w_gate, w_up, w_down, layer)
        lam_init = 0.8 - 0.6 * math.exp(-0.3 * layer)
        lam_pad = jnp.zeros((LANES - DIFF_QK,), F32)
        lam_rows = jnp.stack([jnp.concatenate([v[layer], lam_pad])
                              for v in (lambda_q1, lambda_k1, lambda_q2, lambda_k2)])
        run = functools.partial(_encoder_layer, w=w, tables=tables, bias=bias, rel_bias=rel_bias,
                                lam_rows=lam_rows, lam_init=lam_init, tm=tm, blk=blk, tn=tn, tf=tf)
        y_prompt = run(y_prompt)
        y_sample = run(y_sample)
    return (y_prompt, y_sample)
```

```python
import functools
import math

import jax
import jax.numpy as jnp
from jax import lax
from jax.experimental import pallas as pl
from jax.experimental.pallas import tpu as pltpu

D_MODEL = 2048
MLA_HEADS = 8
MLA_NOPE = 128
MLA_ROPE = 64
MLA_QK = MLA_NOPE + MLA_ROPE
MLA_V = 128
Q_LORA = 512
KV_LORA = 256
ROPE_THETA = 10000.0
ROPE_HALF = MLA_ROPE // 2
DIFF_HEADS = 8
DIFF_QK = 64
DIFF_V = 2 * DIFF_QK
REL_BUCKETS = 32
REL_MAX_DIST = 128
D_FF = 5632
EPS = 1e-6
HEAD_OUT = MLA_HEADS * MLA_V

LANES = 128
MLA_QK_PAD = 2 * LANES
VMEM_LIMIT = 56 * 1024 * 1024

F32 = jnp.float32
BF16 = jnp.bfloat16
NEG_BIG = -1e30
LOG2E = math.log2(math.e)
BOUNDED_SOFTMAX_LIMIT = 50.0

NT_DIMS = (((1,), (1,)), ((), ()))


def _params(semantics):
    return pltpu.CompilerParams(dimension_semantics=semantics, vmem_limit_bytes=VMEM_LIMIT)


def _const_spec(shape):
    zeros = (0,) * len(shape)
    return pl.BlockSpec(shape, lambda *_: zeros)


def _rms_rows(x, gain):
    return x * lax.rsqrt(jnp.mean(x * x, axis=-1, keepdims=True) + EPS) * gain


def _rope_table_kernel(inv_row_ref, inv_col_ref, cos_t_ref, sin_t_ref, cos_f_ref, sin_f_ref):
    ts = cos_t_ref.shape[0]
    base = pl.program_id(0) * ts
    pos_rows = (base + lax.broadcasted_iota(jnp.int32, (ts, LANES), 0)).astype(F32)
    ang_t = pos_rows * inv_row_ref[...]
    cos_t_ref[...] = jnp.cos(ang_t)
    sin_t_ref[...] = jnp.sin(ang_t)
    pos_cols = (base + lax.broadcasted_iota(jnp.int32, (ROPE_HALF, ts), 1)).astype(F32)
    ang_f = pos_cols * inv_col_ref[...]
    cos_f_ref[...] = jnp.cos(ang_f)
    sin_f_ref[...] = jnp.sin(ang_f)


def _rope_tables(seq):
    inv = ROPE_THETA ** (-jnp.arange(ROPE_HALF, dtype=F32) / ROPE_HALF)
    inv_row = jnp.tile(inv, LANES // ROPE_HALF)[None, :]
    inv_col = inv[:, None]
    ts = min(seq, 2048)
    tok = jax.ShapeDtypeStruct((seq, LANES), F32)
    feat = jax.ShapeDtypeStruct((ROPE_HALF, seq), F32)
    return pl.pallas_call(
        _rope_table_kernel,
        out_shape=(tok, tok, feat, feat),
        grid=(seq // ts,),
        in_specs=[_const_spec((1, LANES)), _const_spec((ROPE_HALF, 1))],
        out_specs=(
            pl.BlockSpec((ts, LANES), lambda i: (i, 0)),
            pl.BlockSpec((ts, LANES), lambda i: (i, 0)),
            pl.BlockSpec((ROPE_HALF, ts), lambda i: (0, i)),
            pl.BlockSpec((ROPE_HALF, ts), lambda i: (0, i)),
        ),
        compiler_params=_params(("arbitrary",)), name="rope_tables",
    )(inv_row, inv_col)


def _mla_prep_kernel(x_ref, g_mix_ref, w_a_ref, g_qa_ref, g_kva_ref, wq_t_ref, g_q_ref,
                     wk_ref, wv_t_ref, g_kn_ref, g_kr_ref, g_krot_ref,
                     cos_t_ref, sin_t_ref, cos_f_ref, sin_f_ref,
                     q_ref, k_ref, v_ref, *, tk):
    tm = x_ref.shape[0]
    h = _rms_rows(x_ref[...], g_mix_ref[...]).astype(BF16)
    c = jnp.dot(h, w_a_ref[...], preferred_element_type=F32)
    cq = _rms_rows(c[:, :Q_LORA], g_qa_ref[...]).astype(BF16)
    ckv = _rms_rows(c[:, Q_LORA:Q_LORA + KV_LORA], g_kva_ref[...]).astype(BF16)
    k_pe = c[:, 6 * LANES:7 * LANES]
    k_pe_rot = c[:, 7 * LANES:8 * LANES]

    q_t = lax.dot_general(wq_t_ref[...], cq, NT_DIMS, preferred_element_type=F32)
    cos_f = cos_f_ref[...]
    sin_f = sin_f_ref[...]
    scale = MLA_QK ** -0.5 * LOG2E
    g_q = g_q_ref[...]
    for hd in range(MLA_HEADS):
        xh = q_t[hd * MLA_QK:(hd + 1) * MLA_QK]
        rinv = lax.rsqrt(jnp.mean(xh * xh, axis=0, keepdims=True) + EPS)
        xn = xh * rinv * g_q
        x1 = xn[MLA_NOPE:MLA_NOPE + ROPE_HALF]
        x2 = xn[MLA_NOPE + ROPE_HALF:]
        q_ref[0, hd, 0:MLA_NOPE, :] = (xn[:MLA_NOPE] * scale).astype(BF16)
        q_ref[0, hd, MLA_NOPE:MLA_NOPE + ROPE_HALF, :] = ((x1 * cos_f - x2 * sin_f) * scale).astype(BF16)
        q_ref[0, hd, MLA_NOPE + ROPE_HALF:MLA_QK, :] = ((x2 * cos_f + x1 * sin_f) * scale).astype(BF16)
        q_ref[0, hd, MLA_QK:, :] = jnp.zeros((MLA_QK_PAD - MLA_QK, tm), BF16)

    k_nope = jnp.dot(ckv, wk_ref[...], preferred_element_type=F32)
    rope_base = (k_pe * g_kr_ref[...]) * cos_t_ref[...] + (k_pe_rot * g_krot_ref[...]) * sin_t_ref[...]
    ss_pe = jnp.sum(k_pe * k_pe, axis=-1, keepdims=True)
    for hd in range(MLA_HEADS):
        kh = k_nope[:, hd * MLA_NOPE:(hd + 1) * MLA_NOPE]
        ss = jnp.sum(kh * kh, axis=-1, keepdims=True) + ss_pe
        rinv = lax.rsqrt(ss * (1.0 / MLA_QK) + EPS)
        k_ref[0, hd, :, 0:LANES] = (kh * rinv * g_kn_ref[...]).astype(BF16)
        k_ref[0, hd, :, LANES:] = (rope_base * rinv).astype(BF16)

    v_t = lax.dot_general(wv_t_ref[...], ckv, NT_DIMS, preferred_element_type=F32)
    for hd in range(MLA_HEADS):
        for cc in range(tm // tk):
            v_ref[0, hd, cc] = v_t[hd * MLA_V:(hd + 1) * MLA_V, cc * tk:(cc + 1) * tk].astype(BF16)


def _mla_prep(x2d, gb, seq, tm, tk, w, tables):
    tokens = x2d.shape[0]
    nt = seq // tm
    cos_t, sin_t, cos_f, sin_f = tables
    kern = functools.partial(_mla_prep_kernel, tk=tk)
    out_shape = (
        jax.ShapeDtypeStruct((gb, MLA_HEADS, MLA_QK_PAD, seq), BF16),
        jax.ShapeDtypeStruct((gb, MLA_HEADS, seq, MLA_QK_PAD), BF16),
        jax.ShapeDtypeStruct((gb, MLA_HEADS, seq // tk, MLA_V, tk), BF16),
    )
    in_specs = [
        pl.BlockSpec((tm, D_MODEL), lambda i: (i, 0)),
        _const_spec((1, D_MODEL)),
        _const_spec((D_MODEL, 8 * LANES)),
        _const_spec((1, Q_LORA)),
        _const_spec((1, KV_LORA)),
        _const_spec((MLA_HEADS * MLA_QK, Q_LORA)),
        _const_spec((MLA_QK, 1)),
        _const_spec((KV_LORA, MLA_HEADS * MLA_NOPE)),
        _const_spec((MLA_HEADS * MLA_V, KV_LORA)),
        _const_spec((1, LANES)),
        _const_spec((1, LANES)),
        _const_spec((1, LANES)),
        pl.BlockSpec((tm, LANES), lambda i: (i % nt, 0)),
        pl.BlockSpec((tm, LANES), lambda i: (i % nt, 0)),
        pl.BlockSpec((ROPE_HALF, tm), lambda i: (0, i % nt)),
        pl.BlockSpec((ROPE_HALF, tm), lambda i: (0, i % nt)),
    ]
    out_specs = (
        pl.BlockSpec((1, MLA_HEADS, MLA_QK_PAD, tm), lambda i: (i // nt, 0, 0, i % nt)),
        pl.BlockSpec((1, MLA_HEADS, tm, MLA_QK_PAD), lambda i: (i // nt, 0, i % nt, 0)),
        pl.BlockSpec((1, MLA_HEADS, tm // tk, MLA_V, tk), lambda i: (i // nt, 0, i % nt, 0, 0)),
    )
    return pl.pallas_call(
        kern, out_shape=out_shape, grid=(tokens // tm,), in_specs=in_specs, out_specs=out_specs,
        compiler_params=_params(("arbitrary",)), name="mla_prep",
    )(x2d, w["g_mix"], w["w_a"], w["g_qa"], w["g_kva"], w["wq_t"], w["g_q"], w["wk_nope"],
      w["wv_t"], w["g_kn"], w["g_kr"], w["g_krot"], cos_t, sin_t, cos_f, sin_f)


def _group_norm_cols(x_t, gain_col):
    rows, tm = x_t.shape
    x3 = x_t.reshape(rows // DIFF_QK, DIFF_QK, tm)
    rinv = lax.rsqrt(jnp.mean(x3 * x3, axis=1, keepdims=True) + EPS)
    return (x3 * rinv * gain_col[None]).reshape(rows, tm)


def _diff_prep_kernel(x_ref, g_mix_ref, wq_t_ref, wk_t_ref, wv_t_ref, g_q_ref, g_k_ref,
                      q_ref, k_ref, v_ref, *, tk):
    tm = x_ref.shape[0]
    h = _rms_rows(x_ref[...], g_mix_ref[...]).astype(BF16)
    scale = DIFF_QK ** -0.5 * LOG2E
    q_t = lax.dot_general(wq_t_ref[...], h, NT_DIMS, preferred_element_type=F32)
    qn = (_group_norm_cols(q_t, g_q_ref[...]) * scale).astype(BF16)
    zeros = jnp.zeros((DIFF_QK, tm), BF16)
    for hd in range(DIFF_HEADS):
        r0 = hd * DIFF_V
        q_ref[0, hd, 0, 0:DIFF_QK, :] = qn[r0:r0 + DIFF_QK]
        q_ref[0, hd, 0, DIFF_QK:, :] = zeros
        q_ref[0, hd, 1, 0:DIFF_QK, :] = zeros
        q_ref[0, hd, 1, DIFF_QK:, :] = qn[r0 + DIFF_QK:r0 + DIFF_V]

    k_t = lax.dot_general(wk_t_ref[...], h, NT_DIMS, preferred_element_type=F32)
    kn = _group_norm_cols(k_t, g_k_ref[...]).T
    for hd in range(DIFF_HEADS):
        k_ref[0, hd] = kn[:, hd * DIFF_V:(hd + 1) * DIFF_V].astype(BF16)

    v_t = lax.dot_general(wv_t_ref[...], h, NT_DIMS, preferred_element_type=F32)
    for hd in range(DIFF_HEADS):
        for cc in range(tm // tk):
            v_ref[0, hd, cc] = v_t[hd * DIFF_V:(hd + 1) * DIFF_V, cc * tk:(cc + 1) * tk].astype(BF16)


def _diff_prep(x2d, gb, seq, tm, tk, w):
    tokens = x2d.shape[0]
    nt = seq // tm
    kern = functools.partial(_diff_prep_kernel, tk=tk)
    out_shape = (
        jax.ShapeDtypeStruct((gb, DIFF_HEADS, 2, DIFF_V, seq), BF16),
        jax.ShapeDtypeStruct((gb, DIFF_HEADS, seq, DIFF_V), BF16),
        jax.ShapeDtypeStruct((gb, DIFF_HEADS, seq // tk, DIFF_V, tk), BF16),
    )
    in_specs = [
        pl.BlockSpec((tm, D_MODEL), lambda i: (i, 0)),
        _const_spec((1, D_MODEL)),
        _const_spec((HEAD_OUT, D_MODEL)),
        _const_spec((HEAD_OUT, D_MODEL)),
        _const_spec((HEAD_OUT, D_MODEL)),
        _const_spec((DIFF_QK, 1)),
        _const_spec((DIFF_QK, 1)),
    ]
    out_specs = (
        pl.BlockSpec((1, DIFF_HEADS, 2, DIFF_V, tm), lambda i: (i // nt, 0, 0, 0, i % nt)),
        pl.BlockSpec((1, DIFF_HEADS, tm, DIFF_V), lambda i: (i // nt, 0, i % nt, 0)),
        pl.BlockSpec((1, DIFF_HEADS, tm // tk, DIFF_V, tk), lambda i: (i // nt, 0, i % nt, 0, 0)),
    )
    return pl.pallas_call(
        kern, out_shape=out_shape, grid=(tokens // tm,), in_specs=in_specs, out_specs=out_specs,
        compiler_params=_params(("arbitrary",)), name="diff_prep",
    )(x2d, w["g_mix"], w["w_dq_t"], w["w_dk_t"], w["w_dv_t"], w["g_dq"], w["g_dk"])


def _bias_tile_kernel(bucket_ref, rb_ref, out_ref):
    hd = pl.program_id(0)
    bucket = bucket_ref[0]
    acc = jnp.zeros(bucket.shape, F32)
    for b in range(REL_BUCKETS):
        acc = jnp.where(bucket == b, rb_ref[b, hd], acc)
    out_ref[0, 0] = acc * LOG2E


def _t5_bucket(rel):
    nb = REL_BUCKETS // 2
    ret = jnp.where(rel > 0, nb, 0)
    n = jnp.abs(rel)
    max_exact = nb // 2
    nf = jnp.maximum(n, 1).astype(F32)
    large = max_exact + (jnp.log(nf / max_exact) / math.log(REL_MAX_DIST / max_exact)
                         * (nb - max_exact)).astype(jnp.int32)
    large = jnp.minimum(large, nb - 1)
    return ret + jnp.where(n < max_exact, n, large)


def _near_tiles(q_tiles):
    return 2 * q_tiles + 3


def _bias_tiles(rel_bias, blk, q_tiles):
    n = _near_tiles(q_tiles)
    d = jnp.arange(blk, dtype=jnp.int32)
    offs = (jnp.arange(n, dtype=jnp.int32) - n // 2) * blk
    rel = offs[:, None, None] + d[None, :, None] - d[None, None, :]
    bucket = _t5_bucket(rel)
    return pl.pallas_call(
        _bias_tile_kernel,
        out_shape=jax.ShapeDtypeStruct((DIFF_HEADS, n, blk, blk), F32),
        grid=(DIFF_HEADS, n),
        in_specs=[
            pl.BlockSpec((1, blk, blk), lambda hd, t: (t, 0, 0)),
            pl.BlockSpec(memory_space=pltpu.SMEM),
        ],
        out_specs=pl.BlockSpec((1, 1, blk, blk), lambda hd, t: (hd, t, 0, 0)),
        compiler_params=_params(("arbitrary", "arbitrary")), name="bias_tiles",
    )(bucket, rel_bias)


class _FlashState:
    def __init__(self, q_ref, k_ref, v_ref, scratch, blk, q_tiles):
        self.q_ref, self.k_ref, self.v_ref = q_ref, k_ref, v_ref
        (self.s_ref, self.p_ref, self.alpha_ref, self.m_ref, self.l_ref, self.acc_ref,
         self.kmax_ref) = scratch
        self.blk = blk
        self.streams = [(qt, mp) for qt in range(q_tiles) for mp in range(q_ref.shape[2])]
        self.nc = v_ref.shape[2]

    def q(self, qt, mp):
        return self.q_ref[0, 0, mp, :, qt * self.blk:(qt + 1) * self.blk]

    def key_chunk(self, c):
        return self.k_ref[0, 0, pl.ds(pl.multiple_of(c * self.blk, self.blk), self.blk), :]

    def update_key_norm(self):
        def body(c, mx):
            kc = self.key_chunk(c).astype(F32)
            n2 = jnp.sum(kc * kc, axis=-1, keepdims=True)
            return jnp.maximum(mx, jnp.max(n2, axis=0, keepdims=True))

        mx = lax.fori_loop(0, self.nc, body, jnp.zeros((1, 1), F32))
        self.kmax_ref[...] = jnp.broadcast_to(mx, self.kmax_ref.shape)

    def score_bound(self, extra):
        kmax2 = self.kmax_ref[:, 0:1]
        worst = None
        for sx, (qt, mp) in enumerate(self.streams):
            q = self.q(qt, mp).astype(F32)
            bound = jnp.sqrt(jnp.sum(q * q, axis=0, keepdims=True) * kmax2) + extra
            self.m_ref[sx] = bound
            tile_max = jnp.max(bound)
            worst = tile_max if worst is None else jnp.maximum(worst, tile_max)
        return worst


    def init_bounded(self):
        self.l_ref[...] = jnp.zeros(self.l_ref.shape, F32)
        self.acc_ref[...] = jnp.zeros(self.acc_ref.shape, F32)
        self.p_ref[1] = jnp.zeros(self.p_ref.shape[1:], BF16)

    def pv_bounded(self, c, half):
        vc = self.v_ref[0, 0, c]
        for sx, (qt, mp) in enumerate(self.streams):
            self.acc_ref[sx] += jnp.dot(vc, self.p_ref[half, sx], preferred_element_type=F32)

    def step_bounded(self, c, half, cst=None, bias_tiles=None):
        self.pv_bounded(jnp.maximum(c - 1, 0), 1 - half)
        kc = self.key_chunk(c)
        for sx, (qt, mp) in enumerate(self.streams):
            s = jnp.dot(kc, self.q(qt, mp), preferred_element_type=F32)
            if bias_tiles is not None:
                s = s + bias_tiles[qt]
            shift = self.m_ref[sx] if cst is None else self.m_ref[sx] - cst
            p = jnp.exp2(s - shift)
            self.l_ref[sx] += jnp.sum(p, axis=0, keepdims=True)
            self.p_ref[half, sx] = p.astype(BF16)

    def finish_bounded(self):
        self.pv_bounded(self.nc - 1, 1)


    def init_online(self):
        self.m_ref[...] = jnp.full(self.m_ref.shape, NEG_BIG, F32)
        self.l_ref[...] = jnp.zeros(self.l_ref.shape, F32)
        self.acc_ref[...] = jnp.zeros(self.acc_ref.shape, F32)
        self.p_ref[0] = jnp.zeros(self.p_ref.shape[1:], BF16)
        self.alpha_ref[...] = jnp.ones(self.alpha_ref.shape, F32)
        self.qk(0, 0)

    def qk(self, c, half):
        kc = self.key_chunk(c)
        for sx, (qt, mp) in enumerate(self.streams):
            self.s_ref[half, sx] = jnp.dot(kc, self.q(qt, mp), preferred_element_type=F32)

    def pv_online(self, c):
        vc = self.v_ref[0, 0, c]
        for sx, (qt, mp) in enumerate(self.streams):
            self.acc_ref[sx] = (self.alpha_ref[sx] * self.acc_ref[sx]
                                + jnp.dot(vc, self.p_ref[0, sx], preferred_element_type=F32))

    def softmax_online(self, half, cst, bias_tiles):
        for sx, (qt, mp) in enumerate(self.streams):
            s = self.s_ref[half, sx]
            if bias_tiles is not None:
                s = s + bias_tiles[qt]
            mc = jnp.max(s, axis=0, keepdims=True)
            if cst is not None:
                mc = mc + cst
            m_old = self.m_ref[sx]
            m_new = jnp.maximum(m_old, mc)
            alpha = jnp.exp2(m_old - m_new)
            shift = m_new if cst is None else m_new - cst
            p = jnp.exp2(s - shift)
            self.l_ref[sx] = alpha * self.l_ref[sx] + jnp.sum(p, axis=0, keepdims=True)
            self.p_ref[0, sx] = p.astype(BF16)
            self.alpha_ref[sx] = alpha
            self.m_ref[sx] = m_new

    def step_online(self, c, half, cst=None, bias_tiles=None):
        self.pv_online(jnp.maximum(c - 1, 0))
        self.qk(jnp.minimum(c + 1, self.nc - 1), 1 - half)
        self.softmax_online(half, cst, bias_tiles)

    def finish_online(self):
        self.pv_online(self.nc - 1)

    def run(self, extra, schedule):
        use_bounded = self.score_bound(extra) <= BOUNDED_SOFTMAX_LIMIT

        @pl.when(use_bounded)
        def _():
            self.init_bounded()
            schedule(self.step_bounded)
            self.finish_bounded()

        @pl.when(jnp.logical_not(use_bounded))
        def _():
            self.init_online()
            schedule(self.step_online)
            self.finish_online()


def _flash_scratch(n_maps, head_dim, blk):
    return [
        pltpu.VMEM((2, n_maps, blk, blk), F32),
        pltpu.VMEM((2, n_maps, blk, blk), BF16),
        pltpu.VMEM((n_maps, 1, blk), F32),
        pltpu.VMEM((n_maps, 1, blk), F32),
        pltpu.VMEM((n_maps, 1, blk), F32),
        pltpu.VMEM((n_maps, head_dim, blk), F32),
        pltpu.VMEM((1, LANES), F32),
    ]


def _mla_attn_kernel(q_ref, k_ref, v_ref, o_ref, *scratch, blk, q_tiles, unroll):
    st = _FlashState(q_ref, k_ref, v_ref, scratch, blk, q_tiles)

    @pl.when(pl.program_id(2) == 0)
    def _():
        st.update_key_norm()

    def schedule(step):
        def group(cc, carry):
            for u in range(unroll):
                step(unroll * cc + u, u % 2)
            return carry

        lax.fori_loop(0, st.nc // unroll, group, 0)

    st.run(0.0, schedule)
    for sx, (qt, _) in enumerate(st.streams):
        o_t = st.acc_ref[sx] / st.l_ref[sx]
        o_ref[qt * blk:(qt + 1) * blk, :] = o_t.T.astype(BF16)


def _mla_attention(q, k, v, gb, seq, blk, q_tiles):
    nc = seq // blk
    nq = nc // q_tiles
    unroll = 4 if nc % 4 == 0 else 2
    assert nc % unroll == 0 and nc % q_tiles == 0
    kern = functools.partial(_mla_attn_kernel, blk=blk, q_tiles=q_tiles, unroll=unroll)
    q = q.reshape(gb, MLA_HEADS, 1, MLA_QK_PAD, seq)
    tq = q_tiles * blk
    return pl.pallas_call(
        kern,
        out_shape=jax.ShapeDtypeStruct((gb * seq, HEAD_OUT), BF16),
        grid=(gb, MLA_HEADS, nq),
        in_specs=[
            pl.BlockSpec((1, 1, 1, MLA_QK_PAD, tq), lambda b, hd, i: (b, hd, 0, 0, i)),
            pl.BlockSpec((1, 1, seq, MLA_QK_PAD), lambda b, hd, i: (b, hd, 0, 0)),
            pl.BlockSpec((1, 1, nc, MLA_V, blk), lambda b, hd, i: (b, hd, 0, 0, 0)),
        ],
        out_specs=pl.BlockSpec((tq, MLA_V), lambda b, hd, i: (b * nq + i, hd)),
        scratch_shapes=_flash_scratch(q_tiles, MLA_V, blk),
        compiler_params=_params(("arbitrary", "arbitrary", "arbitrary")), name="mla_attn",
    )(q, k, v)


def _diff_attn_kernel(q_ref, k_ref, v_ref, bias_ref, rb_ref, lam_ref, g_sub_ref, o_ref,
                      *scratch, blk, q_tiles, lam_init):
    hd = pl.program_id(1)
    i_first = pl.program_id(2) * q_tiles
    st = _FlashState(q_ref, k_ref, v_ref, scratch, blk, q_tiles)
    acc_ref, l_ref = st.acc_ref, st.l_ref
    n_pairs = st.nc // 2

    @pl.when(pl.program_id(2) == 0)
    def _():
        st.update_key_norm()

    bias_max = jnp.abs(rb_ref[0, hd])
    for b in range(1, REL_BUCKETS):
        bias_max = jnp.maximum(bias_max, jnp.abs(rb_ref[b, hd]))

    far_before = rb_ref[REL_BUCKETS // 2 - 1, hd] * LOG2E
    far_after = rb_ref[REL_BUCKETS - 1, hd] * LOG2E
    first_near = (i_first + 1) // 2 - 1
    last_near = (i_first + q_tiles) // 2

    near_lo = jnp.maximum(first_near, 0)
    near_hi = jnp.minimum(last_near + 1, n_pairs)

    def schedule(step):
        def far_pair(cst):
            def body(cc, carry):
                step(2 * cc, 0, cst=cst)
                step(2 * cc + 1, 1, cst=cst)
                return carry
            return body

        def near_pair(cc, carry):
            for half in range(2):
                c = 2 * cc + half
                tiles = [bias_ref[0, c - (i_first + qt) + q_tiles + 1] for qt in range(q_tiles)]
                step(c, half, bias_tiles=tiles)
            return carry

        lax.fori_loop(0, near_lo, far_pair(far_before), 0)
        lax.fori_loop(near_lo, near_hi, near_pair, 0)
        lax.fori_loop(near_hi, n_pairs, far_pair(far_after), 0)

    st.run(bias_max * LOG2E, schedule)

    lam_v = lam_ref[...]
    lam = (jnp.exp(jnp.sum(lam_v[0:1] * lam_v[1:2], axis=-1, keepdims=True))
           - jnp.exp(jnp.sum(lam_v[2:3] * lam_v[3:4], axis=-1, keepdims=True)) + lam_init)
    for qt in range(q_tiles):
        s0, s1 = 2 * qt, 2 * qt + 1
        o_t = acc_ref[s0] / l_ref[s0] - lam * (acc_ref[s1] / l_ref[s1])
        o_t = o_t * lax.rsqrt(jnp.mean(o_t * o_t, axis=0, keepdims=True) + EPS) * g_sub_ref[...]
        o_ref[qt * blk:(qt + 1) * blk, :] = (o_t * (1.0 - lam_init)).T.astype(BF16)


def _diff_attention(q, k, v, bias, rel_bias, lam_rows, g_sub, gb, seq, blk, q_tiles, lam_init):
    nc = seq // blk
    nq = nc // q_tiles
    assert nc % 2 == 0 and nc % q_tiles == 0 and blk >= REL_MAX_DIST
    assert bias.shape[1] == _near_tiles(q_tiles)
    kern = functools.partial(_diff_attn_kernel, blk=blk, q_tiles=q_tiles, lam_init=lam_init)
    tq = q_tiles * blk
    return pl.pallas_call(
        kern,
        out_shape=jax.ShapeDtypeStruct((gb * seq, HEAD_OUT), BF16),
        grid=(gb, DIFF_HEADS, nq),
        in_specs=[
            pl.BlockSpec((1, 1, 2, DIFF_V, tq), lambda b, hd, i: (b, hd, 0, 0, i)),
            pl.BlockSpec((1, 1, seq, DIFF_V), lambda b, hd, i: (b, hd, 0, 0)),
            pl.BlockSpec((1, 1, nc, DIFF_V, blk), lambda b, hd, i: (b, hd, 0, 0, 0)),
            pl.BlockSpec((1, bias.shape[1], blk, blk), lambda b, hd, i: (hd, 0, 0, 0),
                         pipeline_mode=pl.Buffered(1)),
            pl.BlockSpec(memory_space=pltpu.SMEM),
            _const_spec((4, LANES)),
            _const_spec((DIFF_V, 1)),
        ],
        out_specs=pl.BlockSpec((tq, DIFF_V), lambda b, hd, i: (b * nq + i, hd)),
        scratch_shapes=_flash_scratch(2 * q_tiles, DIFF_V, blk),
        compiler_params=_params(("arbitrary", "arbitrary", "arbitrary")), name="diff_attn",
    )(q, k, v, bias, rel_bias, lam_rows, g_sub)


def _merge_kernel(x_ref, g_mix_ref, oa_ref, ob_ref, wga_ref, wgb_ref, wua_ref, wub_ref,
                  out_ref, h_ref):
    @pl.when(pl.program_id(1) == 0)
    def _():
        h_ref[...] = _rms_rows(x_ref[...], g_mix_ref[...]).astype(BF16)

    h = h_ref[...]
    ga = jnp.dot(h, wga_ref[...], preferred_element_type=F32)
    gb_ = jnp.dot(h, wgb_ref[...], preferred_element_type=F32)
    ua = jnp.dot(oa_ref[...], wua_ref[...], preferred_element_type=F32)
    ub = jnp.dot(ob_ref[...], wub_ref[...], preferred_element_type=F32)
    out_ref[...] = (jax.nn.sigmoid(ga) * ua + jax.nn.sigmoid(gb_) * ub).astype(BF16)


def _merge(x2d, o_a, o_b, w, tm, tn):
    tokens = x2d.shape[0]
    return pl.pallas_call(
        _merge_kernel,
        out_shape=jax.ShapeDtypeStruct((tokens, D_MODEL), BF16),
        grid=(tokens // tm, D_MODEL // tn),
        in_specs=[
            pl.BlockSpec((tm, D_MODEL), lambda i, j: (i, 0)),
            _const_spec((1, D_MODEL)),
            pl.BlockSpec((tm, HEAD_OUT), lambda i, j: (i, 0)),
            pl.BlockSpec((tm, HEAD_OUT), lambda i, j: (i, 0)),
            pl.BlockSpec((D_MODEL, tn), lambda i, j: (0, j)),
            pl.BlockSpec((D_MODEL, tn), lambda i, j: (0, j)),
            pl.BlockSpec((HEAD_OUT, tn), lambda i, j: (0, j)),
            pl.BlockSpec((HEAD_OUT, tn), lambda i, j: (0, j)),
        ],
        out_specs=pl.BlockSpec((tm, tn), lambda i, j: (i, j)),
        scratch_shapes=[pltpu.VMEM((tm, D_MODEL), BF16)],
        compiler_params=_params(("arbitrary", "arbitrary")), name="gated_merge",
    )(x2d, w["g_mix"], o_a, o_b, w["w_ga"], w["w_gb"], w["w_ua"], w["w_ub"])


def _out_proj_kernel(x_ref, m_ref, wo_ref, out_ref):
    out_ref[...] = x_ref[...] + jnp.dot(m_ref[...], wo_ref[...], preferred_element_type=F32)


def _out_proj(x2d, merged, w, tm):
    tokens = x2d.shape[0]
    return pl.pallas_call(
        _out_proj_kernel,
        out_shape=jax.ShapeDtypeStruct((tokens, D_MODEL), F32),
        grid=(tokens // tm,),
        in_specs=[
            pl.BlockSpec((tm, D_MODEL), lambda i: (i, 0)),
            pl.BlockSpec((tm, D_MODEL), lambda i: (i, 0)),
            _const_spec((D_MODEL, D_MODEL)),
        ],
        out_specs=pl.BlockSpec((tm, D_MODEL), lambda i: (i, 0)),
        compiler_params=_params(("arbitrary",)), name="out_proj",
    )(x2d, merged, w["w_o"])


def _ffn_kernel(x_ref, g_ref, wg_ref, wu_ref, wd_ref, out_ref, h_ref):
    @pl.when(pl.program_id(1) == 0)
    def _():
        x = x_ref[...]
        h_ref[...] = _rms_rows(x, g_ref[...]).astype(BF16)
        out_ref[...] = x

    h = h_ref[...]
    g = jnp.dot(h, wg_ref[...], preferred_element_type=F32)
    u = jnp.dot(h, wu_ref[...], preferred_element_type=F32)
    a = (g * jax.nn.sigmoid(g) * u).astype(BF16)
    out_ref[...] += jnp.dot(a, wd_ref[...], preferred_element_type=F32)


def _ffn(x2d, w, tm, tf):
    tokens = x2d.shape[0]
    return pl.pallas_call(
        _ffn_kernel,
        out_shape=jax.ShapeDtypeStruct((tokens, D_MODEL), F32),
        grid=(tokens // tm, D_FF // tf),
        in_specs=[
            pl.BlockSpec((tm, D_MODEL), lambda i, j: (i, 0)),
            _const_spec((1, D_MODEL)),
            pl.BlockSpec((D_MODEL, tf), lambda i, j: (0, j)),
            pl.BlockSpec((D_MODEL, tf), lambda i, j: (0, j)),
            pl.BlockSpec((tf, D_MODEL), lambda i, j: (j, 0)),
        ],
        out_specs=pl.BlockSpec((tm, D_MODEL), lambda i, j: (i, 0)),
        scratch_shapes=[pltpu.VMEM((tm, D_MODEL), BF16)],
        compiler_params=_params(("arbitrary", "arbitrary")), name="swiglu_ffn",
    )(x2d, w["g_ffn"], w["w_gate"], w["w_up"], w["w_down"])


def _prepare_weights(mix_norm, w_in, q_a_norm, wq_b, kv_a_norm, wkv_b, mla_q_norm, mla_k_norm,
                     diff_q_norm, diff_k_norm, diff_subln, w_up_mla, w_up_diff, w_o, ffn_norm,
                     w_gate, w_up, w_down, layer):
    win = w_in[layer]
    o_cq, o_ckv, o_kpe = 0, Q_LORA, Q_LORA + KV_LORA
    o_dq = o_kpe + MLA_ROPE
    o_dk = o_dq + HEAD_OUT
    o_dv = o_dk + HEAD_OUT
    o_ga = o_dv + HEAD_OUT
    o_gb = o_ga + D_MODEL
    w_kpe = win[:, o_kpe:o_dq]
    w_kpe_rot = jnp.concatenate([-w_kpe[:, ROPE_HALF:], w_kpe[:, :ROPE_HALF]], axis=1)
    zpad = jnp.zeros((D_MODEL, LANES - MLA_ROPE), F32)
    w_a = jnp.concatenate([win[:, o_cq:o_kpe], w_kpe, zpad, w_kpe_rot, zpad], axis=1)
    wkv = wkv_b[layer].reshape(KV_LORA, MLA_HEADS, MLA_NOPE + MLA_V)
    gk = mla_k_norm[layer]
    gk_rope = gk[MLA_NOPE:]
    lane_pad = jnp.zeros((LANES - MLA_ROPE,), F32)
    return {
        "g_mix": mix_norm[layer][None, :],
        "w_a": w_a.astype(BF16),
        "g_qa": q_a_norm[layer][None, :],
        "g_kva": kv_a_norm[layer][None, :],
        "wq_t": wq_b[layer].T.astype(BF16),
        "g_q": mla_q_norm[layer][:, None],
        "wk_nope": wkv[:, :, :MLA_NOPE].reshape(KV_LORA, MLA_HEADS * MLA_NOPE).astype(BF16),
        "wv_t": wkv[:, :, MLA_NOPE:].reshape(KV_LORA, MLA_HEADS * MLA_V).T.astype(BF16),
        "g_kn": gk[None, :MLA_NOPE],
        "g_kr": jnp.concatenate([gk_rope, lane_pad])[None, :],
        "g_krot": jnp.concatenate([gk_rope[ROPE_HALF:], gk_rope[:ROPE_HALF], lane_pad])[None, :],
        "w_dq_t": win[:, o_dq:o_dk].T.astype(BF16),
        "w_dk_t": win[:, o_dk:o_dv].T.astype(BF16),
        "w_dv_t": win[:, o_dv:o_ga].T.astype(BF16),
        "g_dq": diff_q_norm[layer][:, None],
        "g_dk": diff_k_norm[layer][:, None],
        "g_sub": diff_subln[layer][:, None],
        "w_ga": win[:, o_ga:o_gb].astype(BF16),
        "w_gb": win[:, o_gb:].astype(BF16),
        "w_ua": w_up_mla[layer].astype(BF16),
        "w_ub": w_up_diff[layer].astype(BF16),
        "w_o": w_o[layer].astype(BF16),
        "g_ffn": ffn_norm[layer][None, :],
        "w_gate": w_gate[layer].astype(BF16),
        "w_up": w_up[layer].astype(BF16),
        "w_down": w_down[layer].astype(BF16),
    }


def _encoder_layer(x, w, tables, bias, rel_bias, lam_rows, lam_init, *, tm, blk, tn, tf,
                   mla_q_tiles, diff_q_tiles):
    gb, seq, _ = x.shape
    x2d = x.reshape(gb * seq, D_MODEL)
    q_a, k_a, v_a = _mla_prep(x2d, gb, seq, tm, blk, w, tables)
    q_d, k_d, v_d = _diff_prep(x2d, gb, seq, tm, blk, w)
    o_a = _mla_attention(q_a, k_a, v_a, gb, seq, blk, mla_q_tiles)
    o_b = _diff_attention(q_d, k_d, v_d, bias, rel_bias, lam_rows, w["g_sub"], gb, seq, blk,
                          diff_q_tiles, lam_init)
    merged = _merge(x2d, o_a, o_b, w, tm, tn)
    x1 = _out_proj(x2d, merged, w, tm)
    y = _ffn(x1, w, tm, tf)
    return y.reshape(gb, seq, D_MODEL)


def kernel(x_prompt, x_sample, mix_norm, w_in, q_a_norm, wq_b, kv_a_norm, wkv_b, mla_q_norm, mla_k_norm, diff_q_norm, diff_k_norm, lambda_q1, lambda_k1, lambda_q2, lambda_k2, diff_subln, w_up_mla, w_up_diff, w_o, ffn_norm, w_gate, w_up, w_down, rel_bias):
    tm, blk, tn, tf = 512, 512, 512, 512
    mla_q_tiles, diff_q_tiles = 2, 2
    depth = w_in.shape[0]
    max_seq = max(x_prompt.shape[1], x_sample.shape[1])
    tables = _rope_tables(max_seq)
    bias = _bias_tiles(rel_bias, blk, diff_q_tiles)
    y_prompt, y_sample = x_prompt, x_sample
    for layer in range(depth):
        w = _prepare_weights(mix_norm, w_in, q_a_norm, wq_b, kv_a_norm, wkv_b, mla_q_norm,
                             mla_k_norm, diff_q_norm, diff_k_norm, diff_subln, w_up_mla,
                             w_up_diff, w_o, ffn_norm, w_gate, w_up, w_down, layer)
        lam_init = 0.8 - 0.6 * math.exp(-0.3 * layer)
        lam_pad = jnp.zeros((LANES - DIFF_QK,), F32)
        lam_rows = jnp.stack([jnp.concatenate([v[layer], lam_pad])
                              for v in (lambda_q1, lambda_k1, lambda_q2, lambda_k2)])
        run = functools.partial(_encoder_layer, w=w, tables=tables, bias=bias, rel_bias=rel_bias,
                                lam_rows=lam_rows, lam_init=lam_init, tm=tm, blk=blk, tn=tn, tf=tf,
                                mla_q_tiles=mla_q_tiles, diff_q_tiles=diff_q_tiles)
        y_prompt = run(y_prompt)
        y_sample = run(y_sample)
    return (y_prompt, y_sample)
```

```python
import functools
import math

import jax
import jax.numpy as jnp
from jax import lax
from jax.experimental import pallas as pl
from jax.experimental.pallas import tpu as pltpu

D_MODEL = 2048
MLA_HEADS = 8
MLA_NOPE = 128
MLA_ROPE = 64
MLA_QK = MLA_NOPE + MLA_ROPE
MLA_V = 128
Q_LORA = 512
KV_LORA = 256
ROPE_THETA = 10000.0
ROPE_HALF = MLA_ROPE // 2
DIFF_HEADS = 8
DIFF_QK = 64
DIFF_V = 2 * DIFF_QK
REL_BUCKETS = 32
REL_MAX_DIST = 128
D_FF = 5632
EPS = 1e-6
HEAD_OUT = MLA_HEADS * MLA_V

LANES = 128
MLA_QK_PAD = 2 * LANES
VMEM_LIMIT = 56 * 1024 * 1024

F32 = jnp.float32
BF16 = jnp.bfloat16
NEG_BIG = -1e30
LOG2E = math.log2(math.e)
BOUNDED_SOFTMAX_LIMIT = 50.0

NT_DIMS = (((1,), (1,)), ((), ()))


def _params(semantics):
    return pltpu.CompilerParams(dimension_semantics=semantics, vmem_limit_bytes=VMEM_LIMIT)


def _const_spec(shape):
    zeros = (0,) * len(shape)
    return pl.BlockSpec(shape, lambda *_: zeros)


def _rms_rows(x, gain):
    return x * lax.rsqrt(jnp.mean(x * x, axis=-1, keepdims=True) + EPS) * gain


def _rope_table_kernel(inv_row_ref, inv_col_ref, cos_t_ref, sin_t_ref, cos_f_ref, sin_f_ref):
    ts = cos_t_ref.shape[0]
    base = pl.program_id(0) * ts
    pos_rows = (base + lax.broadcasted_iota(jnp.int32, (ts, LANES), 0)).astype(F32)
    ang_t = pos_rows * inv_row_ref[...]
    cos_t_ref[...] = jnp.cos(ang_t)
    sin_t_ref[...] = jnp.sin(ang_t)
    pos_cols = (base + lax.broadcasted_iota(jnp.int32, (ROPE_HALF, ts), 1)).astype(F32)
    ang_f = pos_cols * inv_col_ref[...]
    cos_f_ref[...] = jnp.cos(ang_f)
    sin_f_ref[...] = jnp.sin(ang_f)


def _rope_tables(seq):
    inv = ROPE_THETA ** (-jnp.arange(ROPE_HALF, dtype=F32) / ROPE_HALF)
    inv_row = jnp.tile(inv, LANES // ROPE_HALF)[None, :]
    inv_col = inv[:, None]
    ts = min(seq, 2048)
    tok = jax.ShapeDtypeStruct((seq, LANES), F32)
    feat = jax.ShapeDtypeStruct((ROPE_HALF, seq), F32)
    return pl.pallas_call(
        _rope_table_kernel,
        out_shape=(tok, tok, feat, feat),
        grid=(seq // ts,),
        in_specs=[_const_spec((1, LANES)), _const_spec((ROPE_HALF, 1))],
        out_specs=(
            pl.BlockSpec((ts, LANES), lambda i: (i, 0)),
            pl.BlockSpec((ts, LANES), lambda i: (i, 0)),
            pl.BlockSpec((ROPE_HALF, ts), lambda i: (0, i)),
            pl.BlockSpec((ROPE_HALF, ts), lambda i: (0, i)),
        ),
        compiler_params=_params(("arbitrary",)), name="rope_tables",
    )(inv_row, inv_col)


def _mla_prep_kernel(x_ref, g_mix_ref, w_a_ref, g_qa_ref, g_kva_ref, wq_t_ref, g_q_ref,
                     wk_ref, wv_t_ref, g_kn_ref, g_kr_ref, g_krot_ref,
                     cos_t_ref, sin_t_ref, cos_f_ref, sin_f_ref,
                     q_ref, k_ref, v_ref, *, tk):
    tm = x_ref.shape[0]
    h = _rms_rows(x_ref[...], g_mix_ref[...]).astype(BF16)
    c = jnp.dot(h, w_a_ref[...], preferred_element_type=F32)
    cq = _rms_rows(c[:, :Q_LORA], g_qa_ref[...]).astype(BF16)
    ckv = _rms_rows(c[:, Q_LORA:Q_LORA + KV_LORA], g_kva_ref[...]).astype(BF16)
    k_pe = c[:, 6 * LANES:7 * LANES]
    k_pe_rot = c[:, 7 * LANES:8 * LANES]

    q_t = lax.dot_general(wq_t_ref[...], cq, NT_DIMS, preferred_element_type=F32)
    cos_f = cos_f_ref[...]
    sin_f = sin_f_ref[...]
    scale = MLA_QK ** -0.5 * LOG2E
    g_q = g_q_ref[...]
    for hd in range(MLA_HEADS):
        xh = q_t[hd * MLA_QK:(hd + 1) * MLA_QK]
        rinv = lax.rsqrt(jnp.mean(xh * xh, axis=0, keepdims=True) + EPS)
        xn = xh * rinv * g_q
        x1 = xn[MLA_NOPE:MLA_NOPE + ROPE_HALF]
        x2 = xn[MLA_NOPE + ROPE_HALF:]
        q_ref[0, hd, 0:MLA_NOPE, :] = (xn[:MLA_NOPE] * scale).astype(BF16)
        q_ref[0, hd, MLA_NOPE:MLA_NOPE + ROPE_HALF, :] = ((x1 * cos_f - x2 * sin_f) * scale).astype(BF16)
        q_ref[0, hd, MLA_NOPE + ROPE_HALF:MLA_QK, :] = ((x2 * cos_f + x1 * sin_f) * scale).astype(BF16)
        q_ref[0, hd, MLA_QK:, :] = jnp.zeros((MLA_QK_PAD - MLA_QK, tm), BF16)

    k_nope = jnp.dot(ckv, wk_ref[...], preferred_element_type=F32)
    rope_base = (k_pe * g_kr_ref[...]) * cos_t_ref[...] + (k_pe_rot * g_krot_ref[...]) * sin_t_ref[...]
    ss_pe = jnp.sum(k_pe * k_pe, axis=-1, keepdims=True)
    for hd in range(MLA_HEADS):
        kh = k_nope[:, hd * MLA_NOPE:(hd + 1) * MLA_NOPE]
        ss = jnp.sum(kh * kh, axis=-1, keepdims=True) + ss_pe
        rinv = lax.rsqrt(ss * (1.0 / MLA_QK) + EPS)
        k_ref[0, hd, :, 0:LANES] = (kh * rinv * g_kn_ref[...]).astype(BF16)
        k_ref[0, hd, :, LANES:] = (rope_base * rinv).astype(BF16)

    v_t = lax.dot_general(wv_t_ref[...], ckv, NT_DIMS, preferred_element_type=F32)
    for hd in range(MLA_HEADS):
        for cc in range(tm // tk):
            v_ref[0, hd, cc] = v_t[hd * MLA_V:(hd + 1) * MLA_V, cc * tk:(cc + 1) * tk].astype(BF16)


def _mla_prep(x2d, gb, seq, tm, tk, w, tables):
    tokens = x2d.shape[0]
    nt = seq // tm
    cos_t, sin_t, cos_f, sin_f = tables
    kern = functools.partial(_mla_prep_kernel, tk=tk)
    out_shape = (
        jax.ShapeDtypeStruct((gb, MLA_HEADS, MLA_QK_PAD, seq), BF16),
        jax.ShapeDtypeStruct((gb, MLA_HEADS, seq, MLA_QK_PAD), BF16),
        jax.ShapeDtypeStruct((gb, MLA_HEADS, seq // tk, MLA_V, tk), BF16),
    )
    in_specs = [
        pl.BlockSpec((tm, D_MODEL), lambda i: (i, 0)),
        _const_spec((1, D_MODEL)),
        _const_spec((D_MODEL, 8 * LANES)),
        _const_spec((1, Q_LORA)),
        _const_spec((1, KV_LORA)),
        _const_spec((MLA_HEADS * MLA_QK, Q_LORA)),
        _const_spec((MLA_QK, 1)),
        _const_spec((KV_LORA, MLA_HEADS * MLA_NOPE)),
        _const_spec((MLA_HEADS * MLA_V, KV_LORA)),
        _const_spec((1, LANES)),
        _const_spec((1, LANES)),
        _const_spec((1, LANES)),
        pl.BlockSpec((tm, LANES), lambda i: (i % nt, 0)),
        pl.BlockSpec((tm, LANES), lambda i: (i % nt, 0)),
        pl.BlockSpec((ROPE_HALF, tm), lambda i: (0, i % nt)),
        pl.BlockSpec((ROPE_HALF, tm), lambda i: (0, i % nt)),
    ]
    out_specs = (
        pl.BlockSpec((1, MLA_HEADS, MLA_QK_PAD, tm), lambda i: (i // nt, 0, 0, i % nt)),
        pl.BlockSpec((1, MLA_HEADS, tm, MLA_QK_PAD), lambda i: (i // nt, 0, i % nt, 0)),
        pl.BlockSpec((1, MLA_HEADS, tm // tk, MLA_V, tk), lambda i: (i // nt, 0, i % nt, 0, 0)),
    )
    return pl.pallas_call(
        kern, out_shape=out_shape, grid=(tokens // tm,), in_specs=in_specs, out_specs=out_specs,
        compiler_params=_params(("arbitrary",)), name="mla_prep",
    )(x2d, w["g_mix"], w["w_a"], w["g_qa"], w["g_kva"], w["wq_t"], w["g_q"], w["wk_nope"],
      w["wv_t"], w["g_kn"], w["g_kr"], w["g_krot"], cos_t, sin_t, cos_f, sin_f)


def _group_norm_cols(x_t, gain_col):
    rows, tm = x_t.shape
    x3 = x_t.reshape(rows // DIFF_QK, DIFF_QK, tm)
    rinv = lax.rsqrt(jnp.mean(x3 * x3, axis=1, keepdims=True) + EPS)
    return (x3 * rinv * gain_col[None]).reshape(rows, tm)


def _diff_prep_kernel(x_ref, g_mix_ref, wq_t_ref, wk_t_ref, wv_t_ref, g_q_ref, g_k_ref,
                      q_ref, k_ref, v_ref, *, tk):
    tm = x_ref.shape[0]
    h = _rms_rows(x_ref[...], g_mix_ref[...]).astype(BF16)
    scale = DIFF_QK ** -0.5 * LOG2E
    q_t = lax.dot_general(wq_t_ref[...], h, NT_DIMS, preferred_element_type=F32)
    qn = (_group_norm_cols(q_t, g_q_ref[...]) * scale).astype(BF16)
    zeros = jnp.zeros((DIFF_QK, tm), BF16)
    for hd in range(DIFF_HEADS):
        r0 = hd * DIFF_V
        q_ref[0, hd, 0, 0:DIFF_QK, :] = qn[r0:r0 + DIFF_QK]
        q_ref[0, hd, 0, DIFF_QK:, :] = zeros
        q_ref[0, hd, 1, 0:DIFF_QK, :] = zeros
        q_ref[0, hd, 1, DIFF_QK:, :] = qn[r0 + DIFF_QK:r0 + DIFF_V]

    k_t = lax.dot_general(wk_t_ref[...], h, NT_DIMS, preferred_element_type=F32)
    kn = _group_norm_cols(k_t, g_k_ref[...]).T
    for hd in range(DIFF_HEADS):
        k_ref[0, hd] = kn[:, hd * DIFF_V:(hd + 1) * DIFF_V].astype(BF16)

    v_t = lax.dot_general(wv_t_ref[...], h, NT_DIMS, preferred_element_type=F32)
    for hd in range(DIFF_HEADS):
        for cc in range(tm // tk):
            v_ref[0, hd, cc] = v_t[hd * DIFF_V:(hd + 1) * DIFF_V, cc * tk:(cc + 1) * tk].astype(BF16)


def _diff_prep(x2d, gb, seq, tm, tk, w):
    tokens = x2d.shape[0]
    nt = seq // tm
    kern = functools.partial(_diff_prep_kernel, tk=tk)
    out_shape = (
        jax.ShapeDtypeStruct((gb, DIFF_HEADS, 2, DIFF_V, seq), BF16),
        jax.ShapeDtypeStruct((gb, DIFF_HEADS, seq, DIFF_V), BF16),
        jax.ShapeDtypeStruct((gb, DIFF_HEADS, seq // tk, DIFF_V, tk), BF16),
    )
    in_specs = [
        pl.BlockSpec((tm, D_MODEL), lambda i: (i, 0)),
        _const_spec((1, D_MODEL)),
        _const_spec((HEAD_OUT, D_MODEL)),
        _const_spec((HEAD_OUT, D_MODEL)),
        _const_spec((HEAD_OUT, D_MODEL)),
        _const_spec((DIFF_QK, 1)),
        _const_spec((DIFF_QK, 1)),
    ]
    out_specs = (
        pl.BlockSpec((1, DIFF_HEADS, 2, DIFF_V, tm), lambda i: (i // nt, 0, 0, 0, i % nt)),
        pl.BlockSpec((1, DIFF_HEADS, tm, DIFF_V), lambda i: (i // nt, 0, i % nt, 0)),
        pl.BlockSpec((1, DIFF_HEADS, tm // tk, DIFF_V, tk), lambda i: (i // nt, 0, i % nt, 0, 0)),
    )
    return pl.pallas_call(
        kern, out_shape=out_shape, grid=(tokens // tm,), in_specs=in_specs, out_specs=out_specs,
        compiler_params=_params(("arbitrary",)), name="diff_prep",
    )(x2d, w["g_mix"], w["w_dq_t"], w["w_dk_t"], w["w_dv_t"], w["g_dq"], w["g_dk"])


def _bias_tile_kernel(bucket_ref, rb_ref, out_ref):
    hd = pl.program_id(0)
    offset = pl.program_id(1) - pl.num_programs(1) // 2

    @pl.when(offset <= -2)
    def _():
        out_ref[0, 0] = jnp.full(out_ref.shape[2:], rb_ref[REL_BUCKETS // 2 - 1, hd] * LOG2E, F32)

    @pl.when(offset >= 2)
    def _():
        out_ref[0, 0] = jnp.full(out_ref.shape[2:], rb_ref[REL_BUCKETS - 1, hd] * LOG2E, F32)

    @pl.when(jnp.abs(offset) < 2)
    def _():
        bucket = bucket_ref[0]
        acc = jnp.zeros(bucket.shape, F32)
        for b in range(REL_BUCKETS):
            acc = jnp.where(bucket == b, rb_ref[b, hd], acc)
        out_ref[0, 0] = acc * LOG2E


def _t5_bucket(rel):
    nb = REL_BUCKETS // 2
    ret = jnp.where(rel > 0, nb, 0)
    n = jnp.abs(rel)
    max_exact = nb // 2
    nf = jnp.maximum(n, 1).astype(F32)
    large = max_exact + (jnp.log(nf / max_exact) / math.log(REL_MAX_DIST / max_exact)
                         * (nb - max_exact)).astype(jnp.int32)
    large = jnp.minimum(large, nb - 1)
    return ret + jnp.where(n < max_exact, n, large)


def _near_tiles(q_tiles):
    return 2 * q_tiles + 3


def _bias_tiles(rel_bias, blk, q_tiles):
    assert blk >= REL_MAX_DIST
    n = _near_tiles(q_tiles)
    d = jnp.arange(blk, dtype=jnp.int32)
    offs = (jnp.arange(n, dtype=jnp.int32) - n // 2) * blk
    rel = offs[:, None, None] + d[None, :, None] - d[None, None, :]
    bucket = _t5_bucket(rel)
    return pl.pallas_call(
        _bias_tile_kernel,
        out_shape=jax.ShapeDtypeStruct((DIFF_HEADS, n, blk, blk), F32),
        grid=(DIFF_HEADS, n),
        in_specs=[
            pl.BlockSpec((1, blk, blk), lambda hd, t: (t, 0, 0)),
            pl.BlockSpec(memory_space=pltpu.SMEM),
        ],
        out_specs=pl.BlockSpec((1, 1, blk, blk), lambda hd, t: (hd, t, 0, 0)),
        compiler_params=_params(("arbitrary", "arbitrary")), name="bias_tiles",
    )(bucket, rel_bias)


class _FlashState:
    def __init__(self, q_ref, k_ref, v_ref, scratch, blk, q_tiles):
        self.q_ref, self.k_ref, self.v_ref = q_ref, k_ref, v_ref
        (self.s_ref, self.p_ref, self.alpha_ref, self.m_ref, self.l_ref, self.acc_ref,
         self.kmax_ref) = scratch
        self.blk = blk
        self.streams = [(qt, mp) for qt in range(q_tiles) for mp in range(q_ref.shape[2])]
        self.nc = v_ref.shape[2]

    def q(self, qt, mp):
        return self.q_ref[0, 0, mp, :, qt * self.blk:(qt + 1) * self.blk]

    def key_chunk(self, c):
        return self.k_ref[0, 0, pl.ds(pl.multiple_of(c * self.blk, self.blk), self.blk), :]

    def update_key_norm(self):
        def body(c, mx):
            kc = self.key_chunk(c).astype(F32)
            n2 = jnp.sum(kc * kc, axis=-1, keepdims=True)
            return jnp.maximum(mx, jnp.max(n2, axis=0, keepdims=True))

        mx = lax.fori_loop(0, self.nc, body, jnp.zeros((1, 1), F32))
        self.kmax_ref[...] = jnp.broadcast_to(mx, self.kmax_ref.shape)

    def score_bound(self, extra):
        kmax2 = self.kmax_ref[:, 0:1]
        worst = None
        for sx, (qt, mp) in enumerate(self.streams):
            q = self.q(qt, mp).astype(F32)
            bound = jnp.sqrt(jnp.sum(q * q, axis=0, keepdims=True) * kmax2) + extra
            self.m_ref[sx] = bound
            tile_max = jnp.max(bound)
            worst = tile_max if worst is None else jnp.maximum(worst, tile_max)
        return worst


    def init_bounded(self):
        self.l_ref[...] = jnp.zeros(self.l_ref.shape, F32)
        self.acc_ref[...] = jnp.zeros(self.acc_ref.shape, F32)
        self.p_ref[1] = jnp.zeros(self.p_ref.shape[1:], BF16)

    def pv_bounded(self, c, half):
        vc = self.v_ref[0, 0, c]
        for sx, (qt, mp) in enumerate(self.streams):
            self.acc_ref[sx] += jnp.dot(vc, self.p_ref[half, sx], preferred_element_type=F32)

    def step_bounded(self, c, half, cst=None, bias_tiles=None):
        self.pv_bounded(jnp.maximum(c - 1, 0), 1 - half)
        kc = self.key_chunk(c)
        for sx, (qt, mp) in enumerate(self.streams):
            s = jnp.dot(kc, self.q(qt, mp), preferred_element_type=F32)
            if bias_tiles is not None:
                s = s + bias_tiles[qt]
            shift = self.m_ref[sx] if cst is None else self.m_ref[sx] - cst
            p = jnp.exp2(s - shift)
            self.l_ref[sx] += jnp.sum(p, axis=0, keepdims=True)
            self.p_ref[half, sx] = p.astype(BF16)

    def finish_bounded(self):
        self.pv_bounded(self.nc - 1, 1)


    def init_online(self):
        self.m_ref[...] = jnp.full(self.m_ref.shape, NEG_BIG, F32)
        self.l_ref[...] = jnp.zeros(self.l_ref.shape, F32)
        self.acc_ref[...] = jnp.zeros(self.acc_ref.shape, F32)
        self.p_ref[0] = jnp.zeros(self.p_ref.shape[1:], BF16)
        self.alpha_ref[...] = jnp.ones(self.alpha_ref.shape, F32)
        self.qk(0, 0)

    def qk(self, c, half):
        kc = self.key_chunk(c)
        for sx, (qt, mp) in enumerate(self.streams):
            self.s_ref[half, sx] = jnp.dot(kc, self.q(qt, mp), preferred_element_type=F32)

    def pv_online(self, c):
        vc = self.v_ref[0, 0, c]
        for sx, (qt, mp) in enumerate(self.streams):
            self.acc_ref[sx] = (self.alpha_ref[sx] * self.acc_ref[sx]
                                + jnp.dot(vc, self.p_ref[0, sx], preferred_element_type=F32))

    def softmax_online(self, half, cst, bias_tiles):
        for sx, (qt, mp) in enumerate(self.streams):
            s = self.s_ref[half, sx]
            if bias_tiles is not None:
                s = s + bias_tiles[qt]
            mc = jnp.max(s, axis=0, keepdims=True)
            if cst is not None:
                mc = mc + cst
            m_old = self.m_ref[sx]
            m_new = jnp.maximum(m_old, mc)
            alpha = jnp.exp2(m_old - m_new)
            shift = m_new if cst is None else m_new - cst
            p = jnp.exp2(s - shift)
            self.l_ref[sx] = alpha * self.l_ref[sx] + jnp.sum(p, axis=0, keepdims=True)
            self.p_ref[0, sx] = p.astype(BF16)
            self.alpha_ref[sx] = alpha
            self.m_ref[sx] = m_new

    def step_online(self, c, half, cst=None, bias_tiles=None):
        self.pv_online(jnp.maximum(c - 1, 0))
        self.qk(jnp.minimum(c + 1, self.nc - 1), 1 - half)
        self.softmax_online(half, cst, bias_tiles)

    def finish_online(self):
        self.pv_online(self.nc - 1)

    def run(self, extra, schedule):
        use_bounded = self.score_bound(extra) <= BOUNDED_SOFTMAX_LIMIT

        @pl.when(use_bounded)
        def _():
            self.init_bounded()
            schedule(self.step_bounded)
            self.finish_bounded()

        @pl.when(jnp.logical_not(use_bounded))
        def _():
            self.init_online()
            schedule(self.step_online)
            self.finish_online()


def _flash_scratch(n_maps, head_dim, blk):
    return [
        pltpu.VMEM((2, n_maps, blk, blk), F32),
        pltpu.VMEM((2, n_maps, blk, blk), BF16),
        pltpu.VMEM((n_maps, 1, blk), F32),
        pltpu.VMEM((n_maps, 1, blk), F32),
        pltpu.VMEM((n_maps, 1, blk), F32),
        pltpu.VMEM((n_maps, head_dim, blk), F32),
        pltpu.VMEM((1, LANES), F32),
    ]


def _mla_attn_kernel(q_ref, k_ref, v_ref, o_ref, *scratch, blk, q_tiles, unroll):
    st = _FlashState(q_ref, k_ref, v_ref, scratch, blk, q_tiles)

    @pl.when(pl.program_id(2) == 0)
    def _():
        st.update_key_norm()

    def schedule(step):
        def group(cc, carry):
            for u in range(unroll):
                step(unroll * cc + u, u % 2)
            return carry

        lax.fori_loop(0, st.nc // unroll, group, 0)

    st.run(0.0, schedule)
    for sx, (qt, _) in enumerate(st.streams):
        o_t = st.acc_ref[sx] / st.l_ref[sx]
        o_ref[qt * blk:(qt + 1) * blk, :] = o_t.T.astype(BF16)


def _mla_attention(q, k, v, gb, seq, blk, q_tiles):
    nc = seq // blk
    nq = nc // q_tiles
    unroll = 4 if nc % 4 == 0 else 2
    assert nc % unroll == 0 and nc % q_tiles == 0
    kern = functools.partial(_mla_attn_kernel, blk=blk, q_tiles=q_tiles, unroll=unroll)
    q = q.reshape(gb, MLA_HEADS, 1, MLA_QK_PAD, seq)
    tq = q_tiles * blk
    return pl.pallas_call(
        kern,
        out_shape=jax.ShapeDtypeStruct((gb * seq, HEAD_OUT), BF16),
        grid=(gb, MLA_HEADS, nq),
        in_specs=[
            pl.BlockSpec((1, 1, 1, MLA_QK_PAD, tq), lambda b, hd, i: (b, hd, 0, 0, i)),
            pl.BlockSpec((1, 1, seq, MLA_QK_PAD), lambda b, hd, i: (b, hd, 0, 0)),
            pl.BlockSpec((1, 1, nc, MLA_V, blk), lambda b, hd, i: (b, hd, 0, 0, 0)),
        ],
        out_specs=pl.BlockSpec((tq, MLA_V), lambda b, hd, i: (b * nq + i, hd)),
        scratch_shapes=_flash_scratch(q_tiles, MLA_V, blk),
        compiler_params=_params(("arbitrary", "arbitrary", "arbitrary")), name="mla_attn",
    )(q, k, v)


def _diff_attn_kernel(q_ref, k_ref, v_ref, bias_ref, rb_ref, lam_ref, g_sub_ref, o_ref,
                      *scratch, blk, q_tiles, lam_init):
    hd = pl.program_id(0)
    i_first = pl.program_id(2) * q_tiles
    st = _FlashState(q_ref, k_ref, v_ref, scratch, blk, q_tiles)
    acc_ref, l_ref = st.acc_ref, st.l_ref
    n_pairs = st.nc // 2

    @pl.when(pl.program_id(2) == 0)
    def _():
        st.update_key_norm()

    bias_max = jnp.abs(rb_ref[0, hd])
    for b in range(1, REL_BUCKETS):
        bias_max = jnp.maximum(bias_max, jnp.abs(rb_ref[b, hd]))

    far_before = rb_ref[REL_BUCKETS // 2 - 1, hd] * LOG2E
    far_after = rb_ref[REL_BUCKETS - 1, hd] * LOG2E
    first_near = (i_first + 1) // 2 - 1
    last_near = (i_first + q_tiles) // 2

    near_lo = jnp.maximum(first_near, 0)
    near_hi = jnp.minimum(last_near + 1, n_pairs)

    def schedule(step):
        def far_pair(cst):
            def body(cc, carry):
                step(2 * cc, 0, cst=cst)
                step(2 * cc + 1, 1, cst=cst)
                return carry
            return body

        def near_pair(cc, carry):
            for half in range(2):
                c = 2 * cc + half
                tiles = [bias_ref[0, c - (i_first + qt) + q_tiles + 1] for qt in range(q_tiles)]
                step(c, half, bias_tiles=tiles)
            return carry

        lax.fori_loop(0, near_lo, far_pair(far_before), 0)
        lax.fori_loop(near_lo, near_hi, near_pair, 0)
        lax.fori_loop(near_hi, n_pairs, far_pair(far_after), 0)

    st.run(bias_max * LOG2E, schedule)

    lam_v = lam_ref[...]
    lam = (jnp.exp(jnp.sum(lam_v[0:1] * lam_v[1:2], axis=-1, keepdims=True))
           - jnp.exp(jnp.sum(lam_v[2:3] * lam_v[3:4], axis=-1, keepdims=True)) + lam_init)
    for qt in range(q_tiles):
        s0, s1 = 2 * qt, 2 * qt + 1
        o_t = acc_ref[s0] / l_ref[s0] - lam * (acc_ref[s1] / l_ref[s1])
        o_t = o_t * lax.rsqrt(jnp.mean(o_t * o_t, axis=0, keepdims=True) + EPS) * g_sub_ref[...]
        o_ref[qt * blk:(qt + 1) * blk, :] = (o_t * (1.0 - lam_init)).T.astype(BF16)


def _diff_attention(q, k, v, bias, rel_bias, lam_rows, g_sub, gb, seq, blk, q_tiles, lam_init):
    nc = seq // blk
    nq = nc // q_tiles
    assert nc % 2 == 0 and nc % q_tiles == 0 and blk >= REL_MAX_DIST
    assert bias.shape[1] == _near_tiles(q_tiles)
    kern = functools.partial(_diff_attn_kernel, blk=blk, q_tiles=q_tiles, lam_init=lam_init)
    tq = q_tiles * blk
    return pl.pallas_call(
        kern,
        out_shape=jax.ShapeDtypeStruct((gb * seq, HEAD_OUT), BF16),
        grid=(DIFF_HEADS, gb, nq),
        in_specs=[
            pl.BlockSpec((1, 1, 2, DIFF_V, tq), lambda hd, b, i: (b, hd, 0, 0, i)),
            pl.BlockSpec((1, 1, seq, DIFF_V), lambda hd, b, i: (b, hd, 0, 0)),
            pl.BlockSpec((1, 1, nc, DIFF_V, blk), lambda hd, b, i: (b, hd, 0, 0, 0)),
            pl.BlockSpec((1, bias.shape[1], blk, blk), lambda hd, b, i: (hd, 0, 0, 0),
                         pipeline_mode=pl.Buffered(1)),
            pl.BlockSpec(memory_space=pltpu.SMEM),
            _const_spec((4, LANES)),
            _const_spec((DIFF_V, 1)),
        ],
        out_specs=pl.BlockSpec((tq, DIFF_V), lambda hd, b, i: (b * nq + i, hd)),
        scratch_shapes=_flash_scratch(2 * q_tiles, DIFF_V, blk),
        compiler_params=_params(("arbitrary", "arbitrary", "arbitrary")), name="diff_attn",
    )(q, k, v, bias, rel_bias, lam_rows, g_sub)


def _merge_kernel(x_ref, g_mix_ref, oa_ref, ob_ref, wga_ref, wgb_ref, wua_ref, wub_ref,
                  out_ref, h_ref):
    @pl.when(pl.program_id(1) == 0)
    def _():
        h_ref[...] = _rms_rows(x_ref[...], g_mix_ref[...]).astype(BF16)

    h = h_ref[...]
    ga = jnp.dot(h, wga_ref[...], preferred_element_type=F32)
    gb_ = jnp.dot(h, wgb_ref[...], preferred_element_type=F32)
    ua = jnp.dot(oa_ref[...], wua_ref[...], preferred_element_type=F32)
    ub = jnp.dot(ob_ref[...], wub_ref[...], preferred_element_type=F32)
    out_ref[...] = (jax.nn.sigmoid(ga) * ua + jax.nn.sigmoid(gb_) * ub).astype(BF16)


def _merge(x2d, o_a, o_b, w, tm, tn):
    tokens = x2d.shape[0]
    return pl.pallas_call(
        _merge_kernel,
        out_shape=jax.ShapeDtypeStruct((tokens, D_MODEL), BF16),
        grid=(tokens // tm, D_MODEL // tn),
        in_specs=[
            pl.BlockSpec((tm, D_MODEL), lambda i, j: (i, 0)),
            _const_spec((1, D_MODEL)),
            pl.BlockSpec((tm, HEAD_OUT), lambda i, j: (i, 0)),
            pl.BlockSpec((tm, HEAD_OUT), lambda i, j: (i, 0)),
            pl.BlockSpec((D_MODEL, tn), lambda i, j: (0, j)),
            pl.BlockSpec((D_MODEL, tn), lambda i, j: (0, j)),
            pl.BlockSpec((HEAD_OUT, tn), lambda i, j: (0, j)),
            pl.BlockSpec((HEAD_OUT, tn), lambda i, j: (0, j)),
        ],
        out_specs=pl.BlockSpec((tm, tn), lambda i, j: (i, j)),
        scratch_shapes=[pltpu.VMEM((tm, D_MODEL), BF16)],
        compiler_params=_params(("arbitrary", "arbitrary")), name="gated_merge",
    )(x2d, w["g_mix"], o_a, o_b, w["w_ga"], w["w_gb"], w["w_ua"], w["w_ub"])


def _out_proj_kernel(x_ref, m_ref, wo_ref, out_ref):
    out_ref[...] = x_ref[...] + jnp.dot(m_ref[...], wo_ref[...], preferred_element_type=F32)


def _out_proj(x2d, merged, w, tm):
    tokens = x2d.shape[0]
    return pl.pallas_call(
        _out_proj_kernel,
        out_shape=jax.ShapeDtypeStruct((tokens, D_MODEL), F32),
        grid=(tokens // tm,),
        in_specs=[
            pl.BlockSpec((tm, D_MODEL), lambda i: (i, 0)),
            pl.BlockSpec((tm, D_MODEL), lambda i: (i, 0)),
            _const_spec((D_MODEL, D_MODEL)),
        ],
        out_specs=pl.BlockSpec((tm, D_MODEL), lambda i: (i, 0)),
        compiler_params=_params(("arbitrary",)), name="out_proj",
    )(x2d, merged, w["w_o"])


def _ffn_kernel(x_ref, g_ref, wg_ref, wu_ref, wd_ref, out_ref, h_ref):
    @pl.when(pl.program_id(1) == 0)
    def _():
        x = x_ref[...]
        h_ref[...] = _rms_rows(x, g_ref[...]).astype(BF16)
        out_ref[...] = x

    h = h_ref[...]
    g = jnp.dot(h, wg_ref[...], preferred_element_type=F32)
    u = jnp.dot(h, wu_ref[...], preferred_element_type=F32)
    a = (g * jax.nn.sigmoid(g) * u).astype(BF16)
    out_ref[...] += jnp.dot(a, wd_ref[...], preferred_element_type=F32)


def _ffn(x2d, w, tm, tf):
    tokens = x2d.shape[0]
    return pl.pallas_call(
        _ffn_kernel,
        out_shape=jax.ShapeDtypeStruct((tokens, D_MODEL), F32),
        grid=(tokens // tm, D_FF // tf),
        in_specs=[
            pl.BlockSpec((tm, D_MODEL), lambda i, j: (i, 0)),
            _const_spec((1, D_MODEL)),
            pl.BlockSpec((D_MODEL, tf), lambda i, j: (0, j)),
            pl.BlockSpec((D_MODEL, tf), lambda i, j: (0, j)),
            pl.BlockSpec((tf, D_MODEL), lambda i, j: (j, 0)),
        ],
        out_specs=pl.BlockSpec((tm, D_MODEL), lambda i, j: (i, 0)),
        scratch_shapes=[pltpu.VMEM((tm, D_MODEL), BF16)],
        compiler_params=_params(("arbitrary", "arbitrary")), name="swiglu_ffn",
    )(x2d, w["g_ffn"], w["w_gate"], w["w_up"], w["w_down"])


def _prepare_weights(mix_norm, w_in, q_a_norm, wq_b, kv_a_norm, wkv_b, mla_q_norm, mla_k_norm,
                     diff_q_norm, diff_k_norm, diff_subln, w_up_mla, w_up_diff, w_o, ffn_norm,
                     w_gate, w_up, w_down, layer):
    win = w_in[layer]
    o_cq, o_ckv, o_kpe = 0, Q_LORA, Q_LORA + KV_LORA
    o_dq = o_kpe + MLA_ROPE
    o_dk = o_dq + HEAD_OUT
    o_dv = o_dk + HEAD_OUT
    o_ga = o_dv + HEAD_OUT
    o_gb = o_ga + D_MODEL
    w_kpe = win[:, o_kpe:o_dq]
    w_kpe_rot = jnp.concatenate([-w_kpe[:, ROPE_HALF:], w_kpe[:, :ROPE_HALF]], axis=1)
    zpad = jnp.zeros((D_MODEL, LANES - MLA_ROPE), F32)
    w_a = jnp.concatenate([win[:, o_cq:o_kpe], w_kpe, zpad, w_kpe_rot, zpad], axis=1)
    wkv = wkv_b[layer].reshape(KV_LORA, MLA_HEADS, MLA_NOPE + MLA_V)
    gk = mla_k_norm[layer]
    gk_rope = gk[MLA_NOPE:]
    lane_pad = jnp.zeros((LANES - MLA_ROPE,), F32)
    return {
        "g_mix": mix_norm[layer][None, :],
        "w_a": w_a.astype(BF16),
        "g_qa": q_a_norm[layer][None, :],
        "g_kva": kv_a_norm[layer][None, :],
        "wq_t": wq_b[layer].T.astype(BF16),
        "g_q": mla_q_norm[layer][:, None],
        "wk_nope": wkv[:, :, :MLA_NOPE].reshape(KV_LORA, MLA_HEADS * MLA_NOPE).astype(BF16),
        "wv_t": wkv[:, :, MLA_NOPE:].reshape(KV_LORA, MLA_HEADS * MLA_V).T.astype(BF16),
        "g_kn": gk[None, :MLA_NOPE],
        "g_kr": jnp.concatenate([gk_rope, lane_pad])[None, :],
        "g_krot": jnp.concatenate([gk_rope[ROPE_HALF:], gk_rope[:ROPE_HALF], lane_pad])[None, :],
        "w_dq_t": win[:, o_dq:o_dk].T.astype(BF16),
        "w_dk_t": win[:, o_dk:o_dv].T.astype(BF16),
        "w_dv_t": win[:, o_dv:o_ga].T.astype(BF16),
        "g_dq": diff_q_norm[layer][:, None],
        "g_dk": diff_k_norm[layer][:, None],
        "g_sub": diff_subln[layer][:, None],
        "w_ga": win[:, o_ga:o_gb].astype(BF16),
        "w_gb": win[:, o_gb:].astype(BF16),
        "w_ua": w_up_mla[layer].astype(BF16),
        "w_ub": w_up_diff[layer].astype(BF16),
        "w_o": w_o[layer].astype(BF16),
        "g_ffn": ffn_norm[layer][None, :],
        "w_gate": w_gate[layer].astype(BF16),
        "w_up": w_up[layer].astype(BF16),
        "w_down": w_down[layer].astype(BF16),
    }


def _encoder_layer(x, w, tables, bias, rel_bias, lam_rows, lam_init, *, tm, blk, tn, tf,
                   mla_q_tiles, diff_q_tiles):
    gb, seq, _ = x.shape
    x2d = x.reshape(gb * seq, D_MODEL)
    q_a, k_a, v_a = _mla_prep(x2d, gb, seq, tm, blk, w, tables)
    q_d, k_d, v_d = _diff_prep(x2d, gb, seq, tm, blk, w)
    o_a = _mla_attention(q_a, k_a, v_a, gb, seq, blk, mla_q_tiles)
    o_b = _diff_attention(q_d, k_d, v_d, bias, rel_bias, lam_rows, w["g_sub"], gb, seq, blk,
                          diff_q_tiles, lam_init)
    merged = _merge(x2d, o_a, o_b, w, tm, tn)
    x1 = _out_proj(x2d, merged, w, tm)
    y = _ffn(x1, w, tm, tf)
    return y.reshape(gb, seq, D_MODEL)


def kernel(x_prompt, x_sample, mix_norm, w_in, q_a_norm, wq_b, kv_a_norm, wkv_b, mla_q_norm, mla_k_norm, diff_q_norm, diff_k_norm, lambda_q1, lambda_k1, lambda_q2, lambda_k2, diff_subln, w_up_mla, w_up_diff, w_o, ffn_norm, w_gate, w_up, w_down, rel_bias):
    tm, blk, tn, tf = 512, 512, 512, 512
    mla_q_tiles, diff_q_tiles = 2, 2
    depth = w_in.shape[0]
    max_seq = max(x_prompt.shape[1], x_sample.shape[1])
    tables = _rope_tables(max_seq)
    bias = _bias_tiles(rel_bias, blk, diff_q_tiles)
    y_prompt, y_sample = x_prompt, x_sample
    for layer in range(depth):
        w = _prepare_weights(mix_norm, w_in, q_a_norm, wq_b, kv_a_norm, wkv_b, mla_q_norm,
                             mla_k_norm, diff_q_norm, diff_k_norm, diff_subln, w_up_mla,
                             w_up_diff, w_o, ffn_norm, w_gate, w_up, w_down, layer)
        lam_init = 0.8 - 0.6 * math.exp(-0.3 * layer)
        lam_pad = jnp.zeros((LANES - DIFF_QK,), F32)
        lam_rows = jnp.stack([jnp.concatenate([v[layer], lam_pad])
                              for v in (lambda_q1, lambda_k1, lambda_q2, lambda_k2)])
        run = functools.partial(_encoder_layer, w=w, tables=tables, bias=bias, rel_bias=rel_bias,
                                lam_rows=lam_rows, lam_init=lam_init, tm=tm, blk=blk, tn=tn, tf=tf,
                                mla_q_tiles=mla_q_tiles, diff_q_tiles=diff_q_tiles)
        y_prompt = run(y_prompt)
        y_sample = run(y_sample)
    return (y_prompt, y_sample)
```

```python
import functools
import math
from typing import NamedTuple

import jax
import jax.numpy as jnp
from jax import lax
from jax.experimental import pallas as pl
from jax.experimental.pallas import tpu as pltpu

D_MODEL = 2048
MLA_HEADS = 8
MLA_NOPE = 128
MLA_ROPE = 64
MLA_QK = MLA_NOPE + MLA_ROPE
MLA_V = 128
Q_LORA = 512
KV_LORA = 256
ROPE_THETA = 10000.0
ROPE_HALF = MLA_ROPE // 2
DIFF_HEADS = 8
DIFF_QK = 64
DIFF_V = 2 * DIFF_QK
REL_BUCKETS = 32
REL_MAX_DIST = 128
D_FF = 5632
EPS = 1e-6
HEAD_OUT = MLA_HEADS * MLA_V

LANES = 128
MLA_QK_PAD = 2 * LANES
VMEM_LIMIT = 56 * 1024 * 1024

F32 = jnp.float32
BF16 = jnp.bfloat16
NEG_BIG = -1e30
LOG2E = math.log2(math.e)
BOUNDED_SOFTMAX_LIMIT = 50.0

NT_DIMS = (((1,), (1,)), ((), ()))


def _params(semantics):
    return pltpu.CompilerParams(dimension_semantics=semantics, vmem_limit_bytes=VMEM_LIMIT)


def _const_spec(shape):
    zeros = (0,) * len(shape)
    return pl.BlockSpec(shape, lambda *_: zeros)


def _rms_rows(x, gain):
    return x * lax.rsqrt(jnp.mean(x * x, axis=-1, keepdims=True) + EPS) * gain


def _rope_table_kernel(inv_row_ref, inv_col_ref, cos_t_ref, sin_t_ref, cos_f_ref, sin_f_ref):
    ts = cos_t_ref.shape[0]
    base = pl.program_id(0) * ts
    pos_rows = (base + lax.broadcasted_iota(jnp.int32, (ts, LANES), 0)).astype(F32)
    ang_t = pos_rows * inv_row_ref[...]
    cos_t_ref[...] = jnp.cos(ang_t)
    sin_t_ref[...] = jnp.sin(ang_t)
    pos_cols = (base + lax.broadcasted_iota(jnp.int32, (ROPE_HALF, ts), 1)).astype(F32)
    ang_f = pos_cols * inv_col_ref[...]
    cos_f_ref[...] = jnp.cos(ang_f)
    sin_f_ref[...] = jnp.sin(ang_f)


def _rope_tables(seq):
    inv = ROPE_THETA ** (-jnp.arange(ROPE_HALF, dtype=F32) / ROPE_HALF)
    inv_row = jnp.tile(inv, LANES // ROPE_HALF)[None, :]
    inv_col = inv[:, None]
    ts = min(seq, 2048)
    tok = jax.ShapeDtypeStruct((seq, LANES), F32)
    feat = jax.ShapeDtypeStruct((ROPE_HALF, seq), F32)
    return pl.pallas_call(
        _rope_table_kernel,
        out_shape=(tok, tok, feat, feat),
        grid=(seq // ts,),
        in_specs=[_const_spec((1, LANES)), _const_spec((ROPE_HALF, 1))],
        out_specs=(
            pl.BlockSpec((ts, LANES), lambda i: (i, 0)),
            pl.BlockSpec((ts, LANES), lambda i: (i, 0)),
            pl.BlockSpec((ROPE_HALF, ts), lambda i: (0, i)),
            pl.BlockSpec((ROPE_HALF, ts), lambda i: (0, i)),
        ),
        compiler_params=_params(("arbitrary",)), name="rope_tables",
    )(inv_row, inv_col)


def _mla_prep_kernel(x_ref, g_mix_ref, w_a_ref, g_qa_ref, g_kva_ref, wq_t_ref, g_q_ref,
                     wk_ref, wv_t_ref, g_kn_ref, g_kr_ref, g_krot_ref,
                     cos_t_ref, sin_t_ref, cos_f_ref, sin_f_ref,
                     q_ref, k_ref, v_ref, *, tk):
    tm = x_ref.shape[0]
    h = _rms_rows(x_ref[...], g_mix_ref[...]).astype(BF16)
    c = jnp.dot(h, w_a_ref[...], preferred_element_type=F32)
    cq = _rms_rows(c[:, :Q_LORA], g_qa_ref[...]).astype(BF16)
    ckv = _rms_rows(c[:, Q_LORA:Q_LORA + KV_LORA], g_kva_ref[...]).astype(BF16)
    k_pe = c[:, 6 * LANES:7 * LANES]
    k_pe_rot = c[:, 7 * LANES:8 * LANES]

    q_t = lax.dot_general(wq_t_ref[...], cq, NT_DIMS, preferred_element_type=F32)
    cos_f = cos_f_ref[...]
    sin_f = sin_f_ref[...]
    scale = MLA_QK ** -0.5 * LOG2E
    g_q = g_q_ref[...]
    for hd in range(MLA_HEADS):
        xh = q_t[hd * MLA_QK:(hd + 1) * MLA_QK]
        rinv = lax.rsqrt(jnp.mean(xh * xh, axis=0, keepdims=True) + EPS)
        xn = xh * rinv * g_q
        x1 = xn[MLA_NOPE:MLA_NOPE + ROPE_HALF]
        x2 = xn[MLA_NOPE + ROPE_HALF:]
        q_ref[0, hd, 0:MLA_NOPE, :] = (xn[:MLA_NOPE] * scale).astype(BF16)
        q_ref[0, hd, MLA_NOPE:MLA_NOPE + ROPE_HALF, :] = ((x1 * cos_f - x2 * sin_f) * scale).astype(BF16)
        q_ref[0, hd, MLA_NOPE + ROPE_HALF:MLA_QK, :] = ((x2 * cos_f + x1 * sin_f) * scale).astype(BF16)
        q_ref[0, hd, MLA_QK:, :] = jnp.zeros((MLA_QK_PAD - MLA_QK, tm), BF16)

    k_nope = jnp.dot(ckv, wk_ref[...], preferred_element_type=F32)
    rope_base = (k_pe * g_kr_ref[...]) * cos_t_ref[...] + (k_pe_rot * g_krot_ref[...]) * sin_t_ref[...]
    ss_pe = jnp.sum(k_pe * k_pe, axis=-1, keepdims=True)
    for hd in range(MLA_HEADS):
        kh = k_nope[:, hd * MLA_NOPE:(hd + 1) * MLA_NOPE]
        ss = jnp.sum(kh * kh, axis=-1, keepdims=True) + ss_pe
        rinv = lax.rsqrt(ss * (1.0 / MLA_QK) + EPS)
        k_ref[0, hd, :, 0:LANES] = (kh * rinv * g_kn_ref[...]).astype(BF16)
        k_ref[0, hd, :, LANES:] = (rope_base * rinv).astype(BF16)

    v_t = lax.dot_general(wv_t_ref[...], ckv, NT_DIMS, preferred_element_type=F32)
    for hd in range(MLA_HEADS):
        for cc in range(tm // tk):
            v_ref[0, hd, cc] = v_t[hd * MLA_V:(hd + 1) * MLA_V, cc * tk:(cc + 1) * tk].astype(BF16)


def _mla_prep(x2d, gb, seq, tm, tk, w, tables):
    tokens = x2d.shape[0]
    nt = seq // tm
    cos_t, sin_t, cos_f, sin_f = tables
    kern = functools.partial(_mla_prep_kernel, tk=tk)
    out_shape = (
        jax.ShapeDtypeStruct((gb, MLA_HEADS, MLA_QK_PAD, seq), BF16),
        jax.ShapeDtypeStruct((gb, MLA_HEADS, seq, MLA_QK_PAD), BF16),
        jax.ShapeDtypeStruct((gb, MLA_HEADS, seq // tk, MLA_V, tk), BF16),
    )
    in_specs = [
        pl.BlockSpec((tm, D_MODEL), lambda i: (i, 0)),
        _const_spec((1, D_MODEL)),
        _const_spec((D_MODEL, 8 * LANES)),
        _const_spec((1, Q_LORA)),
        _const_spec((1, KV_LORA)),
        _const_spec((MLA_HEADS * MLA_QK, Q_LORA)),
        _const_spec((MLA_QK, 1)),
        _const_spec((KV_LORA, MLA_HEADS * MLA_NOPE)),
        _const_spec((MLA_HEADS * MLA_V, KV_LORA)),
        _const_spec((1, LANES)),
        _const_spec((1, LANES)),
        _const_spec((1, LANES)),
        pl.BlockSpec((tm, LANES), lambda i: (i % nt, 0)),
        pl.BlockSpec((tm, LANES), lambda i: (i % nt, 0)),
        pl.BlockSpec((ROPE_HALF, tm), lambda i: (0, i % nt)),
        pl.BlockSpec((ROPE_HALF, tm), lambda i: (0, i % nt)),
    ]
    out_specs = (
        pl.BlockSpec((1, MLA_HEADS, MLA_QK_PAD, tm), lambda i: (i // nt, 0, 0, i % nt)),
        pl.BlockSpec((1, MLA_HEADS, tm, MLA_QK_PAD), lambda i: (i // nt, 0, i % nt, 0)),
        pl.BlockSpec((1, MLA_HEADS, tm // tk, MLA_V, tk), lambda i: (i // nt, 0, i % nt, 0, 0)),
    )
    return pl.pallas_call(
        kern, out_shape=out_shape, grid=(tokens // tm,), in_specs=in_specs, out_specs=out_specs,
        compiler_params=_params(("arbitrary",)), name="mla_prep",
    )(x2d, w["g_mix"], w["w_a"], w["g_qa"], w["g_kva"], w["wq_t"], w["g_q"], w["wk_nope"],
      w["wv_t"], w["g_kn"], w["g_kr"], w["g_krot"], cos_t, sin_t, cos_f, sin_f)


def _group_norm_cols(x_t, gain_col):
    rows, tm = x_t.shape
    x3 = x_t.reshape(rows // DIFF_QK, DIFF_QK, tm)
    rinv = lax.rsqrt(jnp.mean(x3 * x3, axis=1, keepdims=True) + EPS)
    return (x3 * rinv * gain_col[None]).reshape(rows, tm)


def _diff_prep_kernel(x_ref, g_mix_ref, wq_t_ref, wk_t_ref, wv_t_ref, g_q_ref, g_k_ref,
                      q_ref, k_ref, v_ref, *, tk):
    tm = x_ref.shape[0]
    h = _rms_rows(x_ref[...], g_mix_ref[...]).astype(BF16)
    scale = DIFF_QK ** -0.5 * LOG2E
    q_t = lax.dot_general(wq_t_ref[...], h, NT_DIMS, preferred_element_type=F32)
    qn = (_group_norm_cols(q_t, g_q_ref[...]) * scale).astype(BF16)
    zeros = jnp.zeros((DIFF_QK, tm), BF16)
    for hd in range(DIFF_HEADS):
        r0 = hd * DIFF_V
        q_ref[0, hd, 0, 0:DIFF_QK, :] = qn[r0:r0 + DIFF_QK]
        q_ref[0, hd, 0, DIFF_QK:, :] = zeros
        q_ref[0, hd, 1, 0:DIFF_QK, :] = zeros
        q_ref[0, hd, 1, DIFF_QK:, :] = qn[r0 + DIFF_QK:r0 + DIFF_V]

    k_t = lax.dot_general(wk_t_ref[...], h, NT_DIMS, preferred_element_type=F32)
    kn = _group_norm_cols(k_t, g_k_ref[...]).T
    for hd in range(DIFF_HEADS):
        k_ref[0, hd] = kn[:, hd * DIFF_V:(hd + 1) * DIFF_V].astype(BF16)

    v_t = lax.dot_general(wv_t_ref[...], h, NT_DIMS, preferred_element_type=F32)
    for hd in range(DIFF_HEADS):
        for cc in range(tm // tk):
            v_ref[0, hd, cc] = v_t[hd * DIFF_V:(hd + 1) * DIFF_V, cc * tk:(cc + 1) * tk].astype(BF16)


def _diff_prep(x2d, gb, seq, tm, tk, w):
    tokens = x2d.shape[0]
    nt = seq // tm
    kern = functools.partial(_diff_prep_kernel, tk=tk)
    out_shape = (
        jax.ShapeDtypeStruct((gb, DIFF_HEADS, 2, DIFF_V, seq), BF16),
        jax.ShapeDtypeStruct((gb, DIFF_HEADS, seq, DIFF_V), BF16),
        jax.ShapeDtypeStruct((gb, DIFF_HEADS, seq // tk, DIFF_V, tk), BF16),
    )
    in_specs = [
        pl.BlockSpec((tm, D_MODEL), lambda i: (i, 0)),
        _const_spec((1, D_MODEL)),
        _const_spec((HEAD_OUT, D_MODEL)),
        _const_spec((HEAD_OUT, D_MODEL)),
        _const_spec((HEAD_OUT, D_MODEL)),
        _const_spec((DIFF_QK, 1)),
        _const_spec((DIFF_QK, 1)),
    ]
    out_specs = (
        pl.BlockSpec((1, DIFF_HEADS, 2, DIFF_V, tm), lambda i: (i // nt, 0, 0, 0, i % nt)),
        pl.BlockSpec((1, DIFF_HEADS, tm, DIFF_V), lambda i: (i // nt, 0, i % nt, 0)),
        pl.BlockSpec((1, DIFF_HEADS, tm // tk, DIFF_V, tk), lambda i: (i // nt, 0, i % nt, 0, 0)),
    )
    return pl.pallas_call(
        kern, out_shape=out_shape, grid=(tokens // tm,), in_specs=in_specs, out_specs=out_specs,
        compiler_params=_params(("arbitrary",)), name="diff_prep",
    )(x2d, w["g_mix"], w["w_dq_t"], w["w_dk_t"], w["w_dv_t"], w["g_dq"], w["g_dk"])


def _bias_tile_kernel(bucket_ref, rb_ref, out_ref):
    hd = pl.program_id(0)
    offset = pl.program_id(1) - pl.num_programs(1) // 2

    @pl.when(offset <= -2)
    def _():
        out_ref[0, 0] = jnp.full(out_ref.shape[2:], rb_ref[REL_BUCKETS // 2 - 1, hd] * LOG2E, F32)

    @pl.when(offset >= 2)
    def _():
        out_ref[0, 0] = jnp.full(out_ref.shape[2:], rb_ref[REL_BUCKETS - 1, hd] * LOG2E, F32)

    @pl.when(jnp.abs(offset) < 2)
    def _():
        bucket = bucket_ref[0]
        acc = jnp.zeros(bucket.shape, F32)
        for b in range(REL_BUCKETS):
            acc = jnp.where(bucket == b, rb_ref[b, hd], acc)
        out_ref[0, 0] = acc * LOG2E


def _t5_bucket(rel):
    nb = REL_BUCKETS // 2
    ret = jnp.where(rel > 0, nb, 0)
    n = jnp.abs(rel)
    max_exact = nb // 2
    nf = jnp.maximum(n, 1).astype(F32)
    large = max_exact + (jnp.log(nf / max_exact) / math.log(REL_MAX_DIST / max_exact)
                         * (nb - max_exact)).astype(jnp.int32)
    large = jnp.minimum(large, nb - 1)
    return ret + jnp.where(n < max_exact, n, large)


def _near_tiles(q_tiles, group):
    return 2 * (group + q_tiles) - 1


def _bias_tiles(rel_bias, blk, q_tiles, group):
    assert blk >= REL_MAX_DIST
    n = _near_tiles(q_tiles, group)
    d = jnp.arange(blk, dtype=jnp.int32)
    offs = (jnp.arange(n, dtype=jnp.int32) - n // 2) * blk
    rel = offs[:, None, None] + d[None, :, None] - d[None, None, :]
    bucket = _t5_bucket(rel)
    return pl.pallas_call(
        _bias_tile_kernel,
        out_shape=jax.ShapeDtypeStruct((DIFF_HEADS, n, blk, blk), F32),
        grid=(DIFF_HEADS, n),
        in_specs=[
            pl.BlockSpec((1, blk, blk), lambda hd, t: (t, 0, 0)),
            pl.BlockSpec(memory_space=pltpu.SMEM),
        ],
        out_specs=pl.BlockSpec((1, 1, blk, blk), lambda hd, t: (hd, t, 0, 0)),
        compiler_params=_params(("arbitrary", "arbitrary")), name="bias_tiles",
    )(bucket, rel_bias)


class _FlashState:
    def __init__(self, q_ref, k_ref, v_ref, scratch, blk, q_tiles):
        self.q_ref, self.k_ref, self.v_ref = q_ref, k_ref, v_ref
        (self.s_ref, self.p_ref, self.alpha_ref, self.m_ref, self.l_ref, self.acc_ref,
         self.kmax_ref) = scratch
        self.blk = blk
        self.streams = [(qt, mp) for qt in range(q_tiles) for mp in range(q_ref.shape[2])]
        self.nc = v_ref.shape[2]

    def q(self, qt, mp):
        return self.q_ref[0, 0, mp, :, qt * self.blk:(qt + 1) * self.blk]

    def key_chunk(self, c):
        return self.k_ref[0, 0, pl.ds(pl.multiple_of(c * self.blk, self.blk), self.blk), :]

    def update_key_norm(self):
        def body(c, mx):
            kc = self.key_chunk(c).astype(F32)
            n2 = jnp.sum(kc * kc, axis=-1, keepdims=True)
            return jnp.maximum(mx, jnp.max(n2, axis=0, keepdims=True))

        mx = lax.fori_loop(0, self.nc, body, jnp.zeros((1, 1), F32))
        self.kmax_ref[...] = jnp.broadcast_to(mx, self.kmax_ref.shape)

    def score_bound(self, extra):
        kmax2 = self.kmax_ref[:, 0:1]
        worst = None
        for sx, (qt, mp) in enumerate(self.streams):
            q = self.q(qt, mp).astype(F32)
            bound = jnp.sqrt(jnp.sum(q * q, axis=0, keepdims=True) * kmax2) + extra
            self.m_ref[sx] = bound
            tile_max = jnp.max(bound)
            worst = tile_max if worst is None else jnp.maximum(worst, tile_max)
        return worst


    def init_bounded(self):
        self.l_ref[...] = jnp.zeros(self.l_ref.shape, F32)
        self.acc_ref[...] = jnp.zeros(self.acc_ref.shape, F32)
        self.p_ref[1] = jnp.zeros(self.p_ref.shape[1:], BF16)

    def pv_bounded(self, c, half):
        vc = self.v_ref[0, 0, c]
        for sx, (qt, mp) in enumerate(self.streams):
            self.acc_ref[sx] += jnp.dot(vc, self.p_ref[half, sx], preferred_element_type=F32)

    def step_bounded(self, c, half, cst=None, bias_tiles=None):
        self.pv_bounded(jnp.maximum(c - 1, 0), 1 - half)
        kc = self.key_chunk(c)
        for sx, (qt, mp) in enumerate(self.streams):
            s = jnp.dot(kc, self.q(qt, mp), preferred_element_type=F32)
            if bias_tiles is not None:
                s = s + bias_tiles[qt]
            shift = self.m_ref[sx] if cst is None else self.m_ref[sx] - cst
            p = jnp.exp2(s - shift)
            self.l_ref[sx] += jnp.sum(p, axis=0, keepdims=True)
            self.p_ref[half, sx] = p.astype(BF16)

    def finish_bounded(self):
        self.pv_bounded(self.nc - 1, 1)


    def init_online(self):
        self.m_ref[...] = jnp.full(self.m_ref.shape, NEG_BIG, F32)
        self.l_ref[...] = jnp.zeros(self.l_ref.shape, F32)
        self.acc_ref[...] = jnp.zeros(self.acc_ref.shape, F32)
        self.p_ref[0] = jnp.zeros(self.p_ref.shape[1:], BF16)
        self.alpha_ref[...] = jnp.ones(self.alpha_ref.shape, F32)
        self.qk(0, 0)

    def qk(self, c, half):
        kc = self.key_chunk(c)
        for sx, (qt, mp) in enumerate(self.streams):
            self.s_ref[half, sx] = jnp.dot(kc, self.q(qt, mp), preferred_element_type=F32)

    def pv_online(self, c):
        vc = self.v_ref[0, 0, c]
        for sx, (qt, mp) in enumerate(self.streams):
            self.acc_ref[sx] = (self.alpha_ref[sx] * self.acc_ref[sx]
                                + jnp.dot(vc, self.p_ref[0, sx], preferred_element_type=F32))

    def softmax_online(self, half, cst, bias_tiles):
        for sx, (qt, mp) in enumerate(self.streams):
            s = self.s_ref[half, sx]
            if bias_tiles is not None:
                s = s + bias_tiles[qt]
            mc = jnp.max(s, axis=0, keepdims=True)
            if cst is not None:
                mc = mc + cst
            m_old = self.m_ref[sx]
            m_new = jnp.maximum(m_old, mc)
            alpha = jnp.exp2(m_old - m_new)
            shift = m_new if cst is None else m_new - cst
            p = jnp.exp2(s - shift)
            self.l_ref[sx] = alpha * self.l_ref[sx] + jnp.sum(p, axis=0, keepdims=True)
            self.p_ref[0, sx] = p.astype(BF16)
            self.alpha_ref[sx] = alpha
            self.m_ref[sx] = m_new

    def step_online(self, c, half, cst=None, bias_tiles=None):
        self.pv_online(jnp.maximum(c - 1, 0))
        self.qk(jnp.minimum(c + 1, self.nc - 1), 1 - half)
        self.softmax_online(half, cst, bias_tiles)

    def finish_online(self):
        self.pv_online(self.nc - 1)

    def run(self, extra, schedule):
        use_bounded = self.score_bound(extra) <= BOUNDED_SOFTMAX_LIMIT

        @pl.when(use_bounded)
        def _():
            self.init_bounded()
            schedule(self.step_bounded)
            self.finish_bounded()

        @pl.when(jnp.logical_not(use_bounded))
        def _():
            self.init_online()
            schedule(self.step_online)
            self.finish_online()


def _flash_scratch(n_maps, head_dim, blk):
    return [
        pltpu.VMEM((2, n_maps, blk, blk), F32),
        pltpu.VMEM((2, n_maps, blk, blk), BF16),
        pltpu.VMEM((n_maps, 1, blk), F32),
        pltpu.VMEM((n_maps, 1, blk), F32),
        pltpu.VMEM((n_maps, 1, blk), F32),
        pltpu.VMEM((n_maps, head_dim, blk), F32),
        pltpu.VMEM((1, LANES), F32),
    ]


def _mla_attn_kernel(q_ref, k_ref, v_ref, o_ref, *scratch, blk, q_tiles, group):
    st = _FlashState(q_ref, k_ref, v_ref, scratch, blk, q_tiles)

    @pl.when(pl.program_id(2) == 0)
    def _():
        st.update_key_norm()

    def schedule(step):
        def body(cc, carry):
            for u in range(group):
                step(group * cc + u, u % 2)
            return carry

        lax.fori_loop(0, st.nc // group, body, 0)

    st.run(0.0, schedule)
    for sx, (qt, _) in enumerate(st.streams):
        o_t = st.acc_ref[sx] / st.l_ref[sx]
        o_ref[qt * blk:(qt + 1) * blk, :] = o_t.T.astype(BF16)


def _mla_attention(q, k, v, gb, seq, blk, q_tiles, group):
    nc = seq // blk
    nq = nc // q_tiles
    assert group % 2 == 0 and nc % group == 0 and nc % q_tiles == 0
    kern = functools.partial(_mla_attn_kernel, blk=blk, q_tiles=q_tiles, group=group)
    q = q.reshape(gb, MLA_HEADS, 1, MLA_QK_PAD, seq)
    tq = q_tiles * blk
    return pl.pallas_call(
        kern,
        out_shape=jax.ShapeDtypeStruct((gb * seq, HEAD_OUT), BF16),
        grid=(gb, MLA_HEADS, nq),
        in_specs=[
            pl.BlockSpec((1, 1, 1, MLA_QK_PAD, tq), lambda b, hd, i: (b, hd, 0, 0, i)),
            pl.BlockSpec((1, 1, seq, MLA_QK_PAD), lambda b, hd, i: (b, hd, 0, 0)),
            pl.BlockSpec((1, 1, nc, MLA_V, blk), lambda b, hd, i: (b, hd, 0, 0, 0)),
        ],
        out_specs=pl.BlockSpec((tq, MLA_V), lambda b, hd, i: (b * nq + i, hd)),
        scratch_shapes=_flash_scratch(q_tiles, MLA_V, blk),
        compiler_params=_params(("arbitrary", "arbitrary", "arbitrary")), name="mla_attn",
    )(q, k, v)


def _diff_attn_kernel(q_ref, k_ref, v_ref, bias_ref, rb_ref, lam_ref, g_sub_ref, o_ref,
                      *scratch, blk, q_tiles, group, lam_init):
    hd = pl.program_id(0)
    i_first = pl.program_id(2) * q_tiles
    st = _FlashState(q_ref, k_ref, v_ref, scratch, blk, q_tiles)
    acc_ref, l_ref = st.acc_ref, st.l_ref
    n_groups = st.nc // group

    @pl.when(pl.program_id(2) == 0)
    def _():
        st.update_key_norm()

    bias_max = jnp.abs(rb_ref[0, hd])
    for b in range(1, REL_BUCKETS):
        bias_max = jnp.maximum(bias_max, jnp.abs(rb_ref[b, hd]))

    far_before = rb_ref[REL_BUCKETS // 2 - 1, hd] * LOG2E
    far_after = rb_ref[REL_BUCKETS - 1, hd] * LOG2E
    first_near = (i_first - 1 + group) // group - 1
    last_near = (i_first + q_tiles) // group
    center_tile = group + q_tiles - 1

    near_lo = jnp.maximum(first_near, 0)
    near_hi = jnp.minimum(last_near + 1, n_groups)

    def schedule(step):
        def far_group(cst):
            def body(cc, carry):
                for u in range(group):
                    step(group * cc + u, u % 2, cst=cst)
                return carry
            return body

        def near_group(cc, carry):
            for u in range(group):
                c = group * cc + u
                tiles = [bias_ref[0, c - (i_first + qt) + center_tile] for qt in range(q_tiles)]
                step(c, u % 2, bias_tiles=tiles)
            return carry

        lax.fori_loop(0, near_lo, far_group(far_before), 0)
        lax.fori_loop(near_lo, near_hi, near_group, 0)
        lax.fori_loop(near_hi, n_groups, far_group(far_after), 0)

    st.run(bias_max * LOG2E, schedule)

    lam_v = lam_ref[...]
    lam = (jnp.exp(jnp.sum(lam_v[0:1] * lam_v[1:2], axis=-1, keepdims=True))
           - jnp.exp(jnp.sum(lam_v[2:3] * lam_v[3:4], axis=-1, keepdims=True)) + lam_init)
    for qt in range(q_tiles):
        s0, s1 = 2 * qt, 2 * qt + 1
        o_t = acc_ref[s0] / l_ref[s0] - lam * (acc_ref[s1] / l_ref[s1])
        o_t = o_t * lax.rsqrt(jnp.mean(o_t * o_t, axis=0, keepdims=True) + EPS) * g_sub_ref[...]
        o_ref[qt * blk:(qt + 1) * blk, :] = (o_t * (1.0 - lam_init)).T.astype(BF16)


def _diff_attention(q, k, v, bias, rel_bias, lam_rows, g_sub, gb, seq, blk, q_tiles, group,
                    lam_init):
    nc = seq // blk
    nq = nc // q_tiles
    assert group % 2 == 0 and nc % group == 0 and nc % q_tiles == 0 and blk >= REL_MAX_DIST
    assert bias.shape[1] == _near_tiles(q_tiles, group)
    kern = functools.partial(_diff_attn_kernel, blk=blk, q_tiles=q_tiles, group=group,
                             lam_init=lam_init)
    tq = q_tiles * blk
    return pl.pallas_call(
        kern,
        out_shape=jax.ShapeDtypeStruct((gb * seq, HEAD_OUT), BF16),
        grid=(DIFF_HEADS, gb, nq),
        in_specs=[
            pl.BlockSpec((1, 1, 2, DIFF_V, tq), lambda hd, b, i: (b, hd, 0, 0, i)),
            pl.BlockSpec((1, 1, seq, DIFF_V), lambda hd, b, i: (b, hd, 0, 0)),
            pl.BlockSpec((1, 1, nc, DIFF_V, blk), lambda hd, b, i: (b, hd, 0, 0, 0)),
            pl.BlockSpec((1, bias.shape[1], blk, blk), lambda hd, b, i: (hd, 0, 0, 0),
                         pipeline_mode=pl.Buffered(1)),
            pl.BlockSpec(memory_space=pltpu.SMEM),
            _const_spec((4, LANES)),
            _const_spec((DIFF_V, 1)),
        ],
        out_specs=pl.BlockSpec((tq, DIFF_V), lambda hd, b, i: (b * nq + i, hd)),
        scratch_shapes=_flash_scratch(2 * q_tiles, DIFF_V, blk),
        compiler_params=_params(("arbitrary", "arbitrary", "arbitrary")), name="diff_attn",
    )(q, k, v, bias, rel_bias, lam_rows, g_sub)


def _merge_kernel(x_ref, g_mix_ref, oa_ref, ob_ref, wga_ref, wgb_ref, wua_ref, wub_ref,
                  out_ref, h_ref):
    @pl.when(pl.program_id(1) == 0)
    def _():
        h_ref[...] = _rms_rows(x_ref[...], g_mix_ref[...]).astype(BF16)

    h = h_ref[...]
    ga = jnp.dot(h, wga_ref[...], preferred_element_type=F32)
    gb_ = jnp.dot(h, wgb_ref[...], preferred_element_type=F32)
    ua = jnp.dot(oa_ref[...], wua_ref[...], preferred_element_type=F32)
    ub = jnp.dot(ob_ref[...], wub_ref[...], preferred_element_type=F32)
    out_ref[...] = (jax.nn.sigmoid(ga) * ua + jax.nn.sigmoid(gb_) * ub).astype(BF16)


def _merge(x2d, o_a, o_b, w, tm, tn):
    tokens = x2d.shape[0]
    return pl.pallas_call(
        _merge_kernel,
        out_shape=jax.ShapeDtypeStruct((tokens, D_MODEL), BF16),
        grid=(tokens // tm, D_MODEL // tn),
        in_specs=[
            pl.BlockSpec((tm, D_MODEL), lambda i, j: (i, 0)),
            _const_spec((1, D_MODEL)),
            pl.BlockSpec((tm, HEAD_OUT), lambda i, j: (i, 0)),
            pl.BlockSpec((tm, HEAD_OUT), lambda i, j: (i, 0)),
            pl.BlockSpec((D_MODEL, tn), lambda i, j: (0, j)),
            pl.BlockSpec((D_MODEL, tn), lambda i, j: (0, j)),
            pl.BlockSpec((HEAD_OUT, tn), lambda i, j: (0, j)),
            pl.BlockSpec((HEAD_OUT, tn), lambda i, j: (0, j)),
        ],
        out_specs=pl.BlockSpec((tm, tn), lambda i, j: (i, j)),
        scratch_shapes=[pltpu.VMEM((tm, D_MODEL), BF16)],
        compiler_params=_params(("arbitrary", "arbitrary")), name="gated_merge",
    )(x2d, w["g_mix"], o_a, o_b, w["w_ga"], w["w_gb"], w["w_ua"], w["w_ub"])


def _out_proj_kernel(x_ref, m_ref, wo_ref, out_ref):
    out_ref[...] = x_ref[...] + jnp.dot(m_ref[...], wo_ref[...], preferred_element_type=F32)


def _out_proj(x2d, merged, w, tm):
    tokens = x2d.shape[0]
    return pl.pallas_call(
        _out_proj_kernel,
        out_shape=jax.ShapeDtypeStruct((tokens, D_MODEL), F32),
        grid=(tokens // tm,),
        in_specs=[
            pl.BlockSpec((tm, D_MODEL), lambda i: (i, 0)),
            pl.BlockSpec((tm, D_MODEL), lambda i: (i, 0)),
            _const_spec((D_MODEL, D_MODEL)),
        ],
        out_specs=pl.BlockSpec((tm, D_MODEL), lambda i: (i, 0)),
        compiler_params=_params(("arbitrary",)), name="out_proj",
    )(x2d, merged, w["w_o"])


def _ffn_kernel(x_ref, g_ref, wg_ref, wu_ref, wd_ref, out_ref, h_ref):
    @pl.when(pl.program_id(1) == 0)
    def _():
        x = x_ref[...]
        h_ref[...] = _rms_rows(x, g_ref[...]).astype(BF16)
        out_ref[...] = x

    h = h_ref[...]
    g = jnp.dot(h, wg_ref[...], preferred_element_type=F32)
    u = jnp.dot(h, wu_ref[...], preferred_element_type=F32)
    a = (g * jax.nn.sigmoid(g) * u).astype(BF16)
    out_ref[...] += jnp.dot(a, wd_ref[...], preferred_element_type=F32)


def _ffn(x2d, w, tm, tf):
    tokens = x2d.shape[0]
    return pl.pallas_call(
        _ffn_kernel,
        out_shape=jax.ShapeDtypeStruct((tokens, D_MODEL), F32),
        grid=(tokens // tm, D_FF // tf),
        in_specs=[
            pl.BlockSpec((tm, D_MODEL), lambda i, j: (i, 0)),
            _const_spec((1, D_MODEL)),
            pl.BlockSpec((D_MODEL, tf), lambda i, j: (0, j)),
            pl.BlockSpec((D_MODEL, tf), lambda i, j: (0, j)),
            pl.BlockSpec((tf, D_MODEL), lambda i, j: (j, 0)),
        ],
        out_specs=pl.BlockSpec((tm, D_MODEL), lambda i, j: (i, 0)),
        scratch_shapes=[pltpu.VMEM((tm, D_MODEL), BF16)],
        compiler_params=_params(("arbitrary", "arbitrary")), name="swiglu_ffn",
    )(x2d, w["g_ffn"], w["w_gate"], w["w_up"], w["w_down"])


def _prepare_weights(mix_norm, w_in, q_a_norm, wq_b, kv_a_norm, wkv_b, mla_q_norm, mla_k_norm,
                     diff_q_norm, diff_k_norm, diff_subln, w_up_mla, w_up_diff, w_o, ffn_norm,
                     w_gate, w_up, w_down, layer):
    win = w_in[layer]
    o_cq, o_ckv, o_kpe = 0, Q_LORA, Q_LORA + KV_LORA
    o_dq = o_kpe + MLA_ROPE
    o_dk = o_dq + HEAD_OUT
    o_dv = o_dk + HEAD_OUT
    o_ga = o_dv + HEAD_OUT
    o_gb = o_ga + D_MODEL
    w_kpe = win[:, o_kpe:o_dq]
    w_kpe_rot = jnp.concatenate([-w_kpe[:, ROPE_HALF:], w_kpe[:, :ROPE_HALF]], axis=1)
    zpad = jnp.zeros((D_MODEL, LANES - MLA_ROPE), F32)
    w_a = jnp.concatenate([win[:, o_cq:o_kpe], w_kpe, zpad, w_kpe_rot, zpad], axis=1)
    wkv = wkv_b[layer].reshape(KV_LORA, MLA_HEADS, MLA_NOPE + MLA_V)
    gk = mla_k_norm[layer]
    gk_rope = gk[MLA_NOPE:]
    lane_pad = jnp.zeros((LANES - MLA_ROPE,), F32)
    return {
        "g_mix": mix_norm[layer][None, :],
        "w_a": w_a.astype(BF16),
        "g_qa": q_a_norm[layer][None, :],
        "g_kva": kv_a_norm[layer][None, :],
        "wq_t": wq_b[layer].T.astype(BF16),
        "g_q": mla_q_norm[layer][:, None],
        "wk_nope": wkv[:, :, :MLA_NOPE].reshape(KV_LORA, MLA_HEADS * MLA_NOPE).astype(BF16),
        "wv_t": wkv[:, :, MLA_NOPE:].reshape(KV_LORA, MLA_HEADS * MLA_V).T.astype(BF16),
        "g_kn": gk[None, :MLA_NOPE],
        "g_kr": jnp.concatenate([gk_rope, lane_pad])[None, :],
        "g_krot": jnp.concatenate([gk_rope[ROPE_HALF:], gk_rope[:ROPE_HALF], lane_pad])[None, :],
        "w_dq_t": win[:, o_dq:o_dk].T.astype(BF16),
        "w_dk_t": win[:, o_dk:o_dv].T.astype(BF16),
        "w_dv_t": win[:, o_dv:o_ga].T.astype(BF16),
        "g_dq": diff_q_norm[layer][:, None],
        "g_dk": diff_k_norm[layer][:, None],
        "g_sub": diff_subln[layer][:, None],
        "w_ga": win[:, o_ga:o_gb].astype(BF16),
        "w_gb": win[:, o_gb:].astype(BF16),
        "w_ua": w_up_mla[layer].astype(BF16),
        "w_ub": w_up_diff[layer].astype(BF16),
        "w_o": w_o[layer].astype(BF16),
        "g_ffn": ffn_norm[layer][None, :],
        "w_gate": w_gate[layer].astype(BF16),
        "w_up": w_up[layer].astype(BF16),
        "w_down": w_down[layer].astype(BF16),
    }


class _Tiles(NamedTuple):
    tm: int = 512
    tn: int = 512
    tf: int = 512
    blk: int = 512
    q_tiles: int = 2
    diff_group: int = 4
    mla_group_max: int = 8


def _encoder_layer(x, w, tables, bias, rel_bias, lam_rows, lam_init, t):
    gb, seq, _ = x.shape
    x2d = x.reshape(gb * seq, D_MODEL)
    mla_group = math.gcd(t.mla_group_max, seq // t.blk)
    q_a, k_a, v_a = _mla_prep(x2d, gb, seq, t.tm, t.blk, w, tables)
    q_d, k_d, v_d = _diff_prep(x2d, gb, seq, t.tm, t.blk, w)
    o_a = _mla_attention(q_a, k_a, v_a, gb, seq, t.blk, t.q_tiles, mla_group)
    o_b = _diff_attention(q_d, k_d, v_d, bias, rel_bias, lam_rows, w["g_sub"], gb, seq, t.blk,
                          t.q_tiles, t.diff_group, lam_init)
    merged = _merge(x2d, o_a, o_b, w, t.tm, t.tn)
    x1 = _out_proj(x2d, merged, w, t.tm)
    y = _ffn(x1, w, t.tm, t.tf)
    return y.reshape(gb, seq, D_MODEL)


def kernel(x_prompt, x_sample, mix_norm, w_in, q_a_norm, wq_b, kv_a_norm, wkv_b, mla_q_norm, mla_k_norm, diff_q_norm, diff_k_norm, lambda_q1, lambda_k1, lambda_q2, lambda_k2, diff_subln, w_up_mla, w_up_diff, w_o, ffn_norm, w_gate, w_up, w_down, rel_bias):
    t = _Tiles()
    depth = w_in.shape[0]
    max_seq = max(x_prompt.shape[1], x_sample.shape[1])
    tables = _rope_tables(max_seq)
    bias = _bias_tiles(rel_bias, t.blk, t.q_tiles, t.diff_group)
    y_prompt, y_sample = x_prompt, x_sample
    for layer in range(depth):
        w = _prepare_weights(mix_norm, w_in, q_a_norm, wq_b, kv_a_norm, wkv_b, mla_q_norm,
                             mla_k_norm, diff_q_norm, diff_k_norm, diff_subln, w_up_mla,
                             w_up_diff, w_o, ffn_norm, w_gate, w_up, w_down, layer)
        lam_init = 0.8 - 0.6 * math.exp(-0.3 * layer)
        lam_pad = jnp.zeros((LANES - DIFF_QK,), F32)
        lam_rows = jnp.stack([jnp.concatenate([v[layer], lam_pad])
                              for v in (lambda_q1, lambda_k1, lambda_q2, lambda_k2)])
        run = functools.partial(_encoder_layer, w=w, tables=tables, bias=bias, rel_bias=rel_bias,
                                lam_rows=lam_rows, lam_init=lam_init, t=t)
        y_prompt = run(y_prompt)
        y_sample = run(y_sample)
    return (y_prompt, y_sample)
```

```python
import functools
import math
from typing import NamedTuple

import jax
import jax.numpy as jnp
from jax import lax
from jax.experimental import pallas as pl
from jax.experimental.pallas import tpu as pltpu

D_MODEL = 2048
MLA_HEADS = 8
MLA_NOPE = 128
MLA_ROPE = 64
MLA_QK = MLA_NOPE + MLA_ROPE
MLA_V = 128
Q_LORA = 512
KV_LORA = 256
ROPE_THETA = 10000.0
ROPE_HALF = MLA_ROPE // 2
DIFF_HEADS = 8
DIFF_QK = 64
DIFF_V = 2 * DIFF_QK
REL_BUCKETS = 32
REL_MAX_DIST = 128
D_FF = 5632
EPS = 1e-6
HEAD_OUT = MLA_HEADS * MLA_V

LANES = 128
MLA_QK_PAD = 2 * LANES
VMEM_LIMIT = 56 * 1024 * 1024

F32 = jnp.float32
BF16 = jnp.bfloat16
NEG_BIG = -1e30
LOG2E = math.log2(math.e)
BOUNDED_SOFTMAX_LIMIT = 50.0

NT_DIMS = (((1,), (1,)), ((), ()))


def _params(semantics):
    return pltpu.CompilerParams(dimension_semantics=semantics, vmem_limit_bytes=VMEM_LIMIT)


def _const_spec(shape):
    zeros = (0,) * len(shape)
    return pl.BlockSpec(shape, lambda *_: zeros)


def _rms_rows(x, gain):
    return x * lax.rsqrt(jnp.mean(x * x, axis=-1, keepdims=True) + EPS) * gain


def _rope_table_kernel(inv_row_ref, inv_col_ref, cos_t_ref, sin_t_ref, cos_f_ref, sin_f_ref):
    ts = cos_t_ref.shape[0]
    base = pl.program_id(0) * ts
    pos_rows = (base + lax.broadcasted_iota(jnp.int32, (ts, LANES), 0)).astype(F32)
    ang_t = pos_rows * inv_row_ref[...]
    cos_t_ref[...] = jnp.cos(ang_t)
    sin_t_ref[...] = jnp.sin(ang_t)
    pos_cols = (base + lax.broadcasted_iota(jnp.int32, (ROPE_HALF, ts), 1)).astype(F32)
    ang_f = pos_cols * inv_col_ref[...]
    cos_f_ref[...] = jnp.cos(ang_f)
    sin_f_ref[...] = jnp.sin(ang_f)


def _rope_tables(seq):
    inv = ROPE_THETA ** (-jnp.arange(ROPE_HALF, dtype=F32) / ROPE_HALF)
    inv_row = jnp.tile(inv, LANES // ROPE_HALF)[None, :]
    inv_col = inv[:, None]
    ts = min(seq, 2048)
    tok = jax.ShapeDtypeStruct((seq, LANES), F32)
    feat = jax.ShapeDtypeStruct((ROPE_HALF, seq), F32)
    return pl.pallas_call(
        _rope_table_kernel,
        out_shape=(tok, tok, feat, feat),
        grid=(seq // ts,),
        in_specs=[_const_spec((1, LANES)), _const_spec((ROPE_HALF, 1))],
        out_specs=(
            pl.BlockSpec((ts, LANES), lambda i: (i, 0)),
            pl.BlockSpec((ts, LANES), lambda i: (i, 0)),
            pl.BlockSpec((ROPE_HALF, ts), lambda i: (0, i)),
            pl.BlockSpec((ROPE_HALF, ts), lambda i: (0, i)),
        ),
        compiler_params=_params(("arbitrary",)), name="rope_tables",
    )(inv_row, inv_col)


def _mla_prep_kernel(x_ref, g_mix_ref, w_a_ref, g_qa_ref, g_kva_ref, wq_t_ref, g_q_ref,
                     wk_ref, wv_t_ref, g_kn_ref, g_kr_ref, g_krot_ref,
                     cos_t_ref, sin_t_ref, cos_f_ref, sin_f_ref,
                     q_ref, k_ref, v_ref, *, tk):
    tm = x_ref.shape[0]
    h = _rms_rows(x_ref[...], g_mix_ref[...]).astype(BF16)
    c = jnp.dot(h, w_a_ref[...], preferred_element_type=F32)
    cq = _rms_rows(c[:, :Q_LORA], g_qa_ref[...]).astype(BF16)
    ckv = _rms_rows(c[:, Q_LORA:Q_LORA + KV_LORA], g_kva_ref[...]).astype(BF16)
    k_pe = c[:, 6 * LANES:7 * LANES]
    k_pe_rot = c[:, 7 * LANES:8 * LANES]

    q_t = lax.dot_general(wq_t_ref[...], cq, NT_DIMS, preferred_element_type=F32)
    cos_f = cos_f_ref[...]
    sin_f = sin_f_ref[...]
    scale = MLA_QK ** -0.5 * LOG2E
    g_q = g_q_ref[...]
    for hd in range(MLA_HEADS):
        xh = q_t[hd * MLA_QK:(hd + 1) * MLA_QK]
        rinv = lax.rsqrt(jnp.mean(xh * xh, axis=0, keepdims=True) + EPS)
        xn = xh * rinv * g_q
        x1 = xn[MLA_NOPE:MLA_NOPE + ROPE_HALF]
        x2 = xn[MLA_NOPE + ROPE_HALF:]
        q_ref[0, hd, 0:MLA_NOPE, :] = (xn[:MLA_NOPE] * scale).astype(BF16)
        q_ref[0, hd, MLA_NOPE:MLA_NOPE + ROPE_HALF, :] = ((x1 * cos_f - x2 * sin_f) * scale).astype(BF16)
        q_ref[0, hd, MLA_NOPE + ROPE_HALF:MLA_QK, :] = ((x2 * cos_f + x1 * sin_f) * scale).astype(BF16)
        q_ref[0, hd, MLA_QK:, :] = jnp.zeros((MLA_QK_PAD - MLA_QK, tm), BF16)

    k_nope = jnp.dot(ckv, wk_ref[...], preferred_element_type=F32)
    rope_base = (k_pe * g_kr_ref[...]) * cos_t_ref[...] + (k_pe_rot * g_krot_ref[...]) * sin_t_ref[...]
    ss_pe = jnp.sum(k_pe * k_pe, axis=-1, keepdims=True)
    for hd in range(MLA_HEADS):
        kh = k_nope[:, hd * MLA_NOPE:(hd + 1) * MLA_NOPE]
        ss = jnp.sum(kh * kh, axis=-1, keepdims=True) + ss_pe
        rinv = lax.rsqrt(ss * (1.0 / MLA_QK) + EPS)
        k_ref[0, hd, :, 0:LANES] = (kh * rinv * g_kn_ref[...]).astype(BF16)
        k_ref[0, hd, :, LANES:] = (rope_base * rinv).astype(BF16)

    v_t = lax.dot_general(wv_t_ref[...], ckv, NT_DIMS, preferred_element_type=F32)
    for hd in range(MLA_HEADS):
        for cc in range(tm // tk):
            v_ref[0, hd, cc] = v_t[hd * MLA_V:(hd + 1) * MLA_V, cc * tk:(cc + 1) * tk].astype(BF16)


def _mla_prep(x2d, gb, seq, tm, tk, w, tables):
    tokens = x2d.shape[0]
    nt = seq // tm
    cos_t, sin_t, cos_f, sin_f = tables
    kern = functools.partial(_mla_prep_kernel, tk=tk)
    out_shape = (
        jax.ShapeDtypeStruct((gb, MLA_HEADS, MLA_QK_PAD, seq), BF16),
        jax.ShapeDtypeStruct((gb, MLA_HEADS, seq, MLA_QK_PAD), BF16),
        jax.ShapeDtypeStruct((gb, MLA_HEADS, seq // tk, MLA_V, tk), BF16),
    )
    in_specs = [
        pl.BlockSpec((tm, D_MODEL), lambda i: (i, 0)),
        _const_spec((1, D_MODEL)),
        _const_spec((D_MODEL, 8 * LANES)),
        _const_spec((1, Q_LORA)),
        _const_spec((1, KV_LORA)),
        _const_spec((MLA_HEADS * MLA_QK, Q_LORA)),
        _const_spec((MLA_QK, 1)),
        _const_spec((KV_LORA, MLA_HEADS * MLA_NOPE)),
        _const_spec((MLA_HEADS * MLA_V, KV_LORA)),
        _const_spec((1, LANES)),
        _const_spec((1, LANES)),
        _const_spec((1, LANES)),
        pl.BlockSpec((tm, LANES), lambda i: (i % nt, 0)),
        pl.BlockSpec((tm, LANES), lambda i: (i % nt, 0)),
        pl.BlockSpec((ROPE_HALF, tm), lambda i: (0, i % nt)),
        pl.BlockSpec((ROPE_HALF, tm), lambda i: (0, i % nt)),
    ]
    out_specs = (
        pl.BlockSpec((1, MLA_HEADS, MLA_QK_PAD, tm), lambda i: (i // nt, 0, 0, i % nt)),
        pl.BlockSpec((1, MLA_HEADS, tm, MLA_QK_PAD), lambda i: (i // nt, 0, i % nt, 0)),
        pl.BlockSpec((1, MLA_HEADS, tm // tk, MLA_V, tk), lambda i: (i // nt, 0, i % nt, 0, 0)),
    )
    return pl.pallas_call(
        kern, out_shape=out_shape, grid=(tokens // tm,), in_specs=in_specs, out_specs=out_specs,
        compiler_params=_params(("arbitrary",)), name="mla_prep",
    )(x2d, w["g_mix"], w["w_a"], w["g_qa"], w["g_kva"], w["wq_t"], w["g_q"], w["wk_nope"],
      w["wv_t"], w["g_kn"], w["g_kr"], w["g_krot"], cos_t, sin_t, cos_f, sin_f)


def _group_norm_cols(x_t, gain_col):
    rows, tm = x_t.shape
    x3 = x_t.reshape(rows // DIFF_QK, DIFF_QK, tm)
    rinv = lax.rsqrt(jnp.mean(x3 * x3, axis=1, keepdims=True) + EPS)
    return (x3 * rinv * gain_col[None]).reshape(rows, tm)


def _diff_prep_kernel(x_ref, g_mix_ref, wq_t_ref, wk_t_ref, wv_t_ref, g_q_ref, g_k_ref,
                      q_ref, k_ref, v_ref, *, tk):
    tm = x_ref.shape[0]
    h = _rms_rows(x_ref[...], g_mix_ref[...]).astype(BF16)
    scale = DIFF_QK ** -0.5 * LOG2E
    q_t = lax.dot_general(wq_t_ref[...], h, NT_DIMS, preferred_element_type=F32)
    qn = (_group_norm_cols(q_t, g_q_ref[...]) * scale).astype(BF16)
    zeros = jnp.zeros((DIFF_QK, tm), BF16)
    for hd in range(DIFF_HEADS):
        r0 = hd * DIFF_V
        q_ref[0, hd, 0, 0:DIFF_QK, :] = qn[r0:r0 + DIFF_QK]
        q_ref[0, hd, 0, DIFF_QK:, :] = zeros
        q_ref[0, hd, 1, 0:DIFF_QK, :] = zeros
        q_ref[0, hd, 1, DIFF_QK:, :] = qn[r0 + DIFF_QK:r0 + DIFF_V]

    k_t = lax.dot_general(wk_t_ref[...], h, NT_DIMS, preferred_element_type=F32)
    kn = _group_norm_cols(k_t, g_k_ref[...]).T
    for hd in range(DIFF_HEADS):
        k_ref[0, hd] = kn[:, hd * DIFF_V:(hd + 1) * DIFF_V].astype(BF16)

    v_t = lax.dot_general(wv_t_ref[...], h, NT_DIMS, preferred_element_type=F32)
    for hd in range(DIFF_HEADS):
        for cc in range(tm // tk):
            v_ref[0, hd, cc] = v_t[hd * DIFF_V:(hd + 1) * DIFF_V, cc * tk:(cc + 1) * tk].astype(BF16)


def _diff_prep(x2d, gb, seq, tm, tk, w):
    tokens = x2d.shape[0]
    nt = seq // tm
    kern = functools.partial(_diff_prep_kernel, tk=tk)
    out_shape = (
        jax.ShapeDtypeStruct((gb, DIFF_HEADS, 2, DIFF_V, seq), BF16),
        jax.ShapeDtypeStruct((gb, DIFF_HEADS, seq, DIFF_V), BF16),
        jax.ShapeDtypeStruct((gb, DIFF_HEADS, seq // tk, DIFF_V, tk), BF16),
    )
    in_specs = [
        pl.BlockSpec((tm, D_MODEL), lambda i: (i, 0)),
        _const_spec((1, D_MODEL)),
        _const_spec((HEAD_OUT, D_MODEL)),
        _const_spec((HEAD_OUT, D_MODEL)),
        _const_spec((HEAD_OUT, D_MODEL)),
        _const_spec((DIFF_QK, 1)),
        _const_spec((DIFF_QK, 1)),
    ]
    out_specs = (
        pl.BlockSpec((1, DIFF_HEADS, 2, DIFF_V, tm), lambda i: (i // nt, 0, 0, 0, i % nt)),
        pl.BlockSpec((1, DIFF_HEADS, tm, DIFF_V), lambda i: (i // nt, 0, i % nt, 0)),
        pl.BlockSpec((1, DIFF_HEADS, tm // tk, DIFF_V, tk), lambda i: (i // nt, 0, i % nt, 0, 0)),
    )
    return pl.pallas_call(
        kern, out_shape=out_shape, grid=(tokens // tm,), in_specs=in_specs, out_specs=out_specs,
        compiler_params=_params(("arbitrary",)), name="diff_prep",
    )(x2d, w["g_mix"], w["w_dq_t"], w["w_dk_t"], w["w_dv_t"], w["g_dq"], w["g_dk"])


def _bias_tile_kernel(bucket_ref, rb_ref, out_ref):
    hd = pl.program_id(0)
    offset = pl.program_id(1) - pl.num_programs(1) // 2

    @pl.when(offset <= -2)
    def _():
        out_ref[0, 0] = jnp.full(out_ref.shape[2:], rb_ref[REL_BUCKETS // 2 - 1, hd] * LOG2E, F32)

    @pl.when(offset >= 2)
    def _():
        out_ref[0, 0] = jnp.full(out_ref.shape[2:], rb_ref[REL_BUCKETS - 1, hd] * LOG2E, F32)

    @pl.when(jnp.abs(offset) < 2)
    def _():
        bucket = bucket_ref[0]
        acc = jnp.zeros(bucket.shape, F32)
        for b in range(REL_BUCKETS):
            acc = jnp.where(bucket == b, rb_ref[b, hd], acc)
        out_ref[0, 0] = acc * LOG2E


def _t5_bucket(rel):
    nb = REL_BUCKETS // 2
    ret = jnp.where(rel > 0, nb, 0)
    n = jnp.abs(rel)
    max_exact = nb // 2
    nf = jnp.maximum(n, 1).astype(F32)
    large = max_exact + (jnp.log(nf / max_exact) / math.log(REL_MAX_DIST / max_exact)
                         * (nb - max_exact)).astype(jnp.int32)
    large = jnp.minimum(large, nb - 1)
    return ret + jnp.where(n < max_exact, n, large)


def _near_tiles(q_tiles, group):
    return 2 * (group + q_tiles) - 1


def _bias_tiles(rel_bias, blk, q_tiles, group):
    assert blk >= REL_MAX_DIST
    n = _near_tiles(q_tiles, group)
    d = jnp.arange(blk, dtype=jnp.int32)
    offs = (jnp.arange(n, dtype=jnp.int32) - n // 2) * blk
    rel = offs[:, None, None] + d[None, :, None] - d[None, None, :]
    bucket = _t5_bucket(rel)
    return pl.pallas_call(
        _bias_tile_kernel,
        out_shape=jax.ShapeDtypeStruct((DIFF_HEADS, n, blk, blk), F32),
        grid=(DIFF_HEADS, n),
        in_specs=[
            pl.BlockSpec((1, blk, blk), lambda hd, t: (t, 0, 0)),
            pl.BlockSpec(memory_space=pltpu.SMEM),
        ],
        out_specs=pl.BlockSpec((1, 1, blk, blk), lambda hd, t: (hd, t, 0, 0)),
        compiler_params=_params(("arbitrary", "arbitrary")), name="bias_tiles",
    )(bucket, rel_bias)


class _FlashState:
    def __init__(self, q_ref, k_ref, v_ref, scratch, blk, q_tiles):
        self.q_ref, self.k_ref, self.v_ref = q_ref, k_ref, v_ref
        (self.s_ref, self.p_ref, self.alpha_ref, self.m_ref, self.l_ref, self.acc_ref) = scratch
        self.blk = blk
        self.bound = None
        self.streams = [(qt, mp) for qt in range(q_tiles) for mp in range(q_ref.shape[2])]
        self.nc = v_ref.shape[2]

    def q(self, qt, mp):
        return self.q_ref[0, 0, mp, :, qt * self.blk:(qt + 1) * self.blk]

    def key_chunk(self, c):
        return self.k_ref[0, 0, pl.ds(pl.multiple_of(c * self.blk, self.blk), self.blk), :]


    def init_bounded(self):
        self.l_ref[...] = jnp.zeros(self.l_ref.shape, F32)
        self.acc_ref[...] = jnp.zeros(self.acc_ref.shape, F32)
        self.p_ref[1] = jnp.zeros(self.p_ref.shape[1:], BF16)

    def pv_bounded(self, c, half):
        vc = self.v_ref[0, 0, c]
        for sx, (qt, mp) in enumerate(self.streams):
            self.acc_ref[sx] += jnp.dot(vc, self.p_ref[half, sx], preferred_element_type=F32)

    def step_bounded(self, c, half, cst=None, bias_tiles=None):
        self.pv_bounded(jnp.maximum(c - 1, 0), 1 - half)
        kc = self.key_chunk(c)
        for sx, (qt, mp) in enumerate(self.streams):
            s = jnp.dot(kc, self.q(qt, mp), preferred_element_type=F32)
            if bias_tiles is not None:
                s = s + bias_tiles[qt]
            shift = self.bound if cst is None else self.bound - cst
            p = jnp.exp2(s - shift)
            self.l_ref[sx] += jnp.sum(p, axis=0, keepdims=True)
            self.p_ref[half, sx] = p.astype(BF16)

    def finish_bounded(self):
        self.pv_bounded(self.nc - 1, 1)


    def init_online(self):
        self.m_ref[...] = jnp.full(self.m_ref.shape, NEG_BIG, F32)
        self.l_ref[...] = jnp.zeros(self.l_ref.shape, F32)
        self.acc_ref[...] = jnp.zeros(self.acc_ref.shape, F32)
        self.p_ref[0] = jnp.zeros(self.p_ref.shape[1:], BF16)
        self.alpha_ref[...] = jnp.ones(self.alpha_ref.shape, F32)
        self.qk(0, 0)

    def qk(self, c, half):
        kc = self.key_chunk(c)
        for sx, (qt, mp) in enumerate(self.streams):
            self.s_ref[half, sx] = jnp.dot(kc, self.q(qt, mp), preferred_element_type=F32)

    def pv_online(self, c):
        vc = self.v_ref[0, 0, c]
        for sx, (qt, mp) in enumerate(self.streams):
            self.acc_ref[sx] = (self.alpha_ref[sx] * self.acc_ref[sx]
                                + jnp.dot(vc, self.p_ref[0, sx], preferred_element_type=F32))

    def softmax_online(self, half, cst, bias_tiles):
        for sx, (qt, mp) in enumerate(self.streams):
            s = self.s_ref[half, sx]
            if bias_tiles is not None:
                s = s + bias_tiles[qt]
            mc = jnp.max(s, axis=0, keepdims=True)
            if cst is not None:
                mc = mc + cst
            m_old = self.m_ref[sx]
            m_new = jnp.maximum(m_old, mc)
            alpha = jnp.exp2(m_old - m_new)
            shift = m_new if cst is None else m_new - cst
            p = jnp.exp2(s - shift)
            self.l_ref[sx] = alpha * self.l_ref[sx] + jnp.sum(p, axis=0, keepdims=True)
            self.p_ref[0, sx] = p.astype(BF16)
            self.alpha_ref[sx] = alpha
            self.m_ref[sx] = m_new

    def step_online(self, c, half, cst=None, bias_tiles=None):
        self.pv_online(jnp.maximum(c - 1, 0))
        self.qk(jnp.minimum(c + 1, self.nc - 1), 1 - half)
        self.softmax_online(half, cst, bias_tiles)

    def finish_online(self):
        self.pv_online(self.nc - 1)

    def run(self, bound, schedule):
        self.bound = bound
        use_bounded = bound <= BOUNDED_SOFTMAX_LIMIT

        @pl.when(use_bounded)
        def _():
            self.init_bounded()
            schedule(self.step_bounded)
            self.finish_bounded()

        @pl.when(jnp.logical_not(use_bounded))
        def _():
            self.init_online()
            schedule(self.step_online)
            self.finish_online()


def _flash_scratch(n_maps, head_dim, blk):
    return [
        pltpu.VMEM((2, n_maps, blk, blk), F32),
        pltpu.VMEM((2, n_maps, blk, blk), BF16),
        pltpu.VMEM((n_maps, 1, blk), F32),
        pltpu.VMEM((n_maps, 1, blk), F32),
        pltpu.VMEM((n_maps, 1, blk), F32),
        pltpu.VMEM((n_maps, head_dim, blk), F32),
    ]


def _score_bound(head_dim, g_q, g_k):
    return (head_dim ** 0.5 * LOG2E * 1.02 * jnp.max(jnp.abs(g_q)) * jnp.max(jnp.abs(g_k))).reshape(1)


def _mla_attn_kernel(bound_ref, q_ref, k_ref, v_ref, o_ref, *scratch, blk, q_tiles, group):
    st = _FlashState(q_ref, k_ref, v_ref, scratch, blk, q_tiles)

    def schedule(step):
        def body(cc, carry):
            for u in range(group):
                step(group * cc + u, u % 2)
            return carry

        lax.fori_loop(0, st.nc // group, body, 0)

    st.run(bound_ref[0], schedule)
    for sx, (qt, _) in enumerate(st.streams):
        o_t = st.acc_ref[sx] / st.l_ref[sx]
        o_ref[qt * blk:(qt + 1) * blk, :] = o_t.T.astype(BF16)


def _mla_attention(bound, q, k, v, gb, seq, blk, q_tiles, group):
    nc = seq // blk
    nq = nc // q_tiles
    assert group % 2 == 0 and nc % group == 0 and nc % q_tiles == 0
    kern = functools.partial(_mla_attn_kernel, blk=blk, q_tiles=q_tiles, group=group)
    q = q.reshape(gb, MLA_HEADS, 1, MLA_QK_PAD, seq)
    tq = q_tiles * blk
    return pl.pallas_call(
        kern,
        out_shape=jax.ShapeDtypeStruct((gb * seq, HEAD_OUT), BF16),
        grid=(gb, MLA_HEADS, nq),
        in_specs=[
            pl.BlockSpec(memory_space=pltpu.SMEM),
            pl.BlockSpec((1, 1, 1, MLA_QK_PAD, tq), lambda b, hd, i: (b, hd, 0, 0, i)),
            pl.BlockSpec((1, 1, seq, MLA_QK_PAD), lambda b, hd, i: (b, hd, 0, 0)),
            pl.BlockSpec((1, 1, nc, MLA_V, blk), lambda b, hd, i: (b, hd, 0, 0, 0)),
        ],
        out_specs=pl.BlockSpec((tq, MLA_V), lambda b, hd, i: (b * nq + i, hd)),
        scratch_shapes=_flash_scratch(q_tiles, MLA_V, blk),
        compiler_params=_params(("arbitrary", "arbitrary", "arbitrary")), name="mla_attn",
    )(bound, q, k, v)


def _diff_attn_kernel(bound_ref, q_ref, k_ref, v_ref, bias_ref, rb_ref, lam_ref, g_sub_ref, o_ref,
                      *scratch, blk, q_tiles, group, lam_init):
    hd = pl.program_id(0)
    i_first = pl.program_id(2) * q_tiles
    st = _FlashState(q_ref, k_ref, v_ref, scratch, blk, q_tiles)
    acc_ref, l_ref = st.acc_ref, st.l_ref
    n_groups = st.nc // group

    bias_max = jnp.abs(rb_ref[0, hd])
    for b in range(1, REL_BUCKETS):
        bias_max = jnp.maximum(bias_max, jnp.abs(rb_ref[b, hd]))

    far_before = rb_ref[REL_BUCKETS // 2 - 1, hd] * LOG2E
    far_after = rb_ref[REL_BUCKETS - 1, hd] * LOG2E
    first_near = (i_first - 1 + group) // group - 1
    last_near = (i_first + q_tiles) // group
    center_tile = group + q_tiles - 1

    near_lo = jnp.maximum(first_near, 0)
    near_hi = jnp.minimum(last_near + 1, n_groups)

    def schedule(step):
        def far_group(cst):
            def body(cc, carry):
                for u in range(group):
                    step(group * cc + u, u % 2, cst=cst)
                return carry
            return body

        def near_group(cc, carry):
            for u in range(group):
                c = group * cc + u
                tiles = [bias_ref[0, c - (i_first + qt) + center_tile] for qt in range(q_tiles)]
                step(c, u % 2, bias_tiles=tiles)
            return carry

        lax.fori_loop(0, near_lo, far_group(far_before), 0)
        lax.fori_loop(near_lo, near_hi, near_group, 0)
        lax.fori_loop(near_hi, n_groups, far_group(far_after), 0)

    st.run(bound_ref[0] + bias_max * LOG2E, schedule)

    lam_v = lam_ref[...]
    lam = (jnp.exp(jnp.sum(lam_v[0:1] * lam_v[1:2], axis=-1, keepdims=True))
           - jnp.exp(jnp.sum(lam_v[2:3] * lam_v[3:4], axis=-1, keepdims=True)) + lam_init)
    for qt in range(q_tiles):
        s0, s1 = 2 * qt, 2 * qt + 1
        o_t = acc_ref[s0] / l_ref[s0] - lam * (acc_ref[s1] / l_ref[s1])
        o_t = o_t * lax.rsqrt(jnp.mean(o_t * o_t, axis=0, keepdims=True) + EPS) * g_sub_ref[...]
        o_ref[qt * blk:(qt + 1) * blk, :] = (o_t * (1.0 - lam_init)).T.astype(BF16)


def _diff_attention(bound, q, k, v, bias, rel_bias, lam_rows, g_sub, gb, seq, blk, q_tiles, group,
                    lam_init):
    nc = seq // blk
    nq = nc // q_tiles
    assert group % 2 == 0 and nc % group == 0 and nc % q_tiles == 0 and blk >= REL_MAX_DIST
    assert bias.shape[1] == _near_tiles(q_tiles, group)
    kern = functools.partial(_diff_attn_kernel, blk=blk, q_tiles=q_tiles, group=group,
                             lam_init=lam_init)
    tq = q_tiles * blk
    return pl.pallas_call(
        kern,
        out_shape=jax.ShapeDtypeStruct((gb * seq, HEAD_OUT), BF16),
        grid=(DIFF_HEADS, gb, nq),
        in_specs=[
            pl.BlockSpec(memory_space=pltpu.SMEM),
            pl.BlockSpec((1, 1, 2, DIFF_V, tq), lambda hd, b, i: (b, hd, 0, 0, i)),
            pl.BlockSpec((1, 1, seq, DIFF_V), lambda hd, b, i: (b, hd, 0, 0)),
            pl.BlockSpec((1, 1, nc, DIFF_V, blk), lambda hd, b, i: (b, hd, 0, 0, 0)),
            pl.BlockSpec((1, bias.shape[1], blk, blk), lambda hd, b, i: (hd, 0, 0, 0),
                         pipeline_mode=pl.Buffered(1)),
            pl.BlockSpec(memory_space=pltpu.SMEM),
            _const_spec((4, LANES)),
            _const_spec((DIFF_V, 1)),
        ],
        out_specs=pl.BlockSpec((tq, DIFF_V), lambda hd, b, i: (b * nq + i, hd)),
        scratch_shapes=_flash_scratch(2 * q_tiles, DIFF_V, blk),
        compiler_params=_params(("arbitrary", "arbitrary", "arbitrary")), name="diff_attn",
    )(bound, q, k, v, bias, rel_bias, lam_rows, g_sub)


def _merge_kernel(x_ref, g_mix_ref, oa_ref, ob_ref, wga_ref, wgb_ref, wua_ref, wub_ref,
                  out_ref, h_ref):
    @pl.when(pl.program_id(1) == 0)
    def _():
        h_ref[...] = _rms_rows(x_ref[...], g_mix_ref[...]).astype(BF16)

    h = h_ref[...]
    ga = jnp.dot(h, wga_ref[...], preferred_element_type=F32)
    gb_ = jnp.dot(h, wgb_ref[...], preferred_element_type=F32)
    ua = jnp.dot(oa_ref[...], wua_ref[...], preferred_element_type=F32)
    ub = jnp.dot(ob_ref[...], wub_ref[...], preferred_element_type=F32)
    out_ref[...] = (jax.nn.sigmoid(ga) * ua + jax.nn.sigmoid(gb_) * ub).astype(BF16)


def _merge(x2d, o_a, o_b, w, tm, tn):
    tokens = x2d.shape[0]
    return pl.pallas_call(
        _merge_kernel,
        out_shape=jax.ShapeDtypeStruct((tokens, D_MODEL), BF16),
        grid=(tokens // tm, D_MODEL // tn),
        in_specs=[
            pl.BlockSpec((tm, D_MODEL), lambda i, j: (i, 0)),
            _const_spec((1, D_MODEL)),
            pl.BlockSpec((tm, HEAD_OUT), lambda i, j: (i, 0)),
            pl.BlockSpec((tm, HEAD_OUT), lambda i, j: (i, 0)),
            pl.BlockSpec((D_MODEL, tn), lambda i, j: (0, j)),
            pl.BlockSpec((D_MODEL, tn), lambda i, j: (0, j)),
            pl.BlockSpec((HEAD_OUT, tn), lambda i, j: (0, j)),
            pl.BlockSpec((HEAD_OUT, tn), lambda i, j: (0, j)),
        ],
        out_specs=pl.BlockSpec((tm, tn), lambda i, j: (i, j)),
        scratch_shapes=[pltpu.VMEM((tm, D_MODEL), BF16)],
        compiler_params=_params(("arbitrary", "arbitrary")), name="gated_merge",
    )(x2d, w["g_mix"], o_a, o_b, w["w_ga"], w["w_gb"], w["w_ua"], w["w_ub"])


def _out_proj_kernel(x_ref, m_ref, wo_ref, out_ref):
    out_ref[...] = x_ref[...] + jnp.dot(m_ref[...], wo_ref[...], preferred_element_type=F32)


def _out_proj(x2d, merged, w, tm):
    tokens = x2d.shape[0]
    return pl.pallas_call(
        _out_proj_kernel,
        out_shape=jax.ShapeDtypeStruct((tokens, D_MODEL), F32),
        grid=(tokens // tm,),
        in_specs=[
            pl.BlockSpec((tm, D_MODEL), lambda i: (i, 0)),
            pl.BlockSpec((tm, D_MODEL), lambda i: (i, 0)),
            _const_spec((D_MODEL, D_MODEL)),
        ],
        out_specs=pl.BlockSpec((tm, D_MODEL), lambda i: (i, 0)),
        compiler_params=_params(("arbitrary",)), name="out_proj",
    )(x2d, merged, w["w_o"])


def _ffn_kernel(x_ref, g_ref, wg_ref, wu_ref, wd_ref, out_ref, h_ref):
    @pl.when(pl.program_id(1) == 0)
    def _():
        x = x_ref[...]
        h_ref[...] = _rms_rows(x, g_ref[...]).astype(BF16)
        out_ref[...] = x

    h = h_ref[...]
    g = jnp.dot(h, wg_ref[...], preferred_element_type=F32)
    u = jnp.dot(h, wu_ref[...], preferred_element_type=F32)
    a = (g * jax.nn.sigmoid(g) * u).astype(BF16)
    out_ref[...] += jnp.dot(a, wd_ref[...], preferred_element_type=F32)


def _ffn(x2d, w, tm, tf):
    tokens = x2d.shape[0]
    return pl.pallas_call(
        _ffn_kernel,
        out_shape=jax.ShapeDtypeStruct((tokens, D_MODEL), F32),
        grid=(tokens // tm, D_FF // tf),
        in_specs=[
            pl.BlockSpec((tm, D_MODEL), lambda i, j: (i, 0)),
            _const_spec((1, D_MODEL)),
            pl.BlockSpec((D_MODEL, tf), lambda i, j: (0, j)),
            pl.BlockSpec((D_MODEL, tf), lambda i, j: (0, j)),
            pl.BlockSpec((tf, D_MODEL), lambda i, j: (j, 0)),
        ],
        out_specs=pl.BlockSpec((tm, D_MODEL), lambda i, j: (i, 0)),
        scratch_shapes=[pltpu.VMEM((tm, D_MODEL), BF16)],
        compiler_params=_params(("arbitrary", "arbitrary")), name="swiglu_ffn",
    )(x2d, w["g_ffn"], w["w_gate"], w["w_up"], w["w_down"])


def _prepare_weights(mix_norm, w_in, q_a_norm, wq_b, kv_a_norm, wkv_b, mla_q_norm, mla_k_norm,
                     diff_q_norm, diff_k_norm, diff_subln, w_up_mla, w_up_diff, w_o, ffn_norm,
                     w_gate, w_up, w_down, layer):
    win = w_in[layer]
    o_cq, o_ckv, o_kpe = 0, Q_LORA, Q_LORA + KV_LORA
    o_dq = o_kpe + MLA_ROPE
    o_dk = o_dq + HEAD_OUT
    o_dv = o_dk + HEAD_OUT
    o_ga = o_dv + HEAD_OUT
    o_gb = o_ga + D_MODEL
    w_kpe = win[:, o_kpe:o_dq]
    w_kpe_rot = jnp.concatenate([-w_kpe[:, ROPE_HALF:], w_kpe[:, :ROPE_HALF]], axis=1)
    zpad = jnp.zeros((D_MODEL, LANES - MLA_ROPE), F32)
    w_a = jnp.concatenate([win[:, o_cq:o_kpe], w_kpe, zpad, w_kpe_rot, zpad], axis=1)
    wkv = wkv_b[layer].reshape(KV_LORA, MLA_HEADS, MLA_NOPE + MLA_V)
    gk = mla_k_norm[layer]
    gk_rope = gk[MLA_NOPE:]
    lane_pad = jnp.zeros((LANES - MLA_ROPE,), F32)
    return {
        "g_mix": mix_norm[layer][None, :],
        "w_a": w_a.astype(BF16),
        "g_qa": q_a_norm[layer][None, :],
        "g_kva": kv_a_norm[layer][None, :],
        "wq_t": wq_b[layer].T.astype(BF16),
        "g_q": mla_q_norm[layer][:, None],
        "wk_nope": wkv[:, :, :MLA_NOPE].reshape(KV_LORA, MLA_HEADS * MLA_NOPE).astype(BF16),
        "wv_t": wkv[:, :, MLA_NOPE:].reshape(KV_LORA, MLA_HEADS * MLA_V).T.astype(BF16),
        "g_kn": gk[None, :MLA_NOPE],
        "g_kr": jnp.concatenate([gk_rope, lane_pad])[None, :],
        "g_krot": jnp.concatenate([gk_rope[ROPE_HALF:], gk_rope[:ROPE_HALF], lane_pad])[None, :],
        "w_dq_t": win[:, o_dq:o_dk].T.astype(BF16),
        "w_dk_t": win[:, o_dk:o_dv].T.astype(BF16),
        "w_dv_t": win[:, o_dv:o_ga].T.astype(BF16),
        "g_dq": diff_q_norm[layer][:, None],
        "g_dk": diff_k_norm[layer][:, None],
        "g_sub": diff_subln[layer][:, None],
        "w_ga": win[:, o_ga:o_gb].astype(BF16),
        "w_gb": win[:, o_gb:].astype(BF16),
        "w_ua": w_up_mla[layer].astype(BF16),
        "w_ub": w_up_diff[layer].astype(BF16),
        "w_o": w_o[layer].astype(BF16),
        "mla_bound": _score_bound(MLA_QK, mla_q_norm[layer], gk),
        "diff_bound": _score_bound(DIFF_QK, diff_q_norm[layer], diff_k_norm[layer]),
        "g_ffn": ffn_norm[layer][None, :],
        "w_gate": w_gate[layer].astype(BF16),
        "w_up": w_up[layer].astype(BF16),
        "w_down": w_down[layer].astype(BF16),
    }


class _Tiles(NamedTuple):
    tm: int = 512
    tn: int = 512
    tf: int = 512
    blk: int = 512
    q_tiles: int = 2
    diff_group: int = 4
    mla_group_max: int = 8


def _encoder_layer(x, w, tables, bias, rel_bias, lam_rows, lam_init, t):
    gb, seq, _ = x.shape
    x2d = x.reshape(gb * seq, D_MODEL)
    mla_group = math.gcd(t.mla_group_max, seq // t.blk)
    q_a, k_a, v_a = _mla_prep(x2d, gb, seq, t.tm, t.blk, w, tables)
    q_d, k_d, v_d = _diff_prep(x2d, gb, seq, t.tm, t.blk, w)
    o_a = _mla_attention(w["mla_bound"], q_a, k_a, v_a, gb, seq, t.blk, t.q_tiles, mla_group)
    o_b = _diff_attention(w["diff_bound"], q_d, k_d, v_d, bias, rel_bias, lam_rows, w["g_sub"], gb,
                          seq, t.blk, t.q_tiles, t.diff_group, lam_init)
    merged = _merge(x2d, o_a, o_b, w, t.tm, t.tn)
    x1 = _out_proj(x2d, merged, w, t.tm)
    y = _ffn(x1, w, t.tm, t.tf)
    return y.reshape(gb, seq, D_MODEL)


def kernel(x_prompt, x_sample, mix_norm, w_in, q_a_norm, wq_b, kv_a_norm, wkv_b, mla_q_norm, mla_k_norm, diff_q_norm, diff_k_norm, lambda_q1, lambda_k1, lambda_q2, lambda_k2, diff_subln, w_up_mla, w_up_diff, w_o, ffn_norm, w_gate, w_up, w_down, rel_bias):
    t = _Tiles()
    depth = w_in.shape[0]
    max_seq = max(x_prompt.shape[1], x_sample.shape[1])
    tables = _rope_tables(max_seq)
    bias = _bias_tiles(rel_bias, t.blk, t.q_tiles, t.diff_group)
    y_prompt, y_sample = x_prompt, x_sample
    for layer in range(depth):
        w = _prepare_weights(mix_norm, w_in, q_a_norm, wq_b, kv_a_norm, wkv_b, mla_q_norm,
                             mla_k_norm, diff_q_norm, diff_k_norm, diff_subln, w_up_mla,
                             w_up_diff, w_o, ffn_norm, w_gate, w_up, w_down, layer)
        lam_init = 0.8 - 0.6 * math.exp(-0.3 * layer)
        lam_pad = jnp.zeros((LANES - DIFF_QK,), F32)
        lam_rows = jnp.stack([jnp.concatenate([v[layer], lam_pad])
                              for v in (lambda_q1, lambda_k1, lambda_q2, lambda_k2)])
        run = functools.partial(_encoder_layer, w=w, tables=tables, bias=bias, rel_bias=rel_bias,
                                lam_rows=lam_rows, lam_init=lam_init, t=t)
        y_prompt = run(y_prompt)
        y_sample = run(y_sample)
    return (y_prompt, y_sample)
```

```python
import functools
import math
from typing import NamedTuple

import jax
import jax.numpy as jnp
from jax import lax
from jax.experimental import pallas as pl
from jax.experimental.pallas import tpu as pltpu

D_MODEL = 2048
MLA_HEADS = 8
MLA_NOPE = 128
MLA_ROPE = 64
MLA_QK = MLA_NOPE + MLA_ROPE
MLA_V = 128
Q_LORA = 512
KV_LORA = 256
ROPE_THETA = 10000.0
ROPE_HALF = MLA_ROPE // 2
DIFF_HEADS = 8
DIFF_QK = 64
DIFF_V = 2 * DIFF_QK
REL_BUCKETS = 32
REL_MAX_DIST = 128
D_FF = 5632
EPS = 1e-6
HEAD_OUT = MLA_HEADS * MLA_V

LANES = 128
MLA_QK_PAD = 2 * LANES
VMEM_LIMIT = 56 * 1024 * 1024

F32 = jnp.float32
BF16 = jnp.bfloat16
NEG_BIG = -1e30
LOG2E = math.log2(math.e)
BOUNDED_SOFTMAX_LIMIT = 50.0
BIAS_TILES = 5

NT_DIMS = (((1,), (1,)), ((), ()))


def _params(semantics):
    return pltpu.CompilerParams(dimension_semantics=semantics, vmem_limit_bytes=VMEM_LIMIT)


def _const_spec(shape):
    zeros = (0,) * len(shape)
    return pl.BlockSpec(shape, lambda *_: zeros)


def _rms_rows(x, gain):
    return x * lax.rsqrt(jnp.mean(x * x, axis=-1, keepdims=True) + EPS) * gain


def _rope_table_kernel(inv_row_ref, inv_col_ref, cos_t_ref, sin_t_ref, cos_f_ref, sin_f_ref):
    ts = cos_t_ref.shape[0]
    base = pl.program_id(0) * ts
    pos_rows = (base + lax.broadcasted_iota(jnp.int32, (ts, LANES), 0)).astype(F32)
    ang_t = pos_rows * inv_row_ref[...]
    cos_t_ref[...] = jnp.cos(ang_t)
    sin_t_ref[...] = jnp.sin(ang_t)
    pos_cols = (base + lax.broadcasted_iota(jnp.int32, (ROPE_HALF, ts), 1)).astype(F32)
    ang_f = pos_cols * inv_col_ref[...]
    cos_f_ref[...] = jnp.cos(ang_f)
    sin_f_ref[...] = jnp.sin(ang_f)


def _rope_tables(seq):
    inv = ROPE_THETA ** (-jnp.arange(ROPE_HALF, dtype=F32) / ROPE_HALF)
    inv_row = jnp.tile(inv, LANES // ROPE_HALF)[None, :]
    inv_col = inv[:, None]
    ts = min(seq, 2048)
    tok = jax.ShapeDtypeStruct((seq, LANES), F32)
    feat = jax.ShapeDtypeStruct((ROPE_HALF, seq), F32)
    return pl.pallas_call(
        _rope_table_kernel,
        out_shape=(tok, tok, feat, feat),
        grid=(seq // ts,),
        in_specs=[_const_spec((1, LANES)), _const_spec((ROPE_HALF, 1))],
        out_specs=(
            pl.BlockSpec((ts, LANES), lambda i: (i, 0)),
            pl.BlockSpec((ts, LANES), lambda i: (i, 0)),
            pl.BlockSpec((ROPE_HALF, ts), lambda i: (0, i)),
            pl.BlockSpec((ROPE_HALF, ts), lambda i: (0, i)),
        ),
        compiler_params=_params(("arbitrary",)), name="rope_tables",
    )(inv_row, inv_col)


def _mla_prep_kernel(x_ref, g_mix_ref, w_a_ref, g_qa_ref, g_kva_ref, wq_t_ref, g_q_ref,
                     wk_ref, wv_t_ref, g_kn_ref, g_kr_ref, g_krot_ref,
                     cos_t_ref, sin_t_ref, cos_f_ref, sin_f_ref,
                     q_ref, k_ref, v_ref, *, tk):
    tm = x_ref.shape[0]
    h = _rms_rows(x_ref[...], g_mix_ref[...]).astype(BF16)
    c = jnp.dot(h, w_a_ref[...], preferred_element_type=F32)
    cq = _rms_rows(c[:, :Q_LORA], g_qa_ref[...]).astype(BF16)
    ckv = _rms_rows(c[:, Q_LORA:Q_LORA + KV_LORA], g_kva_ref[...]).astype(BF16)
    k_pe = c[:, 6 * LANES:7 * LANES]
    k_pe_rot = c[:, 7 * LANES:8 * LANES]

    q_t = lax.dot_general(wq_t_ref[...], cq, NT_DIMS, preferred_element_type=F32)
    cos_f = cos_f_ref[...]
    sin_f = sin_f_ref[...]
    scale = MLA_QK ** -0.5 * LOG2E
    g_q = g_q_ref[...]
    for hd in range(MLA_HEADS):
        xh = q_t[hd * MLA_QK:(hd + 1) * MLA_QK]
        rinv = lax.rsqrt(jnp.mean(xh * xh, axis=0, keepdims=True) + EPS)
        xn = xh * rinv * g_q
        x1 = xn[MLA_NOPE:MLA_NOPE + ROPE_HALF]
        x2 = xn[MLA_NOPE + ROPE_HALF:]
        q_ref[0, hd, 0:MLA_NOPE, :] = (xn[:MLA_NOPE] * scale).astype(BF16)
        q_ref[0, hd, MLA_NOPE:MLA_NOPE + ROPE_HALF, :] = ((x1 * cos_f - x2 * sin_f) * scale).astype(BF16)
        q_ref[0, hd, MLA_NOPE + ROPE_HALF:MLA_QK, :] = ((x2 * cos_f + x1 * sin_f) * scale).astype(BF16)
        q_ref[0, hd, MLA_QK:, :] = jnp.zeros((MLA_QK_PAD - MLA_QK, tm), BF16)

    k_nope = jnp.dot(ckv, wk_ref[...], preferred_element_type=F32)
    rope_base = (k_pe * g_kr_ref[...]) * cos_t_ref[...] + (k_pe_rot * g_krot_ref[...]) * sin_t_ref[...]
    ss_pe = jnp.sum(k_pe * k_pe, axis=-1, keepdims=True)
    for hd in range(MLA_HEADS):
        kh = k_nope[:, hd * MLA_NOPE:(hd + 1) * MLA_NOPE]
        ss = jnp.sum(kh * kh, axis=-1, keepdims=True) + ss_pe
        rinv = lax.rsqrt(ss * (1.0 / MLA_QK) + EPS)
        k_ref[0, hd, :, 0:LANES] = (kh * rinv * g_kn_ref[...]).astype(BF16)
        k_ref[0, hd, :, LANES:] = (rope_base * rinv).astype(BF16)

    v_t = lax.dot_general(wv_t_ref[...], ckv, NT_DIMS, preferred_element_type=F32)
    for hd in range(MLA_HEADS):
        for cc in range(tm // tk):
            v_ref[0, hd, cc] = v_t[hd * MLA_V:(hd + 1) * MLA_V, cc * tk:(cc + 1) * tk].astype(BF16)


def _mla_prep(x2d, gb, seq, tm, tk, w, tables):
    tokens = x2d.shape[0]
    nt = seq // tm
    cos_t, sin_t, cos_f, sin_f = tables
    kern = functools.partial(_mla_prep_kernel, tk=tk)
    out_shape = (
        jax.ShapeDtypeStruct((gb, MLA_HEADS, MLA_QK_PAD, seq), BF16),
        jax.ShapeDtypeStruct((gb, MLA_HEADS, seq, MLA_QK_PAD), BF16),
        jax.ShapeDtypeStruct((gb, MLA_HEADS, seq // tk, MLA_V, tk), BF16),
    )
    in_specs = [
        pl.BlockSpec((tm, D_MODEL), lambda i: (i, 0)),
        _const_spec((1, D_MODEL)),
        _const_spec((D_MODEL, 8 * LANES)),
        _const_spec((1, Q_LORA)),
        _const_spec((1, KV_LORA)),
        _const_spec((MLA_HEADS * MLA_QK, Q_LORA)),
        _const_spec((MLA_QK, 1)),
        _const_spec((KV_LORA, MLA_HEADS * MLA_NOPE)),
        _const_spec((MLA_HEADS * MLA_V, KV_LORA)),
        _const_spec((1, LANES)),
        _const_spec((1, LANES)),
        _const_spec((1, LANES)),
        pl.BlockSpec((tm, LANES), lambda i: (i % nt, 0)),
        pl.BlockSpec((tm, LANES), lambda i: (i % nt, 0)),
        pl.BlockSpec((ROPE_HALF, tm), lambda i: (0, i % nt)),
        pl.BlockSpec((ROPE_HALF, tm), lambda i: (0, i % nt)),
    ]
    out_specs = (
        pl.BlockSpec((1, MLA_HEADS, MLA_QK_PAD, tm), lambda i: (i // nt, 0, 0, i % nt)),
        pl.BlockSpec((1, MLA_HEADS, tm, MLA_QK_PAD), lambda i: (i // nt, 0, i % nt, 0)),
        pl.BlockSpec((1, MLA_HEADS, tm // tk, MLA_V, tk), lambda i: (i // nt, 0, i % nt, 0, 0)),
    )
    return pl.pallas_call(
        kern, out_shape=out_shape, grid=(tokens // tm,), in_specs=in_specs, out_specs=out_specs,
        compiler_params=_params(("arbitrary",)), name="mla_prep",
    )(x2d, w["g_mix"], w["w_a"], w["g_qa"], w["g_kva"], w["wq_t"], w["g_q"], w["wk_nope"],
      w["wv_t"], w["g_kn"], w["g_kr"], w["g_krot"], cos_t, sin_t, cos_f, sin_f)


def _group_norm_cols(x_t, gain_col):
    rows, tm = x_t.shape
    x3 = x_t.reshape(rows // DIFF_QK, DIFF_QK, tm)
    rinv = lax.rsqrt(jnp.mean(x3 * x3, axis=1, keepdims=True) + EPS)
    return (x3 * rinv * gain_col[None]).reshape(rows, tm)


def _diff_prep_kernel(x_ref, g_mix_ref, wq_t_ref, wk_t_ref, wv_t_ref, g_q_ref, g_k_ref,
                      q_ref, k_ref, v_ref, *, tk):
    tm = x_ref.shape[0]
    h = _rms_rows(x_ref[...], g_mix_ref[...]).astype(BF16)
    scale = DIFF_QK ** -0.5 * LOG2E
    q_t = lax.dot_general(wq_t_ref[...], h, NT_DIMS, preferred_element_type=F32)
    qn = (_group_norm_cols(q_t, g_q_ref[...]) * scale).astype(BF16)
    zeros = jnp.zeros((DIFF_QK, tm), BF16)
    for hd in range(DIFF_HEADS):
        r0 = hd * DIFF_V
        q_ref[0, hd, 0, 0:DIFF_QK, :] = qn[r0:r0 + DIFF_QK]
        q_ref[0, hd, 0, DIFF_QK:, :] = zeros
        q_ref[0, hd, 1, 0:DIFF_QK, :] = zeros
        q_ref[0, hd, 1, DIFF_QK:, :] = qn[r0 + DIFF_QK:r0 + DIFF_V]

    k_t = lax.dot_general(wk_t_ref[...], h, NT_DIMS, preferred_element_type=F32)
    kn = _group_norm_cols(k_t, g_k_ref[...]).T
    for hd in range(DIFF_HEADS):
        k_ref[0, hd] = kn[:, hd * DIFF_V:(hd + 1) * DIFF_V].astype(BF16)

    v_t = lax.dot_general(wv_t_ref[...], h, NT_DIMS, preferred_element_type=F32)
    for hd in range(DIFF_HEADS):
        for cc in range(tm // tk):
            v_ref[0, hd, cc] = v_t[hd * DIFF_V:(hd + 1) * DIFF_V, cc * tk:(cc + 1) * tk].astype(BF16)


def _diff_prep(x2d, gb, seq, tm, tk, w):
    tokens = x2d.shape[0]
    nt = seq // tm
    kern = functools.partial(_diff_prep_kernel, tk=tk)
    out_shape = (
        jax.ShapeDtypeStruct((gb, DIFF_HEADS, 2, DIFF_V, seq), BF16),
        jax.ShapeDtypeStruct((gb, DIFF_HEADS, seq, DIFF_V), BF16),
        jax.ShapeDtypeStruct((gb, DIFF_HEADS, seq // tk, DIFF_V, tk), BF16),
    )
    in_specs = [
        pl.BlockSpec((tm, D_MODEL), lambda i: (i, 0)),
        _const_spec((1, D_MODEL)),
        _const_spec((HEAD_OUT, D_MODEL)),
        _const_spec((HEAD_OUT, D_MODEL)),
        _const_spec((HEAD_OUT, D_MODEL)),
        _const_spec((DIFF_QK, 1)),
        _const_spec((DIFF_QK, 1)),
    ]
    out_specs = (
        pl.BlockSpec((1, DIFF_HEADS, 2, DIFF_V, tm), lambda i: (i // nt, 0, 0, 0, i % nt)),
        pl.BlockSpec((1, DIFF_HEADS, tm, DIFF_V), lambda i: (i // nt, 0, i % nt, 0)),
        pl.BlockSpec((1, DIFF_HEADS, tm // tk, DIFF_V, tk), lambda i: (i // nt, 0, i % nt, 0, 0)),
    )
    return pl.pallas_call(
        kern, out_shape=out_shape, grid=(tokens // tm,), in_specs=in_specs, out_specs=out_specs,
        compiler_params=_params(("arbitrary",)), name="diff_prep",
    )(x2d, w["g_mix"], w["w_dq_t"], w["w_dk_t"], w["w_dv_t"], w["g_dq"], w["g_dk"])


def _bias_tile_kernel(bucket_ref, rb_ref, out_ref):
    hd = pl.program_id(0)
    offset = pl.program_id(1) - pl.num_programs(1) // 2

    @pl.when(offset <= -2)
    def _():
        out_ref[0, 0] = jnp.full(out_ref.shape[2:], rb_ref[REL_BUCKETS // 2 - 1, hd] * LOG2E, F32)

    @pl.when(offset >= 2)
    def _():
        out_ref[0, 0] = jnp.full(out_ref.shape[2:], rb_ref[REL_BUCKETS - 1, hd] * LOG2E, F32)

    @pl.when(jnp.abs(offset) < 2)
    def _():
        bucket = bucket_ref[0]
        acc = jnp.zeros(bucket.shape, F32)
        for b in range(REL_BUCKETS):
            acc = jnp.where(bucket == b, rb_ref[b, hd], acc)
        out_ref[0, 0] = acc * LOG2E


def _t5_bucket(rel):
    nb = REL_BUCKETS // 2
    ret = jnp.where(rel > 0, nb, 0)
    n = jnp.abs(rel)
    max_exact = nb // 2
    nf = jnp.maximum(n, 1).astype(F32)
    large = max_exact + (jnp.log(nf / max_exact) / math.log(REL_MAX_DIST / max_exact)
                         * (nb - max_exact)).astype(jnp.int32)
    large = jnp.minimum(large, nb - 1)
    return ret + jnp.where(n < max_exact, n, large)


def _bias_tiles(rel_bias, blk):
    assert blk >= REL_MAX_DIST
    n = BIAS_TILES
    d = jnp.arange(blk, dtype=jnp.int32)
    offs = (jnp.arange(n, dtype=jnp.int32) - n // 2) * blk
    rel = offs[:, None, None] + d[None, :, None] - d[None, None, :]
    bucket = _t5_bucket(rel)
    return pl.pallas_call(
        _bias_tile_kernel,
        out_shape=jax.ShapeDtypeStruct((DIFF_HEADS, n, blk, blk), F32),
        grid=(DIFF_HEADS, n),
        in_specs=[
            pl.BlockSpec((1, blk, blk), lambda hd, t: (t, 0, 0)),
            pl.BlockSpec(memory_space=pltpu.SMEM),
        ],
        out_specs=pl.BlockSpec((1, 1, blk, blk), lambda hd, t: (hd, t, 0, 0)),
        compiler_params=_params(("arbitrary", "arbitrary")), name="bias_tiles",
    )(bucket, rel_bias)


class _FlashState:
    def __init__(self, q_ref, k_ref, v_ref, scratch, blk, q_tiles):
        self.q_ref, self.k_ref, self.v_ref = q_ref, k_ref, v_ref
        (self.s_ref, self.p_ref, self.alpha_ref, self.m_ref, self.l_ref, self.acc_ref) = scratch
        self.blk = blk
        self.bound = None
        self.streams = [(qt, mp) for qt in range(q_tiles) for mp in range(q_ref.shape[2])]
        self.nc = v_ref.shape[2]

    def q(self, qt, mp):
        return self.q_ref[0, 0, mp, :, qt * self.blk:(qt + 1) * self.blk]

    def key_chunk(self, c):
        return self.k_ref[0, 0, pl.ds(pl.multiple_of(c * self.blk, self.blk), self.blk), :]


    def init_bounded(self):
        self.l_ref[...] = jnp.zeros(self.l_ref.shape, F32)
        self.acc_ref[...] = jnp.zeros(self.acc_ref.shape, F32)
        self.p_ref[1] = jnp.zeros(self.p_ref.shape[1:], BF16)

    def pv_bounded(self, c, half):
        vc = self.v_ref[0, 0, c]
        for sx, (qt, mp) in enumerate(self.streams):
            self.acc_ref[sx] += jnp.dot(vc, self.p_ref[half, sx], preferred_element_type=F32)

    def step_bounded(self, c, half, cst=None, bias_tiles=None):
        self.pv_bounded(jnp.maximum(c - 1, 0), 1 - half)
        kc = self.key_chunk(c)
        for sx, (qt, mp) in enumerate(self.streams):
            s = jnp.dot(kc, self.q(qt, mp), preferred_element_type=F32)
            if bias_tiles is not None:
                s = s + bias_tiles[qt]
            shift = self.bound if cst is None else self.bound - cst
            p = jnp.exp2(s - shift)
            self.l_ref[sx] += jnp.sum(p, axis=0, keepdims=True)
            self.p_ref[half, sx] = p.astype(BF16)

    def finish_bounded(self):
        self.pv_bounded(self.nc - 1, 1)


    def init_online(self):
        self.m_ref[...] = jnp.full(self.m_ref.shape, NEG_BIG, F32)
        self.l_ref[...] = jnp.zeros(self.l_ref.shape, F32)
        self.acc_ref[...] = jnp.zeros(self.acc_ref.shape, F32)
        self.p_ref[0] = jnp.zeros(self.p_ref.shape[1:], BF16)
        self.alpha_ref[...] = jnp.ones(self.alpha_ref.shape, F32)
        self.qk(0, 0)

    def qk(self, c, half):
        kc = self.key_chunk(c)
        for sx, (qt, mp) in enumerate(self.streams):
            self.s_ref[half, sx] = jnp.dot(kc, self.q(qt, mp), preferred_element_type=F32)

    def pv_online(self, c):
        vc = self.v_ref[0, 0, c]
        for sx, (qt, mp) in enumerate(self.streams):
            self.acc_ref[sx] = (self.alpha_ref[sx] * self.acc_ref[sx]
                                + jnp.dot(vc, self.p_ref[0, sx], preferred_element_type=F32))

    def softmax_online(self, half, cst, bias_tiles):
        for sx, (qt, mp) in enumerate(self.streams):
            s = self.s_ref[half, sx]
            if bias_tiles is not None:
                s = s + bias_tiles[qt]
            mc = jnp.max(s, axis=0, keepdims=True)
            if cst is not None:
                mc = mc + cst
            m_old = self.m_ref[sx]
            m_new = jnp.maximum(m_old, mc)
            alpha = jnp.exp2(m_old - m_new)
            shift = m_new if cst is None else m_new - cst
            p = jnp.exp2(s - shift)
            self.l_ref[sx] = alpha * self.l_ref[sx] + jnp.sum(p, axis=0, keepdims=True)
            self.p_ref[0, sx] = p.astype(BF16)
            self.alpha_ref[sx] = alpha
            self.m_ref[sx] = m_new

    def step_online(self, c, half, cst=None, bias_tiles=None):
        self.pv_online(jnp.maximum(c - 1, 0))
        self.qk(jnp.minimum(c + 1, self.nc - 1), 1 - half)
        self.softmax_online(half, cst, bias_tiles)

    def finish_online(self):
        self.pv_online(self.nc - 1)

    def run(self, bound, schedule):
        self.bound = bound
        use_bounded = bound <= BOUNDED_SOFTMAX_LIMIT

        @pl.when(use_bounded)
        def _():
            self.init_bounded()
            schedule(self.step_bounded)
            self.finish_bounded()

        @pl.when(jnp.logical_not(use_bounded))
        def _():
            self.init_online()
            schedule(self.step_online)
            self.finish_online()


def _flash_scratch(n_maps, head_dim, blk):
    return [
        pltpu.VMEM((2, n_maps, blk, blk), F32),
        pltpu.VMEM((2, n_maps, blk, blk), BF16),
        pltpu.VMEM((n_maps, 1, blk), F32),
        pltpu.VMEM((n_maps, 1, blk), F32),
        pltpu.VMEM((n_maps, 1, blk), F32),
        pltpu.VMEM((n_maps, head_dim, blk), F32),
    ]


def _score_bound(head_dim, g_q, g_k):
    return (head_dim ** 0.5 * LOG2E * 1.02 * jnp.max(jnp.abs(g_q)) * jnp.max(jnp.abs(g_k))).reshape(1)


def _mla_attn_kernel(bound_ref, q_ref, k_ref, v_ref, o_ref, *scratch, blk, q_tiles, group):
    st = _FlashState(q_ref, k_ref, v_ref, scratch, blk, q_tiles)

    def schedule(step):
        def body(cc, carry):
            for u in range(group):
                step(group * cc + u, u % 2)
            return carry

        lax.fori_loop(0, st.nc // group, body, 0)

    st.run(bound_ref[0], schedule)
    for sx, (qt, _) in enumerate(st.streams):
        o_t = st.acc_ref[sx] / st.l_ref[sx]
        o_ref[qt * blk:(qt + 1) * blk, :] = o_t.T.astype(BF16)


def _mla_attention(bound, q, k, v, gb, seq, blk, q_tiles, group):
    nc = seq // blk
    nq = nc // q_tiles
    assert group % 2 == 0 and nc % group == 0 and nc % q_tiles == 0
    kern = functools.partial(_mla_attn_kernel, blk=blk, q_tiles=q_tiles, group=group)
    q = q.reshape(gb, MLA_HEADS, 1, MLA_QK_PAD, seq)
    tq = q_tiles * blk
    return pl.pallas_call(
        kern,
        out_shape=jax.ShapeDtypeStruct((gb * seq, HEAD_OUT), BF16),
        grid=(gb, MLA_HEADS, nq),
        in_specs=[
            pl.BlockSpec(memory_space=pltpu.SMEM),
            pl.BlockSpec((1, 1, 1, MLA_QK_PAD, tq), lambda b, hd, i: (b, hd, 0, 0, i)),
            pl.BlockSpec((1, 1, seq, MLA_QK_PAD), lambda b, hd, i: (b, hd, 0, 0)),
            pl.BlockSpec((1, 1, nc, MLA_V, blk), lambda b, hd, i: (b, hd, 0, 0, 0)),
        ],
        out_specs=pl.BlockSpec((tq, MLA_V), lambda b, hd, i: (b * nq + i, hd)),
        scratch_shapes=_flash_scratch(q_tiles, MLA_V, blk),
        compiler_params=_params(("arbitrary", "arbitrary", "arbitrary")), name="mla_attn",
    )(bound, q, k, v)


def _diff_attn_kernel(bound_ref, q_ref, k_ref, v_ref, bias_ref, rb_ref, lam_ref, g_sub_ref, o_ref,
                      *scratch, blk, q_tiles, group, lam_init):
    hd = pl.program_id(0)
    i_first = pl.program_id(2) * q_tiles
    st = _FlashState(q_ref, k_ref, v_ref, scratch, blk, q_tiles)
    acc_ref, l_ref = st.acc_ref, st.l_ref
    n_groups = st.nc // group

    bias_max = jnp.abs(rb_ref[0, hd])
    for b in range(1, REL_BUCKETS):
        bias_max = jnp.maximum(bias_max, jnp.abs(rb_ref[b, hd]))

    far_before = rb_ref[REL_BUCKETS // 2 - 1, hd] * LOG2E
    far_after = rb_ref[REL_BUCKETS - 1, hd] * LOG2E
    first_near = (i_first - 1 + group) // group - 1
    last_near = (i_first + q_tiles) // group

    near_lo = jnp.maximum(first_near, 0)
    near_hi = jnp.minimum(last_near + 1, n_groups)

    def schedule(step):
        def far_group(cst):
            def body(cc, carry):
                for u in range(group):
                    step(group * cc + u, u % 2, cst=cst)
                return carry
            return body

        def near_group(cc, carry):
            for u in range(group):
                c = group * cc + u
                far = BIAS_TILES // 2
                tiles = [bias_ref[0, jnp.clip(c - (i_first + qt), -far, far) + far]
                         for qt in range(q_tiles)]
                step(c, u % 2, bias_tiles=tiles)
            return carry

        lax.fori_loop(0, near_lo, far_group(far_before), 0)
        lax.fori_loop(near_lo, near_hi, near_group, 0)
        lax.fori_loop(near_hi, n_groups, far_group(far_after), 0)

    st.run(bound_ref[0] + bias_max * LOG2E, schedule)

    lam_v = lam_ref[...]
    lam = (jnp.exp(jnp.sum(lam_v[0:1] * lam_v[1:2], axis=-1, keepdims=True))
           - jnp.exp(jnp.sum(lam_v[2:3] * lam_v[3:4], axis=-1, keepdims=True)) + lam_init)
    for qt in range(q_tiles):
        s0, s1 = 2 * qt, 2 * qt + 1
        o_t = acc_ref[s0] / l_ref[s0] - lam * (acc_ref[s1] / l_ref[s1])
        o_t = o_t * lax.rsqrt(jnp.mean(o_t * o_t, axis=0, keepdims=True) + EPS) * g_sub_ref[...]
        o_ref[qt * blk:(qt + 1) * blk, :] = (o_t * (1.0 - lam_init)).T.astype(BF16)


def _diff_attention(bound, q, k, v, bias, rel_bias, lam_rows, g_sub, gb, seq, blk, q_tiles, group,
                    lam_init):
    nc = seq // blk
    nq = nc // q_tiles
    assert group % 2 == 0 and nc % group == 0 and nc % q_tiles == 0 and blk >= REL_MAX_DIST
    kern = functools.partial(_diff_attn_kernel, blk=blk, q_tiles=q_tiles, group=group,
                             lam_init=lam_init)
    tq = q_tiles * blk
    return pl.pallas_call(
        kern,
        out_shape=jax.ShapeDtypeStruct((gb * seq, HEAD_OUT), BF16),
        grid=(DIFF_HEADS, gb, nq),
        in_specs=[
            pl.BlockSpec(memory_space=pltpu.SMEM),
            pl.BlockSpec((1, 1, 2, DIFF_V, tq), lambda hd, b, i: (b, hd, 0, 0, i)),
            pl.BlockSpec((1, 1, seq, DIFF_V), lambda hd, b, i: (b, hd, 0, 0)),
            pl.BlockSpec((1, 1, nc, DIFF_V, blk), lambda hd, b, i: (b, hd, 0, 0, 0)),
            pl.BlockSpec((1, BIAS_TILES, blk, blk), lambda hd, b, i: (hd, 0, 0, 0),
                         pipeline_mode=pl.Buffered(1)),
            pl.BlockSpec(memory_space=pltpu.SMEM),
            _const_spec((4, LANES)),
            _const_spec((DIFF_V, 1)),
        ],
        out_specs=pl.BlockSpec((tq, DIFF_V), lambda hd, b, i: (b * nq + i, hd)),
        scratch_shapes=_flash_scratch(2 * q_tiles, DIFF_V, blk),
        compiler_params=_params(("arbitrary", "arbitrary", "arbitrary")), name="diff_attn",
    )(bound, q, k, v, bias, rel_bias, lam_rows, g_sub)


def _merge_kernel(x_ref, g_mix_ref, oa_ref, ob_ref, wga_ref, wgb_ref, wua_ref, wub_ref,
                  out_ref, h_ref):
    @pl.when(pl.program_id(1) == 0)
    def _():
        h_ref[...] = _rms_rows(x_ref[...], g_mix_ref[...]).astype(BF16)

    h = h_ref[...]
    ga = jnp.dot(h, wga_ref[...], preferred_element_type=F32)
    gb_ = jnp.dot(h, wgb_ref[...], preferred_element_type=F32)
    ua = jnp.dot(oa_ref[...], wua_ref[...], preferred_element_type=F32)
    ub = jnp.dot(ob_ref[...], wub_ref[...], preferred_element_type=F32)
    out_ref[...] = (jax.nn.sigmoid(ga) * ua + jax.nn.sigmoid(gb_) * ub).astype(BF16)


def _merge(x2d, o_a, o_b, w, tm, tn):
    tokens = x2d.shape[0]
    return pl.pallas_call(
        _merge_kernel,
        out_shape=jax.ShapeDtypeStruct((tokens, D_MODEL), BF16),
        grid=(tokens // tm, D_MODEL // tn),
        in_specs=[
            pl.BlockSpec((tm, D_MODEL), lambda i, j: (i, 0)),
            _const_spec((1, D_MODEL)),
            pl.BlockSpec((tm, HEAD_OUT), lambda i, j: (i, 0)),
            pl.BlockSpec((tm, HEAD_OUT), lambda i, j: (i, 0)),
            pl.BlockSpec((D_MODEL, tn), lambda i, j: (0, j)),
            pl.BlockSpec((D_MODEL, tn), lambda i, j: (0, j)),
            pl.BlockSpec((HEAD_OUT, tn), lambda i, j: (0, j)),
            pl.BlockSpec((HEAD_OUT, tn), lambda i, j: (0, j)),
        ],
        out_specs=pl.BlockSpec((tm, tn), lambda i, j: (i, j)),
        scratch_shapes=[pltpu.VMEM((tm, D_MODEL), BF16)],
        compiler_params=_params(("arbitrary", "arbitrary")), name="gated_merge",
    )(x2d, w["g_mix"], o_a, o_b, w["w_ga"], w["w_gb"], w["w_ua"], w["w_ub"])


def _out_proj_kernel(x_ref, m_ref, wo_ref, out_ref):
    out_ref[...] = x_ref[...] + jnp.dot(m_ref[...], wo_ref[...], preferred_element_type=F32)


def _out_proj(x2d, merged, w, tm):
    tokens = x2d.shape[0]
    return pl.pallas_call(
        _out_proj_kernel,
        out_shape=jax.ShapeDtypeStruct((tokens, D_MODEL), F32),
        grid=(tokens // tm,),
        in_specs=[
            pl.BlockSpec((tm, D_MODEL), lambda i: (i, 0)),
            pl.BlockSpec((tm, D_MODEL), lambda i: (i, 0)),
            _const_spec((D_MODEL, D_MODEL)),
        ],
        out_specs=pl.BlockSpec((tm, D_MODEL), lambda i: (i, 0)),
        compiler_params=_params(("arbitrary",)), name="out_proj",
    )(x2d, merged, w["w_o"])


def _ffn_kernel(x_ref, g_ref, wg_ref, wu_ref, wd_ref, out_ref, h_ref):
    @pl.when(pl.program_id(1) == 0)
    def _():
        x = x_ref[...]
        h_ref[...] = _rms_rows(x, g_ref[...]).astype(BF16)
        out_ref[...] = x

    h = h_ref[...]
    g = jnp.dot(h, wg_ref[...], preferred_element_type=F32)
    u = jnp.dot(h, wu_ref[...], preferred_element_type=F32)
    a = (g * jax.nn.sigmoid(g) * u).astype(BF16)
    out_ref[...] += jnp.dot(a, wd_ref[...], preferred_element_type=F32)


def _ffn(x2d, w, tm, tf):
    tokens = x2d.shape[0]
    return pl.pallas_call(
        _ffn_kernel,
        out_shape=jax.ShapeDtypeStruct((tokens, D_MODEL), F32),
        grid=(tokens // tm, D_FF // tf),
        in_specs=[
            pl.BlockSpec((tm, D_MODEL), lambda i, j: (i, 0)),
            _const_spec((1, D_MODEL)),
            pl.BlockSpec((D_MODEL, tf), lambda i, j: (0, j)),
            pl.BlockSpec((D_MODEL, tf), lambda i, j: (0, j)),
            pl.BlockSpec((tf, D_MODEL), lambda i, j: (j, 0)),
        ],
        out_specs=pl.BlockSpec((tm, D_MODEL), lambda i, j: (i, 0)),
        scratch_shapes=[pltpu.VMEM((tm, D_MODEL), BF16)],
        compiler_params=_params(("arbitrary", "arbitrary")), name="swiglu_ffn",
    )(x2d, w["g_ffn"], w["w_gate"], w["w_up"], w["w_down"])


def _prepare_weights(mix_norm, w_in, q_a_norm, wq_b, kv_a_norm, wkv_b, mla_q_norm, mla_k_norm,
                     diff_q_norm, diff_k_norm, diff_subln, w_up_mla, w_up_diff, w_o, ffn_norm,
                     w_gate, w_up, w_down, layer):
    win = w_in[layer]
    o_cq, o_ckv, o_kpe = 0, Q_LORA, Q_LORA + KV_LORA
    o_dq = o_kpe + MLA_ROPE
    o_dk = o_dq + HEAD_OUT
    o_dv = o_dk + HEAD_OUT
    o_ga = o_dv + HEAD_OUT
    o_gb = o_ga + D_MODEL
    w_kpe = win[:, o_kpe:o_dq]
    w_kpe_rot = jnp.concatenate([-w_kpe[:, ROPE_HALF:], w_kpe[:, :ROPE_HALF]], axis=1)
    zpad = jnp.zeros((D_MODEL, LANES - MLA_ROPE), F32)
    w_a = jnp.concatenate([win[:, o_cq:o_kpe], w_kpe, zpad, w_kpe_rot, zpad], axis=1)
    wkv = wkv_b[layer].reshape(KV_LORA, MLA_HEADS, MLA_NOPE + MLA_V)
    gk = mla_k_norm[layer]
    gk_rope = gk[MLA_NOPE:]
    lane_pad = jnp.zeros((LANES - MLA_ROPE,), F32)
    return {
        "g_mix": mix_norm[layer][None, :],
        "w_a": w_a.astype(BF16),
        "g_qa": q_a_norm[layer][None, :],
        "g_kva": kv_a_norm[layer][None, :],
        "wq_t": wq_b[layer].T.astype(BF16),
        "g_q": mla_q_norm[layer][:, None],
        "wk_nope": wkv[:, :, :MLA_NOPE].reshape(KV_LORA, MLA_HEADS * MLA_NOPE).astype(BF16),
        "wv_t": wkv[:, :, MLA_NOPE:].reshape(KV_LORA, MLA_HEADS * MLA_V).T.astype(BF16),
        "g_kn": gk[None, :MLA_NOPE],
        "g_kr": jnp.concatenate([gk_rope, lane_pad])[None, :],
        "g_krot": jnp.concatenate([gk_rope[ROPE_HALF:], gk_rope[:ROPE_HALF], lane_pad])[None, :],
        "w_dq_t": win[:, o_dq:o_dk].T.astype(BF16),
        "w_dk_t": win[:, o_dk:o_dv].T.astype(BF16),
        "w_dv_t": win[:, o_dv:o_ga].T.astype(BF16),
        "g_dq": diff_q_norm[layer][:, None],
        "g_dk": diff_k_norm[layer][:, None],
        "g_sub": diff_subln[layer][:, None],
        "w_ga": win[:, o_ga:o_gb].astype(BF16),
        "w_gb": win[:, o_gb:].astype(BF16),
        "w_ua": w_up_mla[layer].astype(BF16),
        "w_ub": w_up_diff[layer].astype(BF16),
        "w_o": w_o[layer].astype(BF16),
        "mla_bound": _score_bound(MLA_QK, mla_q_norm[layer], gk),
        "diff_bound": _score_bound(DIFF_QK, diff_q_norm[layer], diff_k_norm[layer]),
        "g_ffn": ffn_norm[layer][None, :],
        "w_gate": w_gate[layer].astype(BF16),
        "w_up": w_up[layer].astype(BF16),
        "w_down": w_down[layer].astype(BF16),
    }


class _Tiles(NamedTuple):
    tm: int = 512
    tn: int = 512
    tf: int = 512
    blk: int = 512
    q_tiles: int = 2
    mla_group_max: int = 8
    diff_group_max: int = 4


def _encoder_layer(x, w, tables, bias, rel_bias, lam_rows, lam_init, t):
    gb, seq, _ = x.shape
    x2d = x.reshape(gb * seq, D_MODEL)
    mla_group = math.gcd(t.mla_group_max, seq // t.blk)
    diff_group = math.gcd(t.diff_group_max, seq // t.blk)
    q_a, k_a, v_a = _mla_prep(x2d, gb, seq, t.tm, t.blk, w, tables)
    q_d, k_d, v_d = _diff_prep(x2d, gb, seq, t.tm, t.blk, w)
    o_a = _mla_attention(w["mla_bound"], q_a, k_a, v_a, gb, seq, t.blk, t.q_tiles, mla_group)
    o_b = _diff_attention(w["diff_bound"], q_d, k_d, v_d, bias, rel_bias, lam_rows, w["g_sub"], gb,
                          seq, t.blk, t.q_tiles, diff_group, lam_init)
    merged = _merge(x2d, o_a, o_b, w, t.tm, t.tn)
    x1 = _out_proj(x2d, merged, w, t.tm)
    y = _ffn(x1, w, t.tm, t.tf)
    return y.reshape(gb, seq, D_MODEL)


def kernel(x_prompt, x_sample, mix_norm, w_in, q_a_norm, wq_b, kv_a_norm, wkv_b, mla_q_norm, mla_k_norm, diff_q_norm, diff_k_norm, lambda_q1, lambda_k1, lambda_q2, lambda_k2, diff_subln, w_up_mla, w_up_diff, w_o, ffn_norm, w_gate, w_up, w_down, rel_bias):
    t = _Tiles()
    depth = w_in.shape[0]
    max_seq = max(x_prompt.shape[1], x_sample.shape[1])
    tables = _rope_tables(max_seq)
    bias = _bias_tiles(rel_bias, t.blk)
    y_prompt, y_sample = x_prompt, x_sample
    for layer in range(depth):
        w = _prepare_weights(mix_norm, w_in, q_a_norm, wq_b, kv_a_norm, wkv_b, mla_q_norm,
                             mla_k_norm, diff_q_norm, diff_k_norm, diff_subln, w_up_mla,
                             w_up_diff, w_o, ffn_norm, w_gate, w_up, w_down, layer)
        lam_init = 0.8 - 0.6 * math.exp(-0.3 * layer)
        lam_pad = jnp.zeros((LANES - DIFF_QK,), F32)
        lam_rows = jnp.stack([jnp.concatenate([v[layer], lam_pad])
                              for v in (lambda_q1, lambda_k1, lambda_q2, lambda_k2)])
        run = functools.partial(_encoder_layer, w=w, tables=tables, bias=bias, rel_bias=rel_bias,
                                lam_rows=lam_rows, lam_init=lam_init, t=t)
        y_prompt = run(y_prompt)
        y_sample = run(y_sample)
    return (y_prompt, y_sample)
```

```python
import functools
import math
from typing import NamedTuple

import jax
import jax.numpy as jnp
from jax import lax
from jax.experimental import pallas as pl
from jax.experimental.pallas import tpu as pltpu

D_MODEL = 2048
MLA_HEADS = 8
MLA_NOPE = 128
MLA_ROPE = 64
MLA_QK = MLA_NOPE + MLA_ROPE
MLA_V = 128
Q_LORA = 512
KV_LORA = 256
ROPE_THETA = 10000.0
ROPE_HALF = MLA_ROPE // 2
DIFF_HEADS = 8
DIFF_QK = 64
DIFF_V = 2 * DIFF_QK
REL_BUCKETS = 32
REL_MAX_DIST = 128
D_FF = 5632
EPS = 1e-6
HEAD_OUT = MLA_HEADS * MLA_V

LANES = 128
MLA_QK_PAD = 2 * LANES
K_PE_COL = Q_LORA + KV_LORA
W_A_COLS = K_PE_COL + 2 * LANES
ROPE_TABLE_ROWS = 2048
VMEM_LIMIT = 56 * 1024 * 1024

F32 = jnp.float32
BF16 = jnp.bfloat16
NEG_BIG = -1e30
LOG2E = math.log2(math.e)
BOUNDED_SOFTMAX_LIMIT = 50.0
BIAS_TILES = 5

NT_DIMS = (((1,), (1,)), ((), ()))


def _params(semantics):
    return pltpu.CompilerParams(dimension_semantics=semantics, vmem_limit_bytes=VMEM_LIMIT)


def _const_spec(shape):
    zeros = (0,) * len(shape)
    return pl.BlockSpec(shape, lambda *_: zeros)


def _rms_rows(x, gain):
    return x * lax.rsqrt(jnp.mean(x * x, axis=-1, keepdims=True) + EPS) * gain


def _rope_table_kernel(inv_row_ref, inv_col_ref, cos_t_ref, sin_t_ref, cos_f_ref, sin_f_ref):
    ts = cos_t_ref.shape[0]
    base = pl.program_id(0) * ts
    pos_rows = (base + lax.broadcasted_iota(jnp.int32, (ts, LANES), 0)).astype(F32)
    ang_t = pos_rows * inv_row_ref[...]
    cos_t_ref[...] = jnp.cos(ang_t)
    sin_t_ref[...] = jnp.sin(ang_t)
    pos_cols = (base + lax.broadcasted_iota(jnp.int32, (ROPE_HALF, ts), 1)).astype(F32)
    ang_f = pos_cols * inv_col_ref[...]
    cos_f_ref[...] = jnp.cos(ang_f)
    sin_f_ref[...] = jnp.sin(ang_f)


def _rope_tables(seq):
    inv = ROPE_THETA ** (-jnp.arange(ROPE_HALF, dtype=F32) / ROPE_HALF)
    inv_row = jnp.tile(inv, LANES // ROPE_HALF)[None, :]
    inv_col = inv[:, None]
    ts = min(seq, ROPE_TABLE_ROWS)
    tok = jax.ShapeDtypeStruct((seq, LANES), F32)
    feat = jax.ShapeDtypeStruct((ROPE_HALF, seq), F32)
    return pl.pallas_call(
        _rope_table_kernel,
        out_shape=(tok, tok, feat, feat),
        grid=(seq // ts,),
        in_specs=[_const_spec((1, LANES)), _const_spec((ROPE_HALF, 1))],
        out_specs=(
            pl.BlockSpec((ts, LANES), lambda i: (i, 0)),
            pl.BlockSpec((ts, LANES), lambda i: (i, 0)),
            pl.BlockSpec((ROPE_HALF, ts), lambda i: (0, i)),
            pl.BlockSpec((ROPE_HALF, ts), lambda i: (0, i)),
        ),
        compiler_params=_params(("arbitrary",)), name="rope_tables",
    )(inv_row, inv_col)


def _mla_prep_kernel(x_ref, g_mix_ref, w_a_ref, g_qa_ref, g_kva_ref, wq_t_ref, g_q_ref,
                     wk_ref, wv_t_ref, g_kn_ref, g_kr_ref, g_krot_ref,
                     cos_t_ref, sin_t_ref, cos_f_ref, sin_f_ref,
                     q_ref, k_ref, v_ref, *, tk):
    tm = x_ref.shape[0]
    h = _rms_rows(x_ref[...], g_mix_ref[...]).astype(BF16)
    c = jnp.dot(h, w_a_ref[...], preferred_element_type=F32)
    cq = _rms_rows(c[:, :Q_LORA], g_qa_ref[...]).astype(BF16)
    ckv = _rms_rows(c[:, Q_LORA:Q_LORA + KV_LORA], g_kva_ref[...]).astype(BF16)
    k_pe = c[:, K_PE_COL:K_PE_COL + LANES]
    k_pe_rot = c[:, K_PE_COL + LANES:W_A_COLS]

    q_t = lax.dot_general(wq_t_ref[...], cq, NT_DIMS, preferred_element_type=F32)
    cos_f = cos_f_ref[...]
    sin_f = sin_f_ref[...]
    scale = MLA_QK ** -0.5 * LOG2E
    g_q = g_q_ref[...]
    for hd in range(MLA_HEADS):
        xh = q_t[hd * MLA_QK:(hd + 1) * MLA_QK]
        rinv = lax.rsqrt(jnp.mean(xh * xh, axis=0, keepdims=True) + EPS)
        xn = xh * rinv * g_q
        x1 = xn[MLA_NOPE:MLA_NOPE + ROPE_HALF]
        x2 = xn[MLA_NOPE + ROPE_HALF:]
        q_ref[0, hd, 0:MLA_NOPE, :] = (xn[:MLA_NOPE] * scale).astype(BF16)
        q_ref[0, hd, MLA_NOPE:MLA_NOPE + ROPE_HALF, :] = ((x1 * cos_f - x2 * sin_f) * scale).astype(BF16)
        q_ref[0, hd, MLA_NOPE + ROPE_HALF:MLA_QK, :] = ((x2 * cos_f + x1 * sin_f) * scale).astype(BF16)
        q_ref[0, hd, MLA_QK:, :] = jnp.zeros((MLA_QK_PAD - MLA_QK, tm), BF16)

    k_nope = jnp.dot(ckv, wk_ref[...], preferred_element_type=F32)
    rope_base = (k_pe * g_kr_ref[...]) * cos_t_ref[...] + (k_pe_rot * g_krot_ref[...]) * sin_t_ref[...]
    ss_pe = jnp.sum(k_pe * k_pe, axis=-1, keepdims=True)
    for hd in range(MLA_HEADS):
        kh = k_nope[:, hd * MLA_NOPE:(hd + 1) * MLA_NOPE]
        ss = jnp.sum(kh * kh, axis=-1, keepdims=True) + ss_pe
        rinv = lax.rsqrt(ss * (1.0 / MLA_QK) + EPS)
        k_ref[0, hd, :, 0:LANES] = (kh * rinv * g_kn_ref[...]).astype(BF16)
        k_ref[0, hd, :, LANES:] = (rope_base * rinv).astype(BF16)

    v_t = lax.dot_general(wv_t_ref[...], ckv, NT_DIMS, preferred_element_type=F32)
    for hd in range(MLA_HEADS):
        for cc in range(tm // tk):
            v_ref[0, hd, cc] = v_t[hd * MLA_V:(hd + 1) * MLA_V, cc * tk:(cc + 1) * tk].astype(BF16)


def _mla_prep(x2d, gb, seq, tm, tk, w, tables):
    tokens = x2d.shape[0]
    nt = seq // tm
    cos_t, sin_t, cos_f, sin_f = tables
    kern = functools.partial(_mla_prep_kernel, tk=tk)
    out_shape = (
        jax.ShapeDtypeStruct((gb, MLA_HEADS, MLA_QK_PAD, seq), BF16),
        jax.ShapeDtypeStruct((gb, MLA_HEADS, seq, MLA_QK_PAD), BF16),
        jax.ShapeDtypeStruct((gb, MLA_HEADS, seq // tk, MLA_V, tk), BF16),
    )
    in_specs = [
        pl.BlockSpec((tm, D_MODEL), lambda i: (i, 0)),
        _const_spec((1, D_MODEL)),
        _const_spec((D_MODEL, W_A_COLS)),
        _const_spec((1, Q_LORA)),
        _const_spec((1, KV_LORA)),
        _const_spec((MLA_HEADS * MLA_QK, Q_LORA)),
        _const_spec((MLA_QK, 1)),
        _const_spec((KV_LORA, MLA_HEADS * MLA_NOPE)),
        _const_spec((MLA_HEADS * MLA_V, KV_LORA)),
        _const_spec((1, LANES)),
        _const_spec((1, LANES)),
        _const_spec((1, LANES)),
        pl.BlockSpec((tm, LANES), lambda i: (i % nt, 0)),
        pl.BlockSpec((tm, LANES), lambda i: (i % nt, 0)),
        pl.BlockSpec((ROPE_HALF, tm), lambda i: (0, i % nt)),
        pl.BlockSpec((ROPE_HALF, tm), lambda i: (0, i % nt)),
    ]
    out_specs = (
        pl.BlockSpec((1, MLA_HEADS, MLA_QK_PAD, tm), lambda i: (i // nt, 0, 0, i % nt)),
        pl.BlockSpec((1, MLA_HEADS, tm, MLA_QK_PAD), lambda i: (i // nt, 0, i % nt, 0)),
        pl.BlockSpec((1, MLA_HEADS, tm // tk, MLA_V, tk), lambda i: (i // nt, 0, i % nt, 0, 0)),
    )
    return pl.pallas_call(
        kern, out_shape=out_shape, grid=(tokens // tm,), in_specs=in_specs, out_specs=out_specs,
        compiler_params=_params(("arbitrary",)), name="mla_prep",
    )(x2d, w["g_mix"], w["w_a"], w["g_qa"], w["g_kva"], w["wq_t"], w["g_q"], w["wk_nope"],
      w["wv_t"], w["g_kn"], w["g_kr"], w["g_krot"], cos_t, sin_t, cos_f, sin_f)


def _group_norm_cols(x_t, gain_col):
    rows, tm = x_t.shape
    x3 = x_t.reshape(rows // DIFF_QK, DIFF_QK, tm)
    rinv = lax.rsqrt(jnp.mean(x3 * x3, axis=1, keepdims=True) + EPS)
    return (x3 * rinv * gain_col[None]).reshape(rows, tm)


def _diff_prep_kernel(x_ref, g_mix_ref, wq_t_ref, wk_t_ref, wv_t_ref, g_q_ref, g_k_ref,
                      q_ref, k_ref, v_ref, *, tk):
    tm = x_ref.shape[0]
    h = _rms_rows(x_ref[...], g_mix_ref[...]).astype(BF16)
    scale = DIFF_QK ** -0.5 * LOG2E
    q_t = lax.dot_general(wq_t_ref[...], h, NT_DIMS, preferred_element_type=F32)
    qn = (_group_norm_cols(q_t, g_q_ref[...]) * scale).astype(BF16)
    zeros = jnp.zeros((DIFF_QK, tm), BF16)
    for hd in range(DIFF_HEADS):
        r0 = hd * DIFF_V
        q_ref[0, hd, 0, 0:DIFF_QK, :] = qn[r0:r0 + DIFF_QK]
        q_ref[0, hd, 0, DIFF_QK:, :] = zeros
        q_ref[0, hd, 1, 0:DIFF_QK, :] = zeros
        q_ref[0, hd, 1, DIFF_QK:, :] = qn[r0 + DIFF_QK:r0 + DIFF_V]

    k_t = lax.dot_general(wk_t_ref[...], h, NT_DIMS, preferred_element_type=F32)
    kn = _group_norm_cols(k_t, g_k_ref[...]).T
    for hd in range(DIFF_HEADS):
        k_ref[0, hd] = kn[:, hd * DIFF_V:(hd + 1) * DIFF_V].astype(BF16)

    v_t = lax.dot_general(wv_t_ref[...], h, NT_DIMS, preferred_element_type=F32)
    for hd in range(DIFF_HEADS):
        for cc in range(tm // tk):
            v_ref[0, hd, cc] = v_t[hd * DIFF_V:(hd + 1) * DIFF_V, cc * tk:(cc + 1) * tk].astype(BF16)


def _diff_prep(x2d, gb, seq, tm, tk, w):
    tokens = x2d.shape[0]
    nt = seq // tm
    kern = functools.partial(_diff_prep_kernel, tk=tk)
    out_shape = (
        jax.ShapeDtypeStruct((gb, DIFF_HEADS, 2, DIFF_V, seq), BF16),
        jax.ShapeDtypeStruct((gb, DIFF_HEADS, seq, DIFF_V), BF16),
        jax.ShapeDtypeStruct((gb, DIFF_HEADS, seq // tk, DIFF_V, tk), BF16),
    )
    in_specs = [
        pl.BlockSpec((tm, D_MODEL), lambda i: (i, 0)),
        _const_spec((1, D_MODEL)),
        _const_spec((HEAD_OUT, D_MODEL)),
        _const_spec((HEAD_OUT, D_MODEL)),
        _const_spec((HEAD_OUT, D_MODEL)),
        _const_spec((DIFF_QK, 1)),
        _const_spec((DIFF_QK, 1)),
    ]
    out_specs = (
        pl.BlockSpec((1, DIFF_HEADS, 2, DIFF_V, tm), lambda i: (i // nt, 0, 0, 0, i % nt)),
        pl.BlockSpec((1, DIFF_HEADS, tm, DIFF_V), lambda i: (i // nt, 0, i % nt, 0)),
        pl.BlockSpec((1, DIFF_HEADS, tm // tk, DIFF_V, tk), lambda i: (i // nt, 0, i % nt, 0, 0)),
    )
    return pl.pallas_call(
        kern, out_shape=out_shape, grid=(tokens // tm,), in_specs=in_specs, out_specs=out_specs,
        compiler_params=_params(("arbitrary",)), name="diff_prep",
    )(x2d, w["g_mix"], w["w_dq_t"], w["w_dk_t"], w["w_dv_t"], w["g_dq"], w["g_dk"])


def _bias_tile_kernel(bucket_ref, rb_ref, out_ref):
    hd = pl.program_id(0)
    offset = pl.program_id(1) - pl.num_programs(1) // 2

    @pl.when(offset <= -2)
    def _():
        out_ref[0, 0] = jnp.full(out_ref.shape[2:], rb_ref[REL_BUCKETS // 2 - 1, hd] * LOG2E, F32)

    @pl.when(offset >= 2)
    def _():
        out_ref[0, 0] = jnp.full(out_ref.shape[2:], rb_ref[REL_BUCKETS - 1, hd] * LOG2E, F32)

    @pl.when(jnp.abs(offset) < 2)
    def _():
        bucket = bucket_ref[0]
        acc = jnp.zeros(bucket.shape, F32)
        for b in range(REL_BUCKETS):
            acc = jnp.where(bucket == b, rb_ref[b, hd], acc)
        out_ref[0, 0] = acc * LOG2E


def _t5_bucket(rel):
    nb = REL_BUCKETS // 2
    ret = jnp.where(rel > 0, nb, 0)
    n = jnp.abs(rel)
    max_exact = nb // 2
    nf = jnp.maximum(n, 1).astype(F32)
    large = max_exact + (jnp.log(nf / max_exact) / math.log(REL_MAX_DIST / max_exact)
                         * (nb - max_exact)).astype(jnp.int32)
    large = jnp.minimum(large, nb - 1)
    return ret + jnp.where(n < max_exact, n, large)


def _bias_tiles(rel_bias, blk):
    assert blk >= REL_MAX_DIST
    n = BIAS_TILES
    d = jnp.arange(blk, dtype=jnp.int32)
    offs = (jnp.arange(n, dtype=jnp.int32) - n // 2) * blk
    rel = offs[:, None, None] + d[None, :, None] - d[None, None, :]
    bucket = _t5_bucket(rel)
    return pl.pallas_call(
        _bias_tile_kernel,
        out_shape=jax.ShapeDtypeStruct((DIFF_HEADS, n, blk, blk), F32),
        grid=(DIFF_HEADS, n),
        in_specs=[
            pl.BlockSpec((1, blk, blk), lambda hd, t: (t, 0, 0)),
            pl.BlockSpec(memory_space=pltpu.SMEM),
        ],
        out_specs=pl.BlockSpec((1, 1, blk, blk), lambda hd, t: (hd, t, 0, 0)),
        compiler_params=_params(("arbitrary", "arbitrary")), name="bias_tiles",
    )(bucket, rel_bias)


class _FlashState:
    def __init__(self, q_ref, k_ref, v_ref, scratch, blk, q_tiles):
        self.q_ref, self.k_ref, self.v_ref = q_ref, k_ref, v_ref
        (self.s_ref, self.p_ref, self.alpha_ref, self.m_ref, self.l_ref, self.acc_ref) = scratch
        self.blk = blk
        self.bound = None
        self.streams = [(qt, mp) for qt in range(q_tiles) for mp in range(q_ref.shape[2])]
        self.nc = v_ref.shape[2]

    def q(self, qt, mp):
        return self.q_ref[0, 0, mp, :, qt * self.blk:(qt + 1) * self.blk]

    def key_chunk(self, c):
        return self.k_ref[0, 0, pl.ds(pl.multiple_of(c * self.blk, self.blk), self.blk), :]


    def init_bounded(self):
        self.l_ref[...] = jnp.zeros(self.l_ref.shape, F32)
        self.acc_ref[...] = jnp.zeros(self.acc_ref.shape, F32)
        self.p_ref[1] = jnp.zeros(self.p_ref.shape[1:], BF16)

    def pv_bounded(self, c, half):
        vc = self.v_ref[0, 0, c]
        for sx, (qt, mp) in enumerate(self.streams):
            self.acc_ref[sx] += jnp.dot(vc, self.p_ref[half, sx], preferred_element_type=F32)

    def step_bounded(self, c, half, cst=None, bias_tiles=None):
        self.pv_bounded(jnp.maximum(c - 1, 0), 1 - half)
        kc = self.key_chunk(c)
        for sx, (qt, mp) in enumerate(self.streams):
            s = jnp.dot(kc, self.q(qt, mp), preferred_element_type=F32)
            if bias_tiles is not None:
                s = s + bias_tiles[qt]
            shift = self.bound if cst is None else self.bound - cst
            p = jnp.exp2(s - shift)
            self.l_ref[sx] += jnp.sum(p, axis=0, keepdims=True)
            self.p_ref[half, sx] = p.astype(BF16)

    def finish_bounded(self):
        self.pv_bounded(self.nc - 1, 1)


    def init_online(self):
        self.m_ref[...] = jnp.full(self.m_ref.shape, NEG_BIG, F32)
        self.l_ref[...] = jnp.zeros(self.l_ref.shape, F32)
        self.acc_ref[...] = jnp.zeros(self.acc_ref.shape, F32)
        self.p_ref[0] = jnp.zeros(self.p_ref.shape[1:], BF16)
        self.alpha_ref[...] = jnp.ones(self.alpha_ref.shape, F32)
        self.qk(0, 0)

    def qk(self, c, half):
        kc = self.key_chunk(c)
        for sx, (qt, mp) in enumerate(self.streams):
            self.s_ref[half, sx] = jnp.dot(kc, self.q(qt, mp), preferred_element_type=F32)

    def pv_online(self, c):
        vc = self.v_ref[0, 0, c]
        for sx, (qt, mp) in enumerate(self.streams):
            self.acc_ref[sx] = (self.alpha_ref[sx] * self.acc_ref[sx]
                                + jnp.dot(vc, self.p_ref[0, sx], preferred_element_type=F32))

    def softmax_online(self, half, cst, bias_tiles):
        for sx, (qt, mp) in enumerate(self.streams):
            s = self.s_ref[half, sx]
            if bias_tiles is not None:
                s = s + bias_tiles[qt]
            mc = jnp.max(s, axis=0, keepdims=True)
            if cst is not None:
                mc = mc + cst
            m_old = self.m_ref[sx]
            m_new = jnp.maximum(m_old, mc)
            alpha = jnp.exp2(m_old - m_new)
            shift = m_new if cst is None else m_new - cst
            p = jnp.exp2(s - shift)
            self.l_ref[sx] = alpha * self.l_ref[sx] + jnp.sum(p, axis=0, keepdims=True)
            self.p_ref[0, sx] = p.astype(BF16)
            self.alpha_ref[sx] = alpha
            self.m_ref[sx] = m_new

    def step_online(self, c, half, cst=None, bias_tiles=None):
        self.pv_online(jnp.maximum(c - 1, 0))
        self.qk(jnp.minimum(c + 1, self.nc - 1), 1 - half)
        self.softmax_online(half, cst, bias_tiles)

    def finish_online(self):
        self.pv_online(self.nc - 1)

    def run(self, bound, schedule):
        self.bound = bound
        use_bounded = bound <= BOUNDED_SOFTMAX_LIMIT

        @pl.when(use_bounded)
        def _():
            self.init_bounded()
            schedule(self.step_bounded)
            self.finish_bounded()

        @pl.when(jnp.logical_not(use_bounded))
        def _():
            self.init_online()
            schedule(self.step_online)
            self.finish_online()


def _flash_scratch(n_maps, head_dim, blk):
    return [
        pltpu.VMEM((2, n_maps, blk, blk), F32),
        pltpu.VMEM((2, n_maps, blk, blk), BF16),
        pltpu.VMEM((n_maps, 1, blk), F32),
        pltpu.VMEM((n_maps, 1, blk), F32),
        pltpu.VMEM((n_maps, 1, blk), F32),
        pltpu.VMEM((n_maps, head_dim, blk), F32),
    ]


def _score_bound(head_dim, g_q, g_k):
    return (head_dim ** 0.5 * LOG2E * 1.02 * jnp.max(jnp.abs(g_q)) * jnp.max(jnp.abs(g_k))).reshape(1)


def _mla_attn_kernel(bound_ref, q_ref, k_ref, v_ref, o_ref, *scratch, blk, q_tiles, group):
    st = _FlashState(q_ref, k_ref, v_ref, scratch, blk, q_tiles)

    def schedule(step):
        def body(cc, carry):
            for u in range(group):
                step(group * cc + u, u % 2)
            return carry

        lax.fori_loop(0, st.nc // group, body, 0)

    st.run(bound_ref[0], schedule)
    for sx, (qt, _) in enumerate(st.streams):
        o_t = st.acc_ref[sx] / st.l_ref[sx]
        o_ref[qt * blk:(qt + 1) * blk, :] = o_t.T.astype(BF16)


def _mla_attention(bound, q, k, v, gb, seq, blk, q_tiles, group):
    nc = seq // blk
    nq = nc // q_tiles
    assert group % 2 == 0 and nc % group == 0 and nc % q_tiles == 0
    kern = functools.partial(_mla_attn_kernel, blk=blk, q_tiles=q_tiles, group=group)
    q = q.reshape(gb, MLA_HEADS, 1, MLA_QK_PAD, seq)
    tq = q_tiles * blk
    return pl.pallas_call(
        kern,
        out_shape=jax.ShapeDtypeStruct((gb * seq, HEAD_OUT), BF16),
        grid=(gb, MLA_HEADS, nq),
        in_specs=[
            pl.BlockSpec(memory_space=pltpu.SMEM),
            pl.BlockSpec((1, 1, 1, MLA_QK_PAD, tq), lambda b, hd, i: (b, hd, 0, 0, i)),
            pl.BlockSpec((1, 1, seq, MLA_QK_PAD), lambda b, hd, i: (b, hd, 0, 0)),
            pl.BlockSpec((1, 1, nc, MLA_V, blk), lambda b, hd, i: (b, hd, 0, 0, 0)),
        ],
        out_specs=pl.BlockSpec((tq, MLA_V), lambda b, hd, i: (b * nq + i, hd)),
        scratch_shapes=_flash_scratch(q_tiles, MLA_V, blk),
        compiler_params=_params(("arbitrary", "arbitrary", "arbitrary")), name="mla_attn",
    )(bound, q, k, v)


def _diff_attn_kernel(bound_ref, q_ref, k_ref, v_ref, bias_ref, rb_ref, lam_ref, g_sub_ref, o_ref,
                      *scratch, blk, q_tiles, group, lam_init):
    hd = pl.program_id(0)
    i_first = pl.program_id(2) * q_tiles
    st = _FlashState(q_ref, k_ref, v_ref, scratch, blk, q_tiles)
    acc_ref, l_ref = st.acc_ref, st.l_ref
    n_groups = st.nc // group

    bias_max = jnp.abs(rb_ref[0, hd])
    for b in range(1, REL_BUCKETS):
        bias_max = jnp.maximum(bias_max, jnp.abs(rb_ref[b, hd]))

    far_before = rb_ref[REL_BUCKETS // 2 - 1, hd] * LOG2E
    far_after = rb_ref[REL_BUCKETS - 1, hd] * LOG2E
    first_near = (i_first - 1 + group) // group - 1
    last_near = (i_first + q_tiles) // group

    near_lo = jnp.maximum(first_near, 0)
    near_hi = jnp.minimum(last_near + 1, n_groups)

    def schedule(step):
        def far_group(cst):
            def body(cc, carry):
                for u in range(group):
                    step(group * cc + u, u % 2, cst=cst)
                return carry
            return body

        def near_group(cc, carry):
            for u in range(group):
                c = group * cc + u
                far = BIAS_TILES // 2
                tiles = [bias_ref[0, jnp.clip(c - (i_first + qt), -far, far) + far]
                         for qt in range(q_tiles)]
                step(c, u % 2, bias_tiles=tiles)
            return carry

        lax.fori_loop(0, near_lo, far_group(far_before), 0)
        lax.fori_loop(near_lo, near_hi, near_group, 0)
        lax.fori_loop(near_hi, n_groups, far_group(far_after), 0)

    st.run(bound_ref[0] + bias_max * LOG2E, schedule)

    lam_v = lam_ref[...]
    lam = (jnp.exp(jnp.sum(lam_v[0:1] * lam_v[1:2], axis=-1, keepdims=True))
           - jnp.exp(jnp.sum(lam_v[2:3] * lam_v[3:4], axis=-1, keepdims=True)) + lam_init)
    for qt in range(q_tiles):
        s0, s1 = 2 * qt, 2 * qt + 1
        o_t = acc_ref[s0] / l_ref[s0] - lam * (acc_ref[s1] / l_ref[s1])
        o_t = o_t * lax.rsqrt(jnp.mean(o_t * o_t, axis=0, keepdims=True) + EPS) * g_sub_ref[...]
        o_ref[qt * blk:(qt + 1) * blk, :] = (o_t * (1.0 - lam_init)).T.astype(BF16)


def _diff_attention(bound, q, k, v, bias, rel_bias, lam_rows, g_sub, gb, seq, blk, q_tiles, group,
                    lam_init):
    nc = seq // blk
    nq = nc // q_tiles
    assert group % 2 == 0 and nc % group == 0 and nc % q_tiles == 0 and blk >= REL_MAX_DIST
    kern = functools.partial(_diff_attn_kernel, blk=blk, q_tiles=q_tiles, group=group,
                             lam_init=lam_init)
    tq = q_tiles * blk
    return pl.pallas_call(
        kern,
        out_shape=jax.ShapeDtypeStruct((gb * seq, HEAD_OUT), BF16),
        grid=(DIFF_HEADS, gb, nq),
        in_specs=[
            pl.BlockSpec(memory_space=pltpu.SMEM),
            pl.BlockSpec((1, 1, 2, DIFF_V, tq), lambda hd, b, i: (b, hd, 0, 0, i)),
            pl.BlockSpec((1, 1, seq, DIFF_V), lambda hd, b, i: (b, hd, 0, 0)),
            pl.BlockSpec((1, 1, nc, DIFF_V, blk), lambda hd, b, i: (b, hd, 0, 0, 0)),
            pl.BlockSpec((1, BIAS_TILES, blk, blk), lambda hd, b, i: (hd, 0, 0, 0),
                         pipeline_mode=pl.Buffered(1)),
            pl.BlockSpec(memory_space=pltpu.SMEM),
            _const_spec((4, LANES)),
            _const_spec((DIFF_V, 1)),
        ],
        out_specs=pl.BlockSpec((tq, DIFF_V), lambda hd, b, i: (b * nq + i, hd)),
        scratch_shapes=_flash_scratch(2 * q_tiles, DIFF_V, blk),
        compiler_params=_params(("arbitrary", "arbitrary", "arbitrary")), name="diff_attn",
    )(bound, q, k, v, bias, rel_bias, lam_rows, g_sub)


def _merge_kernel(x_ref, g_mix_ref, oa_ref, ob_ref, wga_ref, wgb_ref, wua_ref, wub_ref,
                  out_ref, h_ref):
    @pl.when(pl.program_id(1) == 0)
    def _():
        h_ref[...] = _rms_rows(x_ref[...], g_mix_ref[...]).astype(BF16)

    h = h_ref[...]
    ga = jnp.dot(h, wga_ref[...], preferred_element_type=F32)
    gb_ = jnp.dot(h, wgb_ref[...], preferred_element_type=F32)
    ua = jnp.dot(oa_ref[...], wua_ref[...], preferred_element_type=F32)
    ub = jnp.dot(ob_ref[...], wub_ref[...], preferred_element_type=F32)
    out_ref[...] = (jax.nn.sigmoid(ga) * ua + jax.nn.sigmoid(gb_) * ub).astype(BF16)


def _merge(x2d, o_a, o_b, w, tm, tn):
    tokens = x2d.shape[0]
    return pl.pallas_call(
        _merge_kernel,
        out_shape=jax.ShapeDtypeStruct((tokens, D_MODEL), BF16),
        grid=(tokens // tm, D_MODEL // tn),
        in_specs=[
            pl.BlockSpec((tm, D_MODEL), lambda i, j: (i, 0)),
            _const_spec((1, D_MODEL)),
            pl.BlockSpec((tm, HEAD_OUT), lambda i, j: (i, 0)),
            pl.BlockSpec((tm, HEAD_OUT), lambda i, j: (i, 0)),
            pl.BlockSpec((D_MODEL, tn), lambda i, j: (0, j)),
            pl.BlockSpec((D_MODEL, tn), lambda i, j: (0, j)),
            pl.BlockSpec((HEAD_OUT, tn), lambda i, j: (0, j)),
            pl.BlockSpec((HEAD_OUT, tn), lambda i, j: (0, j)),
        ],
        out_specs=pl.BlockSpec((tm, tn), lambda i, j: (i, j)),
        scratch_shapes=[pltpu.VMEM((tm, D_MODEL), BF16)],
        compiler_params=_params(("arbitrary", "arbitrary")), name="gated_merge",
    )(x2d, w["g_mix"], o_a, o_b, w["w_ga"], w["w_gb"], w["w_ua"], w["w_ub"])


def _out_proj_kernel(x_ref, m_ref, wo_ref, out_ref):
    out_ref[...] = x_ref[...] + jnp.dot(m_ref[...], wo_ref[...], preferred_element_type=F32)


def _out_proj(x2d, merged, w, tm):
    tokens = x2d.shape[0]
    return pl.pallas_call(
        _out_proj_kernel,
        out_shape=jax.ShapeDtypeStruct((tokens, D_MODEL), F32),
        grid=(tokens // tm,),
        in_specs=[
            pl.BlockSpec((tm, D_MODEL), lambda i: (i, 0)),
            pl.BlockSpec((tm, D_MODEL), lambda i: (i, 0)),
            _const_spec((D_MODEL, D_MODEL)),
        ],
        out_specs=pl.BlockSpec((tm, D_MODEL), lambda i: (i, 0)),
        compiler_params=_params(("arbitrary",)), name="out_proj",
    )(x2d, merged, w["w_o"])


def _ffn_kernel(x_ref, g_ref, wg_ref, wu_ref, wd_ref, out_ref, h_ref):
    @pl.when(pl.program_id(1) == 0)
    def _():
        x = x_ref[...]
        h_ref[...] = _rms_rows(x, g_ref[...]).astype(BF16)
        out_ref[...] = x

    h = h_ref[...]
    g = jnp.dot(h, wg_ref[...], preferred_element_type=F32)
    u = jnp.dot(h, wu_ref[...], preferred_element_type=F32)
    a = (g * jax.nn.sigmoid(g) * u).astype(BF16)
    out_ref[...] += jnp.dot(a, wd_ref[...], preferred_element_type=F32)


def _ffn(x2d, w, tm, tf):
    tokens = x2d.shape[0]
    return pl.pallas_call(
        _ffn_kernel,
        out_shape=jax.ShapeDtypeStruct((tokens, D_MODEL), F32),
        grid=(tokens // tm, D_FF // tf),
        in_specs=[
            pl.BlockSpec((tm, D_MODEL), lambda i, j: (i, 0)),
            _const_spec((1, D_MODEL)),
            pl.BlockSpec((D_MODEL, tf), lambda i, j: (0, j)),
            pl.BlockSpec((D_MODEL, tf), lambda i, j: (0, j)),
            pl.BlockSpec((tf, D_MODEL), lambda i, j: (j, 0)),
        ],
        out_specs=pl.BlockSpec((tm, D_MODEL), lambda i, j: (i, 0)),
        scratch_shapes=[pltpu.VMEM((tm, D_MODEL), BF16)],
        compiler_params=_params(("arbitrary", "arbitrary")), name="swiglu_ffn",
    )(x2d, w["g_ffn"], w["w_gate"], w["w_up"], w["w_down"])


def _prepare_weights(mix_norm, w_in, q_a_norm, wq_b, kv_a_norm, wkv_b, mla_q_norm, mla_k_norm,
                     diff_q_norm, diff_k_norm, diff_subln, w_up_mla, w_up_diff, w_o, ffn_norm,
                     w_gate, w_up, w_down, layer):
    win = w_in[layer]
    o_cq, o_ckv, o_kpe = 0, Q_LORA, Q_LORA + KV_LORA
    o_dq = o_kpe + MLA_ROPE
    o_dk = o_dq + HEAD_OUT
    o_dv = o_dk + HEAD_OUT
    o_ga = o_dv + HEAD_OUT
    o_gb = o_ga + D_MODEL
    w_kpe = win[:, o_kpe:o_dq]
    w_kpe_rot = jnp.concatenate([-w_kpe[:, ROPE_HALF:], w_kpe[:, :ROPE_HALF]], axis=1)
    zpad = jnp.zeros((D_MODEL, LANES - MLA_ROPE), F32)
    w_a = jnp.concatenate([win[:, o_cq:o_kpe], w_kpe, zpad, w_kpe_rot, zpad], axis=1)
    wkv = wkv_b[layer].reshape(KV_LORA, MLA_HEADS, MLA_NOPE + MLA_V)
    gk = mla_k_norm[layer]
    gk_rope = gk[MLA_NOPE:]
    lane_pad = jnp.zeros((LANES - MLA_ROPE,), F32)
    return {
        "g_mix": mix_norm[layer][None, :],
        "w_a": w_a.astype(BF16),
        "g_qa": q_a_norm[layer][None, :],
        "g_kva": kv_a_norm[layer][None, :],
        "wq_t": wq_b[layer].T.astype(BF16),
        "g_q": mla_q_norm[layer][:, None],
        "wk_nope": wkv[:, :, :MLA_NOPE].reshape(KV_LORA, MLA_HEADS * MLA_NOPE).astype(BF16),
        "wv_t": wkv[:, :, MLA_NOPE:].reshape(KV_LORA, MLA_HEADS * MLA_V).T.astype(BF16),
        "g_kn": gk[None, :MLA_NOPE],
        "g_kr": jnp.concatenate([gk_rope, lane_pad])[None, :],
        "g_krot": jnp.concatenate([gk_rope[ROPE_HALF:], gk_rope[:ROPE_HALF], lane_pad])[None, :],
        "w_dq_t": win[:, o_dq:o_dk].T.astype(BF16),
        "w_dk_t": win[:, o_dk:o_dv].T.astype(BF16),
        "w_dv_t": win[:, o_dv:o_ga].T.astype(BF16),
        "g_dq": diff_q_norm[layer][:, None],
        "g_dk": diff_k_norm[layer][:, None],
        "g_sub": diff_subln[layer][:, None],
        "w_ga": win[:, o_ga:o_gb].astype(BF16),
        "w_gb": win[:, o_gb:].astype(BF16),
        "w_ua": w_up_mla[layer].astype(BF16),
        "w_ub": w_up_diff[layer].astype(BF16),
        "w_o": w_o[layer].astype(BF16),
        "mla_bound": _score_bound(MLA_QK, mla_q_norm[layer], gk),
        "diff_bound": _score_bound(DIFF_QK, diff_q_norm[layer], diff_k_norm[layer]),
        "g_ffn": ffn_norm[layer][None, :],
        "w_gate": w_gate[layer].astype(BF16),
        "w_up": w_up[layer].astype(BF16),
        "w_down": w_down[layer].astype(BF16),
    }


class _Tiles(NamedTuple):
    tm: int = 512
    tn: int = 512
    tf: int = 512
    blk: int = 512
    mla_q_tiles: int = 4
    diff_q_tiles: int = 2
    mla_group_max: int = 4
    diff_group_max: int = 8


def _encoder_layer(x, w, tables, bias, rel_bias, lam_rows, lam_init, t):
    gb, seq, _ = x.shape
    x2d = x.reshape(gb * seq, D_MODEL)
    mla_group = math.gcd(t.mla_group_max, seq // t.blk)
    diff_group = math.gcd(t.diff_group_max, seq // t.blk)
    q_a, k_a, v_a = _mla_prep(x2d, gb, seq, t.tm, t.blk, w, tables)
    q_d, k_d, v_d = _diff_prep(x2d, gb, seq, t.tm, t.blk, w)
    o_a = _mla_attention(w["mla_bound"], q_a, k_a, v_a, gb, seq, t.blk, t.mla_q_tiles, mla_group)
    o_b = _diff_attention(w["diff_bound"], q_d, k_d, v_d, bias, rel_bias, lam_rows, w["g_sub"], gb,
                          seq, t.blk, t.diff_q_tiles, diff_group, lam_init)
    merged = _merge(x2d, o_a, o_b, w, t.tm, t.tn)
    x1 = _out_proj(x2d, merged, w, t.tm)
    y = _ffn(x1, w, t.tm, t.tf)
    return y.reshape(gb, seq, D_MODEL)


def kernel(x_prompt, x_sample, mix_norm, w_in, q_a_norm, wq_b, kv_a_norm, wkv_b, mla_q_norm, mla_k_norm, diff_q_norm, diff_k_norm, lambda_q1, lambda_k1, lambda_q2, lambda_k2, diff_subln, w_up_mla, w_up_diff, w_o, ffn_norm, w_gate, w_up, w_down, rel_bias):
    t = _Tiles()
    depth = w_in.shape[0]
    max_seq = max(x_prompt.shape[1], x_sample.shape[1])
    tables = _rope_tables(max_seq)
    bias = _bias_tiles(rel_bias, t.blk)
    y_prompt, y_sample = x_prompt, x_sample
    for layer in range(depth):
        w = _prepare_weights(mix_norm, w_in, q_a_norm, wq_b, kv_a_norm, wkv_b, mla_q_norm,
                             mla_k_norm, diff_q_norm, diff_k_norm, diff_subln, w_up_mla,
                             w_up_diff, w_o, ffn_norm, w_gate, w_up, w_down, layer)
        lam_init = 0.8 - 0.6 * math.exp(-0.3 * layer)
        lam_pad = jnp.zeros((LANES - DIFF_QK,), F32)
        lam_rows = jnp.stack([jnp.concatenate([v[layer], lam_pad])
                              for v in (lambda_q1, lambda_k1, lambda_q2, lambda_k2)])
        run = functools.partial(_encoder_layer, w=w, tables=tables, bias=bias, rel_bias=rel_bias,
                                lam_rows=lam_rows, lam_init=lam_init, t=t)
        y_prompt = run(y_prompt)
        y_sample = run(y_sample)
    return (y_prompt, y_sample)
```

```python
import functools
import math
from typing import NamedTuple

import jax
import jax.numpy as jnp
from jax import lax
from jax.experimental import pallas as pl
from jax.experimental.pallas import tpu as pltpu

D_MODEL = 2048
MLA_HEADS = 8
MLA_NOPE = 128
MLA_ROPE = 64
MLA_QK = MLA_NOPE + MLA_ROPE
MLA_V = 128
Q_LORA = 512
KV_LORA = 256
ROPE_THETA = 10000.0
ROPE_HALF = MLA_ROPE // 2
DIFF_HEADS = 8
DIFF_QK = 64
DIFF_V = 2 * DIFF_QK
REL_BUCKETS = 32
REL_MAX_DIST = 128
D_FF = 5632
EPS = 1e-6
HEAD_OUT = MLA_HEADS * MLA_V

LANES = 128
MLA_QK_PAD = 2 * LANES
K_PE_COL = Q_LORA + KV_LORA
W_A_COLS = K_PE_COL + 2 * LANES
ROPE_TABLE_ROWS = 2048
VMEM_LIMIT = 56 * 1024 * 1024

F32 = jnp.float32
BF16 = jnp.bfloat16
NEG_BIG = -1e30
LOG2E = math.log2(math.e)
BOUNDED_SOFTMAX_LIMIT = 50.0
BIAS_TILES = 5

NT_DIMS = (((1,), (1,)), ((), ()))


def _params(semantics):
    return pltpu.CompilerParams(dimension_semantics=semantics, vmem_limit_bytes=VMEM_LIMIT)


def _const_spec(shape):
    zeros = (0,) * len(shape)
    return pl.BlockSpec(shape, lambda *_: zeros)


def _rms_rows(x, gain):
    return x * lax.rsqrt(jnp.mean(x * x, axis=-1, keepdims=True) + EPS) * gain


def _rope_table_kernel(inv_row_ref, inv_col_ref, cos_t_ref, sin_t_ref, cos_f_ref, sin_f_ref):
    ts = cos_t_ref.shape[0]
    base = pl.program_id(0) * ts
    pos_rows = (base + lax.broadcasted_iota(jnp.int32, (ts, LANES), 0)).astype(F32)
    ang_t = pos_rows * inv_row_ref[...]
    cos_t_ref[...] = jnp.cos(ang_t)
    sin_t_ref[...] = jnp.sin(ang_t)
    pos_cols = (base + lax.broadcasted_iota(jnp.int32, (ROPE_HALF, ts), 1)).astype(F32)
    ang_f = pos_cols * inv_col_ref[...]
    cos_f_ref[...] = jnp.cos(ang_f)
    sin_f_ref[...] = jnp.sin(ang_f)


def _rope_tables(seq):
    inv = ROPE_THETA ** (-jnp.arange(ROPE_HALF, dtype=F32) / ROPE_HALF)
    inv_row = jnp.tile(inv, LANES // ROPE_HALF)[None, :]
    inv_col = inv[:, None]
    ts = min(seq, ROPE_TABLE_ROWS)
    tok = jax.ShapeDtypeStruct((seq, LANES), F32)
    feat = jax.ShapeDtypeStruct((ROPE_HALF, seq), F32)
    return pl.pallas_call(
        _rope_table_kernel,
        out_shape=(tok, tok, feat, feat),
        grid=(seq // ts,),
        in_specs=[_const_spec((1, LANES)), _const_spec((ROPE_HALF, 1))],
        out_specs=(
            pl.BlockSpec((ts, LANES), lambda i: (i, 0)),
            pl.BlockSpec((ts, LANES), lambda i: (i, 0)),
            pl.BlockSpec((ROPE_HALF, ts), lambda i: (0, i)),
            pl.BlockSpec((ROPE_HALF, ts), lambda i: (0, i)),
        ),
        compiler_params=_params(("arbitrary",)), name="rope_tables",
    )(inv_row, inv_col)


def _mla_prep_kernel(x_ref, g_mix_ref, w_a_ref, g_qa_ref, g_kva_ref, wq_t_ref, g_q_ref,
                     wk_ref, wv_t_ref, g_kn_ref, g_kr_ref, g_krot_ref,
                     cos_t_ref, sin_t_ref, cos_f_ref, sin_f_ref,
                     q_ref, k_ref, v_ref, *, tk):
    tm = x_ref.shape[0]
    h = _rms_rows(x_ref[...], g_mix_ref[...]).astype(BF16)
    c = jnp.dot(h, w_a_ref[...], preferred_element_type=F32)
    cq = _rms_rows(c[:, :Q_LORA], g_qa_ref[...]).astype(BF16)
    ckv = _rms_rows(c[:, Q_LORA:Q_LORA + KV_LORA], g_kva_ref[...]).astype(BF16)
    k_pe = c[:, K_PE_COL:K_PE_COL + LANES]
    k_pe_rot = c[:, K_PE_COL + LANES:W_A_COLS]

    q_t = lax.dot_general(wq_t_ref[...], cq, NT_DIMS, preferred_element_type=F32)
    cos_f = cos_f_ref[...]
    sin_f = sin_f_ref[...]
    scale = MLA_QK ** -0.5 * LOG2E
    g_q = g_q_ref[...]
    for hd in range(MLA_HEADS):
        xh = q_t[hd * MLA_QK:(hd + 1) * MLA_QK]
        rinv = lax.rsqrt(jnp.mean(xh * xh, axis=0, keepdims=True) + EPS)
        xn = xh * rinv * g_q
        x1 = xn[MLA_NOPE:MLA_NOPE + ROPE_HALF]
        x2 = xn[MLA_NOPE + ROPE_HALF:]
        q_ref[0, hd, 0:MLA_NOPE, :] = (xn[:MLA_NOPE] * scale).astype(BF16)
        q_ref[0, hd, MLA_NOPE:MLA_NOPE + ROPE_HALF, :] = ((x1 * cos_f - x2 * sin_f) * scale).astype(BF16)
        q_ref[0, hd, MLA_NOPE + ROPE_HALF:MLA_QK, :] = ((x2 * cos_f + x1 * sin_f) * scale).astype(BF16)
        q_ref[0, hd, MLA_QK:, :] = jnp.zeros((MLA_QK_PAD - MLA_QK, tm), BF16)

    k_nope = jnp.dot(ckv, wk_ref[...], preferred_element_type=F32)
    rope_base = (k_pe * g_kr_ref[...]) * cos_t_ref[...] + (k_pe_rot * g_krot_ref[...]) * sin_t_ref[...]
    ss_pe = jnp.sum(k_pe * k_pe, axis=-1, keepdims=True)
    for hd in range(MLA_HEADS):
        kh = k_nope[:, hd * MLA_NOPE:(hd + 1) * MLA_NOPE]
        ss = jnp.sum(kh * kh, axis=-1, keepdims=True) + ss_pe
        rinv = lax.rsqrt(ss * (1.0 / MLA_QK) + EPS)
        k_ref[0, hd, :, 0:LANES] = (kh * rinv * g_kn_ref[...]).astype(BF16)
        k_ref[0, hd, :, LANES:] = (rope_base * rinv).astype(BF16)

    v_t = lax.dot_general(wv_t_ref[...], ckv, NT_DIMS, preferred_element_type=F32)
    for hd in range(MLA_HEADS):
        for cc in range(tm // tk):
            v_ref[0, hd, cc] = v_t[hd * MLA_V:(hd + 1) * MLA_V, cc * tk:(cc + 1) * tk].astype(BF16)


def _mla_prep(x2d, gb, seq, tm, tk, w, tables):
    tokens = x2d.shape[0]
    nt = seq // tm
    cos_t, sin_t, cos_f, sin_f = tables
    kern = functools.partial(_mla_prep_kernel, tk=tk)
    out_shape = (
        jax.ShapeDtypeStruct((gb, MLA_HEADS, MLA_QK_PAD, seq), BF16),
        jax.ShapeDtypeStruct((gb, MLA_HEADS, seq, MLA_QK_PAD), BF16),
        jax.ShapeDtypeStruct((gb, MLA_HEADS, seq // tk, MLA_V, tk), BF16),
    )
    in_specs = [
        pl.BlockSpec((tm, D_MODEL), lambda i: (i, 0)),
        _const_spec((1, D_MODEL)),
        _const_spec((D_MODEL, W_A_COLS)),
        _const_spec((1, Q_LORA)),
        _const_spec((1, KV_LORA)),
        _const_spec((MLA_HEADS * MLA_QK, Q_LORA)),
        _const_spec((MLA_QK, 1)),
        _const_spec((KV_LORA, MLA_HEADS * MLA_NOPE)),
        _const_spec((MLA_HEADS * MLA_V, KV_LORA)),
        _const_spec((1, LANES)),
        _const_spec((1, LANES)),
        _const_spec((1, LANES)),
        pl.BlockSpec((tm, LANES), lambda i: (i % nt, 0)),
        pl.BlockSpec((tm, LANES), lambda i: (i % nt, 0)),
        pl.BlockSpec((ROPE_HALF, tm), lambda i: (0, i % nt)),
        pl.BlockSpec((ROPE_HALF, tm), lambda i: (0, i % nt)),
    ]
    out_specs = (
        pl.BlockSpec((1, MLA_HEADS, MLA_QK_PAD, tm), lambda i: (i // nt, 0, 0, i % nt)),
        pl.BlockSpec((1, MLA_HEADS, tm, MLA_QK_PAD), lambda i: (i // nt, 0, i % nt, 0)),
        pl.BlockSpec((1, MLA_HEADS, tm // tk, MLA_V, tk), lambda i: (i // nt, 0, i % nt, 0, 0)),
    )
    return pl.pallas_call(
        kern, out_shape=out_shape, grid=(tokens // tm,), in_specs=in_specs, out_specs=out_specs,
        compiler_params=_params(("arbitrary",)), name="mla_prep",
    )(x2d, w["g_mix"], w["w_a"], w["g_qa"], w["g_kva"], w["wq_t"], w["g_q"], w["wk_nope"],
      w["wv_t"], w["g_kn"], w["g_kr"], w["g_krot"], cos_t, sin_t, cos_f, sin_f)


def _group_norm_cols(x_t, gain_col):
    rows, tm = x_t.shape
    x3 = x_t.reshape(rows // DIFF_QK, DIFF_QK, tm)
    rinv = lax.rsqrt(jnp.mean(x3 * x3, axis=1, keepdims=True) + EPS)
    return (x3 * rinv * gain_col[None]).reshape(rows, tm)


def _diff_prep_kernel(x_ref, g_mix_ref, wq_t_ref, wk_t_ref, wv_t_ref, g_q_ref, g_k_ref,
                      q_ref, k_ref, v_ref, *, tk):
    tm = x_ref.shape[0]
    h = _rms_rows(x_ref[...], g_mix_ref[...]).astype(BF16)
    scale = DIFF_QK ** -0.5 * LOG2E
    q_t = lax.dot_general(wq_t_ref[...], h, NT_DIMS, preferred_element_type=F32)
    qn = (_group_norm_cols(q_t, g_q_ref[...]) * scale).astype(BF16)
    zeros = jnp.zeros((DIFF_QK, tm), BF16)
    for hd in range(DIFF_HEADS):
        r0 = hd * DIFF_V
        q_ref[0, hd, 0, 0:DIFF_QK, :] = qn[r0:r0 + DIFF_QK]
        q_ref[0, hd, 0, DIFF_QK:, :] = zeros
        q_ref[0, hd, 1, 0:DIFF_QK, :] = zeros
        q_ref[0, hd, 1, DIFF_QK:, :] = qn[r0 + DIFF_QK:r0 + DIFF_V]

    k_t = lax.dot_general(wk_t_ref[...], h, NT_DIMS, preferred_element_type=F32)
    kn = _group_norm_cols(k_t, g_k_ref[...]).T
    for hd in range(DIFF_HEADS):
        k_ref[0, hd] = kn[:, hd * DIFF_V:(hd + 1) * DIFF_V].astype(BF16)

    v_t = lax.dot_general(wv_t_ref[...], h, NT_DIMS, preferred_element_type=F32)
    for hd in range(DIFF_HEADS):
        for cc in range(tm // tk):
            v_ref[0, hd, cc] = v_t[hd * DIFF_V:(hd + 1) * DIFF_V, cc * tk:(cc + 1) * tk].astype(BF16)


def _diff_prep(x2d, gb, seq, tm, tk, w):
    tokens = x2d.shape[0]
    nt = seq // tm
    kern = functools.partial(_diff_prep_kernel, tk=tk)
    out_shape = (
        jax.ShapeDtypeStruct((gb, DIFF_HEADS, 2, DIFF_V, seq), BF16),
        jax.ShapeDtypeStruct((gb, DIFF_HEADS, seq, DIFF_V), BF16),
        jax.ShapeDtypeStruct((gb, DIFF_HEADS, seq // tk, DIFF_V, tk), BF16),
    )
    in_specs = [
        pl.BlockSpec((tm, D_MODEL), lambda i: (i, 0)),
        _const_spec((1, D_MODEL)),
        _const_spec((HEAD_OUT, D_MODEL)),
        _const_spec((HEAD_OUT, D_MODEL)),
        _const_spec((HEAD_OUT, D_MODEL)),
        _const_spec((DIFF_QK, 1)),
        _const_spec((DIFF_QK, 1)),
    ]
    out_specs = (
        pl.BlockSpec((1, DIFF_HEADS, 2, DIFF_V, tm), lambda i: (i // nt, 0, 0, 0, i % nt)),
        pl.BlockSpec((1, DIFF_HEADS, tm, DIFF_V), lambda i: (i // nt, 0, i % nt, 0)),
        pl.BlockSpec((1, DIFF_HEADS, tm // tk, DIFF_V, tk), lambda i: (i // nt, 0, i % nt, 0, 0)),
    )
    return pl.pallas_call(
        kern, out_shape=out_shape, grid=(tokens // tm,), in_specs=in_specs, out_specs=out_specs,
        compiler_params=_params(("arbitrary",)), name="diff_prep",
    )(x2d, w["g_mix"], w["w_dq_t"], w["w_dk_t"], w["w_dv_t"], w["g_dq"], w["g_dk"])


def _bias_tile_kernel(bucket_ref, rb_ref, out_ref):
    hd = pl.program_id(0)
    offset = pl.program_id(1) - pl.num_programs(1) // 2

    @pl.when(offset <= -2)
    def _():
        out_ref[0, 0] = jnp.full(out_ref.shape[2:], rb_ref[REL_BUCKETS // 2 - 1, hd] * LOG2E, F32)

    @pl.when(offset >= 2)
    def _():
        out_ref[0, 0] = jnp.full(out_ref.shape[2:], rb_ref[REL_BUCKETS - 1, hd] * LOG2E, F32)

    @pl.when(jnp.abs(offset) < 2)
    def _():
        bucket = bucket_ref[0]
        acc = jnp.zeros(bucket.shape, F32)
        for b in range(REL_BUCKETS):
            acc = jnp.where(bucket == b, rb_ref[b, hd], acc)
        out_ref[0, 0] = acc * LOG2E


def _t5_bucket(rel):
    nb = REL_BUCKETS // 2
    ret = jnp.where(rel > 0, nb, 0)
    n = jnp.abs(rel)
    max_exact = nb // 2
    nf = jnp.maximum(n, 1).astype(F32)
    large = max_exact + (jnp.log(nf / max_exact) / math.log(REL_MAX_DIST / max_exact)
                         * (nb - max_exact)).astype(jnp.int32)
    large = jnp.minimum(large, nb - 1)
    return ret + jnp.where(n < max_exact, n, large)


def _bias_tiles(rel_bias, blk):
    assert blk >= REL_MAX_DIST
    n = BIAS_TILES
    d = jnp.arange(blk, dtype=jnp.int32)
    offs = (jnp.arange(n, dtype=jnp.int32) - n // 2) * blk
    rel = offs[:, None, None] + d[None, :, None] - d[None, None, :]
    bucket = _t5_bucket(rel)
    return pl.pallas_call(
        _bias_tile_kernel,
        out_shape=jax.ShapeDtypeStruct((DIFF_HEADS, n, blk, blk), F32),
        grid=(DIFF_HEADS, n),
        in_specs=[
            pl.BlockSpec((1, blk, blk), lambda hd, t: (t, 0, 0)),
            pl.BlockSpec(memory_space=pltpu.SMEM),
        ],
        out_specs=pl.BlockSpec((1, 1, blk, blk), lambda hd, t: (hd, t, 0, 0)),
        compiler_params=_params(("arbitrary", "arbitrary")), name="bias_tiles",
    )(bucket, rel_bias)


class _FlashState:
    def __init__(self, q_ref, k_ref, v_ref, scratch, blk, q_tiles):
        self.q_ref, self.k_ref, self.v_ref = q_ref, k_ref, v_ref
        (self.s_ref, self.p_ref, self.alpha_ref, self.m_ref, self.l_ref, self.acc_ref) = scratch
        self.blk = blk
        self.bound = None
        self.streams = [(qt, mp) for qt in range(q_tiles) for mp in range(q_ref.shape[2])]
        self.nc = v_ref.shape[2]

    def q(self, qt, mp):
        return self.q_ref[0, 0, mp, :, qt * self.blk:(qt + 1) * self.blk]

    def key_chunk(self, c):
        return self.k_ref[0, 0, pl.ds(pl.multiple_of(c * self.blk, self.blk), self.blk), :]


    def init_bounded(self):
        self.l_ref[...] = jnp.zeros(self.l_ref.shape, F32)
        self.acc_ref[...] = jnp.zeros(self.acc_ref.shape, F32)
        self.p_ref[1] = jnp.zeros(self.p_ref.shape[1:], BF16)

    def pv_bounded(self, c, half):
        vc = self.v_ref[0, 0, c]
        for sx, (qt, mp) in enumerate(self.streams):
            self.acc_ref[sx] += jnp.dot(vc, self.p_ref[half, sx], preferred_element_type=F32)

    def step_bounded(self, c, half, cst=None, bias_tiles=None):
        self.pv_bounded(jnp.maximum(c - 1, 0), 1 - half)
        kc = self.key_chunk(c)
        for sx, (qt, mp) in enumerate(self.streams):
            s = jnp.dot(kc, self.q(qt, mp), preferred_element_type=F32)
            if bias_tiles is not None:
                s = s + bias_tiles[qt]
            shift = self.bound if cst is None else self.bound - cst
            p = jnp.exp2(s - shift)
            self.l_ref[sx] += jnp.sum(p, axis=0, keepdims=True)
            self.p_ref[half, sx] = p.astype(BF16)

    def finish_bounded(self):
        self.pv_bounded(self.nc - 1, 1)


    def init_online(self):
        self.m_ref[...] = jnp.full(self.m_ref.shape, NEG_BIG, F32)
        self.l_ref[...] = jnp.zeros(self.l_ref.shape, F32)
        self.acc_ref[...] = jnp.zeros(self.acc_ref.shape, F32)
        self.p_ref[0] = jnp.zeros(self.p_ref.shape[1:], BF16)
        self.alpha_ref[...] = jnp.ones(self.alpha_ref.shape, F32)
        self.qk(0, 0)

    def qk(self, c, half):
        kc = self.key_chunk(c)
        for sx, (qt, mp) in enumerate(self.streams):
            self.s_ref[half, sx] = jnp.dot(kc, self.q(qt, mp), preferred_element_type=F32)

    def pv_online(self, c):
        vc = self.v_ref[0, 0, c]
        for sx, (qt, mp) in enumerate(self.streams):
            self.acc_ref[sx] = (self.alpha_ref[sx] * self.acc_ref[sx]
                                + jnp.dot(vc, self.p_ref[0, sx], preferred_element_type=F32))

    def softmax_online(self, half, cst, bias_tiles):
        for sx, (qt, mp) in enumerate(self.streams):
            s = self.s_ref[half, sx]
            if bias_tiles is not None:
                s = s + bias_tiles[qt]
            mc = jnp.max(s, axis=0, keepdims=True)
            if cst is not None:
                mc = mc + cst
            m_old = self.m_ref[sx]
            m_new = jnp.maximum(m_old, mc)
            alpha = jnp.exp2(m_old - m_new)
            shift = m_new if cst is None else m_new - cst
            p = jnp.exp2(s - shift)
            self.l_ref[sx] = alpha * self.l_ref[sx] + jnp.sum(p, axis=0, keepdims=True)
            self.p_ref[0, sx] = p.astype(BF16)
            self.alpha_ref[sx] = alpha
            self.m_ref[sx] = m_new

    def step_online(self, c, half, cst=None, bias_tiles=None):
        self.pv_online(jnp.maximum(c - 1, 0))
        self.qk(jnp.minimum(c + 1, self.nc - 1), 1 - half)
        self.softmax_online(half, cst, bias_tiles)

    def finish_online(self):
        self.pv_online(self.nc - 1)

    def run(self, bound, schedule):
        self.bound = bound
        use_bounded = bound <= BOUNDED_SOFTMAX_LIMIT

        @pl.when(use_bounded)
        def _():
            self.init_bounded()
            schedule(self.step_bounded)
            self.finish_bounded()

        @pl.when(jnp.logical_not(use_bounded))
        def _():
            self.init_online()
            schedule(self.step_online)
            self.finish_online()


def _flash_scratch(n_maps, head_dim, blk):
    return [
        pltpu.VMEM((2, n_maps, blk, blk), F32),
        pltpu.VMEM((2, n_maps, blk, blk), BF16),
        pltpu.VMEM((n_maps, 1, blk), F32),
        pltpu.VMEM((n_maps, 1, blk), F32),
        pltpu.VMEM((n_maps, 1, blk), F32),
        pltpu.VMEM((n_maps, head_dim, blk), F32),
    ]


def _score_bound(head_dim, g_q, g_k):
    return (head_dim ** 0.5 * LOG2E * 1.02 * jnp.max(jnp.abs(g_q)) * jnp.max(jnp.abs(g_k))).reshape(1)


def _mla_attn_kernel(bound_ref, q_ref, k_ref, v_ref, o_ref, *scratch, blk, q_tiles, group):
    st = _FlashState(q_ref, k_ref, v_ref, scratch, blk, q_tiles)

    def schedule(step):
        def body(cc, carry):
            for u in range(group):
                step(group * cc + u, u % 2)
            return carry

        lax.fori_loop(0, st.nc // group, body, 0)

    st.run(bound_ref[0], schedule)
    for sx, (qt, _) in enumerate(st.streams):
        o_t = st.acc_ref[sx] / st.l_ref[sx]
        o_ref[qt * blk:(qt + 1) * blk, :] = o_t.T.astype(BF16)


def _mla_attention(bound, q, k, v, gb, seq, blk, q_tiles, group):
    nc = seq // blk
    nq = nc // q_tiles
    assert group % 2 == 0 and nc % group == 0 and nc % q_tiles == 0
    kern = functools.partial(_mla_attn_kernel, blk=blk, q_tiles=q_tiles, group=group)
    q = q.reshape(gb, MLA_HEADS, 1, MLA_QK_PAD, seq)
    tq = q_tiles * blk
    return pl.pallas_call(
        kern,
        out_shape=jax.ShapeDtypeStruct((gb * seq, HEAD_OUT), BF16),
        grid=(gb, MLA_HEADS, nq),
        in_specs=[
            pl.BlockSpec(memory_space=pltpu.SMEM),
            pl.BlockSpec((1, 1, 1, MLA_QK_PAD, tq), lambda b, hd, i: (b, hd, 0, 0, i)),
            pl.BlockSpec((1, 1, seq, MLA_QK_PAD), lambda b, hd, i: (b, hd, 0, 0)),
            pl.BlockSpec((1, 1, nc, MLA_V, blk), lambda b, hd, i: (b, hd, 0, 0, 0)),
        ],
        out_specs=pl.BlockSpec((tq, MLA_V), lambda b, hd, i: (b * nq + i, hd)),
        scratch_shapes=_flash_scratch(q_tiles, MLA_V, blk),
        compiler_params=_params(("arbitrary", "arbitrary", "arbitrary")), name="mla_attn",
    )(bound, q, k, v)


def _diff_attn_kernel(bound_ref, q_ref, k_ref, v_ref, bias_ref, rb_ref, lam_ref, g_sub_ref, o_ref,
                      *scratch, blk, q_tiles, group, lam_init):
    hd = pl.program_id(0)
    i_first = pl.program_id(2) * q_tiles
    st = _FlashState(q_ref, k_ref, v_ref, scratch, blk, q_tiles)
    acc_ref, l_ref = st.acc_ref, st.l_ref
    n_groups = st.nc // group

    bias_max = jnp.abs(rb_ref[0, hd])
    for b in range(1, REL_BUCKETS):
        bias_max = jnp.maximum(bias_max, jnp.abs(rb_ref[b, hd]))

    far_before = rb_ref[REL_BUCKETS // 2 - 1, hd] * LOG2E
    far_after = rb_ref[REL_BUCKETS - 1, hd] * LOG2E
    first_near = (i_first - 1 + group) // group - 1
    last_near = (i_first + q_tiles) // group

    near_lo = jnp.maximum(first_near, 0)
    near_hi = jnp.minimum(last_near + 1, n_groups)

    def schedule(step):
        def far_group(cst):
            def body(cc, carry):
                for u in range(group):
                    step(group * cc + u, u % 2, cst=cst)
                return carry
            return body

        def near_group(cc, carry):
            for u in range(group):
                c = group * cc + u
                far = BIAS_TILES // 2
                tiles = [bias_ref[0, jnp.clip(c - (i_first + qt), -far, far) + far]
                         for qt in range(q_tiles)]
                step(c, u % 2, bias_tiles=tiles)
            return carry

        lax.fori_loop(0, near_lo, far_group(far_before), 0)
        lax.fori_loop(near_lo, near_hi, near_group, 0)
        lax.fori_loop(near_hi, n_groups, far_group(far_after), 0)

    st.run(bound_ref[0] + bias_max * LOG2E, schedule)

    lam_v = lam_ref[...]
    lam = (jnp.exp(jnp.sum(lam_v[0:1] * lam_v[1:2], axis=-1, keepdims=True))
           - jnp.exp(jnp.sum(lam_v[2:3] * lam_v[3:4], axis=-1, keepdims=True)) + lam_init)
    for qt in range(q_tiles):
        s0, s1 = 2 * qt, 2 * qt + 1
        o_t = acc_ref[s0] / l_ref[s0] - lam * (acc_ref[s1] / l_ref[s1])
        o_t = o_t * lax.rsqrt(jnp.mean(o_t * o_t, axis=0, keepdims=True) + EPS) * g_sub_ref[...]
        o_ref[qt * blk:(qt + 1) * blk, :] = (o_t * (1.0 - lam_init)).T.astype(BF16)


def _diff_attention(bound, q, k, v, bias, rel_bias, lam_rows, g_sub, gb, seq, blk, q_tiles, group,
                    lam_init):
    nc = seq // blk
    nq = nc // q_tiles
    assert group % 2 == 0 and nc % group == 0 and nc % q_tiles == 0 and blk >= REL_MAX_DIST
    kern = functools.partial(_diff_attn_kernel, blk=blk, q_tiles=q_tiles, group=group,
                             lam_init=lam_init)
    tq = q_tiles * blk
    return pl.pallas_call(
        kern,
        out_shape=jax.ShapeDtypeStruct((gb * seq, HEAD_OUT), BF16),
        grid=(DIFF_HEADS, gb, nq),
        in_specs=[
            pl.BlockSpec(memory_space=pltpu.SMEM),
            pl.BlockSpec((1, 1, 2, DIFF_V, tq), lambda hd, b, i: (b, hd, 0, 0, i)),
            pl.BlockSpec((1, 1, seq, DIFF_V), lambda hd, b, i: (b, hd, 0, 0)),
            pl.BlockSpec((1, 1, nc, DIFF_V, blk), lambda hd, b, i: (b, hd, 0, 0, 0)),
            pl.BlockSpec((1, BIAS_TILES, blk, blk), lambda hd, b, i: (hd, 0, 0, 0),
                         pipeline_mode=pl.Buffered(1)),
            pl.BlockSpec(memory_space=pltpu.SMEM),
            _const_spec((4, LANES)),
            _const_spec((DIFF_V, 1)),
        ],
        out_specs=pl.BlockSpec((tq, DIFF_V), lambda hd, b, i: (b * nq + i, hd)),
        scratch_shapes=_flash_scratch(2 * q_tiles, DIFF_V, blk),
        compiler_params=_params(("arbitrary", "arbitrary", "arbitrary")), name="diff_attn",
    )(bound, q, k, v, bias, rel_bias, lam_rows, g_sub)


def _merge_kernel(x_ref, g_mix_ref, oa_ref, ob_ref, wga_ref, wgb_ref, wua_ref, wub_ref,
                  out_ref, h_ref):
    @pl.when(pl.program_id(1) == 0)
    def _():
        h_ref[...] = _rms_rows(x_ref[...], g_mix_ref[...]).astype(BF16)

    h = h_ref[...]
    ga = jnp.dot(h, wga_ref[...], preferred_element_type=F32)
    gb_ = jnp.dot(h, wgb_ref[...], preferred_element_type=F32)
    ua = jnp.dot(oa_ref[...], wua_ref[...], preferred_element_type=F32)
    ub = jnp.dot(ob_ref[...], wub_ref[...], preferred_element_type=F32)
    out_ref[...] = (jax.nn.sigmoid(ga) * ua + jax.nn.sigmoid(gb_) * ub).astype(BF16)


def _merge(x2d, o_a, o_b, w, tm, tn):
    tokens = x2d.shape[0]
    return pl.pallas_call(
        _merge_kernel,
        out_shape=jax.ShapeDtypeStruct((tokens, D_MODEL), BF16),
        grid=(tokens // tm, D_MODEL // tn),
        in_specs=[
            pl.BlockSpec((tm, D_MODEL), lambda i, j: (i, 0)),
            _const_spec((1, D_MODEL)),
            pl.BlockSpec((tm, HEAD_OUT), lambda i, j: (i, 0)),
            pl.BlockSpec((tm, HEAD_OUT), lambda i, j: (i, 0)),
            pl.BlockSpec((D_MODEL, tn), lambda i, j: (0, j)),
            pl.BlockSpec((D_MODEL, tn), lambda i, j: (0, j)),
            pl.BlockSpec((HEAD_OUT, tn), lambda i, j: (0, j)),
            pl.BlockSpec((HEAD_OUT, tn), lambda i, j: (0, j)),
        ],
        out_specs=pl.BlockSpec((tm, tn), lambda i, j: (i, j)),
        scratch_shapes=[pltpu.VMEM((tm, D_MODEL), BF16)],
        compiler_params=_params(("arbitrary", "arbitrary")), name="gated_merge",
    )(x2d, w["g_mix"], o_a, o_b, w["w_ga"], w["w_gb"], w["w_ua"], w["w_ub"])


def _out_proj_kernel(x_ref, m_ref, wo_ref, out_ref):
    out_ref[...] = x_ref[...] + jnp.dot(m_ref[...], wo_ref[...], preferred_element_type=F32)


def _out_proj(x2d, merged, w, tm):
    tokens = x2d.shape[0]
    return pl.pallas_call(
        _out_proj_kernel,
        out_shape=jax.ShapeDtypeStruct((tokens, D_MODEL), F32),
        grid=(tokens // tm,),
        in_specs=[
            pl.BlockSpec((tm, D_MODEL), lambda i: (i, 0)),
            pl.BlockSpec((tm, D_MODEL), lambda i: (i, 0)),
            _const_spec((D_MODEL, D_MODEL)),
        ],
        out_specs=pl.BlockSpec((tm, D_MODEL), lambda i: (i, 0)),
        compiler_params=_params(("arbitrary",)), name="out_proj",
    )(x2d, merged, w["w_o"])


def _ffn_kernel(x_ref, g_ref, wg_ref, wu_ref, wd_ref, out_ref, h_ref):
    @pl.when(pl.program_id(1) == 0)
    def _():
        x = x_ref[...]
        h_ref[...] = _rms_rows(x, g_ref[...]).astype(BF16)
        out_ref[...] = x

    h = h_ref[...]
    g = jnp.dot(h, wg_ref[...], preferred_element_type=F32)
    u = jnp.dot(h, wu_ref[...], preferred_element_type=F32)
    a = (g * jax.nn.sigmoid(g) * u).astype(BF16)
    out_ref[...] += jnp.dot(a, wd_ref[...], preferred_element_type=F32)


def _ffn(x2d, w, tm, tf):
    tokens = x2d.shape[0]
    return pl.pallas_call(
        _ffn_kernel,
        out_shape=jax.ShapeDtypeStruct((tokens, D_MODEL), F32),
        grid=(tokens // tm, D_FF // tf),
        in_specs=[
            pl.BlockSpec((tm, D_MODEL), lambda i, j: (i, 0)),
            _const_spec((1, D_MODEL)),
            pl.BlockSpec((D_MODEL, tf), lambda i, j: (0, j)),
            pl.BlockSpec((D_MODEL, tf), lambda i, j: (0, j)),
            pl.BlockSpec((tf, D_MODEL), lambda i, j: (j, 0)),
        ],
        out_specs=pl.BlockSpec((tm, D_MODEL), lambda i, j: (i, 0)),
        scratch_shapes=[pltpu.VMEM((tm, D_MODEL), BF16)],
        compiler_params=_params(("arbitrary", "arbitrary")), name="swiglu_ffn",
    )(x2d, w["g_ffn"], w["w_gate"], w["w_up"], w["w_down"])


def _prepare_weights(mix_norm, w_in, q_a_norm, wq_b, kv_a_norm, wkv_b, mla_q_norm, mla_k_norm,
                     diff_q_norm, diff_k_norm, diff_subln, w_up_mla, w_up_diff, w_o, ffn_norm,
                     w_gate, w_up, w_down, layer):
    win = w_in[layer]
    o_cq, o_ckv, o_kpe = 0, Q_LORA, Q_LORA + KV_LORA
    o_dq = o_kpe + MLA_ROPE
    o_dk = o_dq + HEAD_OUT
    o_dv = o_dk + HEAD_OUT
    o_ga = o_dv + HEAD_OUT
    o_gb = o_ga + D_MODEL
    w_kpe = win[:, o_kpe:o_dq]
    w_kpe_rot = jnp.concatenate([-w_kpe[:, ROPE_HALF:], w_kpe[:, :ROPE_HALF]], axis=1)
    zpad = jnp.zeros((D_MODEL, LANES - MLA_ROPE), F32)
    w_a = jnp.concatenate([win[:, o_cq:o_kpe], w_kpe, zpad, w_kpe_rot, zpad], axis=1)
    wkv = wkv_b[layer].reshape(KV_LORA, MLA_HEADS, MLA_NOPE + MLA_V)
    gk = mla_k_norm[layer]
    gk_rope = gk[MLA_NOPE:]
    lane_pad = jnp.zeros((LANES - MLA_ROPE,), F32)
    return {
        "g_mix": mix_norm[layer][None, :],
        "w_a": w_a.astype(BF16),
        "g_qa": q_a_norm[layer][None, :],
        "g_kva": kv_a_norm[layer][None, :],
        "wq_t": wq_b[layer].T.astype(BF16),
        "g_q": mla_q_norm[layer][:, None],
        "wk_nope": wkv[:, :, :MLA_NOPE].reshape(KV_LORA, MLA_HEADS * MLA_NOPE).astype(BF16),
        "wv_t": wkv[:, :, MLA_NOPE:].reshape(KV_LORA, MLA_HEADS * MLA_V).T.astype(BF16),
        "g_kn": gk[None, :MLA_NOPE],
        "g_kr": jnp.concatenate([gk_rope, lane_pad])[None, :],
        "g_krot": jnp.concatenate([gk_rope[ROPE_HALF:], gk_rope[:ROPE_HALF], lane_pad])[None, :],
        "w_dq_t": win[:, o_dq:o_dk].T.astype(BF16),
        "w_dk_t": win[:, o_dk:o_dv].T.astype(BF16),
        "w_dv_t": win[:, o_dv:o_ga].T.astype(BF16),
        "g_dq": diff_q_norm[layer][:, None],
        "g_dk": diff_k_norm[layer][:, None],
        "g_sub": diff_subln[layer][:, None],
        "w_ga": win[:, o_ga:o_gb].astype(BF16),
        "w_gb": win[:, o_gb:].astype(BF16),
        "w_ua": w_up_mla[layer].astype(BF16),
        "w_ub": w_up_diff[layer].astype(BF16),
        "w_o": w_o[layer].astype(BF16),
        "mla_bound": _score_bound(MLA_QK, mla_q_norm[layer], gk),
        "diff_bound": _score_bound(DIFF_QK, diff_q_norm[layer], diff_k_norm[layer]),
        "g_ffn": ffn_norm[layer][None, :],
        "w_gate": w_gate[layer].astype(BF16),
        "w_up": w_up[layer].astype(BF16),
        "w_down": w_down[layer].astype(BF16),
    }


class _Tiles(NamedTuple):
    tm: int = 512
    tn: int = 512
    tf: int = 512
    blk: int = 512
    mla_q_tiles: int = 4
    diff_q_tiles: int = 2
    mla_group_max: int = 4
    diff_group_max: int = 4


def _encoder_layer(x, w, tables, bias, rel_bias, lam_rows, lam_init, t):
    gb, seq, _ = x.shape
    x2d = x.reshape(gb * seq, D_MODEL)
    mla_group = math.gcd(t.mla_group_max, seq // t.blk)
    diff_group = math.gcd(t.diff_group_max, seq // t.blk)
    q_a, k_a, v_a = _mla_prep(x2d, gb, seq, t.tm, t.blk, w, tables)
    q_d, k_d, v_d = _diff_prep(x2d, gb, seq, t.tm, t.blk, w)
    o_a = _mla_attention(w["mla_bound"], q_a, k_a, v_a, gb, seq, t.blk, t.mla_q_tiles, mla_group)
    o_b = _diff_attention(w["diff_bound"], q_d, k_d, v_d, bias, rel_bias, lam_rows, w["g_sub"], gb,
                          seq, t.blk, t.diff_q_tiles, diff_group, lam_init)
    merged = _merge(x2d, o_a, o_b, w, t.tm, t.tn)
    x1 = _out_proj(x2d, merged, w, t.tm)
    y = _ffn(x1, w, t.tm, t.tf)
    return y.reshape(gb, seq, D_MODEL)


def kernel(x_prompt, x_sample, mix_norm, w_in, q_a_norm, wq_b, kv_a_norm, wkv_b, mla_q_norm, mla_k_norm, diff_q_norm, diff_k_norm, lambda_q1, lambda_k1, lambda_q2, lambda_k2, diff_subln, w_up_mla, w_up_diff, w_o, ffn_norm, w_gate, w_up, w_down, rel_bias):
    t = _Tiles()
    depth = w_in.shape[0]
    max_seq = max(x_prompt.shape[1], x_sample.shape[1])
    tables = _rope_tables(max_seq)
    bias = _bias_tiles(rel_bias, t.blk)
    y_prompt, y_sample = x_prompt, x_sample
    for layer in range(depth):
        w = _prepare_weights(mix_norm, w_in, q_a_norm, wq_b, kv_a_norm, wkv_b, mla_q_norm,
                             mla_k_norm, diff_q_norm, diff_k_norm, diff_subln, w_up_mla,
                             w_up_diff, w_o, ffn_norm, w_gate, w_up, w_down, layer)
        lam_init = 0.8 - 0.6 * math.exp(-0.3 * layer)
        lam_pad = jnp.zeros((LANES - DIFF_QK,), F32)
        lam_rows = jnp.stack([jnp.concatenate([v[layer], lam_pad])
                              for v in (lambda_q1, lambda_k1, lambda_q2, lambda_k2)])
        run = functools.partial(_encoder_layer, w=w, tables=tables, bias=bias, rel_bias=rel_bias,
                                lam_rows=lam_rows, lam_init=lam_init, t=t)
        y_prompt = run(y_prompt)
        y_sample = run(y_sample)
    return (y_prompt, y_sample)
```

```python
import functools
import math
from typing import NamedTuple

import jax
import jax.numpy as jnp
from jax import lax
from jax.experimental import pallas as pl
from jax.experimental.pallas import tpu as pltpu

D_MODEL = 2048
MLA_HEADS = 8
MLA_NOPE = 128
MLA_ROPE = 64
MLA_QK = MLA_NOPE + MLA_ROPE
MLA_V = 128
Q_LORA = 512
KV_LORA = 256
ROPE_THETA = 10000.0
ROPE_HALF = MLA_ROPE // 2
DIFF_HEADS = 8
DIFF_QK = 64
DIFF_V = 2 * DIFF_QK
REL_BUCKETS = 32
REL_MAX_DIST = 128
D_FF = 5632
EPS = 1e-6
HEAD_OUT = MLA_HEADS * MLA_V

LANES = 128
MLA_QK_PAD = 2 * LANES
K_PE_COL = Q_LORA + KV_LORA
W_A_COLS = K_PE_COL + 2 * LANES
ROPE_TABLE_ROWS = 2048
VMEM_LIMIT = 56 * 1024 * 1024

F32 = jnp.float32
BF16 = jnp.bfloat16
NEG_BIG = -1e30
LOG2E = math.log2(math.e)
BOUNDED_SOFTMAX_LIMIT = 50.0
BIAS_TILES = 5

NT_DIMS = (((1,), (1,)), ((), ()))


def _params(semantics):
    return pltpu.CompilerParams(dimension_semantics=semantics, vmem_limit_bytes=VMEM_LIMIT)


def _const_spec(shape):
    zeros = (0,) * len(shape)
    return pl.BlockSpec(shape, lambda *_: zeros)


def _rms_rows(x, gain):
    return x * lax.rsqrt(jnp.mean(x * x, axis=-1, keepdims=True) + EPS) * gain


def _rope_table_kernel(inv_row_ref, inv_col_ref, cos_t_ref, sin_t_ref, cos_f_ref, sin_f_ref):
    ts = cos_t_ref.shape[0]
    base = pl.program_id(0) * ts
    pos_rows = (base + lax.broadcasted_iota(jnp.int32, (ts, LANES), 0)).astype(F32)
    ang_t = pos_rows * inv_row_ref[...]
    cos_t_ref[...] = jnp.cos(ang_t)
    sin_t_ref[...] = jnp.sin(ang_t)
    pos_cols = (base + lax.broadcasted_iota(jnp.int32, (ROPE_HALF, ts), 1)).astype(F32)
    ang_f = pos_cols * inv_col_ref[...]
    cos_f_ref[...] = jnp.cos(ang_f)
    sin_f_ref[...] = jnp.sin(ang_f)


def _rope_tables(seq):
    inv = ROPE_THETA ** (-jnp.arange(ROPE_HALF, dtype=F32) / ROPE_HALF)
    inv_row = jnp.tile(inv, LANES // ROPE_HALF)[None, :]
    inv_col = inv[:, None]
    ts = min(seq, ROPE_TABLE_ROWS)
    tok = jax.ShapeDtypeStruct((seq, LANES), F32)
    feat = jax.ShapeDtypeStruct((ROPE_HALF, seq), F32)
    return pl.pallas_call(
        _rope_table_kernel,
        out_shape=(tok, tok, feat, feat),
        grid=(seq // ts,),
        in_specs=[_const_spec((1, LANES)), _const_spec((ROPE_HALF, 1))],
        out_specs=(
            pl.BlockSpec((ts, LANES), lambda i: (i, 0)),
            pl.BlockSpec((ts, LANES), lambda i: (i, 0)),
            pl.BlockSpec((ROPE_HALF, ts), lambda i: (0, i)),
            pl.BlockSpec((ROPE_HALF, ts), lambda i: (0, i)),
        ),
        compiler_params=_params(("arbitrary",)), name="rope_tables",
    )(inv_row, inv_col)


def _mla_prep_kernel(x_ref, g_mix_ref, w_a_ref, g_qa_ref, g_kva_ref, wq_t_ref, g_q_ref,
                     wk_ref, wv_t_ref, g_kn_ref, g_kr_ref, g_krot_ref,
                     cos_t_ref, sin_t_ref, cos_f_ref, sin_f_ref,
                     q_ref, k_ref, v_ref, *, tk):
    tm = x_ref.shape[0]
    h = _rms_rows(x_ref[...], g_mix_ref[...]).astype(BF16)
    c = jnp.dot(h, w_a_ref[...], preferred_element_type=F32)
    cq = _rms_rows(c[:, :Q_LORA], g_qa_ref[...]).astype(BF16)
    ckv = _rms_rows(c[:, Q_LORA:Q_LORA + KV_LORA], g_kva_ref[...]).astype(BF16)
    k_pe = c[:, K_PE_COL:K_PE_COL + LANES]
    k_pe_rot = c[:, K_PE_COL + LANES:W_A_COLS]

    q_t = lax.dot_general(wq_t_ref[...], cq, NT_DIMS, preferred_element_type=F32)
    cos_f = cos_f_ref[...]
    sin_f = sin_f_ref[...]
    scale = MLA_QK ** -0.5 * LOG2E
    g_q = g_q_ref[...]
    for hd in range(MLA_HEADS):
        xh = q_t[hd * MLA_QK:(hd + 1) * MLA_QK]
        rinv = lax.rsqrt(jnp.mean(xh * xh, axis=0, keepdims=True) + EPS)
        xn = xh * rinv * g_q
        x1 = xn[MLA_NOPE:MLA_NOPE + ROPE_HALF]
        x2 = xn[MLA_NOPE + ROPE_HALF:]
        q_ref[0, hd, 0:MLA_NOPE, :] = (xn[:MLA_NOPE] * scale).astype(BF16)
        q_ref[0, hd, MLA_NOPE:MLA_NOPE + ROPE_HALF, :] = ((x1 * cos_f - x2 * sin_f) * scale).astype(BF16)
        q_ref[0, hd, MLA_NOPE + ROPE_HALF:MLA_QK, :] = ((x2 * cos_f + x1 * sin_f) * scale).astype(BF16)
        q_ref[0, hd, MLA_QK:, :] = jnp.zeros((MLA_QK_PAD - MLA_QK, tm), BF16)

    k_nope = jnp.dot(ckv, wk_ref[...], preferred_element_type=F32)
    rope_base = (k_pe * g_kr_ref[...]) * cos_t_ref[...] + (k_pe_rot * g_krot_ref[...]) * sin_t_ref[...]
    ss_pe = jnp.sum(k_pe * k_pe, axis=-1, keepdims=True)
    for hd in range(MLA_HEADS):
        kh = k_nope[:, hd * MLA_NOPE:(hd + 1) * MLA_NOPE]
        ss = jnp.sum(kh * kh, axis=-1, keepdims=True) + ss_pe
        rinv = lax.rsqrt(ss * (1.0 / MLA_QK) + EPS)
        k_ref[0, hd, :, 0:LANES] = (kh * rinv * g_kn_ref[...]).astype(BF16)
        k_ref[0, hd, :, LANES:] = (rope_base * rinv).astype(BF16)

    v_t = lax.dot_general(wv_t_ref[...], ckv, NT_DIMS, preferred_element_type=F32)
    for hd in range(MLA_HEADS):
        for cc in range(tm // tk):
            v_ref[0, hd, cc] = v_t[hd * MLA_V:(hd + 1) * MLA_V, cc * tk:(cc + 1) * tk].astype(BF16)


def _mla_prep(x2d, gb, seq, tm, tk, w, tables):
    tokens = x2d.shape[0]
    nt = seq // tm
    cos_t, sin_t, cos_f, sin_f = tables
    kern = functools.partial(_mla_prep_kernel, tk=tk)
    out_shape = (
        jax.ShapeDtypeStruct((gb, MLA_HEADS, MLA_QK_PAD, seq), BF16),
        jax.ShapeDtypeStruct((gb, MLA_HEADS, seq, MLA_QK_PAD), BF16),
        jax.ShapeDtypeStruct((gb, MLA_HEADS, seq // tk, MLA_V, tk), BF16),
    )
    in_specs = [
        pl.BlockSpec((tm, D_MODEL), lambda i: (i, 0)),
        _const_spec((1, D_MODEL)),
        _const_spec((D_MODEL, W_A_COLS)),
        _const_spec((1, Q_LORA)),
        _const_spec((1, KV_LORA)),
        _const_spec((MLA_HEADS * MLA_QK, Q_LORA)),
        _const_spec((MLA_QK, 1)),
        _const_spec((KV_LORA, MLA_HEADS * MLA_NOPE)),
        _const_spec((MLA_HEADS * MLA_V, KV_LORA)),
        _const_spec((1, LANES)),
        _const_spec((1, LANES)),
        _const_spec((1, LANES)),
        pl.BlockSpec((tm, LANES), lambda i: (i % nt, 0)),
        pl.BlockSpec((tm, LANES), lambda i: (i % nt, 0)),
        pl.BlockSpec((ROPE_HALF, tm), lambda i: (0, i % nt)),
        pl.BlockSpec((ROPE_HALF, tm), lambda i: (0, i % nt)),
    ]
    out_specs = (
        pl.BlockSpec((1, MLA_HEADS, MLA_QK_PAD, tm), lambda i: (i // nt, 0, 0, i % nt)),
        pl.BlockSpec((1, MLA_HEADS, tm, MLA_QK_PAD), lambda i: (i // nt, 0, i % nt, 0)),
        pl.BlockSpec((1, MLA_HEADS, tm // tk, MLA_V, tk), lambda i: (i // nt, 0, i % nt, 0, 0)),
    )
    return pl.pallas_call(
        kern, out_shape=out_shape, grid=(tokens // tm,), in_specs=in_specs, out_specs=out_specs,
        compiler_params=_params(("arbitrary",)), name="mla_prep",
    )(x2d, w["g_mix"], w["w_a"], w["g_qa"], w["g_kva"], w["wq_t"], w["g_q"], w["wk_nope"],
      w["wv_t"], w["g_kn"], w["g_kr"], w["g_krot"], cos_t, sin_t, cos_f, sin_f)


def _group_norm_cols(x_t, gain_col):
    rows, tm = x_t.shape
    x3 = x_t.reshape(rows // DIFF_QK, DIFF_QK, tm)
    rinv = lax.rsqrt(jnp.mean(x3 * x3, axis=1, keepdims=True) + EPS)
    return (x3 * rinv * gain_col[None]).reshape(rows, tm)


def _diff_prep_kernel(x_ref, g_mix_ref, wq_t_ref, wk_t_ref, wv_t_ref, g_q_ref, g_k_ref,
                      q_ref, k_ref, v_ref, *, tk):
    tm = x_ref.shape[0]
    h = _rms_rows(x_ref[...], g_mix_ref[...]).astype(BF16)
    scale = DIFF_QK ** -0.5 * LOG2E
    q_t = lax.dot_general(wq_t_ref[...], h, NT_DIMS, preferred_element_type=F32)
    qn = (_group_norm_cols(q_t, g_q_ref[...]) * scale).astype(BF16)
    zeros = jnp.zeros((DIFF_QK, tm), BF16)
    for hd in range(DIFF_HEADS):
        r0 = hd * DIFF_V
        q_ref[0, hd, 0, 0:DIFF_QK, :] = qn[r0:r0 + DIFF_QK]
        q_ref[0, hd, 0, DIFF_QK:, :] = zeros
        q_ref[0, hd, 1, 0:DIFF_QK, :] = zeros
        q_ref[0, hd, 1, DIFF_QK:, :] = qn[r0 + DIFF_QK:r0 + DIFF_V]

    k_t = lax.dot_general(wk_t_ref[...], h, NT_DIMS, preferred_element_type=F32)
    kn = _group_norm_cols(k_t, g_k_ref[...]).T
    for hd in range(DIFF_HEADS):
        k_ref[0, hd] = kn[:, hd * DIFF_V:(hd + 1) * DIFF_V].astype(BF16)

    v_t = lax.dot_general(wv_t_ref[...], h, NT_DIMS, preferred_element_type=F32)
    for hd in range(DIFF_HEADS):
        for cc in range(tm // tk):
            v_ref[0, hd, cc] = v_t[hd * DIFF_V:(hd + 1) * DIFF_V, cc * tk:(cc + 1) * tk].astype(BF16)


def _diff_prep(x2d, gb, seq, tm, tk, w):
    tokens = x2d.shape[0]
    nt = seq // tm
    kern = functools.partial(_diff_prep_kernel, tk=tk)
    out_shape = (
        jax.ShapeDtypeStruct((gb, DIFF_HEADS, 2, DIFF_V, seq), BF16),
        jax.ShapeDtypeStruct((gb, DIFF_HEADS, seq, DIFF_V), BF16),
        jax.ShapeDtypeStruct((gb, DIFF_HEADS, seq // tk, DIFF_V, tk), BF16),
    )
    in_specs = [
        pl.BlockSpec((tm, D_MODEL), lambda i: (i, 0)),
        _const_spec((1, D_MODEL)),
        _const_spec((HEAD_OUT, D_MODEL)),
        _const_spec((HEAD_OUT, D_MODEL)),
        _const_spec((HEAD_OUT, D_MODEL)),
        _const_spec((DIFF_QK, 1)),
        _const_spec((DIFF_QK, 1)),
    ]
    out_specs = (
        pl.BlockSpec((1, DIFF_HEADS, 2, DIFF_V, tm), lambda i: (i // nt, 0, 0, 0, i % nt)),
        pl.BlockSpec((1, DIFF_HEADS, tm, DIFF_V), lambda i: (i // nt, 0, i % nt, 0)),
        pl.BlockSpec((1, DIFF_HEADS, tm // tk, DIFF_V, tk), lambda i: (i // nt, 0, i % nt, 0, 0)),
    )
    return pl.pallas_call(
        kern, out_shape=out_shape, grid=(tokens // tm,), in_specs=in_specs, out_specs=out_specs,
        compiler_params=_params(("arbitrary",)), name="diff_prep",
    )(x2d, w["g_mix"], w["w_dq_t"], w["w_dk_t"], w["w_dv_t"], w["g_dq"], w["g_dk"])


def _bias_tile_kernel(bucket_ref, rb_ref, out_ref):
    hd = pl.program_id(0)
    offset = pl.program_id(1) - pl.num_programs(1) // 2

    @pl.when(offset <= -2)
    def _():
        out_ref[0, 0] = jnp.full(out_ref.shape[2:], rb_ref[REL_BUCKETS // 2 - 1, hd] * LOG2E, F32)

    @pl.when(offset >= 2)
    def _():
        out_ref[0, 0] = jnp.full(out_ref.shape[2:], rb_ref[REL_BUCKETS - 1, hd] * LOG2E, F32)

    @pl.when(jnp.abs(offset) < 2)
    def _():
        bucket = bucket_ref[0]
        acc = jnp.zeros(bucket.shape, F32)
        for b in range(REL_BUCKETS):
            acc = jnp.where(bucket == b, rb_ref[b, hd], acc)
        out_ref[0, 0] = acc * LOG2E


def _t5_bucket(rel):
    nb = REL_BUCKETS // 2
    ret = jnp.where(rel > 0, nb, 0)
    n = jnp.abs(rel)
    max_exact = nb // 2
    nf = jnp.maximum(n, 1).astype(F32)
    large = max_exact + (jnp.log(nf / max_exact) / math.log(REL_MAX_DIST / max_exact)
                         * (nb - max_exact)).astype(jnp.int32)
    large = jnp.minimum(large, nb - 1)
    return ret + jnp.where(n < max_exact, n, large)


def _bias_tiles(rel_bias, blk):
    assert blk >= REL_MAX_DIST
    n = BIAS_TILES
    d = jnp.arange(blk, dtype=jnp.int32)
    offs = (jnp.arange(n, dtype=jnp.int32) - n // 2) * blk
    rel = offs[:, None, None] + d[None, :, None] - d[None, None, :]
    bucket = _t5_bucket(rel)
    return pl.pallas_call(
        _bias_tile_kernel,
        out_shape=jax.ShapeDtypeStruct((DIFF_HEADS, n, blk, blk), F32),
        grid=(DIFF_HEADS, n),
        in_specs=[
            pl.BlockSpec((1, blk, blk), lambda hd, t: (t, 0, 0)),
            pl.BlockSpec(memory_space=pltpu.SMEM),
        ],
        out_specs=pl.BlockSpec((1, 1, blk, blk), lambda hd, t: (hd, t, 0, 0)),
        compiler_params=_params(("arbitrary", "arbitrary")), name="bias_tiles",
    )(bucket, rel_bias)


class _FlashState:
    def __init__(self, q_ref, k_ref, v_ref, scratch, blk, q_tiles):
        self.q_ref, self.k_ref, self.v_ref = q_ref, k_ref, v_ref
        self.p_ref, self.alpha_ref, self.m_ref, self.l_ref, self.acc_ref = scratch
        self.blk = blk
        self.bound = None
        self.streams = [(qt, mp) for qt in range(q_tiles) for mp in range(q_ref.shape[2])]
        self.nc = v_ref.shape[2]

    def q(self, qt, mp):
        return self.q_ref[0, 0, mp, :, qt * self.blk:(qt + 1) * self.blk]

    def key_chunk(self, c):
        return self.k_ref[0, 0, pl.ds(pl.multiple_of(c * self.blk, self.blk), self.blk), :]

    def init(self, online):
        self.l_ref[...] = jnp.zeros(self.l_ref.shape, F32)
        self.acc_ref[...] = jnp.zeros(self.acc_ref.shape, F32)
        self.p_ref[1] = jnp.zeros(self.p_ref.shape[1:], BF16)
        if online:
            self.m_ref[...] = jnp.full(self.m_ref.shape, NEG_BIG, F32)
            self.alpha_ref[...] = jnp.ones(self.alpha_ref.shape, F32)

    def pv(self, online, c, half):
        vc = self.v_ref[0, 0, c]
        for sx in range(len(self.streams)):
            update = jnp.dot(vc, self.p_ref[half, sx], preferred_element_type=F32)
            if online:
                self.acc_ref[sx] = self.alpha_ref[sx] * self.acc_ref[sx] + update
            else:
                self.acc_ref[sx] += update

    def step(self, online, c, half, cst=None, bias_tiles=None):
        self.pv(online, jnp.maximum(c - 1, 0), 1 - half)
        kc = self.key_chunk(c)
        for sx, (qt, mp) in enumerate(self.streams):
            s = jnp.dot(kc, self.q(qt, mp), preferred_element_type=F32)
            if bias_tiles is not None:
                s = s + bias_tiles[qt]
            l_old = self.l_ref[sx]
            if online:
                mc = jnp.max(s, axis=0, keepdims=True)
                if cst is not None:
                    mc = mc + cst
                m_old = self.m_ref[sx]
                shift = jnp.maximum(m_old, mc)
                alpha = jnp.exp2(m_old - shift)
                l_old = alpha * l_old
                self.alpha_ref[sx] = alpha
                self.m_ref[sx] = shift
            else:
                shift = self.bound
            if cst is not None:
                shift = shift - cst
            p = jnp.exp2(s - shift)
            self.l_ref[sx] = l_old + jnp.sum(p, axis=0, keepdims=True)
            self.p_ref[half, sx] = p.astype(BF16)

    def run(self, bound, schedule):
        self.bound = bound
        use_bound = bound <= BOUNDED_SOFTMAX_LIMIT
        for online, pred in ((False, use_bound), (True, jnp.logical_not(use_bound))):
            @pl.when(pred)
            def _():
                self.init(online)
                schedule(functools.partial(self.step, online))
                self.pv(online, self.nc - 1, 1)


def _flash_scratch(n_streams, head_dim, blk):
    return [
        pltpu.VMEM((2, n_streams, blk, blk), BF16),
        pltpu.VMEM((n_streams, 1, blk), F32),
        pltpu.VMEM((n_streams, 1, blk), F32),
        pltpu.VMEM((n_streams, 1, blk), F32),
        pltpu.VMEM((n_streams, head_dim, blk), F32),
    ]


def _score_bound(head_dim, g_q, g_k):
    return (head_dim ** 0.5 * LOG2E * 1.02 * jnp.max(jnp.abs(g_q)) * jnp.max(jnp.abs(g_k))).reshape(1)


def _mla_attn_kernel(bound_ref, q_ref, k_ref, v_ref, o_ref, *scratch, blk, q_tiles, group):
    st = _FlashState(q_ref, k_ref, v_ref, scratch, blk, q_tiles)

    def schedule(step):
        def body(cc, carry):
            for u in range(group):
                step(group * cc + u, u % 2)
            return carry

        lax.fori_loop(0, st.nc // group, body, 0)

    st.run(bound_ref[0], schedule)
    for sx, (qt, _) in enumerate(st.streams):
        o_t = st.acc_ref[sx] / st.l_ref[sx]
        o_ref[qt * blk:(qt + 1) * blk, :] = o_t.T.astype(BF16)


def _mla_attention(bound, q, k, v, gb, seq, blk, q_tiles, group):
    nc = seq // blk
    nq = nc // q_tiles
    assert group % 2 == 0 and nc % group == 0 and nc % q_tiles == 0
    kern = functools.partial(_mla_attn_kernel, blk=blk, q_tiles=q_tiles, group=group)
    q = q.reshape(gb, MLA_HEADS, 1, MLA_QK_PAD, seq)
    tq = q_tiles * blk
    return pl.pallas_call(
        kern,
        out_shape=jax.ShapeDtypeStruct((gb * seq, HEAD_OUT), BF16),
        grid=(gb, MLA_HEADS, nq),
        in_specs=[
            pl.BlockSpec(memory_space=pltpu.SMEM),
            pl.BlockSpec((1, 1, 1, MLA_QK_PAD, tq), lambda b, hd, i: (b, hd, 0, 0, i)),
            pl.BlockSpec((1, 1, seq, MLA_QK_PAD), lambda b, hd, i: (b, hd, 0, 0)),
            pl.BlockSpec((1, 1, nc, MLA_V, blk), lambda b, hd, i: (b, hd, 0, 0, 0)),
        ],
        out_specs=pl.BlockSpec((tq, MLA_V), lambda b, hd, i: (b * nq + i, hd)),
        scratch_shapes=_flash_scratch(q_tiles, MLA_V, blk),
        compiler_params=_params(("arbitrary", "arbitrary", "arbitrary")), name="mla_attn",
    )(bound, q, k, v)


def _diff_attn_kernel(bound_ref, q_ref, k_ref, v_ref, bias_ref, rb_ref, lam_ref, g_sub_ref, o_ref,
                      *scratch, blk, q_tiles, group, lam_init):
    hd = pl.program_id(0)
    i_first = pl.program_id(2) * q_tiles
    st = _FlashState(q_ref, k_ref, v_ref, scratch, blk, q_tiles)
    acc_ref, l_ref = st.acc_ref, st.l_ref
    n_groups = st.nc // group

    bias_max = jnp.abs(rb_ref[0, hd])
    for b in range(1, REL_BUCKETS):
        bias_max = jnp.maximum(bias_max, jnp.abs(rb_ref[b, hd]))

    far_before = rb_ref[REL_BUCKETS // 2 - 1, hd] * LOG2E
    far_after = rb_ref[REL_BUCKETS - 1, hd] * LOG2E
    first_near = (i_first - 1 + group) // group - 1
    last_near = (i_first + q_tiles) // group

    near_lo = jnp.maximum(first_near, 0)
    near_hi = jnp.minimum(last_near + 1, n_groups)

    def schedule(step):
        def far_group(cst):
            def body(cc, carry):
                for u in range(group):
                    step(group * cc + u, u % 2, cst=cst)
                return carry
            return body

        def near_group(cc, carry):
            for u in range(group):
                c = group * cc + u
                far = BIAS_TILES // 2
                tiles = [bias_ref[0, jnp.clip(c - (i_first + qt), -far, far) + far]
                         for qt in range(q_tiles)]
                step(c, u % 2, bias_tiles=tiles)
            return carry

        lax.fori_loop(0, near_lo, far_group(far_before), 0)
        lax.fori_loop(near_lo, near_hi, near_group, 0)
        lax.fori_loop(near_hi, n_groups, far_group(far_after), 0)

    st.run(bound_ref[0] + bias_max * LOG2E, schedule)

    lam_v = lam_ref[...]
    lam = (jnp.exp(jnp.sum(lam_v[0:1] * lam_v[1:2], axis=-1, keepdims=True))
           - jnp.exp(jnp.sum(lam_v[2:3] * lam_v[3:4], axis=-1, keepdims=True)) + lam_init)
    for qt in range(q_tiles):
        s0, s1 = 2 * qt, 2 * qt + 1
        o_t = acc_ref[s0] / l_ref[s0] - lam * (acc_ref[s1] / l_ref[s1])
        o_t = o_t * lax.rsqrt(jnp.mean(o_t * o_t, axis=0, keepdims=True) + EPS) * g_sub_ref[...]
        o_ref[qt * blk:(qt + 1) * blk, :] = (o_t * (1.0 - lam_init)).T.astype(BF16)


def _diff_attention(bound, q, k, v, bias, rel_bias, lam_rows, g_sub, gb, seq, blk, q_tiles, group,
                    lam_init):
    nc = seq // blk
    nq = nc // q_tiles
    assert group % 2 == 0 and nc % group == 0 and nc % q_tiles == 0 and blk >= REL_MAX_DIST
    kern = functools.partial(_diff_attn_kernel, blk=blk, q_tiles=q_tiles, group=group,
                             lam_init=lam_init)
    tq = q_tiles * blk
    return pl.pallas_call(
        kern,
        out_shape=jax.ShapeDtypeStruct((gb * seq, HEAD_OUT), BF16),
        grid=(DIFF_HEADS, gb, nq),
        in_specs=[
            pl.BlockSpec(memory_space=pltpu.SMEM),
            pl.BlockSpec((1, 1, 2, DIFF_V, tq), lambda hd, b, i: (b, hd, 0, 0, i)),
            pl.BlockSpec((1, 1, seq, DIFF_V), lambda hd, b, i: (b, hd, 0, 0)),
            pl.BlockSpec((1, 1, nc, DIFF_V, blk), lambda hd, b, i: (b, hd, 0, 0, 0)),
            pl.BlockSpec((1, BIAS_TILES, blk, blk), lambda hd, b, i: (hd, 0, 0, 0),
                         pipeline_mode=pl.Buffered(1)),
            pl.BlockSpec(memory_space=pltpu.SMEM),
            _const_spec((4, LANES)),
            _const_spec((DIFF_V, 1)),
        ],
        out_specs=pl.BlockSpec((tq, DIFF_V), lambda hd, b, i: (b * nq + i, hd)),
        scratch_shapes=_flash_scratch(2 * q_tiles, DIFF_V, blk),
        compiler_params=_params(("arbitrary", "arbitrary", "arbitrary")), name="diff_attn",
    )(bound, q, k, v, bias, rel_bias, lam_rows, g_sub)


def _merge_kernel(x_ref, g_mix_ref, oa_ref, ob_ref, wga_ref, wgb_ref, wua_ref, wub_ref,
                  out_ref, h_ref):
    @pl.when(pl.program_id(1) == 0)
    def _():
        h_ref[...] = _rms_rows(x_ref[...], g_mix_ref[...]).astype(BF16)

    h = h_ref[...]
    ga = jnp.dot(h, wga_ref[...], preferred_element_type=F32)
    gb_ = jnp.dot(h, wgb_ref[...], preferred_element_type=F32)
    ua = jnp.dot(oa_ref[...], wua_ref[...], preferred_element_type=F32)
    ub = jnp.dot(ob_ref[...], wub_ref[...], preferred_element_type=F32)
    out_ref[...] = (jax.nn.sigmoid(ga) * ua + jax.nn.sigmoid(gb_) * ub).astype(BF16)


def _merge(x2d, o_a, o_b, w, tm, tn):
    tokens = x2d.shape[0]
    return pl.pallas_call(
        _merge_kernel,
        out_shape=jax.ShapeDtypeStruct((tokens, D_MODEL), BF16),
        grid=(tokens // tm, D_MODEL // tn),
        in_specs=[
            pl.BlockSpec((tm, D_MODEL), lambda i, j: (i, 0)),
            _const_spec((1, D_MODEL)),
            pl.BlockSpec((tm, HEAD_OUT), lambda i, j: (i, 0)),
            pl.BlockSpec((tm, HEAD_OUT), lambda i, j: (i, 0)),
            pl.BlockSpec((D_MODEL, tn), lambda i, j: (0, j)),
            pl.BlockSpec((D_MODEL, tn), lambda i, j: (0, j)),
            pl.BlockSpec((HEAD_OUT, tn), lambda i, j: (0, j)),
            pl.BlockSpec((HEAD_OUT, tn), lambda i, j: (0, j)),
        ],
        out_specs=pl.BlockSpec((tm, tn), lambda i, j: (i, j)),
        scratch_shapes=[pltpu.VMEM((tm, D_MODEL), BF16)],
        compiler_params=_params(("arbitrary", "arbitrary")), name="gated_merge",
    )(x2d, w["g_mix"], o_a, o_b, w["w_ga"], w["w_gb"], w["w_ua"], w["w_ub"])


def _out_proj_kernel(x_ref, m_ref, wo_ref, out_ref):
    out_ref[...] = x_ref[...] + jnp.dot(m_ref[...], wo_ref[...], preferred_element_type=F32)


def _out_proj(x2d, merged, w, tm):
    tokens = x2d.shape[0]
    return pl.pallas_call(
        _out_proj_kernel,
        out_shape=jax.ShapeDtypeStruct((tokens, D_MODEL), F32),
        grid=(tokens // tm,),
        in_specs=[
            pl.BlockSpec((tm, D_MODEL), lambda i: (i, 0)),
            pl.BlockSpec((tm, D_MODEL), lambda i: (i, 0)),
            _const_spec((D_MODEL, D_MODEL)),
        ],
        out_specs=pl.BlockSpec((tm, D_MODEL), lambda i: (i, 0)),
        compiler_params=_params(("arbitrary",)), name="out_proj",
    )(x2d, merged, w["w_o"])


def _ffn_kernel(x_ref, g_ref, wg_ref, wu_ref, wd_ref, out_ref, h_ref):
    @pl.when(pl.program_id(1) == 0)
    def _():
        x = x_ref[...]
        h_ref[...] = _rms_rows(x, g_ref[...]).astype(BF16)
        out_ref[...] = x

    h = h_ref[...]
    g = jnp.dot(h, wg_ref[...], preferred_element_type=F32)
    u = jnp.dot(h, wu_ref[...], preferred_element_type=F32)
    a = (g * jax.nn.sigmoid(g) * u).astype(BF16)
    out_ref[...] += jnp.dot(a, wd_ref[...], preferred_element_type=F32)


def _ffn(x2d, w, tm, tf):
    tokens = x2d.shape[0]
    return pl.pallas_call(
        _ffn_kernel,
        out_shape=jax.ShapeDtypeStruct((tokens, D_MODEL), F32),
        grid=(tokens // tm, D_FF // tf),
        in_specs=[
            pl.BlockSpec((tm, D_MODEL), lambda i, j: (i, 0)),
            _const_spec((1, D_MODEL)),
            pl.BlockSpec((D_MODEL, tf), lambda i, j: (0, j)),
            pl.BlockSpec((D_MODEL, tf), lambda i, j: (0, j)),
            pl.BlockSpec((tf, D_MODEL), lambda i, j: (j, 0)),
        ],
        out_specs=pl.BlockSpec((tm, D_MODEL), lambda i, j: (i, 0)),
        scratch_shapes=[pltpu.VMEM((tm, D_MODEL), BF16)],
        compiler_params=_params(("arbitrary", "arbitrary")), name="swiglu_ffn",
    )(x2d, w["g_ffn"], w["w_gate"], w["w_up"], w["w_down"])


def _prepare_weights(mix_norm, w_in, q_a_norm, wq_b, kv_a_norm, wkv_b, mla_q_norm, mla_k_norm,
                     diff_q_norm, diff_k_norm, diff_subln, w_up_mla, w_up_diff, w_o, ffn_norm,
                     w_gate, w_up, w_down, layer):
    win = w_in[layer]
    o_cq, o_ckv, o_kpe = 0, Q_LORA, Q_LORA + KV_LORA
    o_dq = o_kpe + MLA_ROPE
    o_dk = o_dq + HEAD_OUT
    o_dv = o_dk + HEAD_OUT
    o_ga = o_dv + HEAD_OUT
    o_gb = o_ga + D_MODEL
    w_kpe = win[:, o_kpe:o_dq]
    w_kpe_rot = jnp.concatenate([-w_kpe[:, ROPE_HALF:], w_kpe[:, :ROPE_HALF]], axis=1)
    zpad = jnp.zeros((D_MODEL, LANES - MLA_ROPE), F32)
    w_a = jnp.concatenate([win[:, o_cq:o_kpe], w_kpe, zpad, w_kpe_rot, zpad], axis=1)
    wkv = wkv_b[layer].reshape(KV_LORA, MLA_HEADS, MLA_NOPE + MLA_V)
    gk = mla_k_norm[layer]
    gk_rope = gk[MLA_NOPE:]
    lane_pad = jnp.zeros((LANES - MLA_ROPE,), F32)
    return {
        "g_mix": mix_norm[layer][None, :],
        "w_a": w_a.astype(BF16),
        "g_qa": q_a_norm[layer][None, :],
        "g_kva": kv_a_norm[layer][None, :],
        "wq_t": wq_b[layer].T.astype(BF16),
        "g_q": mla_q_norm[layer][:, None],
        "wk_nope": wkv[:, :, :MLA_NOPE].reshape(KV_LORA, MLA_HEADS * MLA_NOPE).astype(BF16),
        "wv_t": wkv[:, :, MLA_NOPE:].reshape(KV_LORA, MLA_HEADS * MLA_V).T.astype(BF16),
        "g_kn": gk[None, :MLA_NOPE],
        "g_kr": jnp.concatenate([gk_rope, lane_pad])[None, :],
        "g_krot": jnp.concatenate([gk_rope[ROPE_HALF:], gk_rope[:ROPE_HALF], lane_pad])[None, :],
        "w_dq_t": win[:, o_dq:o_dk].T.astype(BF16),
        "w_dk_t": win[:, o_dk:o_dv].T.astype(BF16),
        "w_dv_t": win[:, o_dv:o_ga].T.astype(BF16),
        "g_dq": diff_q_norm[layer][:, None],
        "g_dk": diff_k_norm[layer][:, None],
        "g_sub": diff_subln[layer][:, None],
        "w_ga": win[:, o_ga:o_gb].astype(BF16),
        "w_gb": win[:, o_gb:].astype(BF16),
        "w_ua": w_up_mla[layer].astype(BF16),
        "w_ub": w_up_diff[layer].astype(BF16),
        "w_o": w_o[layer].astype(BF16),
        "mla_bound": _score_bound(MLA_QK, mla_q_norm[layer], gk),
        "diff_bound": _score_bound(DIFF_QK, diff_q_norm[layer], diff_k_norm[layer]),
        "g_ffn": ffn_norm[layer][None, :],
        "w_gate": w_gate[layer].astype(BF16),
        "w_up": w_up[layer].astype(BF16),
        "w_down": w_down[layer].astype(BF16),
    }


class _Tiles(NamedTuple):
    tm: int = 512
    tn: int = 512
    tf: int = 512
    blk: int = 512
    mla_q_tiles: int = 4
    diff_q_tiles: int = 4
    mla_group_max: int = 4
    diff_group_max: int = 2


def _encoder_layer(x, w, tables, bias, rel_bias, lam_rows, lam_init, t):
    gb, seq, _ = x.shape
    x2d = x.reshape(gb * seq, D_MODEL)
    mla_group = math.gcd(t.mla_group_max, seq // t.blk)
    diff_group = math.gcd(t.diff_group_max, seq // t.blk)
    q_a, k_a, v_a = _mla_prep(x2d, gb, seq, t.tm, t.blk, w, tables)
    q_d, k_d, v_d = _diff_prep(x2d, gb, seq, t.tm, t.blk, w)
    o_a = _mla_attention(w["mla_bound"], q_a, k_a, v_a, gb, seq, t.blk, t.mla_q_tiles, mla_group)
    o_b = _diff_attention(w["diff_bound"], q_d, k_d, v_d, bias, rel_bias, lam_rows, w["g_sub"], gb,
                          seq, t.blk, t.diff_q_tiles, diff_group, lam_init)
    merged = _merge(x2d, o_a, o_b, w, t.tm, t.tn)
    x1 = _out_proj(x2d, merged, w, t.tm)
    y = _ffn(x1, w, t.tm, t.tf)
    return y.reshape(gb, seq, D_MODEL)


def kernel(x_prompt, x_sample, mix_norm, w_in, q_a_norm, wq_b, kv_a_norm, wkv_b, mla_q_norm, mla_k_norm, diff_q_norm, diff_k_norm, lambda_q1, lambda_k1, lambda_q2, lambda_k2, diff_subln, w_up_mla, w_up_diff, w_o, ffn_norm, w_gate, w_up, w_down, rel_bias):
    t = _Tiles()
    depth = w_in.shape[0]
    max_seq = max(x_prompt.shape[1], x_sample.shape[1])
    tables = _rope_tables(max_seq)
    bias = _bias_tiles(rel_bias, t.blk)
    y_prompt, y_sample = x_prompt, x_sample
    for layer in range(depth):
        w = _prepare_weights(mix_norm, w_in, q_a_norm, wq_b, kv_a_norm, wkv_b, mla_q_norm,
                             mla_k_norm, diff_q_norm, diff_k_norm, diff_subln, w_up_mla,
                             w_up_diff, w_o, ffn_norm, w_gate, w_up, w_down, layer)
        lam_init = 0.8 - 0.6 * math.exp(-0.3 * layer)
        lam_pad = jnp.zeros((LANES - DIFF_QK,), F32)
        lam_rows = jnp.stack([jnp.concatenate([v[layer], lam_pad])
                              for v in (lambda_q1, lambda_k1, lambda_q2, lambda_k2)])
        run = functools.partial(_encoder_layer, w=w, tables=tables, bias=bias, rel_bias=rel_bias,
                                lam_rows=lam_rows, lam_init=lam_init, t=t)
        y_prompt = run(y_prompt)
        y_sample = run(y_sample)
    return (y_prompt, y_sample)
```

```python
import functools
import math
from typing import NamedTuple

import jax
import jax.numpy as jnp
from jax import lax
from jax.experimental import pallas as pl
from jax.experimental.pallas import tpu as pltpu

D_MODEL = 2048
MLA_HEADS = 8
MLA_NOPE = 128
MLA_ROPE = 64
MLA_QK = MLA_NOPE + MLA_ROPE
MLA_V = 128
Q_LORA = 512
KV_LORA = 256
ROPE_THETA = 10000.0
ROPE_HALF = MLA_ROPE // 2
DIFF_HEADS = 8
DIFF_QK = 64
DIFF_V = 2 * DIFF_QK
REL_BUCKETS = 32
REL_MAX_DIST = 128
D_FF = 5632
EPS = 1e-6
HEAD_OUT = MLA_HEADS * MLA_V

LANES = 128
MLA_QK_PAD = 2 * LANES
K_PE_COL = Q_LORA + KV_LORA
W_A_COLS = K_PE_COL + 2 * LANES
ROPE_TABLE_ROWS = 2048
VMEM_LIMIT = 56 * 1024 * 1024

F32 = jnp.float32
BF16 = jnp.bfloat16
NEG_BIG = -1e30
LOG2E = math.log2(math.e)
BOUNDED_SOFTMAX_LIMIT = 50.0
BIAS_TILES = 5

NT_DIMS = (((1,), (1,)), ((), ()))


def _params(semantics):
    return pltpu.CompilerParams(dimension_semantics=semantics, vmem_limit_bytes=VMEM_LIMIT)


def _const_spec(shape):
    zeros = (0,) * len(shape)
    return pl.BlockSpec(shape, lambda *_: zeros)


def _rms_rows(x, gain):
    return x * lax.rsqrt(jnp.mean(x * x, axis=-1, keepdims=True) + EPS) * gain


def _rope_table_kernel(inv_row_ref, inv_col_ref, cos_t_ref, sin_t_ref, cos_f_ref, sin_f_ref):
    ts = cos_t_ref.shape[0]
    base = pl.program_id(0) * ts
    pos_rows = (base + lax.broadcasted_iota(jnp.int32, (ts, LANES), 0)).astype(F32)
    ang_t = pos_rows * inv_row_ref[...]
    cos_t_ref[...] = jnp.cos(ang_t)
    sin_t_ref[...] = jnp.sin(ang_t)
    pos_cols = (base + lax.broadcasted_iota(jnp.int32, (ROPE_HALF, ts), 1)).astype(F32)
    ang_f = pos_cols * inv_col_ref[...]
    cos_f_ref[...] = jnp.cos(ang_f)
    sin_f_ref[...] = jnp.sin(ang_f)


def _rope_tables(seq):
    inv = ROPE_THETA ** (-jnp.arange(ROPE_HALF, dtype=F32) / ROPE_HALF)
    inv_row = jnp.tile(inv, LANES // ROPE_HALF)[None, :]
    inv_col = inv[:, None]
    ts = min(seq, ROPE_TABLE_ROWS)
    tok = jax.ShapeDtypeStruct((seq, LANES), F32)
    feat = jax.ShapeDtypeStruct((ROPE_HALF, seq), F32)
    return pl.pallas_call(
        _rope_table_kernel,
        out_shape=(tok, tok, feat, feat),
        grid=(seq // ts,),
        in_specs=[_const_spec((1, LANES)), _const_spec((ROPE_HALF, 1))],
        out_specs=(
            pl.BlockSpec((ts, LANES), lambda i: (i, 0)),
            pl.BlockSpec((ts, LANES), lambda i: (i, 0)),
            pl.BlockSpec((ROPE_HALF, ts), lambda i: (0, i)),
            pl.BlockSpec((ROPE_HALF, ts), lambda i: (0, i)),
        ),
        compiler_params=_params(("arbitrary",)), name="rope_tables",
    )(inv_row, inv_col)


def _mla_prep_kernel(x_ref, g_mix_ref, w_a_ref, g_qa_ref, g_kva_ref, wq_t_ref, g_q_ref,
                     wk_ref, wv_t_ref, g_kn_ref, g_kr_ref, g_krot_ref,
                     cos_t_ref, sin_t_ref, cos_f_ref, sin_f_ref,
                     q_ref, k_ref, v_ref, h_ref, *, tk):
    tm = x_ref.shape[0]
    h = _rms_rows(x_ref[...], g_mix_ref[...]).astype(BF16)
    h_ref[...] = h
    c = jnp.dot(h, w_a_ref[...], preferred_element_type=F32)
    cq = _rms_rows(c[:, :Q_LORA], g_qa_ref[...]).astype(BF16)
    ckv = _rms_rows(c[:, Q_LORA:Q_LORA + KV_LORA], g_kva_ref[...]).astype(BF16)
    k_pe = c[:, K_PE_COL:K_PE_COL + LANES]
    k_pe_rot = c[:, K_PE_COL + LANES:W_A_COLS]

    q_t = lax.dot_general(wq_t_ref[...], cq, NT_DIMS, preferred_element_type=F32)
    cos_f = cos_f_ref[...]
    sin_f = sin_f_ref[...]
    scale = MLA_QK ** -0.5 * LOG2E
    g_q = g_q_ref[...]
    for hd in range(MLA_HEADS):
        xh = q_t[hd * MLA_QK:(hd + 1) * MLA_QK]
        rinv = lax.rsqrt(jnp.mean(xh * xh, axis=0, keepdims=True) + EPS)
        xn = xh * rinv * g_q
        x1 = xn[MLA_NOPE:MLA_NOPE + ROPE_HALF]
        x2 = xn[MLA_NOPE + ROPE_HALF:]
        q_ref[0, hd, 0:MLA_NOPE, :] = (xn[:MLA_NOPE] * scale).astype(BF16)
        q_ref[0, hd, MLA_NOPE:MLA_NOPE + ROPE_HALF, :] = ((x1 * cos_f - x2 * sin_f) * scale).astype(BF16)
        q_ref[0, hd, MLA_NOPE + ROPE_HALF:MLA_QK, :] = ((x2 * cos_f + x1 * sin_f) * scale).astype(BF16)
        q_ref[0, hd, MLA_QK:, :] = jnp.zeros((MLA_QK_PAD - MLA_QK, tm), BF16)

    k_nope = jnp.dot(ckv, wk_ref[...], preferred_element_type=F32)
    rope_base = (k_pe * g_kr_ref[...]) * cos_t_ref[...] + (k_pe_rot * g_krot_ref[...]) * sin_t_ref[...]
    ss_pe = jnp.sum(k_pe * k_pe, axis=-1, keepdims=True)
    for hd in range(MLA_HEADS):
        kh = k_nope[:, hd * MLA_NOPE:(hd + 1) * MLA_NOPE]
        ss = jnp.sum(kh * kh, axis=-1, keepdims=True) + ss_pe
        rinv = lax.rsqrt(ss * (1.0 / MLA_QK) + EPS)
        k_ref[0, hd, :, 0:LANES] = (kh * rinv * g_kn_ref[...]).astype(BF16)
        k_ref[0, hd, :, LANES:] = (rope_base * rinv).astype(BF16)

    v_t = lax.dot_general(wv_t_ref[...], ckv, NT_DIMS, preferred_element_type=F32)
    for hd in range(MLA_HEADS):
        for cc in range(tm // tk):
            v_ref[0, hd, cc] = v_t[hd * MLA_V:(hd + 1) * MLA_V, cc * tk:(cc + 1) * tk].astype(BF16)


def _mla_prep(x2d, gb, seq, tm, tk, w, tables):
    tokens = x2d.shape[0]
    nt = seq // tm
    cos_t, sin_t, cos_f, sin_f = tables
    kern = functools.partial(_mla_prep_kernel, tk=tk)
    out_shape = (
        jax.ShapeDtypeStruct((gb, MLA_HEADS, MLA_QK_PAD, seq), BF16),
        jax.ShapeDtypeStruct((gb, MLA_HEADS, seq, MLA_QK_PAD), BF16),
        jax.ShapeDtypeStruct((gb, MLA_HEADS, seq // tk, MLA_V, tk), BF16),
        jax.ShapeDtypeStruct((tokens, D_MODEL), BF16),
    )
    in_specs = [
        pl.BlockSpec((tm, D_MODEL), lambda i: (i, 0)),
        _const_spec((1, D_MODEL)),
        _const_spec((D_MODEL, W_A_COLS)),
        _const_spec((1, Q_LORA)),
        _const_spec((1, KV_LORA)),
        _const_spec((MLA_HEADS * MLA_QK, Q_LORA)),
        _const_spec((MLA_QK, 1)),
        _const_spec((KV_LORA, MLA_HEADS * MLA_NOPE)),
        _const_spec((MLA_HEADS * MLA_V, KV_LORA)),
        _const_spec((1, LANES)),
        _const_spec((1, LANES)),
        _const_spec((1, LANES)),
        pl.BlockSpec((tm, LANES), lambda i: (i % nt, 0)),
        pl.BlockSpec((tm, LANES), lambda i: (i % nt, 0)),
        pl.BlockSpec((ROPE_HALF, tm), lambda i: (0, i % nt)),
        pl.BlockSpec((ROPE_HALF, tm), lambda i: (0, i % nt)),
    ]
    out_specs = (
        pl.BlockSpec((1, MLA_HEADS, MLA_QK_PAD, tm), lambda i: (i // nt, 0, 0, i % nt)),
        pl.BlockSpec((1, MLA_HEADS, tm, MLA_QK_PAD), lambda i: (i // nt, 0, i % nt, 0)),
        pl.BlockSpec((1, MLA_HEADS, tm // tk, MLA_V, tk), lambda i: (i // nt, 0, i % nt, 0, 0)),
        pl.BlockSpec((tm, D_MODEL), lambda i: (i, 0)),
    )
    return pl.pallas_call(
        kern, out_shape=out_shape, grid=(tokens // tm,), in_specs=in_specs, out_specs=out_specs,
        compiler_params=_params(("arbitrary",)), name="mla_prep",
    )(x2d, w["g_mix"], w["w_a"], w["g_qa"], w["g_kva"], w["wq_t"], w["g_q"], w["wk_nope"],
      w["wv_t"], w["g_kn"], w["g_kr"], w["g_krot"], cos_t, sin_t, cos_f, sin_f)


def _group_norm_cols(x_t, gain_col):
    rows, tm = x_t.shape
    x3 = x_t.reshape(rows // DIFF_QK, DIFF_QK, tm)
    rinv = lax.rsqrt(jnp.mean(x3 * x3, axis=1, keepdims=True) + EPS)
    return (x3 * rinv * gain_col[None]).reshape(rows, tm)


def _diff_prep_kernel(h_ref, wq_t_ref, wk_t_ref, wv_t_ref, g_q_ref, g_k_ref,
                      q_ref, k_ref, v_ref, *, tk):
    tm = h_ref.shape[0]
    h = h_ref[...]
    scale = DIFF_QK ** -0.5 * LOG2E
    q_t = lax.dot_general(wq_t_ref[...], h, NT_DIMS, preferred_element_type=F32)
    qn = (_group_norm_cols(q_t, g_q_ref[...]) * scale).astype(BF16)
    zeros = jnp.zeros((DIFF_QK, tm), BF16)
    for hd in range(DIFF_HEADS):
        r0 = hd * DIFF_V
        q_ref[0, hd, 0, 0:DIFF_QK, :] = qn[r0:r0 + DIFF_QK]
        q_ref[0, hd, 0, DIFF_QK:, :] = zeros
        q_ref[0, hd, 1, 0:DIFF_QK, :] = zeros
        q_ref[0, hd, 1, DIFF_QK:, :] = qn[r0 + DIFF_QK:r0 + DIFF_V]

    k_t = lax.dot_general(wk_t_ref[...], h, NT_DIMS, preferred_element_type=F32)
    kn = _group_norm_cols(k_t, g_k_ref[...]).T
    for hd in range(DIFF_HEADS):
        k_ref[0, hd] = kn[:, hd * DIFF_V:(hd + 1) * DIFF_V].astype(BF16)

    v_t = lax.dot_general(wv_t_ref[...], h, NT_DIMS, preferred_element_type=F32)
    for hd in range(DIFF_HEADS):
        for cc in range(tm // tk):
            v_ref[0, hd, cc] = v_t[hd * DIFF_V:(hd + 1) * DIFF_V, cc * tk:(cc + 1) * tk].astype(BF16)


def _diff_prep(h2d, gb, seq, tm, tk, w):
    tokens = h2d.shape[0]
    nt = seq // tm
    kern = functools.partial(_diff_prep_kernel, tk=tk)
    out_shape = (
        jax.ShapeDtypeStruct((gb, DIFF_HEADS, 2, DIFF_V, seq), BF16),
        jax.ShapeDtypeStruct((gb, DIFF_HEADS, seq, DIFF_V), BF16),
        jax.ShapeDtypeStruct((gb, DIFF_HEADS, seq // tk, DIFF_V, tk), BF16),
    )
    in_specs = [
        pl.BlockSpec((tm, D_MODEL), lambda i: (i, 0)),
        _const_spec((HEAD_OUT, D_MODEL)),
        _const_spec((HEAD_OUT, D_MODEL)),
        _const_spec((HEAD_OUT, D_MODEL)),
        _const_spec((DIFF_QK, 1)),
        _const_spec((DIFF_QK, 1)),
    ]
    out_specs = (
        pl.BlockSpec((1, DIFF_HEADS, 2, DIFF_V, tm), lambda i: (i // nt, 0, 0, 0, i % nt)),
        pl.BlockSpec((1, DIFF_HEADS, tm, DIFF_V), lambda i: (i // nt, 0, i % nt, 0)),
        pl.BlockSpec((1, DIFF_HEADS, tm // tk, DIFF_V, tk), lambda i: (i // nt, 0, i % nt, 0, 0)),
    )
    return pl.pallas_call(
        kern, out_shape=out_shape, grid=(tokens // tm,), in_specs=in_specs, out_specs=out_specs,
        compiler_params=_params(("arbitrary",)), name="diff_prep",
    )(h2d, w["w_dq_t"], w["w_dk_t"], w["w_dv_t"], w["g_dq"], w["g_dk"])


def _bias_tile_kernel(bucket_ref, rb_ref, out_ref):
    hd = pl.program_id(0)
    offset = pl.program_id(1) - pl.num_programs(1) // 2

    @pl.when(offset <= -2)
    def _():
        out_ref[0, 0] = jnp.full(out_ref.shape[2:], rb_ref[REL_BUCKETS // 2 - 1, hd] * LOG2E, F32)

    @pl.when(offset >= 2)
    def _():
        out_ref[0, 0] = jnp.full(out_ref.shape[2:], rb_ref[REL_BUCKETS - 1, hd] * LOG2E, F32)

    @pl.when(jnp.abs(offset) < 2)
    def _():
        bucket = bucket_ref[0]
        acc = jnp.zeros(bucket.shape, F32)
        for b in range(REL_BUCKETS):
            acc = jnp.where(bucket == b, rb_ref[b, hd], acc)
        out_ref[0, 0] = acc * LOG2E


def _t5_bucket(rel):
    nb = REL_BUCKETS // 2
    ret = jnp.where(rel > 0, nb, 0)
    n = jnp.abs(rel)
    max_exact = nb // 2
    nf = jnp.maximum(n, 1).astype(F32)
    large = max_exact + (jnp.log(nf / max_exact) / math.log(REL_MAX_DIST / max_exact)
                         * (nb - max_exact)).astype(jnp.int32)
    large = jnp.minimum(large, nb - 1)
    return ret + jnp.where(n < max_exact, n, large)


def _bias_tiles(rel_bias, blk):
    assert blk >= REL_MAX_DIST
    n = BIAS_TILES
    d = jnp.arange(blk, dtype=jnp.int32)
    offs = (jnp.arange(n, dtype=jnp.int32) - n // 2) * blk
    rel = offs[:, None, None] + d[None, :, None] - d[None, None, :]
    bucket = _t5_bucket(rel)
    return pl.pallas_call(
        _bias_tile_kernel,
        out_shape=jax.ShapeDtypeStruct((DIFF_HEADS, n, blk, blk), F32),
        grid=(DIFF_HEADS, n),
        in_specs=[
            pl.BlockSpec((1, blk, blk), lambda hd, t: (t, 0, 0)),
            pl.BlockSpec(memory_space=pltpu.SMEM),
        ],
        out_specs=pl.BlockSpec((1, 1, blk, blk), lambda hd, t: (hd, t, 0, 0)),
        compiler_params=_params(("arbitrary", "arbitrary")), name="bias_tiles",
    )(bucket, rel_bias)


class _FlashState:
    def __init__(self, q_ref, k_ref, v_ref, scratch, blk, q_tiles):
        self.q_ref, self.k_ref, self.v_ref = q_ref, k_ref, v_ref
        self.p_ref, self.alpha_ref, self.m_ref, self.l_ref, self.acc_ref = scratch
        self.blk = blk
        self.bound = None
        self.streams = [(qt, mp) for qt in range(q_tiles) for mp in range(q_ref.shape[2])]
        self.nc = v_ref.shape[2]

    def q(self, qt, mp):
        return self.q_ref[0, 0, mp, :, qt * self.blk:(qt + 1) * self.blk]

    def key_chunk(self, c):
        return self.k_ref[0, 0, pl.ds(pl.multiple_of(c * self.blk, self.blk), self.blk), :]

    def init(self, online):
        self.l_ref[...] = jnp.zeros(self.l_ref.shape, F32)
        self.acc_ref[...] = jnp.zeros(self.acc_ref.shape, F32)
        self.p_ref[1] = jnp.zeros(self.p_ref.shape[1:], BF16)
        if online:
            self.m_ref[...] = jnp.full(self.m_ref.shape, NEG_BIG, F32)
            self.alpha_ref[...] = jnp.ones(self.alpha_ref.shape, F32)

    def pv(self, online, c, half):
        vc = self.v_ref[0, 0, c]
        for sx in range(len(self.streams)):
            update = jnp.dot(vc, self.p_ref[half, sx], preferred_element_type=F32)
            if online:
                self.acc_ref[sx] = self.alpha_ref[sx] * self.acc_ref[sx] + update
            else:
                self.acc_ref[sx] += update

    def step(self, online, c, half, cst=None, bias_tiles=None):
        self.pv(online, jnp.maximum(c - 1, 0), 1 - half)
        kc = self.key_chunk(c)
        for sx, (qt, mp) in enumerate(self.streams):
            s = jnp.dot(kc, self.q(qt, mp), preferred_element_type=F32)
            if bias_tiles is not None:
                s = s + bias_tiles[qt]
            l_old = self.l_ref[sx]
            if online:
                mc = jnp.max(s, axis=0, keepdims=True)
                if cst is not None:
                    mc = mc + cst
                m_old = self.m_ref[sx]
                shift = jnp.maximum(m_old, mc)
                alpha = jnp.exp2(m_old - shift)
                l_old = alpha * l_old
                self.alpha_ref[sx] = alpha
                self.m_ref[sx] = shift
            else:
                shift = self.bound
            if cst is not None:
                shift = shift - cst
            p = jnp.exp2(s - shift)
            self.l_ref[sx] = l_old + jnp.sum(p, axis=0, keepdims=True)
            self.p_ref[half, sx] = p.astype(BF16)

    def run(self, bound, schedule):
        self.bound = bound
        use_bound = bound <= BOUNDED_SOFTMAX_LIMIT
        for online, pred in ((False, use_bound), (True, jnp.logical_not(use_bound))):
            @pl.when(pred)
            def _():
                self.init(online)
                schedule(functools.partial(self.step, online))
                self.pv(online, self.nc - 1, 1)


def _flash_scratch(n_streams, head_dim, blk):
    return [
        pltpu.VMEM((2, n_streams, blk, blk), BF16),
        pltpu.VMEM((n_streams, 1, blk), F32),
        pltpu.VMEM((n_streams, 1, blk), F32),
        pltpu.VMEM((n_streams, 1, blk), F32),
        pltpu.VMEM((n_streams, head_dim, blk), F32),
    ]


def _score_bound(head_dim, g_q, g_k):
    return (head_dim ** 0.5 * LOG2E * 1.02 * jnp.max(jnp.abs(g_q)) * jnp.max(jnp.abs(g_k))).reshape(1)


def _mla_attn_kernel(bound_ref, q_ref, k_ref, v_ref, o_ref, *scratch, blk, q_tiles, group):
    st = _FlashState(q_ref, k_ref, v_ref, scratch, blk, q_tiles)

    def schedule(step):
        def body(cc, carry):
            for u in range(group):
                step(group * cc + u, u % 2)
            return carry

        lax.fori_loop(0, st.nc // group, body, 0)

    st.run(bound_ref[0], schedule)
    for sx, (qt, _) in enumerate(st.streams):
        o_t = st.acc_ref[sx] / st.l_ref[sx]
        o_ref[qt * blk:(qt + 1) * blk, :] = o_t.T.astype(BF16)


def _mla_attention(bound, q, k, v, gb, seq, blk, q_tiles, group):
    nc = seq // blk
    nq = nc // q_tiles
    assert group % 2 == 0 and nc % group == 0 and nc % q_tiles == 0
    kern = functools.partial(_mla_attn_kernel, blk=blk, q_tiles=q_tiles, group=group)
    q = q.reshape(gb, MLA_HEADS, 1, MLA_QK_PAD, seq)
    tq = q_tiles * blk
    return pl.pallas_call(
        kern,
        out_shape=jax.ShapeDtypeStruct((gb * seq, HEAD_OUT), BF16),
        grid=(gb, MLA_HEADS, nq),
        in_specs=[
            pl.BlockSpec(memory_space=pltpu.SMEM),
            pl.BlockSpec((1, 1, 1, MLA_QK_PAD, tq), lambda b, hd, i: (b, hd, 0, 0, i)),
            pl.BlockSpec((1, 1, seq, MLA_QK_PAD), lambda b, hd, i: (b, hd, 0, 0)),
            pl.BlockSpec((1, 1, nc, MLA_V, blk), lambda b, hd, i: (b, hd, 0, 0, 0)),
        ],
        out_specs=pl.BlockSpec((tq, MLA_V), lambda b, hd, i: (b * nq + i, hd)),
        scratch_shapes=_flash_scratch(q_tiles, MLA_V, blk),
        compiler_params=_params(("arbitrary", "arbitrary", "arbitrary")), name="mla_attn",
    )(bound, q, k, v)


def _diff_attn_kernel(bound_ref, q_ref, k_ref, v_ref, bias_ref, rb_ref, lam_ref, g_sub_ref, o_ref,
                      *scratch, blk, q_tiles, group, lam_init):
    hd = pl.program_id(0)
    i_first = pl.program_id(2) * q_tiles
    st = _FlashState(q_ref, k_ref, v_ref, scratch, blk, q_tiles)
    acc_ref, l_ref = st.acc_ref, st.l_ref
    n_groups = st.nc // group

    bias_max = jnp.abs(rb_ref[0, hd])
    for b in range(1, REL_BUCKETS):
        bias_max = jnp.maximum(bias_max, jnp.abs(rb_ref[b, hd]))

    far_before = rb_ref[REL_BUCKETS // 2 - 1, hd] * LOG2E
    far_after = rb_ref[REL_BUCKETS - 1, hd] * LOG2E
    first_near = (i_first - 1 + group) // group - 1
    last_near = (i_first + q_tiles) // group

    near_lo = jnp.maximum(first_near, 0)
    near_hi = jnp.minimum(last_near + 1, n_groups)

    def schedule(step):
        def far_group(cst):
            def body(cc, carry):
                for u in range(group):
                    step(group * cc + u, u % 2, cst=cst)
                return carry
            return body

        def near_group(cc, carry):
            for u in range(group):
                c = group * cc + u
                far = BIAS_TILES // 2
                tiles = [bias_ref[0, jnp.clip(c - (i_first + qt), -far, far) + far]
                         for qt in range(q_tiles)]
                step(c, u % 2, bias_tiles=tiles)
            return carry

        lax.fori_loop(0, near_lo, far_group(far_before), 0)
        lax.fori_loop(near_lo, near_hi, near_group, 0)
        lax.fori_loop(near_hi, n_groups, far_group(far_after), 0)

    st.run(bound_ref[0] + bias_max * LOG2E, schedule)

    lam_v = lam_ref[...]
    lam = (jnp.exp(jnp.sum(lam_v[0:1] * lam_v[1:2], axis=-1, keepdims=True))
           - jnp.exp(jnp.sum(lam_v[2:3] * lam_v[3:4], axis=-1, keepdims=True)) + lam_init)
    for qt in range(q_tiles):
        s0, s1 = 2 * qt, 2 * qt + 1
        o_t = acc_ref[s0] / l_ref[s0] - lam * (acc_ref[s1] / l_ref[s1])
        o_t = o_t * lax.rsqrt(jnp.mean(o_t * o_t, axis=0, keepdims=True) + EPS) * g_sub_ref[...]
        o_ref[qt * blk:(qt + 1) * blk, :] = (o_t * (1.0 - lam_init)).T.astype(BF16)


def _diff_attention(bound, q, k, v, bias, rel_bias, lam_rows, g_sub, gb, seq, blk, q_tiles, group,
                    lam_init):
    nc = seq // blk
    nq = nc // q_tiles
    assert group % 2 == 0 and nc % group == 0 and nc % q_tiles == 0 and blk >= REL_MAX_DIST
    kern = functools.partial(_diff_attn_kernel, blk=blk, q_tiles=q_tiles, group=group,
                             lam_init=lam_init)
    tq = q_tiles * blk
    return pl.pallas_call(
        kern,
        out_shape=jax.ShapeDtypeStruct((gb * seq, HEAD_OUT), BF16),
        grid=(DIFF_HEADS, gb, nq),
        in_specs=[
            pl.BlockSpec(memory_space=pltpu.SMEM),
            pl.BlockSpec((1, 1, 2, DIFF_V, tq), lambda hd, b, i: (b, hd, 0, 0, i)),
            pl.BlockSpec((1, 1, seq, DIFF_V), lambda hd, b, i: (b, hd, 0, 0)),
            pl.BlockSpec((1, 1, nc, DIFF_V, blk), lambda hd, b, i: (b, hd, 0, 0, 0)),
            pl.BlockSpec((1, BIAS_TILES, blk, blk), lambda hd, b, i: (hd, 0, 0, 0),
                         pipeline_mode=pl.Buffered(1)),
            pl.BlockSpec(memory_space=pltpu.SMEM),
            _const_spec((4, LANES)),
            _const_spec((DIFF_V, 1)),
        ],
        out_specs=pl.BlockSpec((tq, DIFF_V), lambda hd, b, i: (b * nq + i, hd)),
        scratch_shapes=_flash_scratch(2 * q_tiles, DIFF_V, blk),
        compiler_params=_params(("arbitrary", "arbitrary", "arbitrary")), name="diff_attn",
    )(bound, q, k, v, bias, rel_bias, lam_rows, g_sub)


def _merge_kernel(h_ref, oa_ref, ob_ref, wga_ref, wgb_ref, wua_ref, wub_ref, out_ref):
    h = h_ref[...]
    ga = jnp.dot(h, wga_ref[...], preferred_element_type=F32)
    gb_ = jnp.dot(h, wgb_ref[...], preferred_element_type=F32)
    ua = jnp.dot(oa_ref[...], wua_ref[...], preferred_element_type=F32)
    ub = jnp.dot(ob_ref[...], wub_ref[...], preferred_element_type=F32)
    out_ref[...] = (jax.nn.sigmoid(ga) * ua + jax.nn.sigmoid(gb_) * ub).astype(BF16)


def _merge(h2d, o_a, o_b, w, tm, tn):
    tokens = h2d.shape[0]
    return pl.pallas_call(
        _merge_kernel,
        out_shape=jax.ShapeDtypeStruct((tokens, D_MODEL), BF16),
        grid=(tokens // tm, D_MODEL // tn),
        in_specs=[
            pl.BlockSpec((tm, D_MODEL), lambda i, j: (i, 0)),
            pl.BlockSpec((tm, HEAD_OUT), lambda i, j: (i, 0)),
            pl.BlockSpec((tm, HEAD_OUT), lambda i, j: (i, 0)),
            pl.BlockSpec((D_MODEL, tn), lambda i, j: (0, j)),
            pl.BlockSpec((D_MODEL, tn), lambda i, j: (0, j)),
            pl.BlockSpec((HEAD_OUT, tn), lambda i, j: (0, j)),
            pl.BlockSpec((HEAD_OUT, tn), lambda i, j: (0, j)),
        ],
        out_specs=pl.BlockSpec((tm, tn), lambda i, j: (i, j)),
        compiler_params=_params(("arbitrary", "arbitrary")), name="gated_merge",
    )(h2d, o_a, o_b, w["w_ga"], w["w_gb"], w["w_ua"], w["w_ub"])


def _out_proj_kernel(x_ref, m_ref, wo_ref, out_ref):
    out_ref[...] = x_ref[...] + jnp.dot(m_ref[...], wo_ref[...], preferred_element_type=F32)


def _out_proj(x2d, merged, w, tm):
    tokens = x2d.shape[0]
    return pl.pallas_call(
        _out_proj_kernel,
        out_shape=jax.ShapeDtypeStruct((tokens, D_MODEL), F32),
        grid=(tokens // tm,),
        in_specs=[
            pl.BlockSpec((tm, D_MODEL), lambda i: (i, 0)),
            pl.BlockSpec((tm, D_MODEL), lambda i: (i, 0)),
            _const_spec((D_MODEL, D_MODEL)),
        ],
        out_specs=pl.BlockSpec((tm, D_MODEL), lambda i: (i, 0)),
        compiler_params=_params(("arbitrary",)), name="out_proj",
    )(x2d, merged, w["w_o"])


def _ffn_kernel(x_ref, g_ref, wg_ref, wu_ref, wd_ref, out_ref, h_ref):
    @pl.when(pl.program_id(1) == 0)
    def _():
        x = x_ref[...]
        h_ref[...] = _rms_rows(x, g_ref[...]).astype(BF16)
        out_ref[...] = x

    h = h_ref[...]
    g = jnp.dot(h, wg_ref[...], preferred_element_type=F32)
    u = jnp.dot(h, wu_ref[...], preferred_element_type=F32)
    a = (g * jax.nn.sigmoid(g) * u).astype(BF16)
    out_ref[...] += jnp.dot(a, wd_ref[...], preferred_element_type=F32)


def _ffn(x2d, w, tm, tf):
    tokens = x2d.shape[0]
    return pl.pallas_call(
        _ffn_kernel,
        out_shape=jax.ShapeDtypeStruct((tokens, D_MODEL), F32),
        grid=(tokens // tm, D_FF // tf),
        in_specs=[
            pl.BlockSpec((tm, D_MODEL), lambda i, j: (i, 0)),
            _const_spec((1, D_MODEL)),
            pl.BlockSpec((D_MODEL, tf), lambda i, j: (0, j)),
            pl.BlockSpec((D_MODEL, tf), lambda i, j: (0, j)),
            pl.BlockSpec((tf, D_MODEL), lambda i, j: (j, 0)),
        ],
        out_specs=pl.BlockSpec((tm, D_MODEL), lambda i, j: (i, 0)),
        scratch_shapes=[pltpu.VMEM((tm, D_MODEL), BF16)],
        compiler_params=_params(("arbitrary", "arbitrary")), name="swiglu_ffn",
    )(x2d, w["g_ffn"], w["w_gate"], w["w_up"], w["w_down"])


def _prepare_weights(mix_norm, w_in, q_a_norm, wq_b, kv_a_norm, wkv_b, mla_q_norm, mla_k_norm,
                     diff_q_norm, diff_k_norm, diff_subln, w_up_mla, w_up_diff, w_o, ffn_norm,
                     w_gate, w_up, w_down, layer):
    win = w_in[layer]
    o_cq, o_ckv, o_kpe = 0, Q_LORA, Q_LORA + KV_LORA
    o_dq = o_kpe + MLA_ROPE
    o_dk = o_dq + HEAD_OUT
    o_dv = o_dk + HEAD_OUT
    o_ga = o_dv + HEAD_OUT
    o_gb = o_ga + D_MODEL
    w_kpe = win[:, o_kpe:o_dq]
    w_kpe_rot = jnp.concatenate([-w_kpe[:, ROPE_HALF:], w_kpe[:, :ROPE_HALF]], axis=1)
    zpad = jnp.zeros((D_MODEL, LANES - MLA_ROPE), F32)
    w_a = jnp.concatenate([win[:, o_cq:o_kpe], w_kpe, zpad, w_kpe_rot, zpad], axis=1)
    wkv = wkv_b[layer].reshape(KV_LORA, MLA_HEADS, MLA_NOPE + MLA_V)
    gk = mla_k_norm[layer]
    gk_rope = gk[MLA_NOPE:]
    lane_pad = jnp.zeros((LANES - MLA_ROPE,), F32)
    return {
        "g_mix": mix_norm[layer][None, :],
        "w_a": w_a.astype(BF16),
        "g_qa": q_a_norm[layer][None, :],
        "g_kva": kv_a_norm[layer][None, :],
        "wq_t": wq_b[layer].T.astype(BF16),
        "g_q": mla_q_norm[layer][:, None],
        "wk_nope": wkv[:, :, :MLA_NOPE].reshape(KV_LORA, MLA_HEADS * MLA_NOPE).astype(BF16),
        "wv_t": wkv[:, :, MLA_NOPE:].reshape(KV_LORA, MLA_HEADS * MLA_V).T.astype(BF16),
        "g_kn": gk[None, :MLA_NOPE],
        "g_kr": jnp.concatenate([gk_rope, lane_pad])[None, :],
        "g_krot": jnp.concatenate([gk_rope[ROPE_HALF:], gk_rope[:ROPE_HALF], lane_pad])[None, :],
        "w_dq_t": win[:, o_dq:o_dk].T.astype(BF16),
        "w_dk_t": win[:, o_dk:o_dv].T.astype(BF16),
        "w_dv_t": win[:, o_dv:o_ga].T.astype(BF16),
        "g_dq": diff_q_norm[layer][:, None],
        "g_dk": diff_k_norm[layer][:, None],
        "g_sub": diff_subln[layer][:, None],
        "w_ga": win[:, o_ga:o_gb].astype(BF16),
        "w_gb": win[:, o_gb:].astype(BF16),
        "w_ua": w_up_mla[layer].astype(BF16),
        "w_ub": w_up_diff[layer].astype(BF16),
        "w_o": w_o[layer].astype(BF16),
        "mla_bound": _score_bound(MLA_QK, mla_q_norm[layer], gk),
        "diff_bound": _score_bound(DIFF_QK, diff_q_norm[layer], diff_k_norm[layer]),
        "g_ffn": ffn_norm[layer][None, :],
        "w_gate": w_gate[layer].astype(BF16),
        "w_up": w_up[layer].astype(BF16),
        "w_down": w_down[layer].astype(BF16),
    }


class _Tiles(NamedTuple):
    tm: int = 512
    tn: int = 1024
    tf: int = 512
    blk: int = 512
    mla_q_tiles: int = 4
    diff_q_tiles: int = 4
    mla_group_max: int = 4
    diff_group_max: int = 2


def _encoder_layer(x, w, tables, bias, rel_bias, lam_rows, lam_init, t):
    gb, seq, _ = x.shape
    x2d = x.reshape(gb * seq, D_MODEL)
    mla_group = math.gcd(t.mla_group_max, seq // t.blk)
    diff_group = math.gcd(t.diff_group_max, seq // t.blk)
    q_a, k_a, v_a, h2d = _mla_prep(x2d, gb, seq, t.tm, t.blk, w, tables)
    q_d, k_d, v_d = _diff_prep(h2d, gb, seq, t.tm, t.blk, w)
    o_a = _mla_attention(w["mla_bound"], q_a, k_a, v_a, gb, seq, t.blk, t.mla_q_tiles, mla_group)
    o_b = _diff_attention(w["diff_bound"], q_d, k_d, v_d, bias, rel_bias, lam_rows, w["g_sub"], gb,
                          seq, t.blk, t.diff_q_tiles, diff_group, lam_init)
    merged = _merge(h2d, o_a, o_b, w, t.tm, t.tn)
    x1 = _out_proj(x2d, merged, w, t.tm)
    y = _ffn(x1, w, t.tm, t.tf)
    return y.reshape(gb, seq, D_MODEL)


def kernel(x_prompt, x_sample, mix_norm, w_in, q_a_norm, wq_b, kv_a_norm, wkv_b, mla_q_norm, mla_k_norm, diff_q_norm, diff_k_norm, lambda_q1, lambda_k1, lambda_q2, lambda_k2, diff_subln, w_up_mla, w_up_diff, w_o, ffn_norm, w_gate, w_up, w_down, rel_bias):
    t = _Tiles()
    depth = w_in.shape[0]
    max_seq = max(x_prompt.shape[1], x_sample.shape[1])
    tables = _rope_tables(max_seq)
    bias = _bias_tiles(rel_bias, t.blk)
    y_prompt, y_sample = x_prompt, x_sample
    for layer in range(depth):
        w = _prepare_weights(mix_norm, w_in, q_a_norm, wq_b, kv_a_norm, wkv_b, mla_q_norm,
                             mla_k_norm, diff_q_norm, diff_k_norm, diff_subln, w_up_mla,
                             w_up_diff, w_o, ffn_norm, w_gate, w_up, w_down, layer)
        lam_init = 0.8 - 0.6 * math.exp(-0.3 * layer)
        lam_pad = jnp.zeros((LANES - DIFF_QK,), F32)
        lam_rows = jnp.stack([jnp.concatenate([v[layer], lam_pad])
                              for v in (lambda_q1, lambda_k1, lambda_q2, lambda_k2)])
        run = functools.partial(_encoder_layer, w=w, tables=tables, bias=bias, rel_bias=rel_bias,
                                lam_rows=lam_rows, lam_init=lam_init, t=t)
        y_prompt = run(y_prompt)
        y_sample = run(y_sample)
    return (y_prompt, y_sample)
```

```python
import functools
import math
from typing import NamedTuple

import jax
import jax.numpy as jnp
from jax import lax
from jax.experimental import pallas as pl
from jax.experimental.pallas import tpu as pltpu

D_MODEL = 2048
MLA_HEADS = 8
MLA_NOPE = 128
MLA_ROPE = 64
MLA_QK = MLA_NOPE + MLA_ROPE
MLA_V = 128
Q_LORA = 512
KV_LORA = 256
ROPE_THETA = 10000.0
ROPE_HALF = MLA_ROPE // 2
DIFF_HEADS = 8
DIFF_QK = 64
DIFF_V = 2 * DIFF_QK
REL_BUCKETS = 32
REL_MAX_DIST = 128
D_FF = 5632
EPS = 1e-6
HEAD_OUT = MLA_HEADS * MLA_V

LANES = 128
MLA_QK_PAD = 2 * LANES
K_PE_COL = Q_LORA + KV_LORA
W_A_COLS = K_PE_COL + 2 * LANES
ROPE_TABLE_ROWS = 2048
VMEM_LIMIT = 56 * 1024 * 1024

F32 = jnp.float32
BF16 = jnp.bfloat16
NEG_BIG = -1e30
LOG2E = math.log2(math.e)
BOUNDED_SOFTMAX_LIMIT = 50.0
BIAS_TILES = 5

NT_DIMS = (((1,), (1,)), ((), ()))


def _params(semantics):
    return pltpu.CompilerParams(dimension_semantics=semantics, vmem_limit_bytes=VMEM_LIMIT)


def _const_spec(shape):
    zeros = (0,) * len(shape)
    return pl.BlockSpec(shape, lambda *_: zeros)


def _rms_rows(x, gain):
    return x * lax.rsqrt(jnp.mean(x * x, axis=-1, keepdims=True) + EPS) * gain


def _rope_table_kernel(inv_row_ref, inv_col_ref, cos_t_ref, sin_t_ref, cos_f_ref, sin_f_ref):
    ts = cos_t_ref.shape[0]
    base = pl.program_id(0) * ts
    pos_rows = (base + lax.broadcasted_iota(jnp.int32, (ts, LANES), 0)).astype(F32)
    ang_t = pos_rows * inv_row_ref[...]
    cos_t_ref[...] = jnp.cos(ang_t)
    sin_t_ref[...] = jnp.sin(ang_t)
    pos_cols = (base + lax.broadcasted_iota(jnp.int32, (ROPE_HALF, ts), 1)).astype(F32)
    ang_f = pos_cols * inv_col_ref[...]
    cos_f_ref[...] = jnp.cos(ang_f)
    sin_f_ref[...] = jnp.sin(ang_f)


def _rope_tables(seq):
    inv = ROPE_THETA ** (-jnp.arange(ROPE_HALF, dtype=F32) / ROPE_HALF)
    inv_row = jnp.tile(inv, LANES // ROPE_HALF)[None, :]
    inv_col = inv[:, None]
    ts = min(seq, ROPE_TABLE_ROWS)
    tok = jax.ShapeDtypeStruct((seq, LANES), F32)
    feat = jax.ShapeDtypeStruct((ROPE_HALF, seq), F32)
    return pl.pallas_call(
        _rope_table_kernel,
        out_shape=(tok, tok, feat, feat),
        grid=(seq // ts,),
        in_specs=[_const_spec((1, LANES)), _const_spec((ROPE_HALF, 1))],
        out_specs=(
            pl.BlockSpec((ts, LANES), lambda i: (i, 0)),
            pl.BlockSpec((ts, LANES), lambda i: (i, 0)),
            pl.BlockSpec((ROPE_HALF, ts), lambda i: (0, i)),
            pl.BlockSpec((ROPE_HALF, ts), lambda i: (0, i)),
        ),
        compiler_params=_params(("arbitrary",)), name="rope_tables",
    )(inv_row, inv_col)


def _mla_prep_kernel(x_ref, g_mix_ref, w_a_ref, g_qa_ref, g_kva_ref, wq_t_ref, g_q_ref,
                     wk_ref, wv_t_ref, g_kn_ref, g_kr_ref, g_krot_ref,
                     cos_t_ref, sin_t_ref, cos_f_ref, sin_f_ref,
                     q_ref, k_ref, v_ref, h_ref, *, tk):
    tm = x_ref.shape[0]
    h = _rms_rows(x_ref[...], g_mix_ref[...]).astype(BF16)
    h_ref[...] = h
    c = jnp.dot(h, w_a_ref[...], preferred_element_type=F32)
    cq = _rms_rows(c[:, :Q_LORA], g_qa_ref[...]).astype(BF16)
    ckv = _rms_rows(c[:, Q_LORA:Q_LORA + KV_LORA], g_kva_ref[...]).astype(BF16)
    k_pe = c[:, K_PE_COL:K_PE_COL + LANES]
    k_pe_rot = c[:, K_PE_COL + LANES:W_A_COLS]

    q_t = lax.dot_general(wq_t_ref[...], cq, NT_DIMS, preferred_element_type=F32)
    cos_f = cos_f_ref[...]
    sin_f = sin_f_ref[...]
    scale = MLA_QK ** -0.5 * LOG2E
    g_q = g_q_ref[...]
    for hd in range(MLA_HEADS):
        xh = q_t[hd * MLA_QK:(hd + 1) * MLA_QK]
        rinv = lax.rsqrt(jnp.mean(xh * xh, axis=0, keepdims=True) + EPS)
        xn = xh * rinv * g_q
        x1 = xn[MLA_NOPE:MLA_NOPE + ROPE_HALF]
        x2 = xn[MLA_NOPE + ROPE_HALF:]
        q_ref[0, hd, 0:MLA_NOPE, :] = (xn[:MLA_NOPE] * scale).astype(BF16)
        q_ref[0, hd, MLA_NOPE:MLA_NOPE + ROPE_HALF, :] = ((x1 * cos_f - x2 * sin_f) * scale).astype(BF16)
        q_ref[0, hd, MLA_NOPE + ROPE_HALF:MLA_QK, :] = ((x2 * cos_f + x1 * sin_f) * scale).astype(BF16)
        q_ref[0, hd, MLA_QK:, :] = jnp.zeros((MLA_QK_PAD - MLA_QK, tm), BF16)

    k_nope = jnp.dot(ckv, wk_ref[...], preferred_element_type=F32)
    rope_base = (k_pe * g_kr_ref[...]) * cos_t_ref[...] + (k_pe_rot * g_krot_ref[...]) * sin_t_ref[...]
    ss_pe = jnp.sum(k_pe * k_pe, axis=-1, keepdims=True)
    for hd in range(MLA_HEADS):
        kh = k_nope[:, hd * MLA_NOPE:(hd + 1) * MLA_NOPE]
        ss = jnp.sum(kh * kh, axis=-1, keepdims=True) + ss_pe
        rinv = lax.rsqrt(ss * (1.0 / MLA_QK) + EPS)
        k_ref[0, hd, :, 0:LANES] = (kh * rinv * g_kn_ref[...]).astype(BF16)
        k_ref[0, hd, :, LANES:] = (rope_base * rinv).astype(BF16)

    v_t = lax.dot_general(wv_t_ref[...], ckv, NT_DIMS, preferred_element_type=F32)
    for hd in range(MLA_HEADS):
        for cc in range(tm // tk):
            v_ref[0, hd, cc] = v_t[hd * MLA_V:(hd + 1) * MLA_V, cc * tk:(cc + 1) * tk].astype(BF16)


def _mla_prep(x2d, gb, seq, tm, tk, w, tables):
    tokens = x2d.shape[0]
    nt = seq // tm
    cos_t, sin_t, cos_f, sin_f = tables
    kern = functools.partial(_mla_prep_kernel, tk=tk)
    out_shape = (
        jax.ShapeDtypeStruct((gb, MLA_HEADS, MLA_QK_PAD, seq), BF16),
        jax.ShapeDtypeStruct((gb, MLA_HEADS, seq, MLA_QK_PAD), BF16),
        jax.ShapeDtypeStruct((gb, MLA_HEADS, seq // tk, MLA_V, tk), BF16),
        jax.ShapeDtypeStruct((tokens, D_MODEL), BF16),
    )
    in_specs = [
        pl.BlockSpec((tm, D_MODEL), lambda i: (i, 0)),
        _const_spec((1, D_MODEL)),
        _const_spec((D_MODEL, W_A_COLS)),
        _const_spec((1, Q_LORA)),
        _const_spec((1, KV_LORA)),
        _const_spec((MLA_HEADS * MLA_QK, Q_LORA)),
        _const_spec((MLA_QK, 1)),
        _const_spec((KV_LORA, MLA_HEADS * MLA_NOPE)),
        _const_spec((MLA_HEADS * MLA_V, KV_LORA)),
        _const_spec((1, LANES)),
        _const_spec((1, LANES)),
        _const_spec((1, LANES)),
        pl.BlockSpec((tm, LANES), lambda i: (i % nt, 0)),
        pl.BlockSpec((tm, LANES), lambda i: (i % nt, 0)),
        pl.BlockSpec((ROPE_HALF, tm), lambda i: (0, i % nt)),
        pl.BlockSpec((ROPE_HALF, tm), lambda i: (0, i % nt)),
    ]
    out_specs = (
        pl.BlockSpec((1, MLA_HEADS, MLA_QK_PAD, tm), lambda i: (i // nt, 0, 0, i % nt)),
        pl.BlockSpec((1, MLA_HEADS, tm, MLA_QK_PAD), lambda i: (i // nt, 0, i % nt, 0)),
        pl.BlockSpec((1, MLA_HEADS, tm // tk, MLA_V, tk), lambda i: (i // nt, 0, i % nt, 0, 0)),
        pl.BlockSpec((tm, D_MODEL), lambda i: (i, 0)),
    )
    return pl.pallas_call(
        kern, out_shape=out_shape, grid=(tokens // tm,), in_specs=in_specs, out_specs=out_specs,
        compiler_params=_params(("arbitrary",)), name="mla_prep",
    )(x2d, w["g_mix"], w["w_a"], w["g_qa"], w["g_kva"], w["wq_t"], w["g_q"], w["wk_nope"],
      w["wv_t"], w["g_kn"], w["g_kr"], w["g_krot"], cos_t, sin_t, cos_f, sin_f)


def _group_norm_cols(x_t, gain_col):
    rows, tm = x_t.shape
    x3 = x_t.reshape(rows // DIFF_QK, DIFF_QK, tm)
    rinv = lax.rsqrt(jnp.mean(x3 * x3, axis=1, keepdims=True) + EPS)
    return (x3 * rinv * gain_col[None]).reshape(rows, tm)


def _diff_prep_kernel(h_ref, wq_t_ref, wk_t_ref, wv_t_ref, g_q_ref, g_k_ref,
                      q_ref, k_ref, v_ref, *, tk):
    tm = h_ref.shape[0]
    h = h_ref[...]
    scale = DIFF_QK ** -0.5 * LOG2E
    q_t = lax.dot_general(wq_t_ref[...], h, NT_DIMS, preferred_element_type=F32)
    qn = (_group_norm_cols(q_t, g_q_ref[...]) * scale).astype(BF16)
    zeros = jnp.zeros((DIFF_QK, tm), BF16)
    for hd in range(DIFF_HEADS):
        r0 = hd * DIFF_V
        q_ref[0, hd, 0, 0:DIFF_QK, :] = qn[r0:r0 + DIFF_QK]
        q_ref[0, hd, 0, DIFF_QK:, :] = zeros
        q_ref[0, hd, 1, 0:DIFF_QK, :] = zeros
        q_ref[0, hd, 1, DIFF_QK:, :] = qn[r0 + DIFF_QK:r0 + DIFF_V]

    k_t = lax.dot_general(wk_t_ref[...], h, NT_DIMS, preferred_element_type=F32)
    kn = _group_norm_cols(k_t, g_k_ref[...]).T
    for hd in range(DIFF_HEADS):
        k_ref[0, hd] = kn[:, hd * DIFF_V:(hd + 1) * DIFF_V].astype(BF16)

    v_t = lax.dot_general(wv_t_ref[...], h, NT_DIMS, preferred_element_type=F32)
    for hd in range(DIFF_HEADS):
        for cc in range(tm // tk):
            v_ref[0, hd, cc] = v_t[hd * DIFF_V:(hd + 1) * DIFF_V, cc * tk:(cc + 1) * tk].astype(BF16)


def _diff_prep(h2d, gb, seq, tm, tk, w):
    tokens = h2d.shape[0]
    nt = seq // tm
    kern = functools.partial(_diff_prep_kernel, tk=tk)
    out_shape = (
        jax.ShapeDtypeStruct((gb, DIFF_HEADS, 2, DIFF_V, seq), BF16),
        jax.ShapeDtypeStruct((gb, DIFF_HEADS, seq, DIFF_V), BF16),
        jax.ShapeDtypeStruct((gb, DIFF_HEADS, seq // tk, DIFF_V, tk), BF16),
    )
    in_specs = [
        pl.BlockSpec((tm, D_MODEL), lambda i: (i, 0)),
        _const_spec((HEAD_OUT, D_MODEL)),
        _const_spec((HEAD_OUT, D_MODEL)),
        _const_spec((HEAD_OUT, D_MODEL)),
        _const_spec((DIFF_QK, 1)),
        _const_spec((DIFF_QK, 1)),
    ]
    out_specs = (
        pl.BlockSpec((1, DIFF_HEADS, 2, DIFF_V, tm), lambda i: (i // nt, 0, 0, 0, i % nt)),
        pl.BlockSpec((1, DIFF_HEADS, tm, DIFF_V), lambda i: (i // nt, 0, i % nt, 0)),
        pl.BlockSpec((1, DIFF_HEADS, tm // tk, DIFF_V, tk), lambda i: (i // nt, 0, i % nt, 0, 0)),
    )
    return pl.pallas_call(
        kern, out_shape=out_shape, grid=(tokens // tm,), in_specs=in_specs, out_specs=out_specs,
        compiler_params=_params(("arbitrary",)), name="diff_prep",
    )(h2d, w["w_dq_t"], w["w_dk_t"], w["w_dv_t"], w["g_dq"], w["g_dk"])


def _bias_tile_kernel(bucket_ref, rb_ref, out_ref):
    hd = pl.program_id(0)
    offset = pl.program_id(1) - pl.num_programs(1) // 2

    @pl.when(offset <= -2)
    def _():
        out_ref[0, 0] = jnp.full(out_ref.shape[2:], rb_ref[REL_BUCKETS // 2 - 1, hd] * LOG2E, F32)

    @pl.when(offset >= 2)
    def _():
        out_ref[0, 0] = jnp.full(out_ref.shape[2:], rb_ref[REL_BUCKETS - 1, hd] * LOG2E, F32)

    @pl.when(jnp.abs(offset) < 2)
    def _():
        bucket = bucket_ref[0]
        acc = jnp.zeros(bucket.shape, F32)
        for b in range(REL_BUCKETS):
            acc = jnp.where(bucket == b, rb_ref[b, hd], acc)
        out_ref[0, 0] = acc * LOG2E


def _t5_bucket(rel):
    nb = REL_BUCKETS // 2
    ret = jnp.where(rel > 0, nb, 0)
    n = jnp.abs(rel)
    max_exact = nb // 2
    nf = jnp.maximum(n, 1).astype(F32)
    large = max_exact + (jnp.log(nf / max_exact) / math.log(REL_MAX_DIST / max_exact)
                         * (nb - max_exact)).astype(jnp.int32)
    large = jnp.minimum(large, nb - 1)
    return ret + jnp.where(n < max_exact, n, large)


def _bias_tiles(rel_bias, blk):
    assert blk >= REL_MAX_DIST
    n = BIAS_TILES
    d = jnp.arange(blk, dtype=jnp.int32)
    offs = (jnp.arange(n, dtype=jnp.int32) - n // 2) * blk
    rel = offs[:, None, None] + d[None, :, None] - d[None, None, :]
    bucket = _t5_bucket(rel)
    return pl.pallas_call(
        _bias_tile_kernel,
        out_shape=jax.ShapeDtypeStruct((DIFF_HEADS, n, blk, blk), F32),
        grid=(DIFF_HEADS, n),
        in_specs=[
            pl.BlockSpec((1, blk, blk), lambda hd, t: (t, 0, 0)),
            pl.BlockSpec(memory_space=pltpu.SMEM),
        ],
        out_specs=pl.BlockSpec((1, 1, blk, blk), lambda hd, t: (hd, t, 0, 0)),
        compiler_params=_params(("arbitrary", "arbitrary")), name="bias_tiles",
    )(bucket, rel_bias)


class _FlashState:
    def __init__(self, q_ref, k_ref, v_ref, scratch, blk, q_tiles):
        self.q_ref, self.k_ref, self.v_ref = q_ref, k_ref, v_ref
        self.p_ref, self.alpha_ref, self.m_ref, self.l_ref, self.acc_ref = scratch
        self.blk = blk
        self.bound = None
        self.streams = [(qt, mp) for qt in range(q_tiles) for mp in range(q_ref.shape[2])]
        self.nc = v_ref.shape[2]

    def q(self, qt, mp):
        return self.q_ref[0, 0, mp, :, qt * self.blk:(qt + 1) * self.blk]

    def key_chunk(self, c):
        return self.k_ref[0, 0, pl.ds(pl.multiple_of(c * self.blk, self.blk), self.blk), :]

    def init(self, online):
        self.l_ref[...] = jnp.zeros(self.l_ref.shape, F32)
        self.acc_ref[...] = jnp.zeros(self.acc_ref.shape, F32)
        self.p_ref[1] = jnp.zeros(self.p_ref.shape[1:], BF16)
        if online:
            self.m_ref[...] = jnp.full(self.m_ref.shape, NEG_BIG, F32)
            self.alpha_ref[...] = jnp.ones(self.alpha_ref.shape, F32)

    def pv(self, online, c, half):
        vc = self.v_ref[0, 0, c]
        for sx in range(len(self.streams)):
            update = jnp.dot(vc, self.p_ref[half, sx], preferred_element_type=F32)
            if online:
                self.acc_ref[sx] = self.alpha_ref[sx] * self.acc_ref[sx] + update
            else:
                self.acc_ref[sx] += update

    def step(self, online, c, half, cst=None, bias_tiles=None):
        self.pv(online, jnp.maximum(c - 1, 0), 1 - half)
        kc = self.key_chunk(c)
        for sx, (qt, mp) in enumerate(self.streams):
            s = jnp.dot(kc, self.q(qt, mp), preferred_element_type=F32)
            if bias_tiles is not None:
                s = s + bias_tiles[qt]
            l_old = self.l_ref[sx]
            if online:
                mc = jnp.max(s, axis=0, keepdims=True)
                if cst is not None:
                    mc = mc + cst
                m_old = self.m_ref[sx]
                shift = jnp.maximum(m_old, mc)
                alpha = jnp.exp2(m_old - shift)
                l_old = alpha * l_old
                self.alpha_ref[sx] = alpha
                self.m_ref[sx] = shift
            else:
                shift = self.bound
            if cst is not None:
                shift = shift - cst
            p = jnp.exp2(s - shift)
            self.l_ref[sx] = l_old + jnp.sum(p, axis=0, keepdims=True)
            self.p_ref[half, sx] = p.astype(BF16)

    def run(self, bound, schedule):
        self.bound = bound
        use_bound = bound <= BOUNDED_SOFTMAX_LIMIT
        for online, pred in ((False, use_bound), (True, jnp.logical_not(use_bound))):
            @pl.when(pred)
            def _():
                self.init(online)
                schedule(functools.partial(self.step, online))
                self.pv(online, self.nc - 1, 1)


def _flash_scratch(n_streams, head_dim, blk):
    return [
        pltpu.VMEM((2, n_streams, blk, blk), BF16),
        pltpu.VMEM((n_streams, 1, blk), F32),
        pltpu.VMEM((n_streams, 1, blk), F32),
        pltpu.VMEM((n_streams, 1, blk), F32),
        pltpu.VMEM((n_streams, head_dim, blk), F32),
    ]


def _score_bound(head_dim, g_q, g_k):
    return (head_dim ** 0.5 * LOG2E * 1.02 * jnp.max(jnp.abs(g_q)) * jnp.max(jnp.abs(g_k))).reshape(1)


def _mla_attn_kernel(bound_ref, q_ref, k_ref, v_ref, o_ref, *scratch, blk, q_tiles, group):
    st = _FlashState(q_ref, k_ref, v_ref, scratch, blk, q_tiles)

    def schedule(step):
        def body(cc, carry):
            for u in range(group):
                step(group * cc + u, u % 2)
            return carry

        lax.fori_loop(0, st.nc // group, body, 0)

    st.run(bound_ref[0], schedule)
    for sx, (qt, _) in enumerate(st.streams):
        o_t = st.acc_ref[sx] / st.l_ref[sx]
        o_ref[qt * blk:(qt + 1) * blk, :] = o_t.T.astype(BF16)


def _mla_attention(bound, q, k, v, gb, seq, blk, q_tiles, group):
    nc = seq // blk
    nq = nc // q_tiles
    assert group % 2 == 0 and nc % group == 0 and nc % q_tiles == 0
    kern = functools.partial(_mla_attn_kernel, blk=blk, q_tiles=q_tiles, group=group)
    q = q.reshape(gb, MLA_HEADS, 1, MLA_QK_PAD, seq)
    tq = q_tiles * blk
    return pl.pallas_call(
        kern,
        out_shape=jax.ShapeDtypeStruct((gb * seq, HEAD_OUT), BF16),
        grid=(gb, MLA_HEADS, nq),
        in_specs=[
            pl.BlockSpec(memory_space=pltpu.SMEM),
            pl.BlockSpec((1, 1, 1, MLA_QK_PAD, tq), lambda b, hd, i: (b, hd, 0, 0, i)),
            pl.BlockSpec((1, 1, seq, MLA_QK_PAD), lambda b, hd, i: (b, hd, 0, 0)),
            pl.BlockSpec((1, 1, nc, MLA_V, blk), lambda b, hd, i: (b, hd, 0, 0, 0)),
        ],
        out_specs=pl.BlockSpec((tq, MLA_V), lambda b, hd, i: (b * nq + i, hd)),
        scratch_shapes=_flash_scratch(q_tiles, MLA_V, blk),
        compiler_params=_params(("arbitrary", "arbitrary", "arbitrary")), name="mla_attn",
    )(bound, q, k, v)


def _diff_attn_kernel(bound_ref, q_ref, k_ref, v_ref, bias_ref, rb_ref, lam_ref, g_sub_ref, o_ref,
                      *scratch, blk, q_tiles, group, lam_init):
    hd = pl.program_id(0)
    i_first = pl.program_id(2) * q_tiles
    st = _FlashState(q_ref, k_ref, v_ref, scratch, blk, q_tiles)
    acc_ref, l_ref = st.acc_ref, st.l_ref
    n_groups = st.nc // group

    bias_max = jnp.abs(rb_ref[0, hd])
    for b in range(1, REL_BUCKETS):
        bias_max = jnp.maximum(bias_max, jnp.abs(rb_ref[b, hd]))

    far_before = rb_ref[REL_BUCKETS // 2 - 1, hd] * LOG2E
    far_after = rb_ref[REL_BUCKETS - 1, hd] * LOG2E
    first_near = (i_first - 1 + group) // group - 1
    last_near = (i_first + q_tiles) // group

    near_lo = jnp.maximum(first_near, 0)
    near_hi = jnp.minimum(last_near + 1, n_groups)

    def schedule(step):
        def far_group(cst):
            def body(cc, carry):
                for u in range(group):
                    step(group * cc + u, u % 2, cst=cst)
                return carry
            return body

        def near_group(cc, carry):
            for u in range(group):
                c = group * cc + u
                far = BIAS_TILES // 2
                tiles = [bias_ref[0, jnp.clip(c - (i_first + qt), -far, far) + far]
                         for qt in range(q_tiles)]
                step(c, u % 2, bias_tiles=tiles)
            return carry

        lax.fori_loop(0, near_lo, far_group(far_before), 0)
        lax.fori_loop(near_lo, near_hi, near_group, 0)
        lax.fori_loop(near_hi, n_groups, far_group(far_after), 0)

    st.run(bound_ref[0] + bias_max * LOG2E, schedule)

    lam_v = lam_ref[...]
    lam = (jnp.exp(jnp.sum(lam_v[0:1] * lam_v[1:2], axis=-1, keepdims=True))
           - jnp.exp(jnp.sum(lam_v[2:3] * lam_v[3:4], axis=-1, keepdims=True)) + lam_init)
    for qt in range(q_tiles):
        s0, s1 = 2 * qt, 2 * qt + 1
        o_t = acc_ref[s0] / l_ref[s0] - lam * (acc_ref[s1] / l_ref[s1])
        o_t = o_t * lax.rsqrt(jnp.mean(o_t * o_t, axis=0, keepdims=True) + EPS) * g_sub_ref[...]
        o_ref[qt * blk:(qt + 1) * blk, :] = (o_t * (1.0 - lam_init)).T.astype(BF16)


def _diff_attention(bound, q, k, v, bias, rel_bias, lam_rows, g_sub, gb, seq, blk, q_tiles, group,
                    lam_init):
    nc = seq // blk
    nq = nc // q_tiles
    assert group % 2 == 0 and nc % group == 0 and nc % q_tiles == 0 and blk >= REL_MAX_DIST
    kern = functools.partial(_diff_attn_kernel, blk=blk, q_tiles=q_tiles, group=group,
                             lam_init=lam_init)
    tq = q_tiles * blk
    return pl.pallas_call(
        kern,
        out_shape=jax.ShapeDtypeStruct((gb * seq, HEAD_OUT), BF16),
        grid=(DIFF_HEADS, gb, nq),
        in_specs=[
            pl.BlockSpec(memory_space=pltpu.SMEM),
            pl.BlockSpec((1, 1, 2, DIFF_V, tq), lambda hd, b, i: (b, hd, 0, 0, i)),
            pl.BlockSpec((1, 1, seq, DIFF_V), lambda hd, b, i: (b, hd, 0, 0)),
            pl.BlockSpec((1, 1, nc, DIFF_V, blk), lambda hd, b, i: (b, hd, 0, 0, 0)),
            pl.BlockSpec((1, BIAS_TILES, blk, blk), lambda hd, b, i: (hd, 0, 0, 0),
                         pipeline_mode=pl.Buffered(1)),
            pl.BlockSpec(memory_space=pltpu.SMEM),
            _const_spec((4, LANES)),
            _const_spec((DIFF_V, 1)),
        ],
        out_specs=pl.BlockSpec((tq, DIFF_V), lambda hd, b, i: (b * nq + i, hd)),
        scratch_shapes=_flash_scratch(2 * q_tiles, DIFF_V, blk),
        compiler_params=_params(("arbitrary", "arbitrary", "arbitrary")), name="diff_attn",
    )(bound, q, k, v, bias, rel_bias, lam_rows, g_sub)


def _merge_kernel(h_ref, oa_ref, ob_ref, wga_ref, wgb_ref, wua_ref, wub_ref, out_ref):
    h = h_ref[...]
    ga = jnp.dot(h, wga_ref[...], preferred_element_type=F32)
    gb_ = jnp.dot(h, wgb_ref[...], preferred_element_type=F32)
    ua = jnp.dot(oa_ref[...], wua_ref[...], preferred_element_type=F32)
    ub = jnp.dot(ob_ref[...], wub_ref[...], preferred_element_type=F32)
    out_ref[...] = (jax.nn.sigmoid(ga) * ua + jax.nn.sigmoid(gb_) * ub).astype(BF16)


def _merge(h2d, o_a, o_b, w, tm, tn):
    tokens = h2d.shape[0]
    return pl.pallas_call(
        _merge_kernel,
        out_shape=jax.ShapeDtypeStruct((tokens, D_MODEL), BF16),
        grid=(tokens // tm, D_MODEL // tn),
        in_specs=[
            pl.BlockSpec((tm, D_MODEL), lambda i, j: (i, 0)),
            pl.BlockSpec((tm, HEAD_OUT), lambda i, j: (i, 0)),
            pl.BlockSpec((tm, HEAD_OUT), lambda i, j: (i, 0)),
            pl.BlockSpec((D_MODEL, tn), lambda i, j: (0, j)),
            pl.BlockSpec((D_MODEL, tn), lambda i, j: (0, j)),
            pl.BlockSpec((HEAD_OUT, tn), lambda i, j: (0, j)),
            pl.BlockSpec((HEAD_OUT, tn), lambda i, j: (0, j)),
        ],
        out_specs=pl.BlockSpec((tm, tn), lambda i, j: (i, j)),
        compiler_params=_params(("arbitrary", "arbitrary")), name="gated_merge",
    )(h2d, o_a, o_b, w["w_ga"], w["w_gb"], w["w_ua"], w["w_ub"])


def _out_proj_kernel(x_ref, m_ref, wo_ref, out_ref):
    out_ref[...] = x_ref[...] + jnp.dot(m_ref[...], wo_ref[...], preferred_element_type=F32)


def _out_proj(x2d, merged, w, tm):
    tokens = x2d.shape[0]
    return pl.pallas_call(
        _out_proj_kernel,
        out_shape=jax.ShapeDtypeStruct((tokens, D_MODEL), F32),
        grid=(tokens // tm,),
        in_specs=[
            pl.BlockSpec((tm, D_MODEL), lambda i: (i, 0)),
            pl.BlockSpec((tm, D_MODEL), lambda i: (i, 0)),
            _const_spec((D_MODEL, D_MODEL)),
        ],
        out_specs=pl.BlockSpec((tm, D_MODEL), lambda i: (i, 0)),
        compiler_params=_params(("arbitrary",)), name="out_proj",
    )(x2d, merged, w["w_o"])


def _ffn_kernel(x_ref, g_ref, wg_ref, wu_ref, wd_ref, out_ref, h_ref):
    @pl.when(pl.program_id(1) == 0)
    def _():
        x = x_ref[...]
        h_ref[...] = _rms_rows(x, g_ref[...]).astype(BF16)
        out_ref[...] = x

    h = h_ref[...]
    g = jnp.dot(h, wg_ref[...], preferred_element_type=F32)
    u = jnp.dot(h, wu_ref[...], preferred_element_type=F32)
    a = (g * jax.nn.sigmoid(g) * u).astype(BF16)
    out_ref[...] += jnp.dot(a, wd_ref[...], preferred_element_type=F32)


def _ffn(x2d, w, tm, tf):
    tokens = x2d.shape[0]
    return pl.pallas_call(
        _ffn_kernel,
        out_shape=jax.ShapeDtypeStruct((tokens, D_MODEL), F32),
        grid=(tokens // tm, D_FF // tf),
        in_specs=[
            pl.BlockSpec((tm, D_MODEL), lambda i, j: (i, 0)),
            _const_spec((1, D_MODEL)),
            pl.BlockSpec((D_MODEL, tf), lambda i, j: (0, j)),
            pl.BlockSpec((D_MODEL, tf), lambda i, j: (0, j)),
            pl.BlockSpec((tf, D_MODEL), lambda i, j: (j, 0)),
        ],
        out_specs=pl.BlockSpec((tm, D_MODEL), lambda i, j: (i, 0)),
        scratch_shapes=[pltpu.VMEM((tm, D_MODEL), BF16)],
        compiler_params=_params(("arbitrary", "arbitrary")), name="swiglu_ffn",
    )(x2d, w["g_ffn"], w["w_gate"], w["w_up"], w["w_down"])


def _prepare_weights(mix_norm, w_in, q_a_norm, wq_b, kv_a_norm, wkv_b, mla_q_norm, mla_k_norm,
                     diff_q_norm, diff_k_norm, diff_subln, w_up_mla, w_up_diff, w_o, ffn_norm,
                     w_gate, w_up, w_down, layer):
    win = w_in[layer]
    o_cq, o_ckv, o_kpe = 0, Q_LORA, Q_LORA + KV_LORA
    o_dq = o_kpe + MLA_ROPE
    o_dk = o_dq + HEAD_OUT
    o_dv = o_dk + HEAD_OUT
    o_ga = o_dv + HEAD_OUT
    o_gb = o_ga + D_MODEL
    w_kpe = win[:, o_kpe:o_dq]
    w_kpe_rot = jnp.concatenate([-w_kpe[:, ROPE_HALF:], w_kpe[:, :ROPE_HALF]], axis=1)
    zpad = jnp.zeros((D_MODEL, LANES - MLA_ROPE), F32)
    w_a = jnp.concatenate([win[:, o_cq:o_kpe], w_kpe, zpad, w_kpe_rot, zpad], axis=1)
    wkv = wkv_b[layer].reshape(KV_LORA, MLA_HEADS, MLA_NOPE + MLA_V)
    gk = mla_k_norm[layer]
    gk_rope = gk[MLA_NOPE:]
    lane_pad = jnp.zeros((LANES - MLA_ROPE,), F32)
    return {
        "g_mix": mix_norm[layer][None, :],
        "w_a": w_a.astype(BF16),
        "g_qa": q_a_norm[layer][None, :],
        "g_kva": kv_a_norm[layer][None, :],
        "wq_t": wq_b[layer].T.astype(BF16),
        "g_q": mla_q_norm[layer][:, None],
        "wk_nope": wkv[:, :, :MLA_NOPE].reshape(KV_LORA, MLA_HEADS * MLA_NOPE).astype(BF16),
        "wv_t": wkv[:, :, MLA_NOPE:].reshape(KV_LORA, MLA_HEADS * MLA_V).T.astype(BF16),
        "g_kn": gk[None, :MLA_NOPE],
        "g_kr": jnp.concatenate([gk_rope, lane_pad])[None, :],
        "g_krot": jnp.concatenate([gk_rope[ROPE_HALF:], gk_rope[:ROPE_HALF], lane_pad])[None, :],
        "w_dq_t": win[:, o_dq:o_dk].T.astype(BF16),
        "w_dk_t": win[:, o_dk:o_dv].T.astype(BF16),
        "w_dv_t": win[:, o_dv:o_ga].T.astype(BF16),
        "g_dq": diff_q_norm[layer][:, None],
        "g_dk": diff_k_norm[layer][:, None],
        "g_sub": diff_subln[layer][:, None],
        "w_ga": win[:, o_ga:o_gb].astype(BF16),
        "w_gb": win[:, o_gb:].astype(BF16),
        "w_ua": w_up_mla[layer].astype(BF16),
        "w_ub": w_up_diff[layer].astype(BF16),
        "w_o": w_o[layer].astype(BF16),
        "mla_bound": _score_bound(MLA_QK, mla_q_norm[layer], gk),
        "diff_bound": _score_bound(DIFF_QK, diff_q_norm[layer], diff_k_norm[layer]),
        "g_ffn": ffn_norm[layer][None, :],
        "w_gate": w_gate[layer].astype(BF16),
        "w_up": w_up[layer].astype(BF16),
        "w_down": w_down[layer].astype(BF16),
    }


class _Tiles(NamedTuple):
    tm: int = 512
    tn: int = 1024
    tf: int = 512
    tm_ffn: int = 1024
    blk: int = 512
    mla_q_tiles: int = 4
    diff_q_tiles: int = 4
    mla_group_max: int = 4
    diff_group_max: int = 2


def _encoder_layer(x, w, tables, bias, rel_bias, lam_rows, lam_init, t):
    gb, seq, _ = x.shape
    x2d = x.reshape(gb * seq, D_MODEL)
    mla_group = math.gcd(t.mla_group_max, seq // t.blk)
    diff_group = math.gcd(t.diff_group_max, seq // t.blk)
    q_a, k_a, v_a, h2d = _mla_prep(x2d, gb, seq, t.tm, t.blk, w, tables)
    q_d, k_d, v_d = _diff_prep(h2d, gb, seq, t.tm, t.blk, w)
    o_a = _mla_attention(w["mla_bound"], q_a, k_a, v_a, gb, seq, t.blk, t.mla_q_tiles, mla_group)
    o_b = _diff_attention(w["diff_bound"], q_d, k_d, v_d, bias, rel_bias, lam_rows, w["g_sub"], gb,
                          seq, t.blk, t.diff_q_tiles, diff_group, lam_init)
    merged = _merge(h2d, o_a, o_b, w, t.tm, t.tn)
    x1 = _out_proj(x2d, merged, w, t.tm)
    y = _ffn(x1, w, t.tm_ffn, t.tf)
    return y.reshape(gb, seq, D_MODEL)


def kernel(x_prompt, x_sample, mix_norm, w_in, q_a_norm, wq_b, kv_a_norm, wkv_b, mla_q_norm, mla_k_norm, diff_q_norm, diff_k_norm, lambda_q1, lambda_k1, lambda_q2, lambda_k2, diff_subln, w_up_mla, w_up_diff, w_o, ffn_norm, w_gate, w_up, w_down, rel_bias):
    t = _Tiles()
    depth = w_in.shape[0]
    max_seq = max(x_prompt.shape[1], x_sample.shape[1])
    tables = _rope_tables(max_seq)
    bias = _bias_tiles(rel_bias, t.blk)
    y_prompt, y_sample = x_prompt, x_sample
    for layer in range(depth):
        w = _prepare_weights(mix_norm, w_in, q_a_norm, wq_b, kv_a_norm, wkv_b, mla_q_norm,
                             mla_k_norm, diff_q_norm, diff_k_norm, diff_subln, w_up_mla,
                             w_up_diff, w_o, ffn_norm, w_gate, w_up, w_down, layer)
        lam_init = 0.8 - 0.6 * math.exp(-0.3 * layer)
        lam_pad = jnp.zeros((LANES - DIFF_QK,), F32)
        lam_rows = jnp.stack([jnp.concatenate([v[layer], lam_pad])
                              for v in (lambda_q1, lambda_k1, lambda_q2, lambda_k2)])
        run = functools.partial(_encoder_layer, w=w, tables=tables, bias=bias, rel_bias=rel_bias,
                                lam_rows=lam_rows, lam_init=lam_init, t=t)
        y_prompt = run(y_prompt)
        y_sample = run(y_sample)
    return (y_prompt, y_sample)
```

```python
import functools
import math
from typing import NamedTuple

import jax
import jax.numpy as jnp
from jax import lax
from jax.experimental import pallas as pl
from jax.experimental.pallas import tpu as pltpu

D_MODEL = 2048
MLA_HEADS = 8
MLA_NOPE = 128
MLA_ROPE = 64
MLA_QK = MLA_NOPE + MLA_ROPE
MLA_V = 128
Q_LORA = 512
KV_LORA = 256
ROPE_THETA = 10000.0
ROPE_HALF = MLA_ROPE // 2
DIFF_HEADS = 8
DIFF_QK = 64
DIFF_V = 2 * DIFF_QK
REL_BUCKETS = 32
REL_MAX_DIST = 128
D_FF = 5632
EPS = 1e-6
HEAD_OUT = MLA_HEADS * MLA_V

LANES = 128
MLA_QK_PAD = 2 * LANES
K_PE_COL = Q_LORA + KV_LORA
W_A_COLS = K_PE_COL + 2 * LANES
ROPE_TABLE_ROWS = 2048
VMEM_LIMIT = 56 * 1024 * 1024

F32 = jnp.float32
BF16 = jnp.bfloat16
NEG_BIG = -1e30
LOG2E = math.log2(math.e)
BOUNDED_SOFTMAX_LIMIT = 50.0
BIAS_TILES = 5

NT_DIMS = (((1,), (1,)), ((), ()))


def _params(semantics):
    return pltpu.CompilerParams(dimension_semantics=semantics, vmem_limit_bytes=VMEM_LIMIT)


def _const_spec(shape):
    zeros = (0,) * len(shape)
    return pl.BlockSpec(shape, lambda *_: zeros)


def _rms_rows(x, gain):
    return x * lax.rsqrt(jnp.mean(x * x, axis=-1, keepdims=True) + EPS) * gain


def _rope_table_kernel(inv_row_ref, inv_col_ref, cos_t_ref, sin_t_ref, cos_f_ref, sin_f_ref):
    ts = cos_t_ref.shape[0]
    base = pl.program_id(0) * ts
    pos_rows = (base + lax.broadcasted_iota(jnp.int32, (ts, LANES), 0)).astype(F32)
    ang_t = pos_rows * inv_row_ref[...]
    cos_t_ref[...] = jnp.cos(ang_t)
    sin_t_ref[...] = jnp.sin(ang_t)
    pos_cols = (base + lax.broadcasted_iota(jnp.int32, (ROPE_HALF, ts), 1)).astype(F32)
    ang_f = pos_cols * inv_col_ref[...]
    cos_f_ref[...] = jnp.cos(ang_f)
    sin_f_ref[...] = jnp.sin(ang_f)


def _rope_tables(seq):
    inv = ROPE_THETA ** (-jnp.arange(ROPE_HALF, dtype=F32) / ROPE_HALF)
    inv_row = jnp.tile(inv, LANES // ROPE_HALF)[None, :]
    inv_col = inv[:, None]
    ts = min(seq, ROPE_TABLE_ROWS)
    tok = jax.ShapeDtypeStruct((seq, LANES), F32)
    feat = jax.ShapeDtypeStruct((ROPE_HALF, seq), F32)
    return pl.pallas_call(
        _rope_table_kernel,
        out_shape=(tok, tok, feat, feat),
        grid=(seq // ts,),
        in_specs=[_const_spec((1, LANES)), _const_spec((ROPE_HALF, 1))],
        out_specs=(
            pl.BlockSpec((ts, LANES), lambda i: (i, 0)),
            pl.BlockSpec((ts, LANES), lambda i: (i, 0)),
            pl.BlockSpec((ROPE_HALF, ts), lambda i: (0, i)),
            pl.BlockSpec((ROPE_HALF, ts), lambda i: (0, i)),
        ),
        compiler_params=_params(("arbitrary",)), name="rope_tables",
    )(inv_row, inv_col)


def _mla_prep_kernel(x_ref, g_mix_ref, w_a_ref, g_qa_ref, g_kva_ref, wq_t_ref, g_q_ref,
                     wk_ref, wv_t_ref, g_kn_ref, g_kr_ref, g_krot_ref,
                     cos_t_ref, sin_t_ref, cos_f_ref, sin_f_ref,
                     q_ref, k_ref, v_ref, h_ref, *, tk):
    tm = x_ref.shape[0]
    h = _rms_rows(x_ref[...], g_mix_ref[...]).astype(BF16)
    h_ref[...] = h
    c = jnp.dot(h, w_a_ref[...], preferred_element_type=F32)
    cq = _rms_rows(c[:, :Q_LORA], g_qa_ref[...]).astype(BF16)
    ckv = _rms_rows(c[:, Q_LORA:Q_LORA + KV_LORA], g_kva_ref[...]).astype(BF16)
    k_pe = c[:, K_PE_COL:K_PE_COL + LANES]
    k_pe_rot = c[:, K_PE_COL + LANES:W_A_COLS]

    q_t = lax.dot_general(wq_t_ref[...], cq, NT_DIMS, preferred_element_type=F32)
    cos_f = cos_f_ref[...]
    sin_f = sin_f_ref[...]
    scale = MLA_QK ** -0.5 * LOG2E
    g_q = g_q_ref[...]
    for hd in range(MLA_HEADS):
        xh = q_t[hd * MLA_QK:(hd + 1) * MLA_QK]
        rinv = lax.rsqrt(jnp.mean(xh * xh, axis=0, keepdims=True) + EPS)
        xn = xh * rinv * g_q
        x1 = xn[MLA_NOPE:MLA_NOPE + ROPE_HALF]
        x2 = xn[MLA_NOPE + ROPE_HALF:]
        q_ref[0, hd, 0:MLA_NOPE, :] = (xn[:MLA_NOPE] * scale).astype(BF16)
        q_ref[0, hd, MLA_NOPE:MLA_NOPE + ROPE_HALF, :] = ((x1 * cos_f - x2 * sin_f) * scale).astype(BF16)
        q_ref[0, hd, MLA_NOPE + ROPE_HALF:MLA_QK, :] = ((x2 * cos_f + x1 * sin_f) * scale).astype(BF16)
        q_ref[0, hd, MLA_QK:, :] = jnp.zeros((MLA_QK_PAD - MLA_QK, tm), BF16)

    k_nope = jnp.dot(ckv, wk_ref[...], preferred_element_type=F32)
    rope_base = (k_pe * g_kr_ref[...]) * cos_t_ref[...] + (k_pe_rot * g_krot_ref[...]) * sin_t_ref[...]
    ss_pe = jnp.sum(k_pe * k_pe, axis=-1, keepdims=True)
    for hd in range(MLA_HEADS):
        kh = k_nope[:, hd * MLA_NOPE:(hd + 1) * MLA_NOPE]
        ss = jnp.sum(kh * kh, axis=-1, keepdims=True) + ss_pe
        rinv = lax.rsqrt(ss * (1.0 / MLA_QK) + EPS)
        k_ref[0, hd, :, 0:LANES] = (kh * rinv * g_kn_ref[...]).astype(BF16)
        k_ref[0, hd, :, LANES:] = (rope_base * rinv).astype(BF16)

    v_t = lax.dot_general(wv_t_ref[...], ckv, NT_DIMS, preferred_element_type=F32)
    for hd in range(MLA_HEADS):
        for cc in range(tm // tk):
            v_ref[0, hd, cc] = v_t[hd * MLA_V:(hd + 1) * MLA_V, cc * tk:(cc + 1) * tk].astype(BF16)


def _mla_prep(x2d, gb, seq, tm, tk, w, tables):
    tokens = x2d.shape[0]
    nt = seq // tm
    cos_t, sin_t, cos_f, sin_f = tables
    kern = functools.partial(_mla_prep_kernel, tk=tk)
    out_shape = (
        jax.ShapeDtypeStruct((gb, MLA_HEADS, MLA_QK_PAD, seq), BF16),
        jax.ShapeDtypeStruct((gb, MLA_HEADS, seq, MLA_QK_PAD), BF16),
        jax.ShapeDtypeStruct((gb, MLA_HEADS, seq // tk, MLA_V, tk), BF16),
        jax.ShapeDtypeStruct((tokens, D_MODEL), BF16),
    )
    in_specs = [
        pl.BlockSpec((tm, D_MODEL), lambda i: (i, 0)),
        _const_spec((1, D_MODEL)),
        _const_spec((D_MODEL, W_A_COLS)),
        _const_spec((1, Q_LORA)),
        _const_spec((1, KV_LORA)),
        _const_spec((MLA_HEADS * MLA_QK, Q_LORA)),
        _const_spec((MLA_QK, 1)),
        _const_spec((KV_LORA, MLA_HEADS * MLA_NOPE)),
        _const_spec((MLA_HEADS * MLA_V, KV_LORA)),
        _const_spec((1, LANES)),
        _const_spec((1, LANES)),
        _const_spec((1, LANES)),
        pl.BlockSpec((tm, LANES), lambda i: (i % nt, 0)),
        pl.BlockSpec((tm, LANES), lambda i: (i % nt, 0)),
        pl.BlockSpec((ROPE_HALF, tm), lambda i: (0, i % nt)),
        pl.BlockSpec((ROPE_HALF, tm), lambda i: (0, i % nt)),
    ]
    out_specs = (
        pl.BlockSpec((1, MLA_HEADS, MLA_QK_PAD, tm), lambda i: (i // nt, 0, 0, i % nt)),
        pl.BlockSpec((1, MLA_HEADS, tm, MLA_QK_PAD), lambda i: (i // nt, 0, i % nt, 0)),
        pl.BlockSpec((1, MLA_HEADS, tm // tk, MLA_V, tk), lambda i: (i // nt, 0, i % nt, 0, 0)),
        pl.BlockSpec((tm, D_MODEL), lambda i: (i, 0)),
    )
    return pl.pallas_call(
        kern, out_shape=out_shape, grid=(tokens // tm,), in_specs=in_specs, out_specs=out_specs,
        compiler_params=_params(("arbitrary",)), name="mla_prep",
    )(x2d, w["g_mix"], w["w_a"], w["g_qa"], w["g_kva"], w["wq_t"], w["g_q"], w["wk_nope"],
      w["wv_t"], w["g_kn"], w["g_kr"], w["g_krot"], cos_t, sin_t, cos_f, sin_f)


def _group_norm_cols(x_t, gain_col):
    rows, tm = x_t.shape
    x3 = x_t.reshape(rows // DIFF_QK, DIFF_QK, tm)
    rinv = lax.rsqrt(jnp.mean(x3 * x3, axis=1, keepdims=True) + EPS)
    return (x3 * rinv * gain_col[None]).reshape(rows, tm)


def _diff_prep_kernel(h_ref, wq_t_ref, wk_t_ref, wv_t_ref, g_q_ref, g_k_ref,
                      q_ref, k_ref, v_ref, *, tk):
    tm = h_ref.shape[0]
    h = h_ref[...]
    scale = DIFF_QK ** -0.5 * LOG2E
    q_t = lax.dot_general(wq_t_ref[...], h, NT_DIMS, preferred_element_type=F32)
    qn = (_group_norm_cols(q_t, g_q_ref[...]) * scale).astype(BF16)
    zeros = jnp.zeros((DIFF_QK, tm), BF16)
    for hd in range(DIFF_HEADS):
        r0 = hd * DIFF_V
        q_ref[0, hd, 0, 0:DIFF_QK, :] = qn[r0:r0 + DIFF_QK]
        q_ref[0, hd, 0, DIFF_QK:, :] = zeros
        q_ref[0, hd, 1, 0:DIFF_QK, :] = zeros
        q_ref[0, hd, 1, DIFF_QK:, :] = qn[r0 + DIFF_QK:r0 + DIFF_V]

    k_t = lax.dot_general(wk_t_ref[...], h, NT_DIMS, preferred_element_type=F32)
    kn = _group_norm_cols(k_t, g_k_ref[...]).T
    for hd in range(DIFF_HEADS):
        k_ref[0, hd] = kn[:, hd * DIFF_V:(hd + 1) * DIFF_V].astype(BF16)

    v_t = lax.dot_general(wv_t_ref[...], h, NT_DIMS, preferred_element_type=F32)
    for hd in range(DIFF_HEADS):
        for cc in range(tm // tk):
            v_ref[0, hd, cc] = v_t[hd * DIFF_V:(hd + 1) * DIFF_V, cc * tk:(cc + 1) * tk].astype(BF16)


def _diff_prep(h2d, gb, seq, tm, tk, w):
    tokens = h2d.shape[0]
    nt = seq // tm
    kern = functools.partial(_diff_prep_kernel, tk=tk)
    out_shape = (
        jax.ShapeDtypeStruct((gb, DIFF_HEADS, 2, DIFF_V, seq), BF16),
        jax.ShapeDtypeStruct((gb, DIFF_HEADS, seq, DIFF_V), BF16),
        jax.ShapeDtypeStruct((gb, DIFF_HEADS, seq // tk, DIFF_V, tk), BF16),
    )
    in_specs = [
        pl.BlockSpec((tm, D_MODEL), lambda i: (i, 0)),
        _const_spec((HEAD_OUT, D_MODEL)),
        _const_spec((HEAD_OUT, D_MODEL)),
        _const_spec((HEAD_OUT, D_MODEL)),
        _const_spec((DIFF_QK, 1)),
        _const_spec((DIFF_QK, 1)),
    ]
    out_specs = (
        pl.BlockSpec((1, DIFF_HEADS, 2, DIFF_V, tm), lambda i: (i // nt, 0, 0, 0, i % nt)),
        pl.BlockSpec((1, DIFF_HEADS, tm, DIFF_V), lambda i: (i // nt, 0, i % nt, 0)),
        pl.BlockSpec((1, DIFF_HEADS, tm // tk, DIFF_V, tk), lambda i: (i // nt, 0, i % nt, 0, 0)),
    )
    return pl.pallas_call(
        kern, out_shape=out_shape, grid=(tokens // tm,), in_specs=in_specs, out_specs=out_specs,
        compiler_params=_params(("arbitrary",)), name="diff_prep",
    )(h2d, w["w_dq_t"], w["w_dk_t"], w["w_dv_t"], w["g_dq"], w["g_dk"])


def _bias_tile_kernel(bucket_ref, rb_ref, out_ref):
    hd = pl.program_id(0)
    offset = pl.program_id(1) - pl.num_programs(1) // 2

    @pl.when(offset <= -2)
    def _():
        out_ref[0, 0] = jnp.full(out_ref.shape[2:], rb_ref[REL_BUCKETS // 2 - 1, hd] * LOG2E, F32)

    @pl.when(offset >= 2)
    def _():
        out_ref[0, 0] = jnp.full(out_ref.shape[2:], rb_ref[REL_BUCKETS - 1, hd] * LOG2E, F32)

    @pl.when(jnp.abs(offset) < 2)
    def _():
        bucket = bucket_ref[0]
        acc = jnp.zeros(bucket.shape, F32)
        for b in range(REL_BUCKETS):
            acc = jnp.where(bucket == b, rb_ref[b, hd], acc)
        out_ref[0, 0] = acc * LOG2E


def _t5_bucket(rel):
    nb = REL_BUCKETS // 2
    ret = jnp.where(rel > 0, nb, 0)
    n = jnp.abs(rel)
    max_exact = nb // 2
    nf = jnp.maximum(n, 1).astype(F32)
    large = max_exact + (jnp.log(nf / max_exact) / math.log(REL_MAX_DIST / max_exact)
                         * (nb - max_exact)).astype(jnp.int32)
    large = jnp.minimum(large, nb - 1)
    return ret + jnp.where(n < max_exact, n, large)


def _bias_tiles(rel_bias, blk):
    assert blk >= REL_MAX_DIST
    n = BIAS_TILES
    d = jnp.arange(blk, dtype=jnp.int32)
    offs = (jnp.arange(n, dtype=jnp.int32) - n // 2) * blk
    rel = offs[:, None, None] + d[None, :, None] - d[None, None, :]
    bucket = _t5_bucket(rel)
    return pl.pallas_call(
        _bias_tile_kernel,
        out_shape=jax.ShapeDtypeStruct((DIFF_HEADS, n, blk, blk), F32),
        grid=(DIFF_HEADS, n),
        in_specs=[
            pl.BlockSpec((1, blk, blk), lambda hd, t: (t, 0, 0)),
            pl.BlockSpec(memory_space=pltpu.SMEM),
        ],
        out_specs=pl.BlockSpec((1, 1, blk, blk), lambda hd, t: (hd, t, 0, 0)),
        compiler_params=_params(("arbitrary", "arbitrary")), name="bias_tiles",
    )(bucket, rel_bias)


class _FlashState:
    def __init__(self, q_ref, k_ref, v_ref, scratch, blk, q_tiles):
        self.q_ref, self.k_ref, self.v_ref = q_ref, k_ref, v_ref
        self.p_ref, self.alpha_ref, self.m_ref, self.l_ref, self.acc_ref = scratch
        self.blk = blk
        self.bound = None
        self.streams = [(qt, mp) for qt in range(q_tiles) for mp in range(q_ref.shape[2])]
        self.nc = v_ref.shape[2]

    def q(self, qt, mp):
        return self.q_ref[0, 0, mp, :, qt * self.blk:(qt + 1) * self.blk]

    def key_chunk(self, c):
        return self.k_ref[0, 0, pl.ds(pl.multiple_of(c * self.blk, self.blk), self.blk), :]

    def init(self, online):
        self.l_ref[...] = jnp.zeros(self.l_ref.shape, F32)
        self.acc_ref[...] = jnp.zeros(self.acc_ref.shape, F32)
        self.p_ref[1] = jnp.zeros(self.p_ref.shape[1:], BF16)
        if online:
            self.m_ref[...] = jnp.full(self.m_ref.shape, NEG_BIG, F32)
            self.alpha_ref[...] = jnp.ones(self.alpha_ref.shape, F32)

    def pv(self, online, c, half):
        vc = self.v_ref[0, 0, c]
        for sx in range(len(self.streams)):
            update = jnp.dot(vc, self.p_ref[half, sx], preferred_element_type=F32)
            if online:
                self.acc_ref[sx] = self.alpha_ref[sx] * self.acc_ref[sx] + update
            else:
                self.acc_ref[sx] += update

    def step(self, online, c, half, cst=None, bias_tiles=None):
        self.pv(online, jnp.maximum(c - 1, 0), 1 - half)
        kc = self.key_chunk(c)
        for sx, (qt, mp) in enumerate(self.streams):
            s = jnp.dot(kc, self.q(qt, mp), preferred_element_type=F32)
            if bias_tiles is not None:
                s = s + bias_tiles[qt]
            l_old = self.l_ref[sx]
            if online:
                mc = jnp.max(s, axis=0, keepdims=True)
                if cst is not None:
                    mc = mc + cst
                m_old = self.m_ref[sx]
                shift = jnp.maximum(m_old, mc)
                alpha = jnp.exp2(m_old - shift)
                l_old = alpha * l_old
                self.alpha_ref[sx] = alpha
                self.m_ref[sx] = shift
            else:
                shift = self.bound
            if cst is not None:
                shift = shift - cst
            p = jnp.exp2(s - shift)
            self.l_ref[sx] = l_old + jnp.sum(p, axis=0, keepdims=True)
            self.p_ref[half, sx] = p.astype(BF16)

    def run(self, bound, schedule):
        self.bound = bound
        use_bound = bound <= BOUNDED_SOFTMAX_LIMIT
        for online, pred in ((False, use_bound), (True, jnp.logical_not(use_bound))):
            @pl.when(pred)
            def _():
                self.init(online)
                schedule(functools.partial(self.step, online))
                self.pv(online, self.nc - 1, 1)


def _flash_scratch(n_streams, head_dim, blk):
    return [
        pltpu.VMEM((2, n_streams, blk, blk), BF16),
        pltpu.VMEM((n_streams, 1, blk), F32),
        pltpu.VMEM((n_streams, 1, blk), F32),
        pltpu.VMEM((n_streams, 1, blk), F32),
        pltpu.VMEM((n_streams, head_dim, blk), F32),
    ]


def _score_bound(head_dim, g_q, g_k):
    return (head_dim ** 0.5 * LOG2E * 1.02 * jnp.max(jnp.abs(g_q)) * jnp.max(jnp.abs(g_k))).reshape(1)


def _mla_attn_kernel(bound_ref, q_ref, k_ref, v_ref, o_ref, *scratch, blk, q_tiles, group):
    st = _FlashState(q_ref, k_ref, v_ref, scratch, blk, q_tiles)

    def schedule(step):
        def body(cc, carry):
            for u in range(group):
                step(group * cc + u, u % 2)
            return carry

        lax.fori_loop(0, st.nc // group, body, 0)

    st.run(bound_ref[0], schedule)
    for sx, (qt, _) in enumerate(st.streams):
        o_t = st.acc_ref[sx] / st.l_ref[sx]
        o_ref[qt * blk:(qt + 1) * blk, :] = o_t.T.astype(BF16)


def _mla_attention(bound, q, k, v, gb, seq, blk, q_tiles, group):
    nc = seq // blk
    nq = nc // q_tiles
    assert group % 2 == 0 and nc % group == 0 and nc % q_tiles == 0
    kern = functools.partial(_mla_attn_kernel, blk=blk, q_tiles=q_tiles, group=group)
    q = q.reshape(gb, MLA_HEADS, 1, MLA_QK_PAD, seq)
    tq = q_tiles * blk
    return pl.pallas_call(
        kern,
        out_shape=jax.ShapeDtypeStruct((gb * seq, HEAD_OUT), BF16),
        grid=(gb, MLA_HEADS, nq),
        in_specs=[
            pl.BlockSpec(memory_space=pltpu.SMEM),
            pl.BlockSpec((1, 1, 1, MLA_QK_PAD, tq), lambda b, hd, i: (b, hd, 0, 0, i)),
            pl.BlockSpec((1, 1, seq, MLA_QK_PAD), lambda b, hd, i: (b, hd, 0, 0)),
            pl.BlockSpec((1, 1, nc, MLA_V, blk), lambda b, hd, i: (b, hd, 0, 0, 0)),
        ],
        out_specs=pl.BlockSpec((tq, MLA_V), lambda b, hd, i: (b * nq + i, hd)),
        scratch_shapes=_flash_scratch(q_tiles, MLA_V, blk),
        compiler_params=_params(("arbitrary", "arbitrary", "arbitrary")), name="mla_attn",
    )(bound, q, k, v)


def _diff_attn_kernel(bound_ref, q_ref, k_ref, v_ref, bias_ref, rb_ref, lam_ref, g_sub_ref, o_ref,
                      *scratch, blk, q_tiles, group, lam_init):
    hd = pl.program_id(0)
    i_first = pl.program_id(2) * q_tiles
    st = _FlashState(q_ref, k_ref, v_ref, scratch, blk, q_tiles)
    acc_ref, l_ref = st.acc_ref, st.l_ref
    n_groups = st.nc // group

    bias_max = jnp.abs(rb_ref[0, hd])
    for b in range(1, REL_BUCKETS):
        bias_max = jnp.maximum(bias_max, jnp.abs(rb_ref[b, hd]))

    far_before = rb_ref[REL_BUCKETS // 2 - 1, hd] * LOG2E
    far_after = rb_ref[REL_BUCKETS - 1, hd] * LOG2E
    first_near = (i_first - 1 + group) // group - 1
    last_near = (i_first + q_tiles) // group

    near_lo = jnp.maximum(first_near, 0)
    near_hi = jnp.minimum(last_near + 1, n_groups)

    def schedule(step):
        def far_group(cst):
            def body(cc, carry):
                for u in range(group):
                    step(group * cc + u, u % 2, cst=cst)
                return carry
            return body

        def near_group(cc, carry):
            for u in range(group):
                c = group * cc + u
                far = BIAS_TILES // 2
                tiles = [bias_ref[0, jnp.clip(c - (i_first + qt), -far, far) + far]
                         for qt in range(q_tiles)]
                step(c, u % 2, bias_tiles=tiles)
            return carry

        lax.fori_loop(0, near_lo, far_group(far_before), 0)
        lax.fori_loop(near_lo, near_hi, near_group, 0)
        lax.fori_loop(near_hi, n_groups, far_group(far_after), 0)

    st.run(bound_ref[0] + bias_max * LOG2E, schedule)

    lam_v = lam_ref[...]
    lam = (jnp.exp(jnp.sum(lam_v[0:1] * lam_v[1:2], axis=-1, keepdims=True))
           - jnp.exp(jnp.sum(lam_v[2:3] * lam_v[3:4], axis=-1, keepdims=True)) + lam_init)
    for qt in range(q_tiles):
        s0, s1 = 2 * qt, 2 * qt + 1
        o_t = acc_ref[s0] / l_ref[s0] - lam * (acc_ref[s1] / l_ref[s1])
        o_t = o_t * lax.rsqrt(jnp.mean(o_t * o_t, axis=0, keepdims=True) + EPS) * g_sub_ref[...]
        o_ref[qt * blk:(qt + 1) * blk, :] = (o_t * (1.0 - lam_init)).T.astype(BF16)


def _diff_attention(bound, q, k, v, bias, rel_bias, lam_rows, g_sub, gb, seq, blk, q_tiles, group,
                    lam_init):
    nc = seq // blk
    nq = nc // q_tiles
    assert group % 2 == 0 and nc % group == 0 and nc % q_tiles == 0 and blk >= REL_MAX_DIST
    kern = functools.partial(_diff_attn_kernel, blk=blk, q_tiles=q_tiles, group=group,
                             lam_init=lam_init)
    tq = q_tiles * blk
    return pl.pallas_call(
        kern,
        out_shape=jax.ShapeDtypeStruct((gb * seq, HEAD_OUT), BF16),
        grid=(DIFF_HEADS, gb, nq),
        in_specs=[
            pl.BlockSpec(memory_space=pltpu.SMEM),
            pl.BlockSpec((1, 1, 2, DIFF_V, tq), lambda hd, b, i: (b, hd, 0, 0, i)),
            pl.BlockSpec((1, 1, seq, DIFF_V), lambda hd, b, i: (b, hd, 0, 0)),
            pl.BlockSpec((1, 1, nc, DIFF_V, blk), lambda hd, b, i: (b, hd, 0, 0, 0)),
            pl.BlockSpec((1, BIAS_TILES, blk, blk), lambda hd, b, i: (hd, 0, 0, 0),
                         pipeline_mode=pl.Buffered(1)),
            pl.BlockSpec(memory_space=pltpu.SMEM),
            _const_spec((4, LANES)),
            _const_spec((DIFF_V, 1)),
        ],
        out_specs=pl.BlockSpec((tq, DIFF_V), lambda hd, b, i: (b * nq + i, hd)),
        scratch_shapes=_flash_scratch(2 * q_tiles, DIFF_V, blk),
        compiler_params=_params(("arbitrary", "arbitrary", "arbitrary")), name="diff_attn",
    )(bound, q, k, v, bias, rel_bias, lam_rows, g_sub)


def _merge_kernel(h_ref, oa_ref, ob_ref, wga_ref, wgb_ref, wua_ref, wub_ref, out_ref):
    h = h_ref[...]
    ga = jnp.dot(h, wga_ref[...], preferred_element_type=F32)
    gb_ = jnp.dot(h, wgb_ref[...], preferred_element_type=F32)
    ua = jnp.dot(oa_ref[...], wua_ref[...], preferred_element_type=F32)
    ub = jnp.dot(ob_ref[...], wub_ref[...], preferred_element_type=F32)
    out_ref[...] = (jax.nn.sigmoid(ga) * ua + jax.nn.sigmoid(gb_) * ub).astype(BF16)


def _merge(h2d, o_a, o_b, w, tm, tn):
    tokens = h2d.shape[0]
    return pl.pallas_call(
        _merge_kernel,
        out_shape=jax.ShapeDtypeStruct((tokens, D_MODEL), BF16),
        grid=(tokens // tm, D_MODEL // tn),
        in_specs=[
            pl.BlockSpec((tm, D_MODEL), lambda i, j: (i, 0)),
            pl.BlockSpec((tm, HEAD_OUT), lambda i, j: (i, 0)),
            pl.BlockSpec((tm, HEAD_OUT), lambda i, j: (i, 0)),
            pl.BlockSpec((D_MODEL, tn), lambda i, j: (0, j)),
            pl.BlockSpec((D_MODEL, tn), lambda i, j: (0, j)),
            pl.BlockSpec((HEAD_OUT, tn), lambda i, j: (0, j)),
            pl.BlockSpec((HEAD_OUT, tn), lambda i, j: (0, j)),
        ],
        out_specs=pl.BlockSpec((tm, tn), lambda i, j: (i, j)),
        compiler_params=_params(("arbitrary", "arbitrary")), name="gated_merge",
    )(h2d, o_a, o_b, w["w_ga"], w["w_gb"], w["w_ua"], w["w_ub"])


def _out_proj_kernel(x_ref, m_ref, wo_ref, out_ref):
    out_ref[...] = x_ref[...] + jnp.dot(m_ref[...], wo_ref[...], preferred_element_type=F32)


def _out_proj(x2d, merged, w, tm):
    tokens = x2d.shape[0]
    return pl.pallas_call(
        _out_proj_kernel,
        out_shape=jax.ShapeDtypeStruct((tokens, D_MODEL), F32),
        grid=(tokens // tm,),
        in_specs=[
            pl.BlockSpec((tm, D_MODEL), lambda i: (i, 0)),
            pl.BlockSpec((tm, D_MODEL), lambda i: (i, 0)),
            _const_spec((D_MODEL, D_MODEL)),
        ],
        out_specs=pl.BlockSpec((tm, D_MODEL), lambda i: (i, 0)),
        compiler_params=_params(("arbitrary",)), name="out_proj",
    )(x2d, merged, w["w_o"])


def _ffn_kernel(x_ref, g_ref, wg_ref, wu_ref, wd_ref, out_ref, h_ref):
    @pl.when(pl.program_id(1) == 0)
    def _():
        x = x_ref[...]
        h_ref[...] = _rms_rows(x, g_ref[...]).astype(BF16)
        out_ref[...] = x

    h = h_ref[...]
    g = jnp.dot(h, wg_ref[...], preferred_element_type=F32)
    u = jnp.dot(h, wu_ref[...], preferred_element_type=F32)
    a = (g * jax.nn.sigmoid(g) * u).astype(BF16)
    out_ref[...] += jnp.dot(a, wd_ref[...], preferred_element_type=F32)


def _ffn(x2d, w, tm, tf):
    tokens = x2d.shape[0]
    return pl.pallas_call(
        _ffn_kernel,
        out_shape=jax.ShapeDtypeStruct((tokens, D_MODEL), F32),
        grid=(tokens // tm, D_FF // tf),
        in_specs=[
            pl.BlockSpec((tm, D_MODEL), lambda i, j: (i, 0)),
            _const_spec((1, D_MODEL)),
            pl.BlockSpec((D_MODEL, tf), lambda i, j: (0, j)),
            pl.BlockSpec((D_MODEL, tf), lambda i, j: (0, j)),
            pl.BlockSpec((tf, D_MODEL), lambda i, j: (j, 0)),
        ],
        out_specs=pl.BlockSpec((tm, D_MODEL), lambda i, j: (i, 0)),
        scratch_shapes=[pltpu.VMEM((tm, D_MODEL), BF16)],
        compiler_params=_params(("arbitrary", "arbitrary")), name="swiglu_ffn",
    )(x2d, w["g_ffn"], w["w_gate"], w["w_up"], w["w_down"])


def _prepare_weights(mix_norm, w_in, q_a_norm, wq_b, kv_a_norm, wkv_b, mla_q_norm, mla_k_norm,
                     diff_q_norm, diff_k_norm, diff_subln, w_up_mla, w_up_diff, w_o, ffn_norm,
                     w_gate, w_up, w_down, layer):
    win = w_in[layer]
    o_cq, o_ckv, o_kpe = 0, Q_LORA, Q_LORA + KV_LORA
    o_dq = o_kpe + MLA_ROPE
    o_dk = o_dq + HEAD_OUT
    o_dv = o_dk + HEAD_OUT
    o_ga = o_dv + HEAD_OUT
    o_gb = o_ga + D_MODEL
    w_kpe = win[:, o_kpe:o_dq]
    w_kpe_rot = jnp.concatenate([-w_kpe[:, ROPE_HALF:], w_kpe[:, :ROPE_HALF]], axis=1)
    zpad = jnp.zeros((D_MODEL, LANES - MLA_ROPE), F32)
    w_a = jnp.concatenate([win[:, o_cq:o_kpe], w_kpe, zpad, w_kpe_rot, zpad], axis=1)
    wkv = wkv_b[layer].reshape(KV_LORA, MLA_HEADS, MLA_NOPE + MLA_V)
    gk = mla_k_norm[layer]
    gk_rope = gk[MLA_NOPE:]
    lane_pad = jnp.zeros((LANES - MLA_ROPE,), F32)
    return {
        "g_mix": mix_norm[layer][None, :],
        "w_a": w_a.astype(BF16),
        "g_qa": q_a_norm[layer][None, :],
        "g_kva": kv_a_norm[layer][None, :],
        "wq_t": wq_b[layer].T.astype(BF16),
        "g_q": mla_q_norm[layer][:, None],
        "wk_nope": wkv[:, :, :MLA_NOPE].reshape(KV_LORA, MLA_HEADS * MLA_NOPE).astype(BF16),
        "wv_t": wkv[:, :, MLA_NOPE:].reshape(KV_LORA, MLA_HEADS * MLA_V).T.astype(BF16),
        "g_kn": gk[None, :MLA_NOPE],
        "g_kr": jnp.concatenate([gk_rope, lane_pad])[None, :],
        "g_krot": jnp.concatenate([gk_rope[ROPE_HALF:], gk_rope[:ROPE_HALF], lane_pad])[None, :],
        "w_dq_t": win[:, o_dq:o_dk].T.astype(BF16),
        "w_dk_t": win[:, o_dk:o_dv].T.astype(BF16),
        "w_dv_t": win[:, o_dv:o_ga].T.astype(BF16),
        "g_dq": diff_q_norm[layer][:, None],
        "g_dk": diff_k_norm[layer][:, None],
        "g_sub": diff_subln[layer][:, None],
        "w_ga": win[:, o_ga:o_gb].astype(BF16),
        "w_gb": win[:, o_gb:].astype(BF16),
        "w_ua": w_up_mla[layer].astype(BF16),
        "w_ub": w_up_diff[layer].astype(BF16),
        "w_o": w_o[layer].astype(BF16),
        "mla_bound": _score_bound(MLA_QK, mla_q_norm[layer], gk),
        "diff_bound": _score_bound(DIFF_QK, diff_q_norm[layer], diff_k_norm[layer]),
        "g_ffn": ffn_norm[layer][None, :],
        "w_gate": w_gate[layer].astype(BF16),
        "w_up": w_up[layer].astype(BF16),
        "w_down": w_down[layer].astype(BF16),
    }


class _Tiles(NamedTuple):
    tm: int = 512
    tn: int = 1024
    tf: int = 512
    tm_wide: int = 1024
    blk: int = 512
    mla_q_tiles: int = 4
    diff_q_tiles: int = 4
    mla_group_max: int = 4
    diff_group_max: int = 2


def _encoder_layer(x, w, tables, bias, rel_bias, lam_rows, lam_init, t):
    gb, seq, _ = x.shape
    x2d = x.reshape(gb * seq, D_MODEL)
    mla_group = math.gcd(t.mla_group_max, seq // t.blk)
    diff_group = math.gcd(t.diff_group_max, seq // t.blk)
    q_a, k_a, v_a, h2d = _mla_prep(x2d, gb, seq, t.tm, t.blk, w, tables)
    q_d, k_d, v_d = _diff_prep(h2d, gb, seq, t.tm, t.blk, w)
    o_a = _mla_attention(w["mla_bound"], q_a, k_a, v_a, gb, seq, t.blk, t.mla_q_tiles, mla_group)
    o_b = _diff_attention(w["diff_bound"], q_d, k_d, v_d, bias, rel_bias, lam_rows, w["g_sub"], gb,
                          seq, t.blk, t.diff_q_tiles, diff_group, lam_init)
    merged = _merge(h2d, o_a, o_b, w, t.tm_wide, t.tn)
    x1 = _out_proj(x2d, merged, w, t.tm)
    y = _ffn(x1, w, t.tm_wide, t.tf)
    return y.reshape(gb, seq, D_MODEL)


def kernel(x_prompt, x_sample, mix_norm, w_in, q_a_norm, wq_b, kv_a_norm, wkv_b, mla_q_norm, mla_k_norm, diff_q_norm, diff_k_norm, lambda_q1, lambda_k1, lambda_q2, lambda_k2, diff_subln, w_up_mla, w_up_diff, w_o, ffn_norm, w_gate, w_up, w_down, rel_bias):
    t = _Tiles()
    depth = w_in.shape[0]
    max_seq = max(x_prompt.shape[1], x_sample.shape[1])
    tables = _rope_tables(max_seq)
    bias = _bias_tiles(rel_bias, t.blk)
    y_prompt, y_sample = x_prompt, x_sample
    for layer in range(depth):
        w = _prepare_weights(mix_norm, w_in, q_a_norm, wq_b, kv_a_norm, wkv_b, mla_q_norm,
                             mla_k_norm, diff_q_norm, diff_k_norm, diff_subln, w_up_mla,
                             w_up_diff, w_o, ffn_norm, w_gate, w_up, w_down, layer)
        lam_init = 0.8 - 0.6 * math.exp(-0.3 * layer)
        lam_pad = jnp.zeros((LANES - DIFF_QK,), F32)
        lam_rows = jnp.stack([jnp.concatenate([v[layer], lam_pad])
                              for v in (lambda_q1, lambda_k1, lambda_q2, lambda_k2)])
        run = functools.partial(_encoder_layer, w=w, tables=tables, bias=bias, rel_bias=rel_bias,
                                lam_rows=lam_rows, lam_init=lam_init, t=t)
        y_prompt = run(y_prompt)
        y_sample = run(y_sample)
    return (y_prompt, y_sample)
```

```python
import functools
import math
from typing import NamedTuple

import jax
import jax.numpy as jnp
from jax import lax
from jax.experimental import pallas as pl
from jax.experimental.pallas import tpu as pltpu

D_MODEL = 2048
MLA_HEADS = 8
MLA_NOPE = 128
MLA_ROPE = 64
MLA_QK = MLA_NOPE + MLA_ROPE
MLA_V = 128
Q_LORA = 512
KV_LORA = 256
ROPE_THETA = 10000.0
ROPE_HALF = MLA_ROPE // 2
DIFF_HEADS = 8
DIFF_QK = 64
DIFF_V = 2 * DIFF_QK
REL_BUCKETS = 32
REL_MAX_DIST = 128
D_FF = 5632
EPS = 1e-6
HEAD_OUT = MLA_HEADS * MLA_V

LANES = 128
MLA_QK_PAD = 2 * LANES
K_PE_COL = Q_LORA + KV_LORA
W_A_COLS = K_PE_COL + 2 * LANES
ROPE_TABLE_ROWS = 2048
VMEM_LIMIT = 56 * 1024 * 1024

F32 = jnp.float32
BF16 = jnp.bfloat16
NEG_BIG = -1e30
LOG2E = math.log2(math.e)
BOUNDED_SOFTMAX_LIMIT = 50.0
BIAS_TILES = 5

NT_DIMS = (((1,), (1,)), ((), ()))


def _params(semantics):
    return pltpu.CompilerParams(dimension_semantics=semantics, vmem_limit_bytes=VMEM_LIMIT)


def _const_spec(shape):
    zeros = (0,) * len(shape)
    return pl.BlockSpec(shape, lambda *_: zeros)


def _rms_rows(x, gain):
    return x * lax.rsqrt(jnp.mean(x * x, axis=-1, keepdims=True) + EPS) * gain


def _rope_table_kernel(inv_row_ref, inv_col_ref, cos_t_ref, sin_t_ref, cos_f_ref, sin_f_ref):
    ts = cos_t_ref.shape[0]
    base = pl.program_id(0) * ts
    pos_rows = (base + lax.broadcasted_iota(jnp.int32, (ts, LANES), 0)).astype(F32)
    ang_t = pos_rows * inv_row_ref[...]
    cos_t_ref[...] = jnp.cos(ang_t)
    sin_t_ref[...] = jnp.sin(ang_t)
    pos_cols = (base + lax.broadcasted_iota(jnp.int32, (ROPE_HALF, ts), 1)).astype(F32)
    ang_f = pos_cols * inv_col_ref[...]
    cos_f_ref[...] = jnp.cos(ang_f)
    sin_f_ref[...] = jnp.sin(ang_f)


def _rope_tables(seq):
    inv = ROPE_THETA ** (-jnp.arange(ROPE_HALF, dtype=F32) / ROPE_HALF)
    inv_row = jnp.tile(inv, LANES // ROPE_HALF)[None, :]
    inv_col = inv[:, None]
    ts = min(seq, ROPE_TABLE_ROWS)
    tok = jax.ShapeDtypeStruct((seq, LANES), F32)
    feat = jax.ShapeDtypeStruct((ROPE_HALF, seq), F32)
    return pl.pallas_call(
        _rope_table_kernel,
        out_shape=(tok, tok, feat, feat),
        grid=(seq // ts,),
        in_specs=[_const_spec((1, LANES)), _const_spec((ROPE_HALF, 1))],
        out_specs=(
            pl.BlockSpec((ts, LANES), lambda i: (i, 0)),
            pl.BlockSpec((ts, LANES), lambda i: (i, 0)),
            pl.BlockSpec((ROPE_HALF, ts), lambda i: (0, i)),
            pl.BlockSpec((ROPE_HALF, ts), lambda i: (0, i)),
        ),
        compiler_params=_params(("arbitrary",)), name="rope_tables",
    )(inv_row, inv_col)


def _mla_prep_kernel(x_ref, g_mix_ref, w_a_ref, g_qa_ref, g_kva_ref, wq_t_ref, g_q_ref,
                     wk_ref, wv_t_ref, g_kn_ref, g_kr_ref, g_krot_ref,
                     cos_t_ref, sin_t_ref, cos_f_ref, sin_f_ref,
                     q_ref, k_ref, v_ref, h_ref, *, tk):
    tm = x_ref.shape[0]
    h = _rms_rows(x_ref[...], g_mix_ref[...]).astype(BF16)
    h_ref[...] = h
    c = jnp.dot(h, w_a_ref[...], preferred_element_type=F32)
    cq = _rms_rows(c[:, :Q_LORA], g_qa_ref[...]).astype(BF16)
    ckv = _rms_rows(c[:, Q_LORA:Q_LORA + KV_LORA], g_kva_ref[...]).astype(BF16)
    k_pe = c[:, K_PE_COL:K_PE_COL + LANES]
    k_pe_rot = c[:, K_PE_COL + LANES:W_A_COLS]

    q_t = lax.dot_general(wq_t_ref[...], cq, NT_DIMS, preferred_element_type=F32)
    cos_f = cos_f_ref[...]
    sin_f = sin_f_ref[...]
    scale = MLA_QK ** -0.5 * LOG2E
    g_q = g_q_ref[...]
    for hd in range(MLA_HEADS):
        xh = q_t[hd * MLA_QK:(hd + 1) * MLA_QK]
        rinv = lax.rsqrt(jnp.mean(xh * xh, axis=0, keepdims=True) + EPS)
        xn = xh * rinv * g_q
        x1 = xn[MLA_NOPE:MLA_NOPE + ROPE_HALF]
        x2 = xn[MLA_NOPE + ROPE_HALF:]
        q_ref[0, hd, 0:MLA_NOPE, :] = (xn[:MLA_NOPE] * scale).astype(BF16)
        q_ref[0, hd, MLA_NOPE:MLA_NOPE + ROPE_HALF, :] = ((x1 * cos_f - x2 * sin_f) * scale).astype(BF16)
        q_ref[0, hd, MLA_NOPE + ROPE_HALF:MLA_QK, :] = ((x2 * cos_f + x1 * sin_f) * scale).astype(BF16)
        q_ref[0, hd, MLA_QK:, :] = jnp.zeros((MLA_QK_PAD - MLA_QK, tm), BF16)

    k_nope = jnp.dot(ckv, wk_ref[...], preferred_element_type=F32)
    rope_base = (k_pe * g_kr_ref[...]) * cos_t_ref[...] + (k_pe_rot * g_krot_ref[...]) * sin_t_ref[...]
    ss_pe = jnp.sum(k_pe * k_pe, axis=-1, keepdims=True)
    for hd in range(MLA_HEADS):
        kh = k_nope[:, hd * MLA_NOPE:(hd + 1) * MLA_NOPE]
        ss = jnp.sum(kh * kh, axis=-1, keepdims=True) + ss_pe
        rinv = lax.rsqrt(ss * (1.0 / MLA_QK) + EPS)
        k_ref[0, hd, :, 0:LANES] = (kh * rinv * g_kn_ref[...]).astype(BF16)
        k_ref[0, hd, :, LANES:] = (rope_base * rinv).astype(BF16)

    v_t = lax.dot_general(wv_t_ref[...], ckv, NT_DIMS, preferred_element_type=F32)
    for hd in range(MLA_HEADS):
        for cc in range(tm // tk):
            v_ref[0, hd, cc] = v_t[hd * MLA_V:(hd + 1) * MLA_V, cc * tk:(cc + 1) * tk].astype(BF16)


def _mla_prep(x2d, gb, seq, tm, tk, w, tables):
    tokens = x2d.shape[0]
    nt = seq // tm
    cos_t, sin_t, cos_f, sin_f = tables
    kern = functools.partial(_mla_prep_kernel, tk=tk)
    out_shape = (
        jax.ShapeDtypeStruct((gb, MLA_HEADS, MLA_QK_PAD, seq), BF16),
        jax.ShapeDtypeStruct((gb, MLA_HEADS, seq, MLA_QK_PAD), BF16),
        jax.ShapeDtypeStruct((gb, MLA_HEADS, seq // tk, MLA_V, tk), BF16),
        jax.ShapeDtypeStruct((tokens, D_MODEL), BF16),
    )
    in_specs = [
        pl.BlockSpec((tm, D_MODEL), lambda i: (i, 0)),
        _const_spec((1, D_MODEL)),
        _const_spec((D_MODEL, W_A_COLS)),
        _const_spec((1, Q_LORA)),
        _const_spec((1, KV_LORA)),
        _const_spec((MLA_HEADS * MLA_QK, Q_LORA)),
        _const_spec((MLA_QK, 1)),
        _const_spec((KV_LORA, MLA_HEADS * MLA_NOPE)),
        _const_spec((MLA_HEADS * MLA_V, KV_LORA)),
        _const_spec((1, LANES)),
        _const_spec((1, LANES)),
        _const_spec((1, LANES)),
        pl.BlockSpec((tm, LANES), lambda i: (i % nt, 0)),
        pl.BlockSpec((tm, LANES), lambda i: (i % nt, 0)),
        pl.BlockSpec((ROPE_HALF, tm), lambda i: (0, i % nt)),
        pl.BlockSpec((ROPE_HALF, tm), lambda i: (0, i % nt)),
    ]
    out_specs = (
        pl.BlockSpec((1, MLA_HEADS, MLA_QK_PAD, tm), lambda i: (i // nt, 0, 0, i % nt)),
        pl.BlockSpec((1, MLA_HEADS, tm, MLA_QK_PAD), lambda i: (i // nt, 0, i % nt, 0)),
        pl.BlockSpec((1, MLA_HEADS, tm // tk, MLA_V, tk), lambda i: (i // nt, 0, i % nt, 0, 0)),
        pl.BlockSpec((tm, D_MODEL), lambda i: (i, 0)),
    )
    return pl.pallas_call(
        kern, out_shape=out_shape, grid=(tokens // tm,), in_specs=in_specs, out_specs=out_specs,
        compiler_params=_params(("arbitrary",)), name="mla_prep",
    )(x2d, w["g_mix"], w["w_a"], w["g_qa"], w["g_kva"], w["wq_t"], w["g_q"], w["wk_nope"],
      w["wv_t"], w["g_kn"], w["g_kr"], w["g_krot"], cos_t, sin_t, cos_f, sin_f)


def _group_norm_cols(x_t, gain_col):
    rows, tm = x_t.shape
    x3 = x_t.reshape(rows // DIFF_QK, DIFF_QK, tm)
    rinv = lax.rsqrt(jnp.mean(x3 * x3, axis=1, keepdims=True) + EPS)
    return (x3 * rinv * gain_col[None]).reshape(rows, tm)


def _diff_prep_kernel(h_ref, wq_t_ref, wk_t_ref, wv_t_ref, g_q_ref, g_k_ref,
                      q_ref, k_ref, v_ref, *, tk):
    tm = h_ref.shape[0]
    h = h_ref[...]
    scale = DIFF_QK ** -0.5 * LOG2E
    q_t = lax.dot_general(wq_t_ref[...], h, NT_DIMS, preferred_element_type=F32)
    qn = (_group_norm_cols(q_t, g_q_ref[...]) * scale).astype(BF16)
    zeros = jnp.zeros((DIFF_QK, tm), BF16)
    for hd in range(DIFF_HEADS):
        r0 = hd * DIFF_V
        q_ref[0, hd, 0, 0:DIFF_QK, :] = qn[r0:r0 + DIFF_QK]
        q_ref[0, hd, 0, DIFF_QK:, :] = zeros
        q_ref[0, hd, 1, 0:DIFF_QK, :] = zeros
        q_ref[0, hd, 1, DIFF_QK:, :] = qn[r0 + DIFF_QK:r0 + DIFF_V]

    k_t = lax.dot_general(wk_t_ref[...], h, NT_DIMS, preferred_element_type=F32)
    kn = _group_norm_cols(k_t, g_k_ref[...]).T
    for hd in range(DIFF_HEADS):
        k_ref[0, hd] = kn[:, hd * DIFF_V:(hd + 1) * DIFF_V].astype(BF16)

    v_t = lax.dot_general(wv_t_ref[...], h, NT_DIMS, preferred_element_type=F32)
    for hd in range(DIFF_HEADS):
        for cc in range(tm // tk):
            v_ref[0, hd, cc] = v_t[hd * DIFF_V:(hd + 1) * DIFF_V, cc * tk:(cc + 1) * tk].astype(BF16)


def _diff_prep(h2d, gb, seq, tm, tk, w):
    tokens = h2d.shape[0]
    nt = seq // tm
    kern = functools.partial(_diff_prep_kernel, tk=tk)
    out_shape = (
        jax.ShapeDtypeStruct((gb, DIFF_HEADS, 2, DIFF_V, seq), BF16),
        jax.ShapeDtypeStruct((gb, DIFF_HEADS, seq, DIFF_V), BF16),
        jax.ShapeDtypeStruct((gb, DIFF_HEADS, seq // tk, DIFF_V, tk), BF16),
    )
    in_specs = [
        pl.BlockSpec((tm, D_MODEL), lambda i: (i, 0)),
        _const_spec((HEAD_OUT, D_MODEL)),
        _const_spec((HEAD_OUT, D_MODEL)),
        _const_spec((HEAD_OUT, D_MODEL)),
        _const_spec((DIFF_QK, 1)),
        _const_spec((DIFF_QK, 1)),
    ]
    out_specs = (
        pl.BlockSpec((1, DIFF_HEADS, 2, DIFF_V, tm), lambda i: (i // nt, 0, 0, 0, i % nt)),
        pl.BlockSpec((1, DIFF_HEADS, tm, DIFF_V), lambda i: (i // nt, 0, i % nt, 0)),
        pl.BlockSpec((1, DIFF_HEADS, tm // tk, DIFF_V, tk), lambda i: (i // nt, 0, i % nt, 0, 0)),
    )
    return pl.pallas_call(
        kern, out_shape=out_shape, grid=(tokens // tm,), in_specs=in_specs, out_specs=out_specs,
        compiler_params=_params(("arbitrary",)), name="diff_prep",
    )(h2d, w["w_dq_t"], w["w_dk_t"], w["w_dv_t"], w["g_dq"], w["g_dk"])


def _bias_tile_kernel(bucket_ref, rb_ref, out_ref):
    hd = pl.program_id(0)
    offset = pl.program_id(1) - pl.num_programs(1) // 2

    @pl.when(offset <= -2)
    def _():
        out_ref[0, 0] = jnp.full(out_ref.shape[2:], rb_ref[REL_BUCKETS // 2 - 1, hd] * LOG2E, F32)

    @pl.when(offset >= 2)
    def _():
        out_ref[0, 0] = jnp.full(out_ref.shape[2:], rb_ref[REL_BUCKETS - 1, hd] * LOG2E, F32)

    @pl.when(jnp.abs(offset) < 2)
    def _():
        bucket = bucket_ref[0]
        acc = jnp.zeros(bucket.shape, F32)
        for b in range(REL_BUCKETS):
            acc = jnp.where(bucket == b, rb_ref[b, hd], acc)
        out_ref[0, 0] = acc * LOG2E


def _t5_bucket(rel):
    nb = REL_BUCKETS // 2
    ret = jnp.where(rel > 0, nb, 0)
    n = jnp.abs(rel)
    max_exact = nb // 2
    nf = jnp.maximum(n, 1).astype(F32)
    large = max_exact + (jnp.log(nf / max_exact) / math.log(REL_MAX_DIST / max_exact)
                         * (nb - max_exact)).astype(jnp.int32)
    large = jnp.minimum(large, nb - 1)
    return ret + jnp.where(n < max_exact, n, large)


def _bias_tiles(rel_bias, blk):
    assert blk >= REL_MAX_DIST
    n = BIAS_TILES
    d = jnp.arange(blk, dtype=jnp.int32)
    offs = (jnp.arange(n, dtype=jnp.int32) - n // 2) * blk
    rel = offs[:, None, None] + d[None, :, None] - d[None, None, :]
    bucket = _t5_bucket(rel)
    return pl.pallas_call(
        _bias_tile_kernel,
        out_shape=jax.ShapeDtypeStruct((DIFF_HEADS, n, blk, blk), F32),
        grid=(DIFF_HEADS, n),
        in_specs=[
            pl.BlockSpec((1, blk, blk), lambda hd, t: (t, 0, 0)),
            pl.BlockSpec(memory_space=pltpu.SMEM),
        ],
        out_specs=pl.BlockSpec((1, 1, blk, blk), lambda hd, t: (hd, t, 0, 0)),
        compiler_params=_params(("arbitrary", "arbitrary")), name="bias_tiles",
    )(bucket, rel_bias)


class _FlashState:
    def __init__(self, q_ref, k_ref, v_ref, scratch, blk, q_tiles):
        self.q_ref, self.k_ref, self.v_ref = q_ref, k_ref, v_ref
        self.p_ref, self.alpha_ref, self.m_ref, self.l_ref, self.acc_ref = scratch
        self.blk = blk
        self.bound = None
        self.streams = [(qt, mp) for qt in range(q_tiles) for mp in range(q_ref.shape[2])]
        self.kpc = self.p_ref.shape[2] // blk
        self.nc = v_ref.shape[2] // self.kpc

    def q(self, qt, mp):
        return self.q_ref[0, 0, mp, :, qt * self.blk:(qt + 1) * self.blk]

    def key_chunk(self, c):
        rows = self.kpc * self.blk
        return self.k_ref[0, 0, pl.ds(pl.multiple_of(c * rows, rows), rows), :]

    def init(self, online):
        self.l_ref[...] = jnp.zeros(self.l_ref.shape, F32)
        self.acc_ref[...] = jnp.zeros(self.acc_ref.shape, F32)
        self.p_ref[1] = jnp.zeros(self.p_ref.shape[1:], BF16)
        if online:
            self.m_ref[...] = jnp.full(self.m_ref.shape, NEG_BIG, F32)
            self.alpha_ref[...] = jnp.ones(self.alpha_ref.shape, F32)

    def pv(self, online, c, half):
        blk = self.blk
        vcs = [self.v_ref[0, 0, c * self.kpc + j] for j in range(self.kpc)]
        for sx in range(len(self.streams)):
            update = sum(jnp.dot(vc, self.p_ref[half, sx, j * blk:(j + 1) * blk, :],
                                 preferred_element_type=F32) for j, vc in enumerate(vcs))
            if online:
                self.acc_ref[sx] = self.alpha_ref[sx] * self.acc_ref[sx] + update
            else:
                self.acc_ref[sx] += update

    def step(self, online, c, half, cst=None, bias_tiles=None):
        self.pv(online, jnp.maximum(c - 1, 0), 1 - half)
        kc = self.key_chunk(c)
        for sx, (qt, mp) in enumerate(self.streams):
            s = jnp.dot(kc, self.q(qt, mp), preferred_element_type=F32)
            if bias_tiles is not None:
                s = s + bias_tiles[qt]
            l_old = self.l_ref[sx]
            if online:
                mc = jnp.max(s, axis=0, keepdims=True)
                if cst is not None:
                    mc = mc + cst
                m_old = self.m_ref[sx]
                shift = jnp.maximum(m_old, mc)
                alpha = jnp.exp2(m_old - shift)
                l_old = alpha * l_old
                self.alpha_ref[sx] = alpha
                self.m_ref[sx] = shift
            else:
                shift = self.bound
            if cst is not None:
                shift = shift - cst
            p = jnp.exp2(s - shift)
            self.l_ref[sx] = l_old + jnp.sum(p, axis=0, keepdims=True)
            self.p_ref[half, sx] = p.astype(BF16)

    def run(self, bound, schedule):
        self.bound = bound
        use_bound = bound <= BOUNDED_SOFTMAX_LIMIT
        for online, pred in ((False, use_bound), (True, jnp.logical_not(use_bound))):
            @pl.when(pred)
            def _():
                self.init(online)
                schedule(functools.partial(self.step, online))
                self.pv(online, self.nc - 1, 1)


def _flash_scratch(n_streams, head_dim, blk, kpc=1):
    return [
        pltpu.VMEM((2, n_streams, kpc * blk, blk), BF16),
        pltpu.VMEM((n_streams, 1, blk), F32),
        pltpu.VMEM((n_streams, 1, blk), F32),
        pltpu.VMEM((n_streams, 1, blk), F32),
        pltpu.VMEM((n_streams, head_dim, blk), F32),
    ]


def _score_bound(head_dim, g_q, g_k):
    return (head_dim ** 0.5 * LOG2E * 1.02 * jnp.max(jnp.abs(g_q)) * jnp.max(jnp.abs(g_k))).reshape(1)


def _mla_attn_kernel(bound_ref, q_ref, k_ref, v_ref, o_ref, *scratch, blk, q_tiles, group):
    st = _FlashState(q_ref, k_ref, v_ref, scratch, blk, q_tiles)

    def schedule(step):
        def body(cc, carry):
            for u in range(group):
                step(group * cc + u, u % 2)
            return carry

        lax.fori_loop(0, st.nc // group, body, 0)

    st.run(bound_ref[0], schedule)
    for sx, (qt, _) in enumerate(st.streams):
        o_t = st.acc_ref[sx] / st.l_ref[sx]
        o_ref[qt * blk:(qt + 1) * blk, :] = o_t.T.astype(BF16)


def _mla_attention(bound, q, k, v, gb, seq, blk, q_tiles, kpc, group):
    nc = seq // blk
    nq = nc // q_tiles
    assert group % 2 == 0 and nc % (kpc * group) == 0 and nc % q_tiles == 0
    kern = functools.partial(_mla_attn_kernel, blk=blk, q_tiles=q_tiles, group=group)
    q = q.reshape(gb, MLA_HEADS, 1, MLA_QK_PAD, seq)
    tq = q_tiles * blk
    return pl.pallas_call(
        kern,
        out_shape=jax.ShapeDtypeStruct((gb * seq, HEAD_OUT), BF16),
        grid=(gb, MLA_HEADS, nq),
        in_specs=[
            pl.BlockSpec(memory_space=pltpu.SMEM),
            pl.BlockSpec((1, 1, 1, MLA_QK_PAD, tq), lambda b, hd, i: (b, hd, 0, 0, i)),
            pl.BlockSpec((1, 1, seq, MLA_QK_PAD), lambda b, hd, i: (b, hd, 0, 0)),
            pl.BlockSpec((1, 1, nc, MLA_V, blk), lambda b, hd, i: (b, hd, 0, 0, 0)),
        ],
        out_specs=pl.BlockSpec((tq, MLA_V), lambda b, hd, i: (b * nq + i, hd)),
        scratch_shapes=_flash_scratch(q_tiles, MLA_V, blk, kpc),
        compiler_params=_params(("arbitrary", "arbitrary", "arbitrary")), name="mla_attn",
    )(bound, q, k, v)


def _diff_attn_kernel(bound_ref, q_ref, k_ref, v_ref, bias_ref, rb_ref, lam_ref, g_sub_ref, o_ref,
                      *scratch, blk, q_tiles, group, lam_init):
    hd = pl.program_id(0)
    i_first = pl.program_id(2) * q_tiles
    st = _FlashState(q_ref, k_ref, v_ref, scratch, blk, q_tiles)
    acc_ref, l_ref = st.acc_ref, st.l_ref
    n_groups = st.nc // group

    bias_max = jnp.abs(rb_ref[0, hd])
    for b in range(1, REL_BUCKETS):
        bias_max = jnp.maximum(bias_max, jnp.abs(rb_ref[b, hd]))

    far_before = rb_ref[REL_BUCKETS // 2 - 1, hd] * LOG2E
    far_after = rb_ref[REL_BUCKETS - 1, hd] * LOG2E
    first_near = (i_first - 1 + group) // group - 1
    last_near = (i_first + q_tiles) // group

    near_lo = jnp.maximum(first_near, 0)
    near_hi = jnp.minimum(last_near + 1, n_groups)

    def schedule(step):
        def far_group(cst):
            def body(cc, carry):
                for u in range(group):
                    step(group * cc + u, u % 2, cst=cst)
                return carry
            return body

        def near_group(cc, carry):
            for u in range(group):
                c = group * cc + u
                far = BIAS_TILES // 2
                tiles = [bias_ref[0, jnp.clip(c - (i_first + qt), -far, far) + far]
                         for qt in range(q_tiles)]
                step(c, u % 2, bias_tiles=tiles)
            return carry

        lax.fori_loop(0, near_lo, far_group(far_before), 0)
        lax.fori_loop(near_lo, near_hi, near_group, 0)
        lax.fori_loop(near_hi, n_groups, far_group(far_after), 0)

    st.run(bound_ref[0] + bias_max * LOG2E, schedule)

    lam_v = lam_ref[...]
    lam = (jnp.exp(jnp.sum(lam_v[0:1] * lam_v[1:2], axis=-1, keepdims=True))
           - jnp.exp(jnp.sum(lam_v[2:3] * lam_v[3:4], axis=-1, keepdims=True)) + lam_init)
    for qt in range(q_tiles):
        s0, s1 = 2 * qt, 2 * qt + 1
        o_t = acc_ref[s0] / l_ref[s0] - lam * (acc_ref[s1] / l_ref[s1])
        o_t = o_t * lax.rsqrt(jnp.mean(o_t * o_t, axis=0, keepdims=True) + EPS) * g_sub_ref[...]
        o_ref[qt * blk:(qt + 1) * blk, :] = (o_t * (1.0 - lam_init)).T.astype(BF16)


def _diff_attention(bound, q, k, v, bias, rel_bias, lam_rows, g_sub, gb, seq, blk, q_tiles, group,
                    lam_init):
    nc = seq // blk
    nq = nc // q_tiles
    assert group % 2 == 0 and nc % group == 0 and nc % q_tiles == 0 and blk >= REL_MAX_DIST
    kern = functools.partial(_diff_attn_kernel, blk=blk, q_tiles=q_tiles, group=group,
                             lam_init=lam_init)
    tq = q_tiles * blk
    return pl.pallas_call(
        kern,
        out_shape=jax.ShapeDtypeStruct((gb * seq, HEAD_OUT), BF16),
        grid=(DIFF_HEADS, gb, nq),
        in_specs=[
            pl.BlockSpec(memory_space=pltpu.SMEM),
            pl.BlockSpec((1, 1, 2, DIFF_V, tq), lambda hd, b, i: (b, hd, 0, 0, i)),
            pl.BlockSpec((1, 1, seq, DIFF_V), lambda hd, b, i: (b, hd, 0, 0)),
            pl.BlockSpec((1, 1, nc, DIFF_V, blk), lambda hd, b, i: (b, hd, 0, 0, 0)),
            pl.BlockSpec((1, BIAS_TILES, blk, blk), lambda hd, b, i: (hd, 0, 0, 0),
                         pipeline_mode=pl.Buffered(1)),
            pl.BlockSpec(memory_space=pltpu.SMEM),
            _const_spec((4, LANES)),
            _const_spec((DIFF_V, 1)),
        ],
        out_specs=pl.BlockSpec((tq, DIFF_V), lambda hd, b, i: (b * nq + i, hd)),
        scratch_shapes=_flash_scratch(2 * q_tiles, DIFF_V, blk),
        compiler_params=_params(("arbitrary", "arbitrary", "arbitrary")), name="diff_attn",
    )(bound, q, k, v, bias, rel_bias, lam_rows, g_sub)


def _merge_kernel(h_ref, oa_ref, ob_ref, wga_ref, wgb_ref, wua_ref, wub_ref, out_ref):
    h = h_ref[...]
    ga = jnp.dot(h, wga_ref[...], preferred_element_type=F32)
    gb_ = jnp.dot(h, wgb_ref[...], preferred_element_type=F32)
    ua = jnp.dot(oa_ref[...], wua_ref[...], preferred_element_type=F32)
    ub = jnp.dot(ob_ref[...], wub_ref[...], preferred_element_type=F32)
    out_ref[...] = (jax.nn.sigmoid(ga) * ua + jax.nn.sigmoid(gb_) * ub).astype(BF16)


def _merge(h2d, o_a, o_b, w, tm, tn):
    tokens = h2d.shape[0]
    return pl.pallas_call(
        _merge_kernel,
        out_shape=jax.ShapeDtypeStruct((tokens, D_MODEL), BF16),
        grid=(tokens // tm, D_MODEL // tn),
        in_specs=[
            pl.BlockSpec((tm, D_MODEL), lambda i, j: (i, 0)),
            pl.BlockSpec((tm, HEAD_OUT), lambda i, j: (i, 0)),
            pl.BlockSpec((tm, HEAD_OUT), lambda i, j: (i, 0)),
            pl.BlockSpec((D_MODEL, tn), lambda i, j: (0, j)),
            pl.BlockSpec((D_MODEL, tn), lambda i, j: (0, j)),
            pl.BlockSpec((HEAD_OUT, tn), lambda i, j: (0, j)),
            pl.BlockSpec((HEAD_OUT, tn), lambda i, j: (0, j)),
        ],
        out_specs=pl.BlockSpec((tm, tn), lambda i, j: (i, j)),
        compiler_params=_params(("arbitrary", "arbitrary")), name="gated_merge",
    )(h2d, o_a, o_b, w["w_ga"], w["w_gb"], w["w_ua"], w["w_ub"])


def _out_proj_kernel(x_ref, m_ref, wo_ref, out_ref):
    out_ref[...] = x_ref[...] + jnp.dot(m_ref[...], wo_ref[...], preferred_element_type=F32)


def _out_proj(x2d, merged, w, tm):
    tokens = x2d.shape[0]
    return pl.pallas_call(
        _out_proj_kernel,
        out_shape=jax.ShapeDtypeStruct((tokens, D_MODEL), F32),
        grid=(tokens // tm,),
        in_specs=[
            pl.BlockSpec((tm, D_MODEL), lambda i: (i, 0)),
            pl.BlockSpec((tm, D_MODEL), lambda i: (i, 0)),
            _const_spec((D_MODEL, D_MODEL)),
        ],
        out_specs=pl.BlockSpec((tm, D_MODEL), lambda i: (i, 0)),
        compiler_params=_params(("arbitrary",)), name="out_proj",
    )(x2d, merged, w["w_o"])


def _ffn_kernel(x_ref, g_ref, wg_ref, wu_ref, wd_ref, out_ref, h_ref):
    @pl.when(pl.program_id(1) == 0)
    def _():
        x = x_ref[...]
        h_ref[...] = _rms_rows(x, g_ref[...]).astype(BF16)
        out_ref[...] = x

    h = h_ref[...]
    g = jnp.dot(h, wg_ref[...], preferred_element_type=F32)
    u = jnp.dot(h, wu_ref[...], preferred_element_type=F32)
    a = (g * jax.nn.sigmoid(g) * u).astype(BF16)
    out_ref[...] += jnp.dot(a, wd_ref[...], preferred_element_type=F32)


def _ffn(x2d, w, tm, tf):
    tokens = x2d.shape[0]
    return pl.pallas_call(
        _ffn_kernel,
        out_shape=jax.ShapeDtypeStruct((tokens, D_MODEL), F32),
        grid=(tokens // tm, D_FF // tf),
        in_specs=[
            pl.BlockSpec((tm, D_MODEL), lambda i, j: (i, 0)),
            _const_spec((1, D_MODEL)),
            pl.BlockSpec((D_MODEL, tf), lambda i, j: (0, j)),
            pl.BlockSpec((D_MODEL, tf), lambda i, j: (0, j)),
            pl.BlockSpec((tf, D_MODEL), lambda i, j: (j, 0)),
        ],
        out_specs=pl.BlockSpec((tm, D_MODEL), lambda i, j: (i, 0)),
        scratch_shapes=[pltpu.VMEM((tm, D_MODEL), BF16)],
        compiler_params=_params(("arbitrary", "arbitrary")), name="swiglu_ffn",
    )(x2d, w["g_ffn"], w["w_gate"], w["w_up"], w["w_down"])


def _prepare_weights(mix_norm, w_in, q_a_norm, wq_b, kv_a_norm, wkv_b, mla_q_norm, mla_k_norm,
                     diff_q_norm, diff_k_norm, diff_subln, w_up_mla, w_up_diff, w_o, ffn_norm,
                     w_gate, w_up, w_down, layer):
    win = w_in[layer]
    o_cq, o_ckv, o_kpe = 0, Q_LORA, Q_LORA + KV_LORA
    o_dq = o_kpe + MLA_ROPE
    o_dk = o_dq + HEAD_OUT
    o_dv = o_dk + HEAD_OUT
    o_ga = o_dv + HEAD_OUT
    o_gb = o_ga + D_MODEL
    w_kpe = win[:, o_kpe:o_dq]
    w_kpe_rot = jnp.concatenate([-w_kpe[:, ROPE_HALF:], w_kpe[:, :ROPE_HALF]], axis=1)
    zpad = jnp.zeros((D_MODEL, LANES - MLA_ROPE), F32)
    w_a = jnp.concatenate([win[:, o_cq:o_kpe], w_kpe, zpad, w_kpe_rot, zpad], axis=1)
    wkv = wkv_b[layer].reshape(KV_LORA, MLA_HEADS, MLA_NOPE + MLA_V)
    gk = mla_k_norm[layer]
    gk_rope = gk[MLA_NOPE:]
    lane_pad = jnp.zeros((LANES - MLA_ROPE,), F32)
    return {
        "g_mix": mix_norm[layer][None, :],
        "w_a": w_a.astype(BF16),
        "g_qa": q_a_norm[layer][None, :],
        "g_kva": kv_a_norm[layer][None, :],
        "wq_t": wq_b[layer].T.astype(BF16),
        "g_q": mla_q_norm[layer][:, None],
        "wk_nope": wkv[:, :, :MLA_NOPE].reshape(KV_LORA, MLA_HEADS * MLA_NOPE).astype(BF16),
        "wv_t": wkv[:, :, MLA_NOPE:].reshape(KV_LORA, MLA_HEADS * MLA_V).T.astype(BF16),
        "g_kn": gk[None, :MLA_NOPE],
        "g_kr": jnp.concatenate([gk_rope, lane_pad])[None, :],
        "g_krot": jnp.concatenate([gk_rope[ROPE_HALF:], gk_rope[:ROPE_HALF], lane_pad])[None, :],
        "w_dq_t": win[:, o_dq:o_dk].T.astype(BF16),
        "w_dk_t": win[:, o_dk:o_dv].T.astype(BF16),
        "w_dv_t": win[:, o_dv:o_ga].T.astype(BF16),
        "g_dq": diff_q_norm[layer][:, None],
        "g_dk": diff_k_norm[layer][:, None],
        "g_sub": diff_subln[layer][:, None],
        "w_ga": win[:, o_ga:o_gb].astype(BF16),
        "w_gb": win[:, o_gb:].astype(BF16),
        "w_ua": w_up_mla[layer].astype(BF16),
        "w_ub": w_up_diff[layer].astype(BF16),
        "w_o": w_o[layer].astype(BF16),
        "mla_bound": _score_bound(MLA_QK, mla_q_norm[layer], gk),
        "diff_bound": _score_bound(DIFF_QK, diff_q_norm[layer], diff_k_norm[layer]),
        "g_ffn": ffn_norm[layer][None, :],
        "w_gate": w_gate[layer].astype(BF16),
        "w_up": w_up[layer].astype(BF16),
        "w_down": w_down[layer].astype(BF16),
    }


class _Tiles(NamedTuple):
    tm: int = 512
    tn: int = 1024
    tf: int = 512
    tm_wide: int = 1024
    blk: int = 512
    mla_q_tiles: int = 4
    diff_q_tiles: int = 4
    mla_kpc: int = 2
    mla_group_max: int = 2
    diff_group_max: int = 2


def _encoder_layer(x, w, tables, bias, rel_bias, lam_rows, lam_init, t):
    gb, seq, _ = x.shape
    x2d = x.reshape(gb * seq, D_MODEL)
    mla_group = math.gcd(t.mla_group_max, seq // (t.blk * t.mla_kpc))
    diff_group = math.gcd(t.diff_group_max, seq // t.blk)
    q_a, k_a, v_a, h2d = _mla_prep(x2d, gb, seq, t.tm, t.blk, w, tables)
    q_d, k_d, v_d = _diff_prep(h2d, gb, seq, t.tm, t.blk, w)
    o_a = _mla_attention(w["mla_bound"], q_a, k_a, v_a, gb, seq, t.blk, t.mla_q_tiles, t.mla_kpc,
                         mla_group)
    o_b = _diff_attention(w["diff_bound"], q_d, k_d, v_d, bias, rel_bias, lam_rows, w["g_sub"], gb,
                          seq, t.blk, t.diff_q_tiles, diff_group, lam_init)
    merged = _merge(h2d, o_a, o_b, w, t.tm_wide, t.tn)
    x1 = _out_proj(x2d, merged, w, t.tm)
    y = _ffn(x1, w, t.tm_wide, t.tf)
    return y.reshape(gb, seq, D_MODEL)


def kernel(x_prompt, x_sample, mix_norm, w_in, q_a_norm, wq_b, kv_a_norm, wkv_b, mla_q_norm, mla_k_norm, diff_q_norm, diff_k_norm, lambda_q1, lambda_k1, lambda_q2, lambda_k2, diff_subln, w_up_mla, w_up_diff, w_o, ffn_norm, w_gate, w_up, w_down, rel_bias):
    t = _Tiles()
    depth = w_in.shape[0]
    max_seq = max(x_prompt.shape[1], x_sample.shape[1])
    tables = _rope_tables(max_seq)
    bias = _bias_tiles(rel_bias, t.blk)
    y_prompt, y_sample = x_prompt, x_sample
    for layer in range(depth):
        w = _prepare_weights(mix_norm, w_in, q_a_norm, wq_b, kv_a_norm, wkv_b, mla_q_norm,
                             mla_k_norm, diff_q_norm, diff_k_norm, diff_subln, w_up_mla,
                             w_up_diff, w_o, ffn_norm, w_gate, w_up, w_down, layer)
        lam_init = 0.8 - 0.6 * math.exp(-0.3 * layer)
        lam_pad = jnp.zeros((LANES - DIFF_QK,), F32)
        lam_rows = jnp.stack([jnp.concatenate([v[layer], lam_pad])
                              for v in (lambda_q1, lambda_k1, lambda_q2, lambda_k2)])
        run = functools.partial(_encoder_layer, w=w, tables=tables, bias=bias, rel_bias=rel_bias,
                                lam_rows=lam_rows, lam_init=lam_init, t=t)
        y_prompt = run(y_prompt)
        y_sample = run(y_sample)
    return (y_prompt, y_sample)
```

```python
import functools
import math
from typing import NamedTuple

import jax
import jax.numpy as jnp
from jax import lax
from jax.experimental import pallas as pl
from jax.experimental.pallas import tpu as pltpu

D_MODEL = 2048
MLA_HEADS = 8
MLA_NOPE = 128
MLA_ROPE = 64
MLA_QK = MLA_NOPE + MLA_ROPE
MLA_V = 128
Q_LORA = 512
KV_LORA = 256
ROPE_THETA = 10000.0
ROPE_HALF = MLA_ROPE // 2
DIFF_HEADS = 8
DIFF_QK = 64
DIFF_V = 2 * DIFF_QK
REL_BUCKETS = 32
REL_MAX_DIST = 128
D_FF = 5632
EPS = 1e-6
HEAD_OUT = MLA_HEADS * MLA_V

LANES = 128
MLA_QK_PAD = 2 * LANES
K_PE_COL = Q_LORA + KV_LORA
W_A_COLS = K_PE_COL + 2 * LANES
ROPE_TABLE_ROWS = 2048
VMEM_LIMIT = 56 * 1024 * 1024

F32 = jnp.float32
BF16 = jnp.bfloat16
NEG_BIG = -1e30
LOG2E = math.log2(math.e)
BOUNDED_SOFTMAX_LIMIT = 50.0
BIAS_TILES = 5

NT_DIMS = (((1,), (1,)), ((), ()))


def _params(semantics):
    return pltpu.CompilerParams(dimension_semantics=semantics, vmem_limit_bytes=VMEM_LIMIT)


def _const_spec(shape):
    zeros = (0,) * len(shape)
    return pl.BlockSpec(shape, lambda *_: zeros)


def _rms_rows(x, gain):
    return x * lax.rsqrt(jnp.mean(x * x, axis=-1, keepdims=True) + EPS) * gain


def _rope_table_kernel(inv_row_ref, inv_col_ref, cos_t_ref, sin_t_ref, cos_f_ref, sin_f_ref):
    ts = cos_t_ref.shape[0]
    base = pl.program_id(0) * ts
    pos_rows = (base + lax.broadcasted_iota(jnp.int32, (ts, LANES), 0)).astype(F32)
    ang_t = pos_rows * inv_row_ref[...]
    cos_t_ref[...] = jnp.cos(ang_t)
    sin_t_ref[...] = jnp.sin(ang_t)
    pos_cols = (base + lax.broadcasted_iota(jnp.int32, (ROPE_HALF, ts), 1)).astype(F32)
    ang_f = pos_cols * inv_col_ref[...]
    cos_f_ref[...] = jnp.cos(ang_f)
    sin_f_ref[...] = jnp.sin(ang_f)


def _rope_tables(seq):
    inv = ROPE_THETA ** (-jnp.arange(ROPE_HALF, dtype=F32) / ROPE_HALF)
    inv_row = jnp.tile(inv, LANES // ROPE_HALF)[None, :]
    inv_col = inv[:, None]
    ts = min(seq, ROPE_TABLE_ROWS)
    tok = jax.ShapeDtypeStruct((seq, LANES), F32)
    feat = jax.ShapeDtypeStruct((ROPE_HALF, seq), F32)
    return pl.pallas_call(
        _rope_table_kernel,
        out_shape=(tok, tok, feat, feat),
        grid=(seq // ts,),
        in_specs=[_const_spec((1, LANES)), _const_spec((ROPE_HALF, 1))],
        out_specs=(
            pl.BlockSpec((ts, LANES), lambda i: (i, 0)),
            pl.BlockSpec((ts, LANES), lambda i: (i, 0)),
            pl.BlockSpec((ROPE_HALF, ts), lambda i: (0, i)),
            pl.BlockSpec((ROPE_HALF, ts), lambda i: (0, i)),
        ),
        compiler_params=_params(("arbitrary",)), name="rope_tables",
    )(inv_row, inv_col)


def _mla_prep_kernel(x_ref, g_mix_ref, w_a_ref, g_qa_ref, g_kva_ref, wq_t_ref, g_q_ref,
                     wk_ref, wv_t_ref, g_kn_ref, g_kr_ref, g_krot_ref,
                     cos_t_ref, sin_t_ref, cos_f_ref, sin_f_ref,
                     q_ref, k_ref, v_ref, h_ref, *, tk):
    tm = x_ref.shape[0]
    h = _rms_rows(x_ref[...], g_mix_ref[...]).astype(BF16)
    h_ref[...] = h
    c = lax.dot_general(h, w_a_ref[...], NT_DIMS, preferred_element_type=F32)
    cq = _rms_rows(c[:, :Q_LORA], g_qa_ref[...]).astype(BF16)
    ckv = _rms_rows(c[:, Q_LORA:Q_LORA + KV_LORA], g_kva_ref[...]).astype(BF16)
    k_pe = c[:, K_PE_COL:K_PE_COL + LANES]
    k_pe_rot = c[:, K_PE_COL + LANES:W_A_COLS]

    q_t = lax.dot_general(wq_t_ref[...], cq, NT_DIMS, preferred_element_type=F32)
    cos_f = cos_f_ref[...]
    sin_f = sin_f_ref[...]
    scale = MLA_QK ** -0.5 * LOG2E
    g_q = g_q_ref[...]
    for hd in range(MLA_HEADS):
        xh = q_t[hd * MLA_QK:(hd + 1) * MLA_QK]
        rinv = lax.rsqrt(jnp.mean(xh * xh, axis=0, keepdims=True) + EPS)
        xn = xh * rinv * g_q
        x1 = xn[MLA_NOPE:MLA_NOPE + ROPE_HALF]
        x2 = xn[MLA_NOPE + ROPE_HALF:]
        q_ref[0, hd, 0:MLA_NOPE, :] = (xn[:MLA_NOPE] * scale).astype(BF16)
        q_ref[0, hd, MLA_NOPE:MLA_NOPE + ROPE_HALF, :] = ((x1 * cos_f - x2 * sin_f) * scale).astype(BF16)
        q_ref[0, hd, MLA_NOPE + ROPE_HALF:MLA_QK, :] = ((x2 * cos_f + x1 * sin_f) * scale).astype(BF16)
        q_ref[0, hd, MLA_QK:, :] = jnp.zeros((MLA_QK_PAD - MLA_QK, tm), BF16)

    k_nope = jnp.dot(ckv, wk_ref[...], preferred_element_type=F32)
    rope_base = (k_pe * g_kr_ref[...]) * cos_t_ref[...] + (k_pe_rot * g_krot_ref[...]) * sin_t_ref[...]
    ss_pe = jnp.sum(k_pe * k_pe, axis=-1, keepdims=True)
    for hd in range(MLA_HEADS):
        kh = k_nope[:, hd * MLA_NOPE:(hd + 1) * MLA_NOPE]
        ss = jnp.sum(kh * kh, axis=-1, keepdims=True) + ss_pe
        rinv = lax.rsqrt(ss * (1.0 / MLA_QK) + EPS)
        k_ref[0, hd, :, 0:LANES] = (kh * rinv * g_kn_ref[...]).astype(BF16)
        k_ref[0, hd, :, LANES:] = (rope_base * rinv).astype(BF16)

    v_t = lax.dot_general(wv_t_ref[...], ckv, NT_DIMS, preferred_element_type=F32)
    for hd in range(MLA_HEADS):
        for cc in range(tm // tk):
            v_ref[0, hd, cc] = v_t[hd * MLA_V:(hd + 1) * MLA_V, cc * tk:(cc + 1) * tk].astype(BF16)


def _mla_prep(x2d, gb, seq, tm, tk, w, tables):
    tokens = x2d.shape[0]
    nt = seq // tm
    cos_t, sin_t, cos_f, sin_f = tables
    kern = functools.partial(_mla_prep_kernel, tk=tk)
    out_shape = (
        jax.ShapeDtypeStruct((gb, MLA_HEADS, MLA_QK_PAD, seq), BF16),
        jax.ShapeDtypeStruct((gb, MLA_HEADS, seq, MLA_QK_PAD), BF16),
        jax.ShapeDtypeStruct((gb, MLA_HEADS, seq // tk, MLA_V, tk), BF16),
        jax.ShapeDtypeStruct((tokens, D_MODEL), BF16),
    )
    in_specs = [
        pl.BlockSpec((tm, D_MODEL), lambda i: (i, 0)),
        _const_spec((1, D_MODEL)),
        _const_spec((W_A_COLS, D_MODEL)),
        _const_spec((1, Q_LORA)),
        _const_spec((1, KV_LORA)),
        _const_spec((MLA_HEADS * MLA_QK, Q_LORA)),
        _const_spec((MLA_QK, 1)),
        _const_spec((KV_LORA, MLA_HEADS * MLA_NOPE)),
        _const_spec((MLA_HEADS * MLA_V, KV_LORA)),
        _const_spec((1, LANES)),
        _const_spec((1, LANES)),
        _const_spec((1, LANES)),
        pl.BlockSpec((tm, LANES), lambda i: (i % nt, 0)),
        pl.BlockSpec((tm, LANES), lambda i: (i % nt, 0)),
        pl.BlockSpec((ROPE_HALF, tm), lambda i: (0, i % nt)),
        pl.BlockSpec((ROPE_HALF, tm), lambda i: (0, i % nt)),
    ]
    out_specs = (
        pl.BlockSpec((1, MLA_HEADS, MLA_QK_PAD, tm), lambda i: (i // nt, 0, 0, i % nt)),
        pl.BlockSpec((1, MLA_HEADS, tm, MLA_QK_PAD), lambda i: (i // nt, 0, i % nt, 0)),
        pl.BlockSpec((1, MLA_HEADS, tm // tk, MLA_V, tk), lambda i: (i // nt, 0, i % nt, 0, 0)),
        pl.BlockSpec((tm, D_MODEL), lambda i: (i, 0)),
    )
    return pl.pallas_call(
        kern, out_shape=out_shape, grid=(tokens // tm,), in_specs=in_specs, out_specs=out_specs,
        compiler_params=_params(("arbitrary",)), name="mla_prep",
    )(x2d, w["g_mix"], w["w_a_t"], w["g_qa"], w["g_kva"], w["wq_t"], w["g_q"], w["wk_nope"],
      w["wv_t"], w["g_kn"], w["g_kr"], w["g_krot"], cos_t, sin_t, cos_f, sin_f)


def _group_norm_cols(x_t, gain_col):
    rows, tm = x_t.shape
    x3 = x_t.reshape(rows // DIFF_QK, DIFF_QK, tm)
    rinv = lax.rsqrt(jnp.mean(x3 * x3, axis=1, keepdims=True) + EPS)
    return (x3 * rinv * gain_col[None]).reshape(rows, tm)


def _diff_prep_kernel(h_ref, wq_t_ref, wk_t_ref, wv_t_ref, g_q_ref, g_k_ref,
                      q_ref, k_ref, v_ref, *, tk):
    tm = h_ref.shape[0]
    h = h_ref[...]
    scale = DIFF_QK ** -0.5 * LOG2E
    q_t = lax.dot_general(wq_t_ref[...], h, NT_DIMS, preferred_element_type=F32)
    qn = (_group_norm_cols(q_t, g_q_ref[...]) * scale).astype(BF16)
    zeros = jnp.zeros((DIFF_QK, tm), BF16)
    for hd in range(DIFF_HEADS):
        r0 = hd * DIFF_V
        q_ref[0, hd, 0, 0:DIFF_QK, :] = qn[r0:r0 + DIFF_QK]
        q_ref[0, hd, 0, DIFF_QK:, :] = zeros
        q_ref[0, hd, 1, 0:DIFF_QK, :] = zeros
        q_ref[0, hd, 1, DIFF_QK:, :] = qn[r0 + DIFF_QK:r0 + DIFF_V]

    k_t = lax.dot_general(wk_t_ref[...], h, NT_DIMS, preferred_element_type=F32)
    kn = _group_norm_cols(k_t, g_k_ref[...]).T
    for hd in range(DIFF_HEADS):
        k_ref[0, hd] = kn[:, hd * DIFF_V:(hd + 1) * DIFF_V].astype(BF16)

    v_t = lax.dot_general(wv_t_ref[...], h, NT_DIMS, preferred_element_type=F32)
    for hd in range(DIFF_HEADS):
        for cc in range(tm // tk):
            v_ref[0, hd, cc] = v_t[hd * DIFF_V:(hd + 1) * DIFF_V, cc * tk:(cc + 1) * tk].astype(BF16)


def _diff_prep(h2d, gb, seq, tm, tk, w):
    tokens = h2d.shape[0]
    nt = seq // tm
    kern = functools.partial(_diff_prep_kernel, tk=tk)
    out_shape = (
        jax.ShapeDtypeStruct((gb, DIFF_HEADS, 2, DIFF_V, seq), BF16),
        jax.ShapeDtypeStruct((gb, DIFF_HEADS, seq, DIFF_V), BF16),
        jax.ShapeDtypeStruct((gb, DIFF_HEADS, seq // tk, DIFF_V, tk), BF16),
    )
    in_specs = [
        pl.BlockSpec((tm, D_MODEL), lambda i: (i, 0)),
        _const_spec((HEAD_OUT, D_MODEL)),
        _const_spec((HEAD_OUT, D_MODEL)),
        _const_spec((HEAD_OUT, D_MODEL)),
        _const_spec((DIFF_QK, 1)),
        _const_spec((DIFF_QK, 1)),
    ]
    out_specs = (
        pl.BlockSpec((1, DIFF_HEADS, 2, DIFF_V, tm), lambda i: (i // nt, 0, 0, 0, i % nt)),
        pl.BlockSpec((1, DIFF_HEADS, tm, DIFF_V), lambda i: (i // nt, 0, i % nt, 0)),
        pl.BlockSpec((1, DIFF_HEADS, tm // tk, DIFF_V, tk), lambda i: (i // nt, 0, i % nt, 0, 0)),
    )
    return pl.pallas_call(
        kern, out_shape=out_shape, grid=(tokens // tm,), in_specs=in_specs, out_specs=out_specs,
        compiler_params=_params(("arbitrary",)), name="diff_prep",
    )(h2d, w["w_dq_t"], w["w_dk_t"], w["w_dv_t"], w["g_dq"], w["g_dk"])


def _bias_tile_kernel(bucket_ref, rb_ref, out_ref):
    hd = pl.program_id(0)
    offset = pl.program_id(1) - pl.num_programs(1) // 2

    @pl.when(offset <= -2)
    def _():
        out_ref[0, 0] = jnp.full(out_ref.shape[2:], rb_ref[REL_BUCKETS // 2 - 1, hd] * LOG2E, F32)

    @pl.when(offset >= 2)
    def _():
        out_ref[0, 0] = jnp.full(out_ref.shape[2:], rb_ref[REL_BUCKETS - 1, hd] * LOG2E, F32)

    @pl.when(jnp.abs(offset) < 2)
    def _():
        bucket = bucket_ref[0]
        acc = jnp.zeros(bucket.shape, F32)
        for b in range(REL_BUCKETS):
            acc = jnp.where(bucket == b, rb_ref[b, hd], acc)
        out_ref[0, 0] = acc * LOG2E


def _t5_bucket(rel):
    nb = REL_BUCKETS // 2
    ret = jnp.where(rel > 0, nb, 0)
    n = jnp.abs(rel)
    max_exact = nb // 2
    nf = jnp.maximum(n, 1).astype(F32)
    large = max_exact + (jnp.log(nf / max_exact) / math.log(REL_MAX_DIST / max_exact)
                         * (nb - max_exact)).astype(jnp.int32)
    large = jnp.minimum(large, nb - 1)
    return ret + jnp.where(n < max_exact, n, large)


def _bias_tiles(rel_bias, blk):
    assert blk >= REL_MAX_DIST
    n = BIAS_TILES
    d = jnp.arange(blk, dtype=jnp.int32)
    offs = (jnp.arange(n, dtype=jnp.int32) - n // 2) * blk
    rel = offs[:, None, None] + d[None, :, None] - d[None, None, :]
    bucket = _t5_bucket(rel)
    return pl.pallas_call(
        _bias_tile_kernel,
        out_shape=jax.ShapeDtypeStruct((DIFF_HEADS, n, blk, blk), F32),
        grid=(DIFF_HEADS, n),
        in_specs=[
            pl.BlockSpec((1, blk, blk), lambda hd, t: (t, 0, 0)),
            pl.BlockSpec(memory_space=pltpu.SMEM),
        ],
        out_specs=pl.BlockSpec((1, 1, blk, blk), lambda hd, t: (hd, t, 0, 0)),
        compiler_params=_params(("arbitrary", "arbitrary")), name="bias_tiles",
    )(bucket, rel_bias)


class _FlashState:
    def __init__(self, q_ref, k_ref, v_ref, scratch, blk, q_tiles):
        self.q_ref, self.k_ref, self.v_ref = q_ref, k_ref, v_ref
        self.p_ref, self.alpha_ref, self.m_ref, self.l_ref, self.acc_ref = scratch
        self.blk = blk
        self.bound = None
        self.streams = [(qt, mp) for qt in range(q_tiles) for mp in range(q_ref.shape[2])]
        self.nc = v_ref.shape[2]

    def q(self, qt, mp):
        return self.q_ref[0, 0, mp, :, qt * self.blk:(qt + 1) * self.blk]

    def key_chunk(self, c):
        return self.k_ref[0, 0, pl.ds(pl.multiple_of(c * self.blk, self.blk), self.blk), :]

    def init(self, online):
        self.l_ref[...] = jnp.zeros(self.l_ref.shape, F32)
        self.acc_ref[...] = jnp.zeros(self.acc_ref.shape, F32)
        self.p_ref[1] = jnp.zeros(self.p_ref.shape[1:], BF16)
        if online:
            self.m_ref[...] = jnp.full(self.m_ref.shape, NEG_BIG, F32)
            self.alpha_ref[...] = jnp.ones(self.alpha_ref.shape, F32)

    def pv(self, online, c, half):
        vc = self.v_ref[0, 0, c]
        for sx in range(len(self.streams)):
            update = jnp.dot(vc, self.p_ref[half, sx], preferred_element_type=F32)
            if online:
                self.acc_ref[sx] = self.alpha_ref[sx] * self.acc_ref[sx] + update
            else:
                self.acc_ref[sx] += update

    def step(self, online, c, half, cst=None, bias_tiles=None):
        self.pv(online, jnp.maximum(c - 1, 0), 1 - half)
        kc = self.key_chunk(c)
        for sx, (qt, mp) in enumerate(self.streams):
            s = jnp.dot(kc, self.q(qt, mp), preferred_element_type=F32)
            if bias_tiles is not None:
                s = s + bias_tiles[qt]
            l_old = self.l_ref[sx]
            if online:
                mc = jnp.max(s, axis=0, keepdims=True)
                if cst is not None:
                    mc = mc + cst
                m_old = self.m_ref[sx]
                shift = jnp.maximum(m_old, mc)
                alpha = jnp.exp2(m_old - shift)
                l_old = alpha * l_old
                self.alpha_ref[sx] = alpha
                self.m_ref[sx] = shift
            else:
                shift = self.bound
            if cst is not None:
                shift = shift - cst
            p = jnp.exp2(s - shift)
            self.l_ref[sx] = l_old + jnp.sum(p, axis=0, keepdims=True)
            self.p_ref[half, sx] = p.astype(BF16)

    def run(self, bound, schedule):
        self.bound = bound
        use_bound = bound <= BOUNDED_SOFTMAX_LIMIT
        for online, pred in ((False, use_bound), (True, jnp.logical_not(use_bound))):
            @pl.when(pred)
            def _():
                self.init(online)
                schedule(functools.partial(self.step, online))
                self.pv(online, self.nc - 1, 1)


def _flash_scratch(n_streams, head_dim, blk):
    return [
        pltpu.VMEM((2, n_streams, blk, blk), BF16),
        pltpu.VMEM((n_streams, 1, blk), F32),
        pltpu.VMEM((n_streams, 1, blk), F32),
        pltpu.VMEM((n_streams, 1, blk), F32),
        pltpu.VMEM((n_streams, head_dim, blk), F32),
    ]


def _score_bound(head_dim, g_q, g_k):
    return (head_dim ** 0.5 * LOG2E * 1.02 * jnp.max(jnp.abs(g_q)) * jnp.max(jnp.abs(g_k))).reshape(1)


def _mla_attn_kernel(bound_ref, q_ref, k_ref, v_ref, o_ref, *scratch, blk, q_tiles, group):
    st = _FlashState(q_ref, k_ref, v_ref, scratch, blk, q_tiles)

    def schedule(step):
        def body(cc, carry):
            for u in range(group):
                step(group * cc + u, u % 2)
            return carry

        lax.fori_loop(0, st.nc // group, body, 0)

    st.run(bound_ref[0], schedule)
    for sx, (qt, _) in enumerate(st.streams):
        o_t = st.acc_ref[sx] / st.l_ref[sx]
        o_ref[qt * blk:(qt + 1) * blk, :] = o_t.T.astype(BF16)


def _mla_attention(bound, q, k, v, gb, seq, blk, q_tiles, group):
    nc = seq // blk
    nq = nc // q_tiles
    assert group % 2 == 0 and nc % group == 0 and nc % q_tiles == 0
    kern = functools.partial(_mla_attn_kernel, blk=blk, q_tiles=q_tiles, group=group)
    q = q.reshape(gb, MLA_HEADS, 1, MLA_QK_PAD, seq)
    tq = q_tiles * blk
    return pl.pallas_call(
        kern,
        out_shape=jax.ShapeDtypeStruct((gb * seq, HEAD_OUT), BF16),
        grid=(gb, MLA_HEADS, nq),
        in_specs=[
            pl.BlockSpec(memory_space=pltpu.SMEM),
            pl.BlockSpec((1, 1, 1, MLA_QK_PAD, tq), lambda b, hd, i: (b, hd, 0, 0, i)),
            pl.BlockSpec((1, 1, seq, MLA_QK_PAD), lambda b, hd, i: (b, hd, 0, 0)),
            pl.BlockSpec((1, 1, nc, MLA_V, blk), lambda b, hd, i: (b, hd, 0, 0, 0)),
        ],
        out_specs=pl.BlockSpec((tq, MLA_V), lambda b, hd, i: (b * nq + i, hd)),
        scratch_shapes=_flash_scratch(q_tiles, MLA_V, blk),
        compiler_params=_params(("arbitrary", "arbitrary", "arbitrary")), name="mla_attn",
    )(bound, q, k, v)


def _diff_attn_kernel(bound_ref, q_ref, k_ref, v_ref, bias_ref, rb_ref, lam_ref, g_sub_ref, o_ref,
                      *scratch, blk, q_tiles, group, lam_init):
    hd = pl.program_id(0)
    i_first = pl.program_id(2) * q_tiles
    st = _FlashState(q_ref, k_ref, v_ref, scratch, blk, q_tiles)
    acc_ref, l_ref = st.acc_ref, st.l_ref
    n_groups = st.nc // group

    bias_max = jnp.abs(rb_ref[0, hd])
    for b in range(1, REL_BUCKETS):
        bias_max = jnp.maximum(bias_max, jnp.abs(rb_ref[b, hd]))

    far_before = rb_ref[REL_BUCKETS // 2 - 1, hd] * LOG2E
    far_after = rb_ref[REL_BUCKETS - 1, hd] * LOG2E
    first_near = (i_first - 1 + group) // group - 1
    last_near = (i_first + q_tiles) // group

    near_lo = jnp.maximum(first_near, 0)
    near_hi = jnp.minimum(last_near + 1, n_groups)

    def schedule(step):
        def far_group(cst):
            def body(cc, carry):
                for u in range(group):
                    step(group * cc + u, u % 2, cst=cst)
                return carry
            return body

        def near_group(cc, carry):
            for u in range(group):
                c = group * cc + u
                far = BIAS_TILES // 2
                tiles = [bias_ref[0, jnp.clip(c - (i_first + qt), -far, far) + far]
                         for qt in range(q_tiles)]
                step(c, u % 2, bias_tiles=tiles)
            return carry

        lax.fori_loop(0, near_lo, far_group(far_before), 0)
        lax.fori_loop(near_lo, near_hi, near_group, 0)
        lax.fori_loop(near_hi, n_groups, far_group(far_after), 0)

    st.run(bound_ref[0] + bias_max * LOG2E, schedule)

    lam_v = lam_ref[...]
    lam = (jnp.exp(jnp.sum(lam_v[0:1] * lam_v[1:2], axis=-1, keepdims=True))
           - jnp.exp(jnp.sum(lam_v[2:3] * lam_v[3:4], axis=-1, keepdims=True)) + lam_init)
    for qt in range(q_tiles):
        s0, s1 = 2 * qt, 2 * qt + 1
        o_t = acc_ref[s0] / l_ref[s0] - lam * (acc_ref[s1] / l_ref[s1])
        o_t = o_t * lax.rsqrt(jnp.mean(o_t * o_t, axis=0, keepdims=True) + EPS) * g_sub_ref[...]
        o_ref[qt * blk:(qt + 1) * blk, :] = (o_t * (1.0 - lam_init)).T.astype(BF16)


def _diff_attention(bound, q, k, v, bias, rel_bias, lam_rows, g_sub, gb, seq, blk, q_tiles, group,
                    lam_init):
    nc = seq // blk
    nq = nc // q_tiles
    assert group % 2 == 0 and nc % group == 0 and nc % q_tiles == 0 and blk >= REL_MAX_DIST
    kern = functools.partial(_diff_attn_kernel, blk=blk, q_tiles=q_tiles, group=group,
                             lam_init=lam_init)
    tq = q_tiles * blk
    return pl.pallas_call(
        kern,
        out_shape=jax.ShapeDtypeStruct((gb * seq, HEAD_OUT), BF16),
        grid=(DIFF_HEADS, gb, nq),
        in_specs=[
            pl.BlockSpec(memory_space=pltpu.SMEM),
            pl.BlockSpec((1, 1, 2, DIFF_V, tq), lambda hd, b, i: (b, hd, 0, 0, i)),
            pl.BlockSpec((1, 1, seq, DIFF_V), lambda hd, b, i: (b, hd, 0, 0)),
            pl.BlockSpec((1, 1, nc, DIFF_V, blk), lambda hd, b, i: (b, hd, 0, 0, 0)),
            pl.BlockSpec((1, BIAS_TILES, blk, blk), lambda hd, b, i: (hd, 0, 0, 0),
                         pipeline_mode=pl.Buffered(1)),
            pl.BlockSpec(memory_space=pltpu.SMEM),
            _const_spec((4, LANES)),
            _const_spec((DIFF_V, 1)),
        ],
        out_specs=pl.BlockSpec((tq, DIFF_V), lambda hd, b, i: (b * nq + i, hd)),
        scratch_shapes=_flash_scratch(2 * q_tiles, DIFF_V, blk),
        compiler_params=_params(("arbitrary", "arbitrary", "arbitrary")), name="diff_attn",
    )(bound, q, k, v, bias, rel_bias, lam_rows, g_sub)


def _merge_kernel(h_ref, oa_ref, ob_ref, wga_ref, wgb_ref, wua_ref, wub_ref, out_ref):
    h = h_ref[...]
    ga = lax.dot_general(h, wga_ref[...], NT_DIMS, preferred_element_type=F32)
    gb_ = lax.dot_general(h, wgb_ref[...], NT_DIMS, preferred_element_type=F32)
    ua = jnp.dot(oa_ref[...], wua_ref[...], preferred_element_type=F32)
    ub = jnp.dot(ob_ref[...], wub_ref[...], preferred_element_type=F32)
    out_ref[...] = (jax.nn.sigmoid(ga) * ua + jax.nn.sigmoid(gb_) * ub).astype(BF16)


def _merge(h2d, o_a, o_b, w, tm, tn):
    tokens = h2d.shape[0]
    return pl.pallas_call(
        _merge_kernel,
        out_shape=jax.ShapeDtypeStruct((tokens, D_MODEL), BF16),
        grid=(tokens // tm, D_MODEL // tn),
        in_specs=[
            pl.BlockSpec((tm, D_MODEL), lambda i, j: (i, 0)),
            pl.BlockSpec((tm, HEAD_OUT), lambda i, j: (i, 0)),
            pl.BlockSpec((tm, HEAD_OUT), lambda i, j: (i, 0)),
            pl.BlockSpec((tn, D_MODEL), lambda i, j: (j, 0)),
            pl.BlockSpec((tn, D_MODEL), lambda i, j: (j, 0)),
            pl.BlockSpec((HEAD_OUT, tn), lambda i, j: (0, j)),
            pl.BlockSpec((HEAD_OUT, tn), lambda i, j: (0, j)),
        ],
        out_specs=pl.BlockSpec((tm, tn), lambda i, j: (i, j)),
        compiler_params=_params(("arbitrary", "arbitrary")), name="gated_merge",
    )(h2d, o_a, o_b, w["w_ga_t"], w["w_gb_t"], w["w_ua"], w["w_ub"])


def _out_proj_kernel(x_ref, m_ref, wo_ref, out_ref):
    out_ref[...] = x_ref[...] + jnp.dot(m_ref[...], wo_ref[...], preferred_element_type=F32)


def _out_proj(x2d, merged, w, tm):
    tokens = x2d.shape[0]
    return pl.pallas_call(
        _out_proj_kernel,
        out_shape=jax.ShapeDtypeStruct((tokens, D_MODEL), F32),
        grid=(tokens // tm,),
        in_specs=[
            pl.BlockSpec((tm, D_MODEL), lambda i: (i, 0)),
            pl.BlockSpec((tm, D_MODEL), lambda i: (i, 0)),
            _const_spec((D_MODEL, D_MODEL)),
        ],
        out_specs=pl.BlockSpec((tm, D_MODEL), lambda i: (i, 0)),
        compiler_params=_params(("arbitrary",)), name="out_proj",
    )(x2d, merged, w["w_o"])


def _ffn_kernel(x_ref, g_ref, wg_ref, wu_ref, wd_ref, out_ref, h_ref):
    @pl.when(pl.program_id(1) == 0)
    def _():
        x = x_ref[...]
        h_ref[...] = _rms_rows(x, g_ref[...]).astype(BF16)
        out_ref[...] = x

    h = h_ref[...]
    g = jnp.dot(h, wg_ref[...], preferred_element_type=F32)
    u = jnp.dot(h, wu_ref[...], preferred_element_type=F32)
    a = (g * jax.nn.sigmoid(g) * u).astype(BF16)
    out_ref[...] += jnp.dot(a, wd_ref[...], preferred_element_type=F32)


def _ffn(x2d, w, tm, tf):
    tokens = x2d.shape[0]
    return pl.pallas_call(
        _ffn_kernel,
        out_shape=jax.ShapeDtypeStruct((tokens, D_MODEL), F32),
        grid=(tokens // tm, D_FF // tf),
        in_specs=[
            pl.BlockSpec((tm, D_MODEL), lambda i, j: (i, 0)),
            _const_spec((1, D_MODEL)),
            pl.BlockSpec((D_MODEL, tf), lambda i, j: (0, j)),
            pl.BlockSpec((D_MODEL, tf), lambda i, j: (0, j)),
            pl.BlockSpec((tf, D_MODEL), lambda i, j: (j, 0)),
        ],
        out_specs=pl.BlockSpec((tm, D_MODEL), lambda i, j: (i, 0)),
        scratch_shapes=[pltpu.VMEM((tm, D_MODEL), BF16)],
        compiler_params=_params(("arbitrary", "arbitrary")), name="swiglu_ffn",
    )(x2d, w["g_ffn"], w["w_gate"], w["w_up"], w["w_down"])


def _prepare_weights(mix_norm, w_in, q_a_norm, wq_b, kv_a_norm, wkv_b, mla_q_norm, mla_k_norm,
                     diff_q_norm, diff_k_norm, diff_subln, w_up_mla, w_up_diff, w_o, ffn_norm,
                     w_gate, w_up, w_down, layer):
    win_t = w_in[layer].T
    o_kpe = K_PE_COL
    o_dq = o_kpe + MLA_ROPE
    o_dk = o_dq + HEAD_OUT
    o_dv = o_dk + HEAD_OUT
    o_ga = o_dv + HEAD_OUT
    o_gb = o_ga + D_MODEL
    w_kpe_t = win_t[o_kpe:o_dq]
    w_kpe_rot_t = jnp.concatenate([-w_kpe_t[ROPE_HALF:], w_kpe_t[:ROPE_HALF]], axis=0)
    zpad = jnp.zeros((LANES - MLA_ROPE, D_MODEL), F32)
    w_a_t = jnp.concatenate([win_t[:o_kpe], w_kpe_t, zpad, w_kpe_rot_t, zpad], axis=0)
    wkv = wkv_b[layer].reshape(KV_LORA, MLA_HEADS, MLA_NOPE + MLA_V)
    gk = mla_k_norm[layer]
    gk_rope = gk[MLA_NOPE:]
    lane_pad = jnp.zeros((LANES - MLA_ROPE,), F32)
    return {
        "g_mix": mix_norm[layer][None, :],
        "w_a_t": w_a_t.astype(BF16),
        "g_qa": q_a_norm[layer][None, :],
        "g_kva": kv_a_norm[layer][None, :],
        "wq_t": wq_b[layer].T.astype(BF16),
        "g_q": mla_q_norm[layer][:, None],
        "wk_nope": wkv[:, :, :MLA_NOPE].reshape(KV_LORA, MLA_HEADS * MLA_NOPE).astype(BF16),
        "wv_t": wkv[:, :, MLA_NOPE:].reshape(KV_LORA, MLA_HEADS * MLA_V).T.astype(BF16),
        "g_kn": gk[None, :MLA_NOPE],
        "g_kr": jnp.concatenate([gk_rope, lane_pad])[None, :],
        "g_krot": jnp.concatenate([gk_rope[ROPE_HALF:], gk_rope[:ROPE_HALF], lane_pad])[None, :],
        "w_dq_t": win_t[o_dq:o_dk].astype(BF16),
        "w_dk_t": win_t[o_dk:o_dv].astype(BF16),
        "w_dv_t": win_t[o_dv:o_ga].astype(BF16),
        "g_dq": diff_q_norm[layer][:, None],
        "g_dk": diff_k_norm[layer][:, None],
        "g_sub": diff_subln[layer][:, None],
        "w_ga_t": win_t[o_ga:o_gb].astype(BF16),
        "w_gb_t": win_t[o_gb:].astype(BF16),
        "w_ua": w_up_mla[layer].astype(BF16),
        "w_ub": w_up_diff[layer].astype(BF16),
        "w_o": w_o[layer].astype(BF16),
        "mla_bound": _score_bound(MLA_QK, mla_q_norm[layer], gk),
        "diff_bound": _score_bound(DIFF_QK, diff_q_norm[layer], diff_k_norm[layer]),
        "g_ffn": ffn_norm[layer][None, :],
        "w_gate": w_gate[layer].astype(BF16),
        "w_up": w_up[layer].astype(BF16),
        "w_down": w_down[layer].astype(BF16),
    }


class _Tiles(NamedTuple):
    tm: int = 512
    tn: int = 1024
    tf: int = 512
    tm_wide: int = 1024
    blk: int = 512
    mla_q_tiles: int = 4
    diff_q_tiles: int = 4
    mla_group_max: int = 4
    diff_group_max: int = 2


def _encoder_layer(x, w, tables, bias, rel_bias, lam_rows, lam_init, t):
    gb, seq, _ = x.shape
    x2d = x.reshape(gb * seq, D_MODEL)
    mla_group = math.gcd(t.mla_group_max, seq // t.blk)
    diff_group = math.gcd(t.diff_group_max, seq // t.blk)
    q_a, k_a, v_a, h2d = _mla_prep(x2d, gb, seq, t.tm, t.blk, w, tables)
    q_d, k_d, v_d = _diff_prep(h2d, gb, seq, t.tm, t.blk, w)
    o_a = _mla_attention(w["mla_bound"], q_a, k_a, v_a, gb, seq, t.blk, t.mla_q_tiles, mla_group)
    o_b = _diff_attention(w["diff_bound"], q_d, k_d, v_d, bias, rel_bias, lam_rows, w["g_sub"], gb,
                          seq, t.blk, t.diff_q_tiles, diff_group, lam_init)
    merged = _merge(h2d, o_a, o_b, w, t.tm_wide, t.tn)
    x1 = _out_proj(x2d, merged, w, t.tm)
    y = _ffn(x1, w, t.tm_wide, t.tf)
    return y.reshape(gb, seq, D_MODEL)


def kernel(x_prompt, x_sample, mix_norm, w_in, q_a_norm, wq_b, kv_a_norm, wkv_b, mla_q_norm, mla_k_norm, diff_q_norm, diff_k_norm, lambda_q1, lambda_k1, lambda_q2, lambda_k2, diff_subln, w_up_mla, w_up_diff, w_o, ffn_norm, w_gate, w_up, w_down, rel_bias):
    t = _Tiles()
    depth = w_in.shape[0]
    max_seq = max(x_prompt.shape[1], x_sample.shape[1])
    tables = _rope_tables(max_seq)
    bias = _bias_tiles(rel_bias, t.blk)
    y_prompt, y_sample = x_prompt, x_sample
    for layer in range(depth):
        w = _prepare_weights(mix_norm, w_in, q_a_norm, wq_b, kv_a_norm, wkv_b, mla_q_norm,
                             mla_k_norm, diff_q_norm, diff_k_norm, diff_subln, w_up_mla,
                             w_up_diff, w_o, ffn_norm, w_gate, w_up, w_down, layer)
        lam_init = 0.8 - 0.6 * math.exp(-0.3 * layer)
        lam_pad = jnp.zeros((LANES - DIFF_QK,), F32)
        lam_rows = jnp.stack([jnp.concatenate([v[layer], lam_pad])
                              for v in (lambda_q1, lambda_k1, lambda_q2, lambda_k2)])
        run = functools.partial(_encoder_layer, w=w, tables=tables, bias=bias, rel_bias=rel_bias,
                                lam_rows=lam_rows, lam_init=lam_init, t=t)
        y_prompt = run(y_prompt)
        y_sample = run(y_sample)
    return (y_prompt, y_sample)
```

```python
import functools
import math
from typing import NamedTuple

import jax
import jax.numpy as jnp
from jax import lax
from jax.experimental import pallas as pl
from jax.experimental.pallas import tpu as pltpu

D_MODEL = 2048
MLA_HEADS = 8
MLA_NOPE = 128
MLA_ROPE = 64
MLA_QK = MLA_NOPE + MLA_ROPE
MLA_V = 128
Q_LORA = 512
KV_LORA = 256
ROPE_THETA = 10000.0
ROPE_HALF = MLA_ROPE // 2
DIFF_HEADS = 8
DIFF_QK = 64
DIFF_V = 2 * DIFF_QK
REL_BUCKETS = 32
REL_MAX_DIST = 128
D_FF = 5632
EPS = 1e-6
HEAD_OUT = MLA_HEADS * MLA_V

LANES = 128
MLA_QK_PAD = 2 * LANES
K_PE_COL = Q_LORA + KV_LORA
W_A_COLS = K_PE_COL + 2 * LANES
ROPE_TABLE_ROWS = 2048
VMEM_LIMIT = 56 * 1024 * 1024

F32 = jnp.float32
BF16 = jnp.bfloat16
NEG_BIG = -1e30
LOG2E = math.log2(math.e)
BOUNDED_SOFTMAX_LIMIT = 50.0
BIAS_TILES = 5

NT_DIMS = (((1,), (1,)), ((), ()))


def _params(semantics):
    return pltpu.CompilerParams(dimension_semantics=semantics, vmem_limit_bytes=VMEM_LIMIT)


def _const_spec(shape):
    zeros = (0,) * len(shape)
    return pl.BlockSpec(shape, lambda *_: zeros)


def _rms_rows(x, gain):
    return x * lax.rsqrt(jnp.mean(x * x, axis=-1, keepdims=True) + EPS) * gain


def _rope_table_kernel(inv_row_ref, inv_col_ref, cos_t_ref, sin_t_ref, cos_f_ref, sin_f_ref):
    ts = cos_t_ref.shape[0]
    base = pl.program_id(0) * ts
    pos_rows = (base + lax.broadcasted_iota(jnp.int32, (ts, LANES), 0)).astype(F32)
    ang_t = pos_rows * inv_row_ref[...]
    cos_t_ref[...] = jnp.cos(ang_t)
    sin_t_ref[...] = jnp.sin(ang_t)
    pos_cols = (base + lax.broadcasted_iota(jnp.int32, (ROPE_HALF, ts), 1)).astype(F32)
    ang_f = pos_cols * inv_col_ref[...]
    cos_f_ref[...] = jnp.cos(ang_f)
    sin_f_ref[...] = jnp.sin(ang_f)


def _rope_tables(seq):
    inv = ROPE_THETA ** (-jnp.arange(ROPE_HALF, dtype=F32) / ROPE_HALF)
    inv_row = jnp.tile(inv, LANES // ROPE_HALF)[None, :]
    inv_col = inv[:, None]
    ts = min(seq, ROPE_TABLE_ROWS)
    tok = jax.ShapeDtypeStruct((seq, LANES), F32)
    feat = jax.ShapeDtypeStruct((ROPE_HALF, seq), F32)
    return pl.pallas_call(
        _rope_table_kernel,
        out_shape=(tok, tok, feat, feat),
        grid=(seq // ts,),
        in_specs=[_const_spec((1, LANES)), _const_spec((ROPE_HALF, 1))],
        out_specs=(
            pl.BlockSpec((ts, LANES), lambda i: (i, 0)),
            pl.BlockSpec((ts, LANES), lambda i: (i, 0)),
            pl.BlockSpec((ROPE_HALF, ts), lambda i: (0, i)),
            pl.BlockSpec((ROPE_HALF, ts), lambda i: (0, i)),
        ),
        compiler_params=_params(("arbitrary",)), name="rope_tables",
    )(inv_row, inv_col)


def _mla_prep_kernel(x_ref, g_mix_ref, w_a_ref, g_qa_ref, g_kva_ref, wq_t_ref, g_q_ref,
                     wk_ref, wv_t_ref, g_kn_ref, g_kr_ref, g_krot_ref,
                     cos_t_ref, sin_t_ref, cos_f_ref, sin_f_ref,
                     q_ref, k_ref, v_ref, h_ref, *, tk):
    tm = x_ref.shape[0]
    h = _rms_rows(x_ref[...], g_mix_ref[...]).astype(BF16)
    h_ref[...] = h
    c = lax.dot_general(h, w_a_ref[...], NT_DIMS, preferred_element_type=F32)
    cq = _rms_rows(c[:, :Q_LORA], g_qa_ref[...]).astype(BF16)
    ckv = _rms_rows(c[:, Q_LORA:Q_LORA + KV_LORA], g_kva_ref[...]).astype(BF16)
    k_pe = c[:, K_PE_COL:K_PE_COL + LANES]
    k_pe_rot = c[:, K_PE_COL + LANES:W_A_COLS]

    q_t = lax.dot_general(wq_t_ref[...], cq, NT_DIMS, preferred_element_type=F32)
    cos_f = cos_f_ref[...]
    sin_f = sin_f_ref[...]
    g_q = g_q_ref[...]
    for hd in range(MLA_HEADS):
        xh = q_t[hd * MLA_QK:(hd + 1) * MLA_QK]
        rinv = lax.rsqrt(jnp.mean(xh * xh, axis=0, keepdims=True) + EPS)
        xn = xh * rinv * g_q
        x1 = xn[MLA_NOPE:MLA_NOPE + ROPE_HALF]
        x2 = xn[MLA_NOPE + ROPE_HALF:]
        q_ref[0, hd, 0:MLA_NOPE, :] = xn[:MLA_NOPE].astype(BF16)
        q_ref[0, hd, MLA_NOPE:MLA_NOPE + ROPE_HALF, :] = (x1 * cos_f - x2 * sin_f).astype(BF16)
        q_ref[0, hd, MLA_NOPE + ROPE_HALF:MLA_QK, :] = (x2 * cos_f + x1 * sin_f).astype(BF16)
        q_ref[0, hd, MLA_QK:, :] = jnp.zeros((MLA_QK_PAD - MLA_QK, tm), BF16)

    k_nope = jnp.dot(ckv, wk_ref[...], preferred_element_type=F32)
    rope_base = (k_pe * g_kr_ref[...]) * cos_t_ref[...] + (k_pe_rot * g_krot_ref[...]) * sin_t_ref[...]
    ss_pe = jnp.sum(k_pe * k_pe, axis=-1, keepdims=True)
    for hd in range(MLA_HEADS):
        kh = k_nope[:, hd * MLA_NOPE:(hd + 1) * MLA_NOPE]
        ss = jnp.sum(kh * kh, axis=-1, keepdims=True) + ss_pe
        rinv = lax.rsqrt(ss * (1.0 / MLA_QK) + EPS)
        k_ref[0, hd, :, 0:LANES] = (kh * rinv * g_kn_ref[...]).astype(BF16)
        k_ref[0, hd, :, LANES:] = (rope_base * rinv).astype(BF16)

    v_t = lax.dot_general(wv_t_ref[...], ckv, NT_DIMS, preferred_element_type=F32)
    for hd in range(MLA_HEADS):
        for cc in range(tm // tk):
            v_ref[0, hd, cc] = v_t[hd * MLA_V:(hd + 1) * MLA_V, cc * tk:(cc + 1) * tk].astype(BF16)


def _mla_prep(x2d, gb, seq, tm, tk, w, tables):
    tokens = x2d.shape[0]
    nt = seq // tm
    cos_t, sin_t, cos_f, sin_f = tables
    kern = functools.partial(_mla_prep_kernel, tk=tk)
    out_shape = (
        jax.ShapeDtypeStruct((gb, MLA_HEADS, MLA_QK_PAD, seq), BF16),
        jax.ShapeDtypeStruct((gb, MLA_HEADS, seq, MLA_QK_PAD), BF16),
        jax.ShapeDtypeStruct((gb, MLA_HEADS, seq // tk, MLA_V, tk), BF16),
        jax.ShapeDtypeStruct((tokens, D_MODEL), BF16),
    )
    in_specs = [
        pl.BlockSpec((tm, D_MODEL), lambda i: (i, 0)),
        _const_spec((1, D_MODEL)),
        _const_spec((W_A_COLS, D_MODEL)),
        _const_spec((1, Q_LORA)),
        _const_spec((1, KV_LORA)),
        _const_spec((MLA_HEADS * MLA_QK, Q_LORA)),
        _const_spec((MLA_QK, 1)),
        _const_spec((KV_LORA, MLA_HEADS * MLA_NOPE)),
        _const_spec((MLA_HEADS * MLA_V, KV_LORA)),
        _const_spec((1, LANES)),
        _const_spec((1, LANES)),
        _const_spec((1, LANES)),
        pl.BlockSpec((tm, LANES), lambda i: (i % nt, 0)),
        pl.BlockSpec((tm, LANES), lambda i: (i % nt, 0)),
        pl.BlockSpec((ROPE_HALF, tm), lambda i: (0, i % nt)),
        pl.BlockSpec((ROPE_HALF, tm), lambda i: (0, i % nt)),
    ]
    out_specs = (
        pl.BlockSpec((1, MLA_HEADS, MLA_QK_PAD, tm), lambda i: (i // nt, 0, 0, i % nt)),
        pl.BlockSpec((1, MLA_HEADS, tm, MLA_QK_PAD), lambda i: (i // nt, 0, i % nt, 0)),
        pl.BlockSpec((1, MLA_HEADS, tm // tk, MLA_V, tk), lambda i: (i // nt, 0, i % nt, 0, 0)),
        pl.BlockSpec((tm, D_MODEL), lambda i: (i, 0)),
    )
    return pl.pallas_call(
        kern, out_shape=out_shape, grid=(tokens // tm,), in_specs=in_specs, out_specs=out_specs,
        compiler_params=_params(("arbitrary",)), name="mla_prep",
    )(x2d, w["g_mix"], w["w_a_t"], w["g_qa"], w["g_kva"], w["wq_t"], w["g_q"], w["wk_nope"],
      w["wv_t"], w["g_kn"], w["g_kr"], w["g_krot"], cos_t, sin_t, cos_f, sin_f)


def _group_norm_cols(x_t, gain_col):
    rows, tm = x_t.shape
    x3 = x_t.reshape(rows // DIFF_QK, DIFF_QK, tm)
    rinv = lax.rsqrt(jnp.mean(x3 * x3, axis=1, keepdims=True) + EPS)
    return (x3 * rinv * gain_col[None]).reshape(rows, tm)


def _diff_prep_kernel(h_ref, wq_t_ref, wk_t_ref, wv_t_ref, g_q_ref, g_k_ref,
                      q_ref, k_ref, v_ref, *, tk):
    tm = h_ref.shape[0]
    h = h_ref[...]
    q_t = lax.dot_general(wq_t_ref[...], h, NT_DIMS, preferred_element_type=F32)
    qn = _group_norm_cols(q_t, g_q_ref[...]).astype(BF16)
    zeros = jnp.zeros((DIFF_QK, tm), BF16)
    for hd in range(DIFF_HEADS):
        r0 = hd * DIFF_V
        q_ref[0, hd, 0, 0:DIFF_QK, :] = qn[r0:r0 + DIFF_QK]
        q_ref[0, hd, 0, DIFF_QK:, :] = zeros
        q_ref[0, hd, 1, 0:DIFF_QK, :] = zeros
        q_ref[0, hd, 1, DIFF_QK:, :] = qn[r0 + DIFF_QK:r0 + DIFF_V]

    k_t = lax.dot_general(wk_t_ref[...], h, NT_DIMS, preferred_element_type=F32)
    kn = _group_norm_cols(k_t, g_k_ref[...]).T
    for hd in range(DIFF_HEADS):
        k_ref[0, hd] = kn[:, hd * DIFF_V:(hd + 1) * DIFF_V].astype(BF16)

    v_t = lax.dot_general(wv_t_ref[...], h, NT_DIMS, preferred_element_type=F32)
    for hd in range(DIFF_HEADS):
        for cc in range(tm // tk):
            v_ref[0, hd, cc] = v_t[hd * DIFF_V:(hd + 1) * DIFF_V, cc * tk:(cc + 1) * tk].astype(BF16)


def _diff_prep(h2d, gb, seq, tm, tk, w):
    tokens = h2d.shape[0]
    nt = seq // tm
    kern = functools.partial(_diff_prep_kernel, tk=tk)
    out_shape = (
        jax.ShapeDtypeStruct((gb, DIFF_HEADS, 2, DIFF_V, seq), BF16),
        jax.ShapeDtypeStruct((gb, DIFF_HEADS, seq, DIFF_V), BF16),
        jax.ShapeDtypeStruct((gb, DIFF_HEADS, seq // tk, DIFF_V, tk), BF16),
    )
    in_specs = [
        pl.BlockSpec((tm, D_MODEL), lambda i: (i, 0)),
        _const_spec((HEAD_OUT, D_MODEL)),
        _const_spec((HEAD_OUT, D_MODEL)),
        _const_spec((HEAD_OUT, D_MODEL)),
        _const_spec((DIFF_QK, 1)),
        _const_spec((DIFF_QK, 1)),
    ]
    out_specs = (
        pl.BlockSpec((1, DIFF_HEADS, 2, DIFF_V, tm), lambda i: (i // nt, 0, 0, 0, i % nt)),
        pl.BlockSpec((1, DIFF_HEADS, tm, DIFF_V), lambda i: (i // nt, 0, i % nt, 0)),
        pl.BlockSpec((1, DIFF_HEADS, tm // tk, DIFF_V, tk), lambda i: (i // nt, 0, i % nt, 0, 0)),
    )
    return pl.pallas_call(
        kern, out_shape=out_shape, grid=(tokens // tm,), in_specs=in_specs, out_specs=out_specs,
        compiler_params=_params(("arbitrary",)), name="diff_prep",
    )(h2d, w["w_dq_t"], w["w_dk_t"], w["w_dv_t"], w["g_dq"], w["g_dk"])


def _bias_tile_kernel(bucket_ref, rb_ref, out_ref):
    hd = pl.program_id(0)
    offset = pl.program_id(1) - pl.num_programs(1) // 2

    @pl.when(offset <= -2)
    def _():
        out_ref[0, 0] = jnp.full(out_ref.shape[2:], rb_ref[REL_BUCKETS // 2 - 1, hd] * LOG2E, F32)

    @pl.when(offset >= 2)
    def _():
        out_ref[0, 0] = jnp.full(out_ref.shape[2:], rb_ref[REL_BUCKETS - 1, hd] * LOG2E, F32)

    def lookup(buckets):
        bucket = bucket_ref[0]
        acc = jnp.zeros(bucket.shape, F32)
        for b in buckets:
            acc = jnp.where(bucket == b, rb_ref[b, hd], acc)
        out_ref[0, 0] = acc * LOG2E

    pl.when(offset == -1)(lambda: lookup(range(REL_BUCKETS // 2)))
    pl.when(offset == 1)(lambda: lookup(range(REL_BUCKETS // 2, REL_BUCKETS)))
    pl.when(offset == 0)(lambda: lookup(range(REL_BUCKETS)))


def _t5_bucket(rel):
    nb = REL_BUCKETS // 2
    ret = jnp.where(rel > 0, nb, 0)
    n = jnp.abs(rel)
    max_exact = nb // 2
    nf = jnp.maximum(n, 1).astype(F32)
    large = max_exact + (jnp.log(nf / max_exact) / math.log(REL_MAX_DIST / max_exact)
                         * (nb - max_exact)).astype(jnp.int32)
    large = jnp.minimum(large, nb - 1)
    return ret + jnp.where(n < max_exact, n, large)


def _bias_tiles(rel_bias, blk):
    assert blk >= REL_MAX_DIST
    n = BIAS_TILES
    d = jnp.arange(blk, dtype=jnp.int32)
    offs = (jnp.arange(n, dtype=jnp.int32) - n // 2) * blk
    rel = offs[:, None, None] + d[None, :, None] - d[None, None, :]
    bucket = _t5_bucket(rel)
    return pl.pallas_call(
        _bias_tile_kernel,
        out_shape=jax.ShapeDtypeStruct((DIFF_HEADS, n, blk, blk), F32),
        grid=(DIFF_HEADS, n),
        in_specs=[
            pl.BlockSpec((1, blk, blk), lambda hd, t: (t, 0, 0)),
            pl.BlockSpec(memory_space=pltpu.SMEM),
        ],
        out_specs=pl.BlockSpec((1, 1, blk, blk), lambda hd, t: (hd, t, 0, 0)),
        compiler_params=_params(("arbitrary", "arbitrary")), name="bias_tiles",
    )(bucket, rel_bias)


class _FlashState:
    def __init__(self, q_ref, k_ref, v_ref, scratch, blk, q_tiles):
        self.q_ref, self.k_ref, self.v_ref = q_ref, k_ref, v_ref
        self.p_ref, self.alpha_ref, self.m_ref, self.l_ref, self.acc_ref = scratch
        self.blk = blk
        self.bound = None
        self.streams = [(qt, mp) for qt in range(q_tiles) for mp in range(q_ref.shape[2])]
        self.nc = v_ref.shape[2]

    def q(self, qt, mp):
        return self.q_ref[0, 0, mp, :, qt * self.blk:(qt + 1) * self.blk]

    def key_chunk(self, c):
        return self.k_ref[0, 0, pl.ds(pl.multiple_of(c * self.blk, self.blk), self.blk), :]

    def init(self, online):
        self.l_ref[...] = jnp.zeros(self.l_ref.shape, F32)
        self.acc_ref[...] = jnp.zeros(self.acc_ref.shape, F32)
        self.p_ref[1] = jnp.zeros(self.p_ref.shape[1:], BF16)
        if online:
            self.m_ref[...] = jnp.full(self.m_ref.shape, NEG_BIG, F32)
            self.alpha_ref[...] = jnp.ones(self.alpha_ref.shape, F32)

    def pv(self, online, c, half):
        vc = self.v_ref[0, 0, c]
        for sx in range(len(self.streams)):
            update = jnp.dot(vc, self.p_ref[half, sx], preferred_element_type=F32)
            if online:
                self.acc_ref[sx] = self.alpha_ref[sx] * self.acc_ref[sx] + update
            else:
                self.acc_ref[sx] += update

    def step(self, online, c, half, cst=None, bias_tiles=None):
        self.pv(online, jnp.maximum(c - 1, 0), 1 - half)
        kc = self.key_chunk(c)
        for sx, (qt, mp) in enumerate(self.streams):
            s = jnp.dot(kc, self.q(qt, mp), preferred_element_type=F32)
            if bias_tiles is not None:
                s = s + bias_tiles[qt]
            l_old = self.l_ref[sx]
            if online:
                mc = jnp.max(s, axis=0, keepdims=True)
                if cst is not None:
                    mc = mc + cst
                m_old = self.m_ref[sx]
                shift = jnp.maximum(m_old, mc)
                alpha = jnp.exp2(m_old - shift)
                l_old = alpha * l_old
                self.alpha_ref[sx] = alpha
                self.m_ref[sx] = shift
            else:
                shift = self.bound
            if cst is not None:
                shift = shift - cst
            p = jnp.exp2(s - shift)
            self.l_ref[sx] = l_old + jnp.sum(p, axis=0, keepdims=True)
            self.p_ref[half, sx] = p.astype(BF16)

    def run(self, bound, schedule):
        self.bound = bound
        use_bound = bound <= BOUNDED_SOFTMAX_LIMIT
        for online, pred in ((False, use_bound), (True, jnp.logical_not(use_bound))):
            @pl.when(pred)
            def _():
                self.init(online)
                schedule(functools.partial(self.step, online))
                self.pv(online, self.nc - 1, 1)


def _flash_scratch(n_streams, head_dim, blk):
    return [
        pltpu.VMEM((2, n_streams, blk, blk), BF16),
        pltpu.VMEM((n_streams, 1, blk), F32),
        pltpu.VMEM((n_streams, 1, blk), F32),
        pltpu.VMEM((n_streams, 1, blk), F32),
        pltpu.VMEM((n_streams, head_dim, blk), F32),
    ]


def _score_bound(head_dim, g_q, g_k):
    return (head_dim ** 0.5 * LOG2E * 1.02 * jnp.max(jnp.abs(g_q)) * jnp.max(jnp.abs(g_k))).reshape(1)


def _mla_attn_kernel(bound_ref, q_ref, k_ref, v_ref, o_ref, *scratch, blk, q_tiles, group):
    st = _FlashState(q_ref, k_ref, v_ref, scratch, blk, q_tiles)

    def schedule(step):
        def body(cc, carry):
            for u in range(group):
                step(group * cc + u, u % 2)
            return carry

        lax.fori_loop(0, st.nc // group, body, 0)

    st.run(bound_ref[0], schedule)
    for sx, (qt, _) in enumerate(st.streams):
        o_t = st.acc_ref[sx] / st.l_ref[sx]
        o_ref[qt * blk:(qt + 1) * blk, :] = o_t.T.astype(BF16)


def _mla_attention(bound, q, k, v, gb, seq, blk, q_tiles, group):
    nc = seq // blk
    nq = nc // q_tiles
    assert group % 2 == 0 and nc % group == 0 and nc % q_tiles == 0
    kern = functools.partial(_mla_attn_kernel, blk=blk, q_tiles=q_tiles, group=group)
    q = q.reshape(gb, MLA_HEADS, 1, MLA_QK_PAD, seq)
    tq = q_tiles * blk
    return pl.pallas_call(
        kern,
        out_shape=jax.ShapeDtypeStruct((gb * seq, HEAD_OUT), BF16),
        grid=(gb, MLA_HEADS, nq),
        in_specs=[
            pl.BlockSpec(memory_space=pltpu.SMEM),
            pl.BlockSpec((1, 1, 1, MLA_QK_PAD, tq), lambda b, hd, i: (b, hd, 0, 0, i)),
            pl.BlockSpec((1, 1, seq, MLA_QK_PAD), lambda b, hd, i: (b, hd, 0, 0)),
            pl.BlockSpec((1, 1, nc, MLA_V, blk), lambda b, hd, i: (b, hd, 0, 0, 0)),
        ],
        out_specs=pl.BlockSpec((tq, MLA_V), lambda b, hd, i: (b * nq + i, hd)),
        scratch_shapes=_flash_scratch(q_tiles, MLA_V, blk),
        compiler_params=_params(("arbitrary", "arbitrary", "arbitrary")), name="mla_attn",
    )(bound, q, k, v)


def _diff_attn_kernel(bound_ref, q_ref, k_ref, v_ref, bias_ref, rb_ref, lam_ref, g_sub_ref, o_ref,
                      *scratch, blk, q_tiles, group, lam_init):
    hd = pl.program_id(0)
    i_first = pl.program_id(2) * q_tiles
    st = _FlashState(q_ref, k_ref, v_ref, scratch, blk, q_tiles)
    acc_ref, l_ref = st.acc_ref, st.l_ref
    n_groups = st.nc // group

    bias_max = jnp.abs(rb_ref[0, hd])
    for b in range(1, REL_BUCKETS):
        bias_max = jnp.maximum(bias_max, jnp.abs(rb_ref[b, hd]))

    far_before = rb_ref[REL_BUCKETS // 2 - 1, hd] * LOG2E
    far_after = rb_ref[REL_BUCKETS - 1, hd] * LOG2E
    first_near = (i_first - 1 + group) // group - 1
    last_near = (i_first + q_tiles) // group

    near_lo = jnp.maximum(first_near, 0)
    near_hi = jnp.minimum(last_near + 1, n_groups)

    def schedule(step):
        def far_group(cst):
            def body(cc, carry):
                for u in range(group):
                    step(group * cc + u, u % 2, cst=cst)
                return carry
            return body

        def near_group(cc, carry):
            for u in range(group):
                c = group * cc + u
                far = BIAS_TILES // 2
                tiles = [bias_ref[0, jnp.clip(c - (i_first + qt), -far, far) + far]
                         for qt in range(q_tiles)]
                step(c, u % 2, bias_tiles=tiles)
            return carry

        lax.fori_loop(0, near_lo, far_group(far_before), 0)
        lax.fori_loop(near_lo, near_hi, near_group, 0)
        lax.fori_loop(near_hi, n_groups, far_group(far_after), 0)

    st.run(bound_ref[0] + bias_max * LOG2E, schedule)

    lam_v = lam_ref[...]
    lam = (jnp.exp(jnp.sum(lam_v[0:1] * lam_v[1:2], axis=-1, keepdims=True))
           - jnp.exp(jnp.sum(lam_v[2:3] * lam_v[3:4], axis=-1, keepdims=True)) + lam_init)
    for qt in range(q_tiles):
        s0, s1 = 2 * qt, 2 * qt + 1
        o_t = acc_ref[s0] / l_ref[s0] - lam * (acc_ref[s1] / l_ref[s1])
        o_t = o_t * lax.rsqrt(jnp.mean(o_t * o_t, axis=0, keepdims=True) + EPS) * g_sub_ref[...]
        o_ref[qt * blk:(qt + 1) * blk, :] = (o_t * (1.0 - lam_init)).T.astype(BF16)


def _diff_attention(bound, q, k, v, bias, rel_bias, lam_rows, g_sub, gb, seq, blk, q_tiles, group,
                    lam_init):
    nc = seq // blk
    nq = nc // q_tiles
    assert group % 2 == 0 and nc % group == 0 and nc % q_tiles == 0 and blk >= REL_MAX_DIST
    kern = functools.partial(_diff_attn_kernel, blk=blk, q_tiles=q_tiles, group=group,
                             lam_init=lam_init)
    tq = q_tiles * blk
    return pl.pallas_call(
        kern,
        out_shape=jax.ShapeDtypeStruct((gb * seq, HEAD_OUT), BF16),
        grid=(DIFF_HEADS, gb, nq),
        in_specs=[
            pl.BlockSpec(memory_space=pltpu.SMEM),
            pl.BlockSpec((1, 1, 2, DIFF_V, tq), lambda hd, b, i: (b, hd, 0, 0, i)),
            pl.BlockSpec((1, 1, seq, DIFF_V), lambda hd, b, i: (b, hd, 0, 0)),
            pl.BlockSpec((1, 1, nc, DIFF_V, blk), lambda hd, b, i: (b, hd, 0, 0, 0)),
            pl.BlockSpec((1, BIAS_TILES, blk, blk), lambda hd, b, i: (hd, 0, 0, 0),
                         pipeline_mode=pl.Buffered(1)),
            pl.BlockSpec(memory_space=pltpu.SMEM),
            _const_spec((4, LANES)),
            _const_spec((DIFF_V, 1)),
        ],
        out_specs=pl.BlockSpec((tq, DIFF_V), lambda hd, b, i: (b * nq + i, hd)),
        scratch_shapes=_flash_scratch(2 * q_tiles, DIFF_V, blk),
        compiler_params=_params(("arbitrary", "arbitrary", "arbitrary")), name="diff_attn",
    )(bound, q, k, v, bias, rel_bias, lam_rows, g_sub)


def _merge_kernel(h_ref, oa_ref, ob_ref, wga_ref, wgb_ref, wua_ref, wub_ref, out_ref):
    h = h_ref[...]
    ga = lax.dot_general(h, wga_ref[...], NT_DIMS, preferred_element_type=F32)
    gb_ = lax.dot_general(h, wgb_ref[...], NT_DIMS, preferred_element_type=F32)
    ua = jnp.dot(oa_ref[...], wua_ref[...], preferred_element_type=F32)
    ub = jnp.dot(ob_ref[...], wub_ref[...], preferred_element_type=F32)
    out_ref[...] = (jax.nn.sigmoid(ga) * ua + jax.nn.sigmoid(gb_) * ub).astype(BF16)


def _merge(h2d, o_a, o_b, w, tm, tn):
    tokens = h2d.shape[0]
    return pl.pallas_call(
        _merge_kernel,
        out_shape=jax.ShapeDtypeStruct((tokens, D_MODEL), BF16),
        grid=(tokens // tm, D_MODEL // tn),
        in_specs=[
            pl.BlockSpec((tm, D_MODEL), lambda i, j: (i, 0)),
            pl.BlockSpec((tm, HEAD_OUT), lambda i, j: (i, 0)),
            pl.BlockSpec((tm, HEAD_OUT), lambda i, j: (i, 0)),
            pl.BlockSpec((tn, D_MODEL), lambda i, j: (j, 0)),
            pl.BlockSpec((tn, D_MODEL), lambda i, j: (j, 0)),
            pl.BlockSpec((HEAD_OUT, tn), lambda i, j: (0, j)),
            pl.BlockSpec((HEAD_OUT, tn), lambda i, j: (0, j)),
        ],
        out_specs=pl.BlockSpec((tm, tn), lambda i, j: (i, j)),
        compiler_params=_params(("arbitrary", "arbitrary")), name="gated_merge",
    )(h2d, o_a, o_b, w["w_ga_t"], w["w_gb_t"], w["w_ua"], w["w_ub"])


def _out_proj_kernel(x_ref, m_ref, wo_ref, out_ref):
    out_ref[...] = x_ref[...] + jnp.dot(m_ref[...], wo_ref[...], preferred_element_type=F32)


def _out_proj(x2d, merged, w, tm):
    tokens = x2d.shape[0]
    return pl.pallas_call(
        _out_proj_kernel,
        out_shape=jax.ShapeDtypeStruct((tokens, D_MODEL), F32),
        grid=(tokens // tm,),
        in_specs=[
            pl.BlockSpec((tm, D_MODEL), lambda i: (i, 0)),
            pl.BlockSpec((tm, D_MODEL), lambda i: (i, 0)),
            _const_spec((D_MODEL, D_MODEL)),
        ],
        out_specs=pl.BlockSpec((tm, D_MODEL), lambda i: (i, 0)),
        compiler_params=_params(("arbitrary",)), name="out_proj",
    )(x2d, merged, w["w_o"])


def _ffn_kernel(x_ref, g_ref, wg_ref, wu_ref, wd_ref, out_ref, h_ref):
    @pl.when(pl.program_id(1) == 0)
    def _():
        x = x_ref[...]
        h_ref[...] = _rms_rows(x, g_ref[...]).astype(BF16)
        out_ref[...] = x

    h = h_ref[...]
    g = jnp.dot(h, wg_ref[...], preferred_element_type=F32)
    u = jnp.dot(h, wu_ref[...], preferred_element_type=F32)
    a = (g * jax.nn.sigmoid(g) * u).astype(BF16)
    out_ref[...] += jnp.dot(a, wd_ref[...], preferred_element_type=F32)


def _ffn(x2d, w, tm, tf):
    tokens = x2d.shape[0]
    return pl.pallas_call(
        _ffn_kernel,
        out_shape=jax.ShapeDtypeStruct((tokens, D_MODEL), F32),
        grid=(tokens // tm, D_FF // tf),
        in_specs=[
            pl.BlockSpec((tm, D_MODEL), lambda i, j: (i, 0)),
            _const_spec((1, D_MODEL)),
            pl.BlockSpec((D_MODEL, tf), lambda i, j: (0, j)),
            pl.BlockSpec((D_MODEL, tf), lambda i, j: (0, j)),
            pl.BlockSpec((tf, D_MODEL), lambda i, j: (j, 0)),
        ],
        out_specs=pl.BlockSpec((tm, D_MODEL), lambda i, j: (i, 0)),
        scratch_shapes=[pltpu.VMEM((tm, D_MODEL), BF16)],
        compiler_params=_params(("arbitrary", "arbitrary")), name="swiglu_ffn",
    )(x2d, w["g_ffn"], w["w_gate"], w["w_up"], w["w_down"])


def _prepare_weights(mix_norm, w_in, q_a_norm, wq_b, kv_a_norm, wkv_b, mla_q_norm, mla_k_norm,
                     diff_q_norm, diff_k_norm, diff_subln, w_up_mla, w_up_diff, w_o, ffn_norm,
                     w_gate, w_up, w_down, layer):
    win_t = w_in[layer].T
    o_kpe = K_PE_COL
    o_dq = o_kpe + MLA_ROPE
    o_dk = o_dq + HEAD_OUT
    o_dv = o_dk + HEAD_OUT
    o_ga = o_dv + HEAD_OUT
    o_gb = o_ga + D_MODEL
    w_kpe_t = win_t[o_kpe:o_dq]
    w_kpe_rot_t = jnp.concatenate([-w_kpe_t[ROPE_HALF:], w_kpe_t[:ROPE_HALF]], axis=0)
    zpad = jnp.zeros((LANES - MLA_ROPE, D_MODEL), F32)
    w_a_t = jnp.concatenate([win_t[:o_kpe], w_kpe_t, zpad, w_kpe_rot_t, zpad], axis=0)
    wkv = wkv_b[layer].reshape(KV_LORA, MLA_HEADS, MLA_NOPE + MLA_V)
    gk = mla_k_norm[layer]
    gk_rope = gk[MLA_NOPE:]
    lane_pad = jnp.zeros((LANES - MLA_ROPE,), F32)
    return {
        "g_mix": mix_norm[layer][None, :],
        "w_a_t": w_a_t.astype(BF16),
        "g_qa": q_a_norm[layer][None, :],
        "g_kva": kv_a_norm[layer][None, :],
        "wq_t": wq_b[layer].T.astype(BF16),
        "g_q": (mla_q_norm[layer] * (MLA_QK ** -0.5 * LOG2E))[:, None],
        "wk_nope": wkv[:, :, :MLA_NOPE].reshape(KV_LORA, MLA_HEADS * MLA_NOPE).astype(BF16),
        "wv_t": wkv[:, :, MLA_NOPE:].reshape(KV_LORA, MLA_HEADS * MLA_V).T.astype(BF16),
        "g_kn": gk[None, :MLA_NOPE],
        "g_kr": jnp.concatenate([gk_rope, lane_pad])[None, :],
        "g_krot": jnp.concatenate([gk_rope[ROPE_HALF:], gk_rope[:ROPE_HALF], lane_pad])[None, :],
        "w_dq_t": win_t[o_dq:o_dk].astype(BF16),
        "w_dk_t": win_t[o_dk:o_dv].astype(BF16),
        "w_dv_t": win_t[o_dv:o_ga].astype(BF16),
        "g_dq": (diff_q_norm[layer] * (DIFF_QK ** -0.5 * LOG2E))[:, None],
        "g_dk": diff_k_norm[layer][:, None],
        "g_sub": diff_subln[layer][:, None],
        "w_ga_t": win_t[o_ga:o_gb].astype(BF16),
        "w_gb_t": win_t[o_gb:].astype(BF16),
        "w_ua": w_up_mla[layer].astype(BF16),
        "w_ub": w_up_diff[layer].astype(BF16),
        "w_o": w_o[layer].astype(BF16),
        "mla_bound": _score_bound(MLA_QK, mla_q_norm[layer], gk),
        "diff_bound": _score_bound(DIFF_QK, diff_q_norm[layer], diff_k_norm[layer]),
        "g_ffn": ffn_norm[layer][None, :],
        "w_gate": w_gate[layer].astype(BF16),
        "w_up": w_up[layer].astype(BF16),
        "w_down": w_down[layer].astype(BF16),
    }


class _Tiles(NamedTuple):
    tm: int = 512
    tn: int = 1024
    tf: int = 512
    tm_wide: int = 1024
    blk: int = 512
    mla_q_tiles: int = 4
    diff_q_tiles: int = 4
    mla_group_max: int = 4
    diff_group_max: int = 2


def _encoder_layer(x, w, tables, bias, rel_bias, lam_rows, lam_init, t):
    gb, seq, _ = x.shape
    x2d = x.reshape(gb * seq, D_MODEL)
    mla_group = math.gcd(t.mla_group_max, seq // t.blk)
    diff_group = math.gcd(t.diff_group_max, seq // t.blk)
    q_a, k_a, v_a, h2d = _mla_prep(x2d, gb, seq, t.tm, t.blk, w, tables)
    q_d, k_d, v_d = _diff_prep(h2d, gb, seq, t.tm, t.blk, w)
    o_a = _mla_attention(w["mla_bound"], q_a, k_a, v_a, gb, seq, t.blk, t.mla_q_tiles, mla_group)
    o_b = _diff_attention(w["diff_bound"], q_d, k_d, v_d, bias, rel_bias, lam_rows, w["g_sub"], gb,
                          seq, t.blk, t.diff_q_tiles, diff_group, lam_init)
    merged = _merge(h2d, o_a, o_b, w, t.tm_wide, t.tn)
    x1 = _out_proj(x2d, merged, w, t.tm)
    y = _ffn(x1, w, t.tm_wide, t.tf)
    return y.reshape(gb, seq, D_MODEL)


def kernel(x_prompt, x_sample, mix_norm, w_in, q_a_norm, wq_b, kv_a_norm, wkv_b, mla_q_norm, mla_k_norm, diff_q_norm, diff_k_norm, lambda_q1, lambda_k1, lambda_q2, lambda_k2, diff_subln, w_up_mla, w_up_diff, w_o, ffn_norm, w_gate, w_up, w_down, rel_bias):
    t = _Tiles()
    depth = w_in.shape[0]
    max_seq = max(x_prompt.shape[1], x_sample.shape[1])
    tables = _rope_tables(max_seq)
    bias = _bias_tiles(rel_bias, t.blk)
    y_prompt, y_sample = x_prompt, x_sample
    for layer in range(depth):
        w = _prepare_weights(mix_norm, w_in, q_a_norm, wq_b, kv_a_norm, wkv_b, mla_q_norm,
                             mla_k_norm, diff_q_norm, diff_k_norm, diff_subln, w_up_mla,
                             w_up_diff, w_o, ffn_norm, w_gate, w_up, w_down, layer)
        lam_init = 0.8 - 0.6 * math.exp(-0.3 * layer)
        lam_pad = jnp.zeros((LANES - DIFF_QK,), F32)
        lam_rows = jnp.stack([jnp.concatenate([v[layer], lam_pad])
                              for v in (lambda_q1, lambda_k1, lambda_q2, lambda_k2)])
        run = functools.partial(_encoder_layer, w=w, tables=tables, bias=bias, rel_bias=rel_bias,
                                lam_rows=lam_rows, lam_init=lam_init, t=t)
        y_prompt = run(y_prompt)
        y_sample = run(y_sample)
    return (y_prompt, y_sample)
```

```python
import functools
import math
from typing import NamedTuple

import jax
import jax.numpy as jnp
from jax import lax
from jax.experimental import pallas as pl
from jax.experimental.pallas import tpu as pltpu

D_MODEL = 2048
MLA_HEADS = 8
MLA_NOPE = 128
MLA_ROPE = 64
MLA_QK = MLA_NOPE + MLA_ROPE
MLA_V = 128
Q_LORA = 512
KV_LORA = 256
ROPE_THETA = 10000.0
ROPE_HALF = MLA_ROPE // 2
DIFF_HEADS = 8
DIFF_QK = 64
DIFF_V = 2 * DIFF_QK
REL_BUCKETS = 32
REL_MAX_DIST = 128
D_FF = 5632
EPS = 1e-6
HEAD_OUT = MLA_HEADS * MLA_V

LANES = 128
MLA_QK_PAD = 2 * LANES
K_PE_COL = Q_LORA + KV_LORA
W_A_COLS = K_PE_COL + 2 * LANES
ROPE_TABLE_ROWS = 2048
VMEM_LIMIT = 56 * 1024 * 1024

F32 = jnp.float32
BF16 = jnp.bfloat16
NEG_BIG = -1e30
LOG2E = math.log2(math.e)
BOUNDED_SOFTMAX_LIMIT = 50.0
BIAS_TILES = 5

NT_DIMS = (((1,), (1,)), ((), ()))


def _params(semantics, fuse_inputs=None):
    return pltpu.CompilerParams(dimension_semantics=semantics, vmem_limit_bytes=VMEM_LIMIT,
                                allow_input_fusion=fuse_inputs)


def _const_spec(shape):
    zeros = (0,) * len(shape)
    return pl.BlockSpec(shape, lambda *_: zeros)


def _rms_rows(x, gain):
    return x * lax.rsqrt(jnp.mean(x * x, axis=-1, keepdims=True) + EPS) * gain


def _rope_table_kernel(inv_row_ref, inv_col_ref, cos_t_ref, sin_t_ref, cos_f_ref, sin_f_ref):
    ts = cos_t_ref.shape[0]
    base = pl.program_id(0) * ts
    pos_rows = (base + lax.broadcasted_iota(jnp.int32, (ts, LANES), 0)).astype(F32)
    ang_t = pos_rows * inv_row_ref[...]
    cos_t_ref[...] = jnp.cos(ang_t)
    sin_t_ref[...] = jnp.sin(ang_t)
    pos_cols = (base + lax.broadcasted_iota(jnp.int32, (ROPE_HALF, ts), 1)).astype(F32)
    ang_f = pos_cols * inv_col_ref[...]
    cos_f_ref[...] = jnp.cos(ang_f)
    sin_f_ref[...] = jnp.sin(ang_f)


def _rope_tables(seq):
    inv = ROPE_THETA ** (-jnp.arange(ROPE_HALF, dtype=F32) / ROPE_HALF)
    inv_row = jnp.tile(inv, LANES // ROPE_HALF)[None, :]
    inv_col = inv[:, None]
    ts = min(seq, ROPE_TABLE_ROWS)
    tok = jax.ShapeDtypeStruct((seq, LANES), F32)
    feat = jax.ShapeDtypeStruct((ROPE_HALF, seq), F32)
    return pl.pallas_call(
        _rope_table_kernel,
        out_shape=(tok, tok, feat, feat),
        grid=(seq // ts,),
        in_specs=[_const_spec((1, LANES)), _const_spec((ROPE_HALF, 1))],
        out_specs=(
            pl.BlockSpec((ts, LANES), lambda i: (i, 0)),
            pl.BlockSpec((ts, LANES), lambda i: (i, 0)),
            pl.BlockSpec((ROPE_HALF, ts), lambda i: (0, i)),
            pl.BlockSpec((ROPE_HALF, ts), lambda i: (0, i)),
        ),
        compiler_params=_params(("arbitrary",)), name="rope_tables",
    )(inv_row, inv_col)


def _mla_prep_kernel(x_ref, g_mix_ref, w_a_ref, g_qa_ref, g_kva_ref, wq_t_ref, g_q_ref,
                     wk_ref, wv_t_ref, g_kn_ref, g_kr_ref, g_krot_ref,
                     cos_t_ref, sin_t_ref, cos_f_ref, sin_f_ref,
                     q_ref, k_ref, v_ref, h_ref, *, tk):
    tm = x_ref.shape[0]
    h = _rms_rows(x_ref[...], g_mix_ref[...]).astype(BF16)
    h_ref[...] = h
    c = lax.dot_general(h, w_a_ref[...], NT_DIMS, preferred_element_type=F32)
    cq = _rms_rows(c[:, :Q_LORA], g_qa_ref[...]).astype(BF16)
    ckv = _rms_rows(c[:, Q_LORA:Q_LORA + KV_LORA], g_kva_ref[...]).astype(BF16)
    k_pe = c[:, K_PE_COL:K_PE_COL + LANES]
    k_pe_rot = c[:, K_PE_COL + LANES:W_A_COLS]

    q_t = lax.dot_general(wq_t_ref[...], cq, NT_DIMS, preferred_element_type=F32)
    cos_f = cos_f_ref[...]
    sin_f = sin_f_ref[...]
    g_q = g_q_ref[...]
    for hd in range(MLA_HEADS):
        xh = q_t[hd * MLA_QK:(hd + 1) * MLA_QK]
        rinv = lax.rsqrt(jnp.mean(xh * xh, axis=0, keepdims=True) + EPS)
        xn = xh * rinv * g_q
        x1 = xn[MLA_NOPE:MLA_NOPE + ROPE_HALF]
        x2 = xn[MLA_NOPE + ROPE_HALF:]
        q_ref[0, hd, 0:MLA_NOPE, :] = xn[:MLA_NOPE].astype(BF16)
        q_ref[0, hd, MLA_NOPE:MLA_NOPE + ROPE_HALF, :] = (x1 * cos_f - x2 * sin_f).astype(BF16)
        q_ref[0, hd, MLA_NOPE + ROPE_HALF:MLA_QK, :] = (x2 * cos_f + x1 * sin_f).astype(BF16)
        q_ref[0, hd, MLA_QK:, :] = jnp.zeros((MLA_QK_PAD - MLA_QK, tm), BF16)

    k_nope = jnp.dot(ckv, wk_ref[...], preferred_element_type=F32)
    rope_base = (k_pe * g_kr_ref[...]) * cos_t_ref[...] + (k_pe_rot * g_krot_ref[...]) * sin_t_ref[...]
    ss_pe = jnp.sum(k_pe * k_pe, axis=-1, keepdims=True)
    for hd in range(MLA_HEADS):
        kh = k_nope[:, hd * MLA_NOPE:(hd + 1) * MLA_NOPE]
        ss = jnp.sum(kh * kh, axis=-1, keepdims=True) + ss_pe
        rinv = lax.rsqrt(ss * (1.0 / MLA_QK) + EPS)
        k_ref[0, hd, :, 0:LANES] = (kh * rinv * g_kn_ref[...]).astype(BF16)
        k_ref[0, hd, :, LANES:] = (rope_base * rinv).astype(BF16)

    v_t = lax.dot_general(wv_t_ref[...], ckv, NT_DIMS, preferred_element_type=F32)
    for hd in range(MLA_HEADS):
        for cc in range(tm // tk):
            v_ref[0, hd, cc] = v_t[hd * MLA_V:(hd + 1) * MLA_V, cc * tk:(cc + 1) * tk].astype(BF16)


def _mla_prep(x2d, gb, seq, tm, tk, w, tables):
    tokens = x2d.shape[0]
    nt = seq // tm
    cos_t, sin_t, cos_f, sin_f = tables
    kern = functools.partial(_mla_prep_kernel, tk=tk)
    out_shape = (
        jax.ShapeDtypeStruct((gb, MLA_HEADS, MLA_QK_PAD, seq), BF16),
        jax.ShapeDtypeStruct((gb, MLA_HEADS, seq, MLA_QK_PAD), BF16),
        jax.ShapeDtypeStruct((gb, MLA_HEADS, seq // tk, MLA_V, tk), BF16),
        jax.ShapeDtypeStruct((tokens, D_MODEL), BF16),
    )
    in_specs = [
        pl.BlockSpec((tm, D_MODEL), lambda i: (i, 0)),
        _const_spec((1, D_MODEL)),
        _const_spec((W_A_COLS, D_MODEL)),
        _const_spec((1, Q_LORA)),
        _const_spec((1, KV_LORA)),
        _const_spec((MLA_HEADS * MLA_QK, Q_LORA)),
        _const_spec((MLA_QK, 1)),
        _const_spec((KV_LORA, MLA_HEADS * MLA_NOPE)),
        _const_spec((MLA_HEADS * MLA_V, KV_LORA)),
        _const_spec((1, LANES)),
        _const_spec((1, LANES)),
        _const_spec((1, LANES)),
        pl.BlockSpec((tm, LANES), lambda i: (i % nt, 0)),
        pl.BlockSpec((tm, LANES), lambda i: (i % nt, 0)),
        pl.BlockSpec((ROPE_HALF, tm), lambda i: (0, i % nt)),
        pl.BlockSpec((ROPE_HALF, tm), lambda i: (0, i % nt)),
    ]
    out_specs = (
        pl.BlockSpec((1, MLA_HEADS, MLA_QK_PAD, tm), lambda i: (i // nt, 0, 0, i % nt)),
        pl.BlockSpec((1, MLA_HEADS, tm, MLA_QK_PAD), lambda i: (i // nt, 0, i % nt, 0)),
        pl.BlockSpec((1, MLA_HEADS, tm // tk, MLA_V, tk), lambda i: (i // nt, 0, i % nt, 0, 0)),
        pl.BlockSpec((tm, D_MODEL), lambda i: (i, 0)),
    )
    return pl.pallas_call(
        kern, out_shape=out_shape, grid=(tokens // tm,), in_specs=in_specs, out_specs=out_specs,
        compiler_params=_params(("arbitrary",)), name="mla_prep",
    )(x2d, w["g_mix"], w["w_a_t"], w["g_qa"], w["g_kva"], w["wq_t"], w["g_q"], w["wk_nope"],
      w["wv_t"], w["g_kn"], w["g_kr"], w["g_krot"], cos_t, sin_t, cos_f, sin_f)


def _group_norm_cols(x_t, gain_col):
    rows, tm = x_t.shape
    x3 = x_t.reshape(rows // DIFF_QK, DIFF_QK, tm)
    rinv = lax.rsqrt(jnp.mean(x3 * x3, axis=1, keepdims=True) + EPS)
    return (x3 * rinv * gain_col[None]).reshape(rows, tm)


def _diff_prep_kernel(h_ref, wq_t_ref, wk_t_ref, wv_t_ref, g_q_ref, g_k_ref,
                      q_ref, k_ref, v_ref, *, tk):
    tm = h_ref.shape[0]
    h = h_ref[...]
    q_t = lax.dot_general(wq_t_ref[...], h, NT_DIMS, preferred_element_type=F32)
    qn = _group_norm_cols(q_t, g_q_ref[...]).astype(BF16)
    zeros = jnp.zeros((DIFF_QK, tm), BF16)
    for hd in range(DIFF_HEADS):
        r0 = hd * DIFF_V
        q_ref[0, hd, 0, 0:DIFF_QK, :] = qn[r0:r0 + DIFF_QK]
        q_ref[0, hd, 0, DIFF_QK:, :] = zeros
        q_ref[0, hd, 1, 0:DIFF_QK, :] = zeros
        q_ref[0, hd, 1, DIFF_QK:, :] = qn[r0 + DIFF_QK:r0 + DIFF_V]

    k_t = lax.dot_general(wk_t_ref[...], h, NT_DIMS, preferred_element_type=F32)
    kn = _group_norm_cols(k_t, g_k_ref[...]).T
    for hd in range(DIFF_HEADS):
        k_ref[0, hd] = kn[:, hd * DIFF_V:(hd + 1) * DIFF_V].astype(BF16)

    v_t = lax.dot_general(wv_t_ref[...], h, NT_DIMS, preferred_element_type=F32)
    for hd in range(DIFF_HEADS):
        for cc in range(tm // tk):
            v_ref[0, hd, cc] = v_t[hd * DIFF_V:(hd + 1) * DIFF_V, cc * tk:(cc + 1) * tk].astype(BF16)


def _diff_prep(h2d, gb, seq, tm, tk, w):
    tokens = h2d.shape[0]
    nt = seq // tm
    kern = functools.partial(_diff_prep_kernel, tk=tk)
    out_shape = (
        jax.ShapeDtypeStruct((gb, DIFF_HEADS, 2, DIFF_V, seq), BF16),
        jax.ShapeDtypeStruct((gb, DIFF_HEADS, seq, DIFF_V), BF16),
        jax.ShapeDtypeStruct((gb, DIFF_HEADS, seq // tk, DIFF_V, tk), BF16),
    )
    in_specs = [
        pl.BlockSpec((tm, D_MODEL), lambda i: (i, 0)),
        _const_spec((HEAD_OUT, D_MODEL)),
        _const_spec((HEAD_OUT, D_MODEL)),
        _const_spec((HEAD_OUT, D_MODEL)),
        _const_spec((DIFF_QK, 1)),
        _const_spec((DIFF_QK, 1)),
    ]
    out_specs = (
        pl.BlockSpec((1, DIFF_HEADS, 2, DIFF_V, tm), lambda i: (i // nt, 0, 0, 0, i % nt)),
        pl.BlockSpec((1, DIFF_HEADS, tm, DIFF_V), lambda i: (i // nt, 0, i % nt, 0)),
        pl.BlockSpec((1, DIFF_HEADS, tm // tk, DIFF_V, tk), lambda i: (i // nt, 0, i % nt, 0, 0)),
    )
    return pl.pallas_call(
        kern, out_shape=out_shape, grid=(tokens // tm,), in_specs=in_specs, out_specs=out_specs,
        compiler_params=_params(("arbitrary",)), name="diff_prep",
    )(h2d, w["w_dq_t"], w["w_dk_t"], w["w_dv_t"], w["g_dq"], w["g_dk"])


def _bias_tile_kernel(bucket_ref, rb_ref, out_ref):
    hd = pl.program_id(0)
    offset = pl.program_id(1) - pl.num_programs(1) // 2

    @pl.when(offset <= -2)
    def _():
        out_ref[0, 0] = jnp.full(out_ref.shape[2:], rb_ref[REL_BUCKETS // 2 - 1, hd] * LOG2E, F32)

    @pl.when(offset >= 2)
    def _():
        out_ref[0, 0] = jnp.full(out_ref.shape[2:], rb_ref[REL_BUCKETS - 1, hd] * LOG2E, F32)

    def lookup(buckets):
        bucket = bucket_ref[0]
        acc = jnp.zeros(bucket.shape, F32)
        for b in buckets:
            acc = jnp.where(bucket == b, rb_ref[b, hd], acc)
        out_ref[0, 0] = acc * LOG2E

    pl.when(offset == -1)(lambda: lookup(range(REL_BUCKETS // 2)))
    pl.when(offset == 1)(lambda: lookup(range(REL_BUCKETS // 2, REL_BUCKETS)))
    pl.when(offset == 0)(lambda: lookup(range(REL_BUCKETS)))


def _t5_bucket(rel):
    nb = REL_BUCKETS // 2
    ret = jnp.where(rel > 0, nb, 0)
    n = jnp.abs(rel)
    max_exact = nb // 2
    nf = jnp.maximum(n, 1).astype(F32)
    large = max_exact + (jnp.log(nf / max_exact) / math.log(REL_MAX_DIST / max_exact)
                         * (nb - max_exact)).astype(jnp.int32)
    large = jnp.minimum(large, nb - 1)
    return ret + jnp.where(n < max_exact, n, large)


def _bias_tiles(rel_bias, blk):
    assert blk >= REL_MAX_DIST
    n = BIAS_TILES
    d = jnp.arange(blk, dtype=jnp.int32)
    offs = (jnp.arange(n, dtype=jnp.int32) - n // 2) * blk
    rel = offs[:, None, None] + d[None, :, None] - d[None, None, :]
    bucket = _t5_bucket(rel)
    return pl.pallas_call(
        _bias_tile_kernel,
        out_shape=jax.ShapeDtypeStruct((DIFF_HEADS, n, blk, blk), F32),
        grid=(DIFF_HEADS, n),
        in_specs=[
            pl.BlockSpec((1, blk, blk), lambda hd, t: (t, 0, 0)),
            pl.BlockSpec(memory_space=pltpu.SMEM),
        ],
        out_specs=pl.BlockSpec((1, 1, blk, blk), lambda hd, t: (hd, t, 0, 0)),
        compiler_params=_params(("arbitrary", "arbitrary")), name="bias_tiles",
    )(bucket, rel_bias)


class _FlashState:
    def __init__(self, q_ref, k_ref, v_ref, scratch, blk, q_tiles):
        self.q_ref, self.k_ref, self.v_ref = q_ref, k_ref, v_ref
        self.p_ref, self.alpha_ref, self.m_ref, self.l_ref, self.acc_ref = scratch
        self.blk = blk
        self.bound = None
        self.streams = [(qt, mp) for qt in range(q_tiles) for mp in range(q_ref.shape[2])]
        self.nc = v_ref.shape[2]

    def q(self, qt, mp):
        return self.q_ref[0, 0, mp, :, qt * self.blk:(qt + 1) * self.blk]

    def key_chunk(self, c):
        return self.k_ref[0, 0, pl.ds(pl.multiple_of(c * self.blk, self.blk), self.blk), :]

    def init(self, online):
        self.l_ref[...] = jnp.zeros(self.l_ref.shape, F32)
        self.acc_ref[...] = jnp.zeros(self.acc_ref.shape, F32)
        self.p_ref[1] = jnp.zeros(self.p_ref.shape[1:], BF16)
        if online:
            self.m_ref[...] = jnp.full(self.m_ref.shape, NEG_BIG, F32)
            self.alpha_ref[...] = jnp.ones(self.alpha_ref.shape, F32)

    def pv(self, online, c, half):
        vc = self.v_ref[0, 0, c]
        for sx in range(len(self.streams)):
            update = jnp.dot(vc, self.p_ref[half, sx], preferred_element_type=F32)
            if online:
                self.acc_ref[sx] = self.alpha_ref[sx] * self.acc_ref[sx] + update
            else:
                self.acc_ref[sx] += update

    def step(self, online, c, half, cst=None, bias_tiles=None):
        self.pv(online, jnp.maximum(c - 1, 0), 1 - half)
        kc = self.key_chunk(c)
        for sx, (qt, mp) in enumerate(self.streams):
            s = jnp.dot(kc, self.q(qt, mp), preferred_element_type=F32)
            if bias_tiles is not None:
                s = s + bias_tiles[qt]
            l_old = self.l_ref[sx]
            if online:
                mc = jnp.max(s, axis=0, keepdims=True)
                if cst is not None:
                    mc = mc + cst
                m_old = self.m_ref[sx]
                shift = jnp.maximum(m_old, mc)
                alpha = jnp.exp2(m_old - shift)
                l_old = alpha * l_old
                self.alpha_ref[sx] = alpha
                self.m_ref[sx] = shift
            else:
                shift = self.bound
            if cst is not None:
                shift = shift - cst
            p = jnp.exp2(s - shift)
            self.l_ref[sx] = l_old + jnp.sum(p, axis=0, keepdims=True)
            self.p_ref[half, sx] = p.astype(BF16)

    def run(self, bound, schedule):
        self.bound = bound
        use_bound = bound <= BOUNDED_SOFTMAX_LIMIT
        for online, pred in ((False, use_bound), (True, jnp.logical_not(use_bound))):
            @pl.when(pred)
            def _():
                self.init(online)
                schedule(functools.partial(self.step, online))
                self.pv(online, self.nc - 1, 1)


def _flash_scratch(n_streams, head_dim, blk):
    return [
        pltpu.VMEM((2, n_streams, blk, blk), BF16),
        pltpu.VMEM((n_streams, 1, blk), F32),
        pltpu.VMEM((n_streams, 1, blk), F32),
        pltpu.VMEM((n_streams, 1, blk), F32),
        pltpu.VMEM((n_streams, head_dim, blk), F32),
    ]


def _score_bound(head_dim, g_q, g_k):
    return (head_dim ** 0.5 * LOG2E * 1.02 * jnp.max(jnp.abs(g_q)) * jnp.max(jnp.abs(g_k))).reshape(1)


def _mla_attn_kernel(bound_ref, q_ref, k_ref, v_ref, o_ref, *scratch, blk, q_tiles, group):
    st = _FlashState(q_ref, k_ref, v_ref, scratch, blk, q_tiles)

    def schedule(step):
        def body(cc, carry):
            for u in range(group):
                step(group * cc + u, u % 2)
            return carry

        lax.fori_loop(0, st.nc // group, body, 0)

    st.run(bound_ref[0], schedule)
    for sx, (qt, _) in enumerate(st.streams):
        o_t = st.acc_ref[sx] / st.l_ref[sx]
        o_ref[qt * blk:(qt + 1) * blk, :] = o_t.T.astype(BF16)


def _mla_attention(bound, q, k, v, gb, seq, blk, q_tiles, group):
    nc = seq // blk
    nq = nc // q_tiles
    assert group % 2 == 0 and nc % group == 0 and nc % q_tiles == 0
    kern = functools.partial(_mla_attn_kernel, blk=blk, q_tiles=q_tiles, group=group)
    q = q.reshape(gb, MLA_HEADS, 1, MLA_QK_PAD, seq)
    tq = q_tiles * blk
    return pl.pallas_call(
        kern,
        out_shape=jax.ShapeDtypeStruct((gb * seq, HEAD_OUT), BF16),
        grid=(gb, MLA_HEADS, nq),
        in_specs=[
            pl.BlockSpec(memory_space=pltpu.SMEM),
            pl.BlockSpec((1, 1, 1, MLA_QK_PAD, tq), lambda b, hd, i: (b, hd, 0, 0, i)),
            pl.BlockSpec((1, 1, seq, MLA_QK_PAD), lambda b, hd, i: (b, hd, 0, 0)),
            pl.BlockSpec((1, 1, nc, MLA_V, blk), lambda b, hd, i: (b, hd, 0, 0, 0)),
        ],
        out_specs=pl.BlockSpec((tq, MLA_V), lambda b, hd, i: (b * nq + i, hd)),
        scratch_shapes=_flash_scratch(q_tiles, MLA_V, blk),
        compiler_params=_params(("arbitrary", "arbitrary", "arbitrary")), name="mla_attn",
    )(bound, q, k, v)


def _diff_attn_kernel(bound_ref, q_ref, k_ref, v_ref, bias_ref, rb_ref, lam_ref, g_sub_ref, o_ref,
                      *scratch, blk, q_tiles, group, lam_init):
    hd = pl.program_id(0)
    i_first = pl.program_id(2) * q_tiles
    st = _FlashState(q_ref, k_ref, v_ref, scratch, blk, q_tiles)
    acc_ref, l_ref = st.acc_ref, st.l_ref
    n_groups = st.nc // group

    bias_max = jnp.abs(rb_ref[0, hd])
    for b in range(1, REL_BUCKETS):
        bias_max = jnp.maximum(bias_max, jnp.abs(rb_ref[b, hd]))

    far_before = rb_ref[REL_BUCKETS // 2 - 1, hd] * LOG2E
    far_after = rb_ref[REL_BUCKETS - 1, hd] * LOG2E
    first_near = (i_first - 1 + group) // group - 1
    last_near = (i_first + q_tiles) // group

    near_lo = jnp.maximum(first_near, 0)
    near_hi = jnp.minimum(last_near + 1, n_groups)

    def schedule(step):
        def far_group(cst):
            def body(cc, carry):
                for u in range(group):
                    step(group * cc + u, u % 2, cst=cst)
                return carry
            return body

        def near_group(cc, carry):
            for u in range(group):
                c = group * cc + u
                far = BIAS_TILES // 2
                tiles = [bias_ref[0, jnp.clip(c - (i_first + qt), -far, far) + far]
                         for qt in range(q_tiles)]
                step(c, u % 2, bias_tiles=tiles)
            return carry

        lax.fori_loop(0, near_lo, far_group(far_before), 0)
        lax.fori_loop(near_lo, near_hi, near_group, 0)
        lax.fori_loop(near_hi, n_groups, far_group(far_after), 0)

    st.run(bound_ref[0] + bias_max * LOG2E, schedule)

    lam_v = lam_ref[...]
    lam = (jnp.exp(jnp.sum(lam_v[0:1] * lam_v[1:2], axis=-1, keepdims=True))
           - jnp.exp(jnp.sum(lam_v[2:3] * lam_v[3:4], axis=-1, keepdims=True)) + lam_init)
    for qt in range(q_tiles):
        s0, s1 = 2 * qt, 2 * qt + 1
        o_t = acc_ref[s0] / l_ref[s0] - lam * (acc_ref[s1] / l_ref[s1])
        o_t = o_t * lax.rsqrt(jnp.mean(o_t * o_t, axis=0, keepdims=True) + EPS) * g_sub_ref[...]
        o_ref[qt * blk:(qt + 1) * blk, :] = (o_t * (1.0 - lam_init)).T.astype(BF16)


def _diff_attention(bound, q, k, v, bias, rel_bias, lam_rows, g_sub, gb, seq, blk, q_tiles, group,
                    lam_init):
    nc = seq // blk
    nq = nc // q_tiles
    assert group % 2 == 0 and nc % group == 0 and nc % q_tiles == 0 and blk >= REL_MAX_DIST
    kern = functools.partial(_diff_attn_kernel, blk=blk, q_tiles=q_tiles, group=group,
                             lam_init=lam_init)
    tq = q_tiles * blk
    return pl.pallas_call(
        kern,
        out_shape=jax.ShapeDtypeStruct((gb * seq, HEAD_OUT), BF16),
        grid=(DIFF_HEADS, gb, nq),
        in_specs=[
            pl.BlockSpec(memory_space=pltpu.SMEM),
            pl.BlockSpec((1, 1, 2, DIFF_V, tq), lambda hd, b, i: (b, hd, 0, 0, i)),
            pl.BlockSpec((1, 1, seq, DIFF_V), lambda hd, b, i: (b, hd, 0, 0)),
            pl.BlockSpec((1, 1, nc, DIFF_V, blk), lambda hd, b, i: (b, hd, 0, 0, 0)),
            pl.BlockSpec((1, BIAS_TILES, blk, blk), lambda hd, b, i: (hd, 0, 0, 0),
                         pipeline_mode=pl.Buffered(1)),
            pl.BlockSpec(memory_space=pltpu.SMEM),
            _const_spec((4, LANES)),
            _const_spec((DIFF_V, 1)),
        ],
        out_specs=pl.BlockSpec((tq, DIFF_V), lambda hd, b, i: (b * nq + i, hd)),
        scratch_shapes=_flash_scratch(2 * q_tiles, DIFF_V, blk),
        compiler_params=_params(("arbitrary", "arbitrary", "arbitrary")), name="diff_attn",
    )(bound, q, k, v, bias, rel_bias, lam_rows, g_sub)


def _merge_kernel(h_ref, oa_ref, ob_ref, wga_ref, wgb_ref, wua_ref, wub_ref, out_ref):
    h = h_ref[...]
    ga = lax.dot_general(h, wga_ref[...], NT_DIMS, preferred_element_type=F32)
    gb_ = lax.dot_general(h, wgb_ref[...], NT_DIMS, preferred_element_type=F32)
    ua = jnp.dot(oa_ref[...], wua_ref[...], preferred_element_type=F32)
    ub = jnp.dot(ob_ref[...], wub_ref[...], preferred_element_type=F32)
    out_ref[...] = (jax.nn.sigmoid(ga) * ua + jax.nn.sigmoid(gb_) * ub).astype(BF16)


def _merge(h2d, o_a, o_b, w, tm, tn):
    tokens = h2d.shape[0]
    return pl.pallas_call(
        _merge_kernel,
        out_shape=jax.ShapeDtypeStruct((tokens, D_MODEL), BF16),
        grid=(tokens // tm, D_MODEL // tn),
        in_specs=[
            pl.BlockSpec((tm, D_MODEL), lambda i, j: (i, 0)),
            pl.BlockSpec((tm, HEAD_OUT), lambda i, j: (i, 0)),
            pl.BlockSpec((tm, HEAD_OUT), lambda i, j: (i, 0)),
            pl.BlockSpec((tn, D_MODEL), lambda i, j: (j, 0)),
            pl.BlockSpec((tn, D_MODEL), lambda i, j: (j, 0)),
            pl.BlockSpec((HEAD_OUT, tn), lambda i, j: (0, j)),
            pl.BlockSpec((HEAD_OUT, tn), lambda i, j: (0, j)),
        ],
        out_specs=pl.BlockSpec((tm, tn), lambda i, j: (i, j)),
        compiler_params=_params(("arbitrary", "arbitrary")), name="gated_merge",
    )(h2d, o_a, o_b, w["w_ga_t"], w["w_gb_t"], w["w_ua"], w["w_ub"])


def _out_proj_kernel(x_ref, m_ref, wo_ref, out_ref):
    out_ref[...] = x_ref[...] + jnp.dot(m_ref[...], wo_ref[...], preferred_element_type=F32)


def _out_proj(x2d, merged, w, tm):
    tokens = x2d.shape[0]
    return pl.pallas_call(
        _out_proj_kernel,
        out_shape=jax.ShapeDtypeStruct((tokens, D_MODEL), F32),
        grid=(tokens // tm,),
        in_specs=[
            pl.BlockSpec((tm, D_MODEL), lambda i: (i, 0)),
            pl.BlockSpec((tm, D_MODEL), lambda i: (i, 0)),
            _const_spec((D_MODEL, D_MODEL)),
        ],
        out_specs=pl.BlockSpec((tm, D_MODEL), lambda i: (i, 0)),
        compiler_params=_params(("arbitrary",)), name="out_proj",
    )(x2d, merged, w["w_o"])


def _ffn_kernel(x_ref, g_ref, wg_ref, wu_ref, wd_ref, out_ref, h_ref):
    @pl.when(pl.program_id(1) == 0)
    def _():
        x = x_ref[...]
        h_ref[...] = _rms_rows(x, g_ref[...]).astype(BF16)
        out_ref[...] = x

    h = h_ref[...]
    g = jnp.dot(h, wg_ref[...], preferred_element_type=F32)
    u = jnp.dot(h, wu_ref[...], preferred_element_type=F32)
    a = (g * jax.nn.sigmoid(g) * u).astype(BF16)
    out_ref[...] += jnp.dot(a, wd_ref[...], preferred_element_type=F32)


def _ffn(x2d, w, tm, tf):
    tokens = x2d.shape[0]
    return pl.pallas_call(
        _ffn_kernel,
        out_shape=jax.ShapeDtypeStruct((tokens, D_MODEL), F32),
        grid=(tokens // tm, D_FF // tf),
        in_specs=[
            pl.BlockSpec((tm, D_MODEL), lambda i, j: (i, 0)),
            _const_spec((1, D_MODEL)),
            pl.BlockSpec((D_MODEL, tf), lambda i, j: (0, j)),
            pl.BlockSpec((D_MODEL, tf), lambda i, j: (0, j)),
            pl.BlockSpec((tf, D_MODEL), lambda i, j: (j, 0)),
        ],
        out_specs=pl.BlockSpec((tm, D_MODEL), lambda i, j: (i, 0)),
        scratch_shapes=[pltpu.VMEM((tm, D_MODEL), BF16)],
        compiler_params=_params(("arbitrary", "arbitrary"), [False, False, True, True, True]),
        name="swiglu_ffn",
    )(x2d, w["g_ffn"], w["w_gate"], w["w_up"], w["w_down"])


def _prepare_weights(mix_norm, w_in, q_a_norm, wq_b, kv_a_norm, wkv_b, mla_q_norm, mla_k_norm,
                     diff_q_norm, diff_k_norm, diff_subln, w_up_mla, w_up_diff, w_o, ffn_norm,
                     w_gate, w_up, w_down, layer):
    win_t = w_in[layer].T
    o_kpe = K_PE_COL
    o_dq = o_kpe + MLA_ROPE
    o_dk = o_dq + HEAD_OUT
    o_dv = o_dk + HEAD_OUT
    o_ga = o_dv + HEAD_OUT
    o_gb = o_ga + D_MODEL
    w_kpe_t = win_t[o_kpe:o_dq]
    w_kpe_rot_t = jnp.concatenate([-w_kpe_t[ROPE_HALF:], w_kpe_t[:ROPE_HALF]], axis=0)
    zpad = jnp.zeros((LANES - MLA_ROPE, D_MODEL), F32)
    w_a_t = jnp.concatenate([win_t[:o_kpe], w_kpe_t, zpad, w_kpe_rot_t, zpad], axis=0)
    wkv = wkv_b[layer].reshape(KV_LORA, MLA_HEADS, MLA_NOPE + MLA_V)
    gk = mla_k_norm[layer]
    gk_rope = gk[MLA_NOPE:]
    lane_pad = jnp.zeros((LANES - MLA_ROPE,), F32)
    return {
        "g_mix": mix_norm[layer][None, :],
        "w_a_t": w_a_t.astype(BF16),
        "g_qa": q_a_norm[layer][None, :],
        "g_kva": kv_a_norm[layer][None, :],
        "wq_t": wq_b[layer].T.astype(BF16),
        "g_q": (mla_q_norm[layer] * (MLA_QK ** -0.5 * LOG2E))[:, None],
        "wk_nope": wkv[:, :, :MLA_NOPE].reshape(KV_LORA, MLA_HEADS * MLA_NOPE).astype(BF16),
        "wv_t": wkv[:, :, MLA_NOPE:].reshape(KV_LORA, MLA_HEADS * MLA_V).T.astype(BF16),
        "g_kn": gk[None, :MLA_NOPE],
        "g_kr": jnp.concatenate([gk_rope, lane_pad])[None, :],
        "g_krot": jnp.concatenate([gk_rope[ROPE_HALF:], gk_rope[:ROPE_HALF], lane_pad])[None, :],
        "w_dq_t": win_t[o_dq:o_dk].astype(BF16),
        "w_dk_t": win_t[o_dk:o_dv].astype(BF16),
        "w_dv_t": win_t[o_dv:o_ga].astype(BF16),
        "g_dq": (diff_q_norm[layer] * (DIFF_QK ** -0.5 * LOG2E))[:, None],
        "g_dk": diff_k_norm[layer][:, None],
        "g_sub": diff_subln[layer][:, None],
        "w_ga_t": win_t[o_ga:o_gb].astype(BF16),
        "w_gb_t": win_t[o_gb:].astype(BF16),
        "w_ua": w_up_mla[layer].astype(BF16),
        "w_ub": w_up_diff[layer].astype(BF16),
        "w_o": w_o[layer].astype(BF16),
        "mla_bound": _score_bound(MLA_QK, mla_q_norm[layer], gk),
        "diff_bound": _score_bound(DIFF_QK, diff_q_norm[layer], diff_k_norm[layer]),
        "g_ffn": ffn_norm[layer][None, :],
        "w_gate": w_gate[layer].astype(BF16),
        "w_up": w_up[layer].astype(BF16),
        "w_down": w_down[layer].astype(BF16),
    }


class _Tiles(NamedTuple):
    tm: int = 512
    tn: int = 1024
    tf: int = 512
    tm_wide: int = 1024
    blk: int = 512
    mla_q_tiles: int = 4
    diff_q_tiles: int = 4
    mla_group_max: int = 4
    diff_group_max: int = 2


def _encoder_layer(x, w, tables, bias, rel_bias, lam_rows, lam_init, t):
    gb, seq, _ = x.shape
    x2d = x.reshape(gb * seq, D_MODEL)
    mla_group = math.gcd(t.mla_group_max, seq // t.blk)
    diff_group = math.gcd(t.diff_group_max, seq // t.blk)
    q_a, k_a, v_a, h2d = _mla_prep(x2d, gb, seq, t.tm, t.blk, w, tables)
    q_d, k_d, v_d = _diff_prep(h2d, gb, seq, t.tm, t.blk, w)
    o_a = _mla_attention(w["mla_bound"], q_a, k_a, v_a, gb, seq, t.blk, t.mla_q_tiles, mla_group)
    o_b = _diff_attention(w["diff_bound"], q_d, k_d, v_d, bias, rel_bias, lam_rows, w["g_sub"], gb,
                          seq, t.blk, t.diff_q_tiles, diff_group, lam_init)
    merged = _merge(h2d, o_a, o_b, w, t.tm_wide, t.tn)
    x1 = _out_proj(x2d, merged, w, t.tm)
    y = _ffn(x1, w, t.tm_wide, t.tf)
    return y.reshape(gb, seq, D_MODEL)


def kernel(x_prompt, x_sample, mix_norm, w_in, q_a_norm, wq_b, kv_a_norm, wkv_b, mla_q_norm, mla_k_norm, diff_q_norm, diff_k_norm, lambda_q1, lambda_k1, lambda_q2, lambda_k2, diff_subln, w_up_mla, w_up_diff, w_o, ffn_norm, w_gate, w_up, w_down, rel_bias):
    t = _Tiles()
    depth = w_in.shape[0]
    max_seq = max(x_prompt.shape[1], x_sample.shape[1])
    tables = _rope_tables(max_seq)
    bias = _bias_tiles(rel_bias, t.blk)
    y_prompt, y_sample = x_prompt, x_sample
    for layer in range(depth):
        w = _prepare_weights(mix_norm, w_in, q_a_norm, wq_b, kv_a_norm, wkv_b, mla_q_norm,
                             mla_k_norm, diff_q_norm, diff_k_norm, diff_subln, w_up_mla,
                             w_up_diff, w_o, ffn_norm, w_gate, w_up, w_down, layer)
        lam_init = 0.8 - 0.6 * math.exp(-0.3 * layer)
        lam_pad = jnp.zeros((LANES - DIFF_QK,), F32)
        lam_rows = jnp.stack([jnp.concatenate([v[layer], lam_pad])
                              for v in (lambda_q1, lambda_k1, lambda_q2, lambda_k2)])
        run = functools.partial(_encoder_layer, w=w, tables=tables, bias=bias, rel_bias=rel_bias,
                                lam_rows=lam_rows, lam_init=lam_init, t=t)
        y_prompt = run(y_prompt)
        y_sample = run(y_sample)
    return (y_prompt, y_sample)
```
